```python
import math
import jax, jax.numpy as jnp
from jax import lax
import numpy as np

D_MODEL = 1024
BATCH = 4
SEQ = 4096
DEPTH = 2
DEC_BATCH = 32
DEC_SEQ = 1
PAST_LEN = 8192
PAGE_SIZE = 128

HEAD_DIM = 64
N_MIXERS = 4
GROUP_WIDTH = D_MODEL // N_MIXERS
N_HEADS = GROUP_WIDTH // HEAD_DIM
MIX_WIDTH = N_MIXERS * GROUP_WIDTH
CONV_W = 4
GDN_CHUNK = 64
GDN_CONV_DIM = 3 * GROUP_WIDTH
SSM_STATE = 128
SSM_NGROUPS = 2
SSM_CHUNK = 64
SSM_CONV_DIM = GROUP_WIDTH + 2 * SSM_NGROUPS * SSM_STATE
MOBA_BLOCK = 256
MOBA_TOPK = 3
MOBA_QUERY_BLOCK = 64
ROPE_THETA = 500000.0
ROPE_DIM = HEAD_DIM // 4
QUERY_BLOCK = 128
N_EXPERTS = 32
TOP_K = 4
D_FF = D_MODEL
SWIGLU_LIMIT = 7.0
SWIGLU_ALPHA = 1.702
EXPERT_BLOCK = 256
N_IN = (3 * GROUP_WIDTH + GROUP_WIDTH + 2 * N_HEADS) + (SSM_CONV_DIM + GROUP_WIDTH + N_HEADS) + 3 * GROUP_WIDTH + 3 * GROUP_WIDTH
DN_ALPHA = (2 * DEPTH) ** 0.25
DN_BETA = (8 * DEPTH) ** -0.25
LN_EPS = 1e-5
RMS_EPS = 1e-6

kernel_name = 'hybrid_gdn_ssd_moba_stickbreak_moe_step'

F32 = jnp.float32


def _layer_norm(x, g, b):
    xf = x.astype(F32)
    mu = xf.mean(-1, keepdims=True)
    var = jnp.square(xf - mu).mean(-1, keepdims=True)
    return ((xf - mu) * lax.rsqrt(var + LN_EPS) * g + b).astype(x.dtype)


def _rms(x):
    xf = x.astype(F32)
    return xf * lax.rsqrt(jnp.mean(xf * xf, -1, keepdims=True) + RMS_EPS)


def _l2norm(x):
    xf = x.astype(F32)
    return xf * lax.rsqrt(jnp.sum(xf * xf, -1, keepdims=True) + RMS_EPS)


def _causal_conv(u, buf, w, b=None):
    length = u.shape[1]
    ext = jnp.concatenate([buf.astype(u.dtype), u], axis=1)
    y = sum(ext[:, i:i + length] * w[i] for i in range(CONV_W))
    if b is not None:
        y = y + b
    return y, ext[:, ext.shape[1] - (CONV_W - 1):]


def _rope(x, pos):
    half = ROPE_DIM // 2
    inv = ROPE_THETA ** (-jnp.arange(half, dtype=F32) * 2.0 / ROPE_DIM)
    ang = pos.astype(F32)[:, None] * inv
    cos = jnp.cos(ang)[None, :, None, :]
    sin = jnp.sin(ang)[None, :, None, :]
    xr = x[..., :ROPE_DIM].astype(F32)
    x1, x2 = xr[..., :half], xr[..., half:]
    rot = jnp.concatenate([x1 * cos - x2 * sin, x2 * cos + x1 * sin], -1)
    return jnp.concatenate([rot.astype(x.dtype), x[..., ROPE_DIM:]], -1)


def _gated_delta_chunked(q, k, v, g, beta, s0):
    bsz, length, nh, dk = q.shape
    dv = v.shape[-1]
    c = min(GDN_CHUNK, length)
    n = -(-length // c)
    pad = n * c - length

    def prep(t):
        t = jnp.pad(t.astype(F32), [(0, 0), (0, pad)] + [(0, 0)] * (t.ndim - 2))
        return jnp.moveaxis(t.reshape((bsz, n, c) + t.shape[2:]), 3, 1)

    q, k, v, g, beta = prep(q) * dk ** -0.5, prep(k), prep(v), prep(g), prep(beta)
    tril = jnp.tril(jnp.ones((c, c), bool))
    strict = jnp.tril(jnp.ones((c, c), bool), -1)
    gc = jnp.cumsum(g, axis=-1)
    diff = gc[..., :, None] - gc[..., None, :]
    decay = jnp.where(tril, jnp.exp(jnp.where(tril, diff, 0.0)), 0.0)
    kb = k * beta[..., None]
    a = jnp.where(strict, jnp.einsum('bhnid,bhnjd->bhnij', kb, k) * decay, 0.0)
    eye = jnp.broadcast_to(jnp.eye(c, dtype=F32), a.shape)
    t_inv = lax.linalg.triangular_solve(a, eye, left_side=True, lower=True, unit_diagonal=True)
    u = jnp.einsum('bhnij,bhnjd->bhnid', t_inv, v * beta[..., None])
    w = jnp.einsum('bhnij,bhnjd->bhnid', t_inv, kb * jnp.exp(gc)[..., None])
    attn = jnp.where(tril, jnp.einsum('bhnid,bhnjd->bhnij', q, k) * decay, 0.0)

    def step(s, inp):
        qi, ki, ui, wi, gi, ai = inp
        v_new = ui - jnp.einsum('bhcd,bhde->bhce', wi, s)
        o = (jnp.einsum('bhcd,bhde->bhce', qi * jnp.exp(gi)[..., None], s)
             + jnp.einsum('bhij,bhje->bhie', ai, v_new))
        g_last = gi[..., -1]
        s = (s * jnp.exp(g_last)[..., None, None]
             + jnp.einsum('bhcd,bhce->bhde', ki * jnp.exp(g_last[..., None] - gi)[..., None], v_new))
        return s, o

    xs = tuple(jnp.moveaxis(t, 2, 0) for t in (q, k, u, w, gc, attn))
    s_final, o = lax.scan(step, s0.astype(F32), xs)
    o = jnp.transpose(o, (1, 0, 3, 2, 4)).reshape(bsz, n * c, nh, dv)[:, :length]
    return o, s_final


def _ssd_chunked(x, dt, a, bm, cm, h0):
    bsz, length, nh, hp = x.shape
    c = min(SSM_CHUNK, length)
    n = -(-length // c)
    pad = n * c - length

    def prep(t):
        t = jnp.pad(t.astype(F32), [(0, 0), (0, pad)] + [(0, 0)] * (t.ndim - 2))
        return t.reshape((bsz, n, c) + t.shape[2:])

    xdt = prep(x * dt[..., None])
    ad = jnp.moveaxis(prep(dt * a), 3, 1)
    bm, cm = prep(bm), prep(cm)
    acs = jnp.cumsum(ad, -1)
    tril = jnp.tril(jnp.ones((c, c), bool))
    seg = acs[..., :, None] - acs[..., None, :]
    lmat = jnp.where(tril, jnp.exp(jnp.where(tril, seg, 0.0)), 0.0)
    cb = jnp.einsum('bnlhd,bnshd->bhnls', cm, bm) * lmat
    y_diag = jnp.einsum('bhnls,bnshp->bnlhp', cb, xdt)
    decay_states = jnp.exp(acs[..., -1:] - acs)
    states = jnp.einsum('bnlhd,bhnl,bnlhp->bnhpd', bm, decay_states, xdt)
    chunk_decay = jnp.exp(acs[..., -1])

    def step(h, inp):
        st, dec = inp
        return h * dec[..., None, None] + st, h

    h_final, h_in = lax.scan(step, h0.astype(F32),
                             (jnp.moveaxis(states, 1, 0), jnp.moveaxis(chunk_decay, 2, 0)))
    h_in = jnp.moveaxis(h_in, 0, 1)
    y_off = jnp.einsum('bnlhd,bnhpd,bhnl->bnlhp', cm, h_in, jnp.exp(acs))
    y = (y_diag + y_off).reshape(bsz, n * c, nh, hp)[:, :length]
    return y, h_final


def _moba_attend(q, k, v, q_pos0):
    bsz, lq, nh, hd = q.shape
    t_len = k.shape[1]
    nb = -(-t_len // MOBA_BLOCK)
    padk = nb * MOBA_BLOCK - t_len
    kb = jnp.pad(k, ((0, 0), (0, padk), (0, 0), (0, 0))).reshape(bsz, nb, MOBA_BLOCK, nh, hd).transpose(0, 3, 1, 2, 4)
    vb = jnp.pad(v, ((0, 0), (0, padk), (0, 0), (0, 0))).reshape(bsz, nb, MOBA_BLOCK, nh, hd).transpose(0, 3, 1, 2, 4)
    kmean = kb.astype(F32).mean(axis=3)
    qbs = min(MOBA_QUERY_BLOCK, lq)
    nq = -(-lq // qbs)
    qp = jnp.pad(q, ((0, 0), (0, nq * qbs - lq), (0, 0), (0, 0))).reshape(bsz, nq, qbs, nh, hd).transpose(1, 0, 3, 2, 4)
    qpos = q_pos0 + jnp.arange(nq * qbs).reshape(nq, qbs)
    topk = min(MOBA_TOPK, nb)
    bi = jnp.arange(bsz)[:, None, None, None]
    hi = jnp.arange(nh)[None, :, None, None]
    scale = hd ** -0.5

    def one_block(args):
        qblk, pos = args
        qf = qblk.astype(F32)
        own = pos // MOBA_BLOCK
        own_idx = jnp.minimum(own, nb - 1)
        gate = jnp.einsum('bhqd,bhnd->bhqn', qf, kmean)
        gate = jnp.where(jnp.arange(nb)[None, :] < own[:, None], gate, -jnp.inf)
        _, sel = lax.top_k(gate, topk)
        idx = jnp.concatenate([sel, jnp.broadcast_to(own_idx[None, None, :, None], (bsz, nh, qbs, 1))], -1)
        kg = kb[bi, hi, idx].astype(F32)
        vg = vb[bi, hi, idx].astype(F32)
        s = jnp.einsum('bhqd,bhqjkd->bhqjk', qf, kg) * scale
        sel_ok = jnp.arange(topk)[None, :] < own[:, None]
        own_ok = (own_idx[:, None] * MOBA_BLOCK + jnp.arange(MOBA_BLOCK)[None, :]) <= pos[:, None]
        mask = jnp.concatenate([jnp.broadcast_to(sel_ok[:, :, None], (qbs, topk, MOBA_BLOCK)), own_ok[:, None, :]], 1)
        s = jnp.where(mask, s, -jnp.inf)
        p = jax.nn.softmax(s.reshape(bsz, nh, qbs, -1), axis=-1).reshape(s.shape)
        return jnp.einsum('bhqjk,bhqjkd->bhqd', p, vg)

    out = lax.map(one_block, (qp, qpos))
    return out.transpose(1, 0, 3, 2, 4).reshape(bsz, nq * qbs, nh, hd)[:, :lq].astype(q.dtype)


def _stick_breaking_attend(q, k, v, q_pos0):
    bsz, lq, nh, hd = q.shape
    t_len = k.shape[1]
    qbs = min(QUERY_BLOCK, lq)
    nq = -(-lq // qbs)
    qp = jnp.pad(q, ((0, 0), (0, nq * qbs - lq), (0, 0), (0, 0))).reshape(bsz, nq, qbs, nh, hd).transpose(1, 0, 3, 2, 4)
    qpos = q_pos0 + jnp.arange(nq * qbs).reshape(nq, qbs)
    kf = jnp.swapaxes(k, 1, 2).astype(F32)
    vf = jnp.swapaxes(v, 1, 2).astype(F32)
    kpos = jnp.arange(t_len)
    scale = hd ** -0.5

    def one_block(args):
        qblk, pos = args
        z = jnp.einsum('bhqd,bhkd->bhqk', qblk.astype(F32), kf) * scale
        past = kpos[None, :] < pos[:, None]
        log_keep = jnp.where(past, jax.nn.log_sigmoid(-z), 0.0)
        log_after = lax.cumsum(log_keep, axis=3, reverse=True) - log_keep
        w = jnp.where(past, jnp.exp(jax.nn.log_sigmoid(z) + log_after), 0.0)
        return jnp.einsum('bhqk,bhkd->bhqd', w, vf)

    out = lax.map(one_block, (qp, qpos))
    return out.transpose(1, 0, 3, 2, 4).reshape(bsz, nq * qbs, nh, hd)[:, :lq].astype(q.dtype)


def _token_mixers(x, pos0, st, lw):
    gdn_conv0, gdn_s0, ssm_conv0, ssm_h0, past_kc, past_vc, past_ks, past_vs = st
    (w_in, gdn_conv_w, gdn_a_log, gdn_dt_bias, gdn_norm_w, ssm_conv_w, ssm_conv_b,
     ssm_a_log, ssm_dt_bias, ssm_d, ssm_norm_w, w_out) = lw
    bsz, length, _ = x.shape
    proj = x @ w_in
    sizes = (3 * GROUP_WIDTH, GROUP_WIDTH, N_HEADS, N_HEADS, SSM_CONV_DIM, GROUP_WIDTH, N_HEADS,
             3 * GROUP_WIDTH, 3 * GROUP_WIDTH)
    offs, run = [], 0
    for sz in sizes[:-1]:
        run += sz
        offs.append(run)
    gdn_qkv, gdn_z, gdn_b, gdn_a, ssm_xbc, ssm_z, ssm_dt, moba_qkv, sb_qkv = jnp.split(proj, offs, axis=-1)

    qkv, gdn_conv_new = _causal_conv(gdn_qkv, gdn_conv0, gdn_conv_w)
    qkv = jax.nn.silu(qkv).reshape(bsz, length, 3, N_HEADS, HEAD_DIM)
    gq, gk, gv = _l2norm(qkv[:, :, 0]), _l2norm(qkv[:, :, 1]), qkv[:, :, 2]
    g = -jnp.exp(gdn_a_log.astype(F32)) * jax.nn.softplus(gdn_a.astype(F32) + gdn_dt_bias)
    beta = jax.nn.sigmoid(gdn_b.astype(F32))
    o, gdn_s_new = _gated_delta_chunked(gq, gk, gv, g, beta, gdn_s0)
    gz = gdn_z.reshape(bsz, length, N_HEADS, HEAD_DIM).astype(F32)
    gdn_out = (_rms(o) * gdn_norm_w * jax.nn.silu(gz)).reshape(bsz, length, GROUP_WIDTH).astype(x.dtype)

    xbc, ssm_conv_new = _causal_conv(ssm_xbc, ssm_conv0, ssm_conv_w, ssm_conv_b)
    xbc = jax.nn.silu(xbc)
    xs, bm, cm = jnp.split(xbc, [GROUP_WIDTH, GROUP_WIDTH + SSM_NGROUPS * SSM_STATE], axis=-1)
    hpg = N_HEADS // SSM_NGROUPS
    bm = jnp.repeat(bm.reshape(bsz, length, SSM_NGROUPS, SSM_STATE), hpg, axis=2)
    cm = jnp.repeat(cm.reshape(bsz, length, SSM_NGROUPS, SSM_STATE), hpg, axis=2)
    dt = jax.nn.softplus(ssm_dt.astype(F32) + ssm_dt_bias)
    xh = xs.reshape(bsz, length, N_HEADS, HEAD_DIM).astype(F32)
    y, ssm_h_new = _ssd_chunked(xh, dt, -jnp.exp(ssm_a_log.astype(F32)), bm, cm, ssm_h0)
    y = y + ssm_d[:, None] * xh
    yz = (y.reshape(bsz, length, GROUP_WIDTH) * jax.nn.silu(ssm_z.astype(F32))).reshape(bsz, length, SSM_NGROUPS, -1)
    ssm_out = (_rms(yz).reshape(bsz, length, GROUP_WIDTH) * ssm_norm_w).astype(x.dtype)

    pos = pos0 + jnp.arange(length)
    mqkv = moba_qkv.reshape(bsz, length, 3, N_HEADS, HEAD_DIM)
    mq, mk, mv = _rope(mqkv[:, :, 0], pos), _rope(mqkv[:, :, 1], pos), mqkv[:, :, 2]
    mk_all = jnp.concatenate([past_kc.astype(x.dtype), mk], 1)
    mv_all = jnp.concatenate([past_vc.astype(x.dtype), mv], 1)
    moba_out = _moba_attend(mq, mk_all, mv_all, pos0).reshape(bsz, length, GROUP_WIDTH)

    sqkv = sb_qkv.reshape(bsz, length, 3, N_HEADS, HEAD_DIM)
    sq, sk, sv = sqkv[:, :, 0], sqkv[:, :, 1], sqkv[:, :, 2]
    sk_all = jnp.concatenate([past_ks.astype(x.dtype), sk], 1)
    sv_all = jnp.concatenate([past_vs.astype(x.dtype), sv], 1)
    sb_out = _stick_breaking_attend(sq, sk_all, sv_all, pos0).reshape(bsz, length, GROUP_WIDTH)

    mix = jnp.concatenate([gdn_out, ssm_out, moba_out, sb_out], -1) @ w_out
    new_state = (mk, mv, sk, sv, gdn_conv_new, gdn_s_new.astype(x.dtype), ssm_conv_new, ssm_h_new.astype(x.dtype))
    return mix, new_state


def _moe_ffn(x, router_w, router_b, w_gu, b_gu, w_down, b_down):
    xt = x.reshape(-1, D_MODEL)
    n_tok = xt.shape[0]
    logits = (xt @ router_w + router_b).astype(F32)
    top_val, top_idx = lax.top_k(logits, TOP_K)
    gates = jax.nn.softmax(top_val, axis=-1)
    n_assign = n_tok * TOP_K
    blk = min(EXPERT_BLOCK, n_assign)
    n_blocks = -(-n_assign // blk) + N_EXPERTS
    flat_e = top_idx.reshape(-1)
    flat_tok = jnp.arange(n_assign, dtype=jnp.int32) // TOP_K
    order = jnp.argsort(flat_e)
    sorted_e = flat_e[order]
    counts = jnp.zeros((N_EXPERTS,), jnp.int32).at[flat_e].add(1)
    blocks_per_e = (counts + blk - 1) // blk
    blk_end = jnp.cumsum(blocks_per_e)
    blk_start = blk_end - blocks_per_e
    row_start = jnp.cumsum(counts) - counts
    rank = jnp.arange(n_assign, dtype=jnp.int32) - row_start[sorted_e]
    dest = blk_start[sorted_e] * blk + rank
    rows_tok = jnp.zeros((n_blocks * blk,), jnp.int32).at[dest].set(flat_tok[order])
    rows_gate = jnp.zeros((n_blocks * blk,), F32).at[dest].set(gates.reshape(-1)[order])
    block_e = jnp.minimum(jnp.searchsorted(blk_end, jnp.arange(n_blocks, dtype=jnp.int32), side='right'), N_EXPERTS - 1)
    xb = xt[rows_tok].reshape(n_blocks, blk, D_MODEL)

    def expert_block(args):
        xblk, e = args
        h = xblk @ w_gu[e] + b_gu[e]
        gate = jnp.minimum(h[:, :D_FF], SWIGLU_LIMIT)
        up = jnp.clip(h[:, D_FF:], -SWIGLU_LIMIT, SWIGLU_LIMIT)
        act = (up + 1.0) * gate * jax.nn.sigmoid(SWIGLU_ALPHA * gate)
        return act @ w_down[e] + b_down[e]

    yb = lax.map(expert_block, (xb, block_e))
    y = jax.ops.segment_sum(yb.reshape(-1, D_MODEL).astype(F32) * rows_gate[:, None], rows_tok, num_segments=n_tok)
    return y.astype(x.dtype).reshape(x.shape)


def _decoder_layer(x, pos0, st, lw, fw):
    ln1_g, ln1_b, router_w, router_b, w_gu, b_gu, w_down, b_down, ln2_g, ln2_b = fw
    mix, new_state = _token_mixers(x, pos0, st, lw)
    x = _layer_norm(DN_ALPHA * x + mix, ln1_g, ln1_b)
    x = _layer_norm(DN_ALPHA * x + _moe_ffn(x, router_w, router_b, w_gu, b_gu, w_down, b_down), ln2_g, ln2_b)
    return x, new_state


def setup_inputs(seed: int = 0) -> dict:
    key = jax.random.key(seed)
    ks = jax.random.split(key, 40)
    n_pages = PAST_LEN // PAGE_SIZE
    n_used = DEC_BATCH * n_pages
    n_pool = n_used + max(1, n_used // 4)

    def nrm(k, shape, scale):
        return scale * jax.random.normal(k, shape, F32)

    def dt_bias(k):
        dt = jnp.exp(jax.random.uniform(k, (DEPTH, N_HEADS), F32, math.log(1e-3), math.log(1e-1)))
        return dt + jnp.log(-jnp.expm1(-dt))

    def a_log(k):
        return jnp.log(jax.random.uniform(k, (DEPTH, N_HEADS), F32, 1.0, 16.0))

    kv_shape = (DEPTH, n_pool, PAGE_SIZE, N_HEADS, HEAD_DIM)
    page_table = jax.random.permutation(ks[4], n_pool)[:n_used].reshape(DEC_BATCH, n_pages).astype(jnp.int32)
    return {
        'x_prompt': nrm(ks[0], (BATCH, SEQ, D_MODEL), 1.0),
        'x_sample': nrm(ks[1], (DEC_BATCH, DEC_SEQ, D_MODEL), 1.0),
        'cache_moba_k': nrm(ks[2], kv_shape, 1.0),
        'cache_moba_v': nrm(ks[3], kv_shape, 1.0),
        'cache_sb_k': nrm(ks[5], kv_shape, 1.0),
        'cache_sb_v': nrm(ks[6], kv_shape, 1.0),
        'state_gdn_conv': nrm(ks[7], (DEPTH, DEC_BATCH, CONV_W - 1, GDN_CONV_DIM), 1.0),
        'state_gdn_rec': nrm(ks[8], (DEPTH, DEC_BATCH, N_HEADS, HEAD_DIM, HEAD_DIM), 0.1),
        'state_ssm_conv': nrm(ks[9], (DEPTH, DEC_BATCH, CONV_W - 1, SSM_CONV_DIM), 1.0),
        'state_ssm_rec': nrm(ks[10], (DEPTH, DEC_BATCH, N_HEADS, HEAD_DIM, SSM_STATE), 0.1),
        'page_table': page_table,
        'w_in': nrm(ks[11], (DEPTH, D_MODEL, N_IN), D_MODEL ** -0.5),
        'gdn_conv_w': nrm(ks[12], (DEPTH, CONV_W, GDN_CONV_DIM), CONV_W ** -0.5),
        'gdn_a_log': a_log(ks[13]),
        'gdn_dt_bias': dt_bias(ks[14]),
        'gdn_norm_w': 1.0 + nrm(ks[15], (DEPTH, HEAD_DIM), 0.02),
        'ssm_conv_w': nrm(ks[16], (DEPTH, CONV_W, SSM_CONV_DIM), CONV_W ** -0.5),
        'ssm_conv_b': nrm(ks[17], (DEPTH, SSM_CONV_DIM), 0.02),
        'ssm_a_log': a_log(ks[18]),
        'ssm_dt_bias': dt_bias(ks[19]),
        'ssm_d': 1.0 + nrm(ks[20], (DEPTH, N_HEADS), 0.02),
        'ssm_norm_w': 1.0 + nrm(ks[21], (DEPTH, GROUP_WIDTH), 0.02),
        'w_out': nrm(ks[22], (DEPTH, MIX_WIDTH, D_MODEL), MIX_WIDTH ** -0.5 * DN_BETA),
        'ln1_g': 1.0 + nrm(ks[23], (DEPTH, D_MODEL), 0.02),
        'ln1_b': nrm(ks[24], (DEPTH, D_MODEL), 0.02),
        'router_w': nrm(ks[25], (DEPTH, D_MODEL, N_EXPERTS), D_MODEL ** -0.5),
        'router_b': nrm(ks[26], (DEPTH, N_EXPERTS), 0.01),
        'expert_w_gu': nrm(ks[27], (DEPTH, N_EXPERTS, D_MODEL, 2 * D_FF), D_MODEL ** -0.5),
        'expert_b_gu': nrm(ks[28], (DEPTH, N_EXPERTS, 2 * D_FF), 0.02),
        'expert_w_down': nrm(ks[29], (DEPTH, N_EXPERTS, D_FF, D_MODEL), D_FF ** -0.5 * DN_BETA),
        'expert_b_down': nrm(ks[30], (DEPTH, N_EXPERTS, D_MODEL), 0.02),
        'ln2_g': 1.0 + nrm(ks[31], (DEPTH, D_MODEL), 0.02),
        'ln2_b': nrm(ks[32], (DEPTH, D_MODEL), 0.02),
    }


def reference(x_prompt, x_sample, cache_moba_k, cache_moba_v, cache_sb_k, cache_sb_v,
              state_gdn_conv, state_gdn_rec, state_ssm_conv, state_ssm_rec, page_table,
              w_in, gdn_conv_w, gdn_a_log, gdn_dt_bias, gdn_norm_w,
              ssm_conv_w, ssm_conv_b, ssm_a_log, ssm_dt_bias, ssm_d, ssm_norm_w,
              w_out, ln1_g, ln1_b, router_w, router_b,
              expert_w_gu, expert_b_gu, expert_w_down, expert_b_down, ln2_g, ln2_b):
    bp = x_prompt.shape[0]
    bs, n_pages = page_table.shape
    past_len = n_pages * PAGE_SIZE
    dtype = x_prompt.dtype

    def gather_pages(pool):
        return pool[page_table].reshape(bs, past_len, N_HEADS, HEAD_DIM)

    empty = jnp.zeros((bp, 0, N_HEADS, HEAD_DIM), dtype)
    prompt_state = (jnp.zeros((bp, CONV_W - 1, GDN_CONV_DIM), dtype),
                    jnp.zeros((bp, N_HEADS, HEAD_DIM, HEAD_DIM), F32),
                    jnp.zeros((bp, CONV_W - 1, SSM_CONV_DIM), dtype),
                    jnp.zeros((bp, N_HEADS, HEAD_DIM, SSM_STATE), F32),
                    empty, empty, empty, empty)
    yp, ys = x_prompt, x_sample
    new_p, new_s = [], []
    for l in range(DEPTH):
        lw = (w_in[l], gdn_conv_w[l], gdn_a_log[l], gdn_dt_bias[l], gdn_norm_w[l], ssm_conv_w[l], ssm_conv_b[l],
              ssm_a_log[l], ssm_dt_bias[l], ssm_d[l], ssm_norm_w[l], w_out[l])
        fw = (ln1_g[l], ln1_b[l], router_w[l], router_b[l], expert_w_gu[l], expert_b_gu[l],
              expert_w_down[l], expert_b_down[l], ln2_g[l], ln2_b[l])
        sample_state = (state_gdn_conv[l], state_gdn_rec[l], state_ssm_conv[l], state_ssm_rec[l],
                        gather_pages(cache_moba_k[l]), gather_pages(cache_moba_v[l]),
                        gather_pages(cache_sb_k[l]), gather_pages(cache_sb_v[l]))
        yp, st_p = _decoder_layer(yp, 0, prompt_state, lw, fw)
        ys, st_s = _decoder_layer(ys, past_len, sample_state, lw, fw)
        new_p.append(st_p)
        new_s.append(st_s)

    def stk(states, i):
        return jnp.stack([s[i] for s in states])

    return (yp, ys,
            stk(new_p, 0), stk(new_p, 1), stk(new_p, 2), stk(new_p, 3),
            stk(new_p, 4), stk(new_p, 5), stk(new_p, 6), stk(new_p, 7),
            stk(new_s, 0), stk(new_s, 1), stk(new_s, 2), stk(new_s, 3),
            stk(new_s, 4), stk(new_s, 5), stk(new_s, 6), stk(new_s, 7))
```

```python
import functools
import math

import jax
import jax.numpy as jnp
import numpy as np
from jax import lax
from jax.experimental import pallas as pl
from jax.experimental.pallas import tpu as pltpu

D_MODEL = 1024
DEPTH = 2
PAGE_SIZE = 128
HEAD_DIM = 64
N_MIXERS = 4
GROUP_WIDTH = D_MODEL // N_MIXERS
N_HEADS = GROUP_WIDTH // HEAD_DIM
MIX_WIDTH = N_MIXERS * GROUP_WIDTH
CONV_W = 4
GDN_CHUNK = 64
GDN_CONV_DIM = 3 * GROUP_WIDTH
SSM_STATE = 128
SSM_NGROUPS = 2
SSM_CHUNK = 64
SSM_CONV_DIM = GROUP_WIDTH + 2 * SSM_NGROUPS * SSM_STATE
MOBA_BLOCK = 256
MOBA_TOPK = 3
MOBA_QUERY_BLOCK = 64
ROPE_THETA = 500000.0
ROPE_DIM = HEAD_DIM // 4
QUERY_BLOCK = 128
N_EXPERTS = 32
TOP_K = 4
D_FF = D_MODEL
SWIGLU_LIMIT = 7.0
SWIGLU_ALPHA = 1.702
EXPERT_BLOCK = 256
N_IN = (3 * GROUP_WIDTH + GROUP_WIDTH + 2 * N_HEADS) + (SSM_CONV_DIM + GROUP_WIDTH + N_HEADS) + 3 * GROUP_WIDTH + 3 * GROUP_WIDTH
DN_ALPHA = (2 * DEPTH) ** 0.25
LN_EPS = 1e-5
RMS_EPS = 1e-6

F32 = jnp.float32
BF16 = jnp.bfloat16

LANE = 128
VMEM_LIMIT = 48 * 1024 * 1024


def _round_up(n, m):
    return -(-n // m) * m


def _matmul_kernel(x_ref, w_ref, o_ref):
    o_ref[...] = jnp.dot(x_ref[...].astype(BF16), w_ref[...], preferred_element_type=F32)


def _matmul(x, w_bf16, tm):
    m, k = x.shape
    n = w_bf16.shape[1]
    tm = min(tm, m)
    return pl.pallas_call(
        _matmul_kernel,
        grid=(m // tm,),
        in_specs=[pl.BlockSpec((tm, k), lambda i: (i, 0)),
                  pl.BlockSpec((k, n), lambda i: (0, 0))],
        out_specs=pl.BlockSpec((tm, n), lambda i: (i, 0)),
        out_shape=jax.ShapeDtypeStruct((m, n), F32),
        compiler_params=pltpu.CompilerParams(dimension_semantics=("arbitrary",), vmem_limit_bytes=VMEM_LIMIT),
        name="dense_proj",
    )(x, w_bf16)


def _layer_norm(x, g, b):
    xf = x.astype(F32)
    mu = xf.mean(-1, keepdims=True)
    var = jnp.square(xf - mu).mean(-1, keepdims=True)
    return ((xf - mu) * lax.rsqrt(var + LN_EPS) * g + b).astype(x.dtype)


def _rms(x):
    xf = x.astype(F32)
    return xf * lax.rsqrt(jnp.mean(xf * xf, -1, keepdims=True) + RMS_EPS)


def _l2norm(x):
    xf = x.astype(F32)
    return xf * lax.rsqrt(jnp.sum(xf * xf, -1, keepdims=True) + RMS_EPS)


def _causal_conv(u, buf, w, b=None):
    length = u.shape[1]
    ext = jnp.concatenate([buf.astype(u.dtype), u], axis=1)
    y = sum(ext[:, i:i + length] * w[i] for i in range(CONV_W))
    if b is not None:
        y = y + b
    return y, ext[:, ext.shape[1] - (CONV_W - 1):]


def _rope(x, pos):
    half = ROPE_DIM // 2
    inv = ROPE_THETA ** (-jnp.arange(half, dtype=F32) * 2.0 / ROPE_DIM)
    ang = pos.astype(F32)[:, None] * inv
    cos = jnp.cos(ang)[None, :, None, :]
    sin = jnp.sin(ang)[None, :, None, :]
    xr = x[..., :ROPE_DIM].astype(F32)
    x1, x2 = xr[..., :half], xr[..., half:]
    rot = jnp.concatenate([x1 * cos - x2 * sin, x2 * cos + x1 * sin], -1)
    return jnp.concatenate([rot.astype(x.dtype), x[..., ROPE_DIM:]], -1)


def _gated_delta_chunked(q, k, v, g, beta, s0):
    bsz, length, nh, dk = q.shape
    dv = v.shape[-1]
    c = min(GDN_CHUNK, length)
    n = -(-length // c)
    pad = n * c - length

    def prep(t):
        t = jnp.pad(t.astype(F32), [(0, 0), (0, pad)] + [(0, 0)] * (t.ndim - 2))
        return jnp.moveaxis(t.reshape((bsz, n, c) + t.shape[2:]), 3, 1)

    q, k, v, g, beta = prep(q) * dk ** -0.5, prep(k), prep(v), prep(g), prep(beta)
    tril = jnp.tril(jnp.ones((c, c), bool))
    strict = jnp.tril(jnp.ones((c, c), bool), -1)
    gc = jnp.cumsum(g, axis=-1)
    diff = gc[..., :, None] - gc[..., None, :]
    decay = jnp.where(tril, jnp.exp(jnp.where(tril, diff, 0.0)), 0.0)
    kb = k * beta[..., None]
    a = jnp.where(strict, jnp.einsum('bhnid,bhnjd->bhnij', kb, k) * decay, 0.0)
    eye = jnp.broadcast_to(jnp.eye(c, dtype=F32), a.shape)
    t_inv = lax.linalg.triangular_solve(a, eye, left_side=True, lower=True, unit_diagonal=True)
    u = jnp.einsum('bhnij,bhnjd->bhnid', t_inv, v * beta[..., None])
    w = jnp.einsum('bhnij,bhnjd->bhnid', t_inv, kb * jnp.exp(gc)[..., None])
    attn = jnp.where(tril, jnp.einsum('bhnid,bhnjd->bhnij', q, k) * decay, 0.0)

    def step(s, inp):
        qi, ki, ui, wi, gi, ai = inp
        v_new = ui - jnp.einsum('bhcd,bhde->bhce', wi, s)
        o = (jnp.einsum('bhcd,bhde->bhce', qi * jnp.exp(gi)[..., None], s)
             + jnp.einsum('bhij,bhje->bhie', ai, v_new))
        g_last = gi[..., -1]
        s = (s * jnp.exp(g_last)[..., None, None]
             + jnp.einsum('bhcd,bhce->bhde', ki * jnp.exp(g_last[..., None] - gi)[..., None], v_new))
        return s, o

    xs = tuple(jnp.moveaxis(t, 2, 0) for t in (q, k, u, w, gc, attn))
    s_final, o = lax.scan(step, s0.astype(F32), xs)
    o = jnp.transpose(o, (1, 0, 3, 2, 4)).reshape(bsz, n * c, nh, dv)[:, :length]
    return o, s_final


def _ssd_chunked(x, dt, a, bm, cm, h0):
    bsz, length, nh, hp = x.shape
    c = min(SSM_CHUNK, length)
    n = -(-length // c)
    pad = n * c - length

    def prep(t):
        t = jnp.pad(t.astype(F32), [(0, 0), (0, pad)] + [(0, 0)] * (t.ndim - 2))
        return t.reshape((bsz, n, c) + t.shape[2:])

    xdt = prep(x * dt[..., None])
    ad = jnp.moveaxis(prep(dt * a), 3, 1)
    bm, cm = prep(bm), prep(cm)
    acs = jnp.cumsum(ad, -1)
    tril = jnp.tril(jnp.ones((c, c), bool))
    seg = acs[..., :, None] - acs[..., None, :]
    lmat = jnp.where(tril, jnp.exp(jnp.where(tril, seg, 0.0)), 0.0)
    cb = jnp.einsum('bnlhd,bnshd->bhnls', cm, bm) * lmat
    y_diag = jnp.einsum('bhnls,bnshp->bnlhp', cb, xdt)
    decay_states = jnp.exp(acs[..., -1:] - acs)
    states = jnp.einsum('bnlhd,bhnl,bnlhp->bnhpd', bm, decay_states, xdt)
    chunk_decay = jnp.exp(acs[..., -1])

    def step(h, inp):
        st, dec = inp
        return h * dec[..., None, None] + st, h

    h_final, h_in = lax.scan(step, h0.astype(F32),
                             (jnp.moveaxis(states, 1, 0), jnp.moveaxis(chunk_decay, 2, 0)))
    h_in = jnp.moveaxis(h_in, 0, 1)
    y_off = jnp.einsum('bnlhd,bnhpd,bhnl->bnlhp', cm, h_in, jnp.exp(acs))
    y = (y_diag + y_off).reshape(bsz, n * c, nh, hp)[:, :length]
    return y, h_final


def _moba_attend(q, k, v, q_pos0):
    bsz, lq, nh, hd = q.shape
    t_len = k.shape[1]
    nb = -(-t_len // MOBA_BLOCK)
    padk = nb * MOBA_BLOCK - t_len
    kb = jnp.pad(k, ((0, 0), (0, padk), (0, 0), (0, 0))).reshape(bsz, nb, MOBA_BLOCK, nh, hd).transpose(0, 3, 1, 2, 4)
    vb = jnp.pad(v, ((0, 0), (0, padk), (0, 0), (0, 0))).reshape(bsz, nb, MOBA_BLOCK, nh, hd).transpose(0, 3, 1, 2, 4)
    kmean = kb.astype(F32).mean(axis=3)
    qbs = min(MOBA_QUERY_BLOCK, lq)
    nq = -(-lq // qbs)
    qp = jnp.pad(q, ((0, 0), (0, nq * qbs - lq), (0, 0), (0, 0))).reshape(bsz, nq, qbs, nh, hd).transpose(1, 0, 3, 2, 4)
    qpos = q_pos0 + jnp.arange(nq * qbs).reshape(nq, qbs)
    topk = min(MOBA_TOPK, nb)
    bi = jnp.arange(bsz)[:, None, None, None]
    hi = jnp.arange(nh)[None, :, None, None]
    scale = hd ** -0.5

    def one_block(args):
        qblk, pos = args
        qf = qblk.astype(F32)
        own = pos // MOBA_BLOCK
        own_idx = jnp.minimum(own, nb - 1)
        gate = jnp.einsum('bhqd,bhnd->bhqn', qf, kmean)
        gate = jnp.where(jnp.arange(nb)[None, :] < own[:, None], gate, -jnp.inf)
        _, sel = lax.top_k(gate, topk)
        idx = jnp.concatenate([sel, jnp.broadcast_to(own_idx[None, None, :, None], (bsz, nh, qbs, 1))], -1)
        kg = kb[bi, hi, idx].astype(F32)
        vg = vb[bi, hi, idx].astype(F32)
        s = jnp.einsum('bhqd,bhqjkd->bhqjk', qf, kg) * scale
        sel_ok = jnp.arange(topk)[None, :] < own[:, None]
        own_ok = (own_idx[:, None] * MOBA_BLOCK + jnp.arange(MOBA_BLOCK)[None, :]) <= pos[:, None]
        mask = jnp.concatenate([jnp.broadcast_to(sel_ok[:, :, None], (qbs, topk, MOBA_BLOCK)), own_ok[:, None, :]], 1)
        s = jnp.where(mask, s, -jnp.inf)
        p = jax.nn.softmax(s.reshape(bsz, nh, qbs, -1), axis=-1).reshape(s.shape)
        return jnp.einsum('bhqjk,bhqjkd->bhqd', p, vg)

    out = lax.map(one_block, (qp, qpos))
    return out.transpose(1, 0, 3, 2, 4).reshape(bsz, nq * qbs, nh, hd)[:, :lq].astype(q.dtype)


def _stick_breaking_attend(q, k, v, q_pos0):
    bsz, lq, nh, hd = q.shape
    t_len = k.shape[1]
    qbs = min(QUERY_BLOCK, lq)
    nq = -(-lq // qbs)
    qp = jnp.pad(q, ((0, 0), (0, nq * qbs - lq), (0, 0), (0, 0))).reshape(bsz, nq, qbs, nh, hd).transpose(1, 0, 3, 2, 4)
    qpos = q_pos0 + jnp.arange(nq * qbs).reshape(nq, qbs)
    kf = jnp.swapaxes(k, 1, 2).astype(F32)
    vf = jnp.swapaxes(v, 1, 2).astype(F32)
    kpos = jnp.arange(t_len)
    scale = hd ** -0.5

    def one_block(args):
        qblk, pos = args
        z = jnp.einsum('bhqd,bhkd->bhqk', qblk.astype(F32), kf) * scale
        past = kpos[None, :] < pos[:, None]
        log_keep = jnp.where(past, jax.nn.log_sigmoid(-z), 0.0)
        log_after = lax.cumsum(log_keep, axis=3, reverse=True) - log_keep
        w = jnp.where(past, jnp.exp(jax.nn.log_sigmoid(z) + log_after), 0.0)
        return jnp.einsum('bhqk,bhkd->bhqd', w, vf)

    out = lax.map(one_block, (qp, qpos))
    return out.transpose(1, 0, 3, 2, 4).reshape(bsz, nq * qbs, nh, hd)[:, :lq].astype(q.dtype)


def _token_mixers(x, pos0, st, lw):
    gdn_conv0, gdn_s0, ssm_conv0, ssm_h0, past_kc, past_vc, past_ks, past_vs = st
    (w_in, gdn_conv_w, gdn_a_log, gdn_dt_bias, gdn_norm_w, ssm_conv_w, ssm_conv_b,
     ssm_a_log, ssm_dt_bias, ssm_d, ssm_norm_w, w_out) = lw
    bsz, length, _ = x.shape
    n_pad = _round_up(N_IN, LANE)
    w_in_p = jnp.pad(w_in, ((0, 0), (0, n_pad - N_IN))).astype(BF16)
    proj = _matmul(x.reshape(bsz * length, D_MODEL), w_in_p, 256)[:, :N_IN].reshape(bsz, length, N_IN)
    sizes = (3 * GROUP_WIDTH, GROUP_WIDTH, N_HEADS, N_HEADS, SSM_CONV_DIM, GROUP_WIDTH, N_HEADS,
             3 * GROUP_WIDTH, 3 * GROUP_WIDTH)
    offs, run = [], 0
    for sz in sizes[:-1]:
        run += sz
        offs.append(run)
    gdn_qkv, gdn_z, gdn_b, gdn_a, ssm_xbc, ssm_z, ssm_dt, moba_qkv, sb_qkv = jnp.split(proj, offs, axis=-1)

    qkv, gdn_conv_new = _causal_conv(gdn_qkv, gdn_conv0, gdn_conv_w)
    qkv = jax.nn.silu(qkv).reshape(bsz, length, 3, N_HEADS, HEAD_DIM)
    gq, gk, gv = _l2norm(qkv[:, :, 0]), _l2norm(qkv[:, :, 1]), qkv[:, :, 2]
    g = -jnp.exp(gdn_a_log.astype(F32)) * jax.nn.softplus(gdn_a.astype(F32) + gdn_dt_bias)
    beta = jax.nn.sigmoid(gdn_b.astype(F32))
    o, gdn_s_new = _gated_delta_chunked(gq, gk, gv, g, beta, gdn_s0)
    gz = gdn_z.reshape(bsz, length, N_HEADS, HEAD_DIM).astype(F32)
    gdn_out = (_rms(o) * gdn_norm_w * jax.nn.silu(gz)).reshape(bsz, length, GROUP_WIDTH).astype(x.dtype)

    xbc, ssm_conv_new = _causal_conv(ssm_xbc, ssm_conv0, ssm_conv_w, ssm_conv_b)
    xbc = jax.nn.silu(xbc)
    xs, bm, cm = jnp.split(xbc, [GROUP_WIDTH, GROUP_WIDTH + SSM_NGROUPS * SSM_STATE], axis=-1)
    hpg = N_HEADS // SSM_NGROUPS
    bm = jnp.repeat(bm.reshape(bsz, length, SSM_NGROUPS, SSM_STATE), hpg, axis=2)
    cm = jnp.repeat(cm.reshape(bsz, length, SSM_NGROUPS, SSM_STATE), hpg, axis=2)
    dt = jax.nn.softplus(ssm_dt.astype(F32) + ssm_dt_bias)
    xh = xs.reshape(bsz, length, N_HEADS, HEAD_DIM).astype(F32)
    y, ssm_h_new = _ssd_chunked(xh, dt, -jnp.exp(ssm_a_log.astype(F32)), bm, cm, ssm_h0)
    y = y + ssm_d[:, None] * xh
    yz = (y.reshape(bsz, length, GROUP_WIDTH) * jax.nn.silu(ssm_z.astype(F32))).reshape(bsz, length, SSM_NGROUPS, -1)
    ssm_out = (_rms(yz).reshape(bsz, length, GROUP_WIDTH) * ssm_norm_w).astype(x.dtype)

    pos = pos0 + jnp.arange(length)
    mqkv = moba_qkv.reshape(bsz, length, 3, N_HEADS, HEAD_DIM)
    mq, mk, mv = _rope(mqkv[:, :, 0], pos), _rope(mqkv[:, :, 1], pos), mqkv[:, :, 2]
    mk_all = jnp.concatenate([past_kc.astype(x.dtype), mk], 1)
    mv_all = jnp.concatenate([past_vc.astype(x.dtype), mv], 1)
    moba_out = _moba_attend(mq, mk_all, mv_all, pos0).reshape(bsz, length, GROUP_WIDTH)

    sqkv = sb_qkv.reshape(bsz, length, 3, N_HEADS, HEAD_DIM)
    sq, sk, sv = sqkv[:, :, 0], sqkv[:, :, 1], sqkv[:, :, 2]
    sk_all = jnp.concatenate([past_ks.astype(x.dtype), sk], 1)
    sv_all = jnp.concatenate([past_vs.astype(x.dtype), sv], 1)
    sb_out = _stick_breaking_attend(sq, sk_all, sv_all, pos0).reshape(bsz, length, GROUP_WIDTH)

    cat = jnp.concatenate([gdn_out, ssm_out, moba_out, sb_out], -1)
    mix = _matmul(cat.reshape(bsz * length, MIX_WIDTH), w_out.astype(BF16), 256).reshape(bsz, length, D_MODEL)
    new_state = (mk, mv, sk, sv, gdn_conv_new, gdn_s_new.astype(x.dtype), ssm_conv_new, ssm_h_new.astype(x.dtype))
    return mix, new_state


def _moe_ffn(x, router_w, router_b, w_gu, b_gu, w_down, b_down):
    xt = x.reshape(-1, D_MODEL)
    n_tok = xt.shape[0]
    logits = (xt @ router_w + router_b).astype(F32)
    top_val, top_idx = lax.top_k(logits, TOP_K)
    gates = jax.nn.softmax(top_val, axis=-1)
    n_assign = n_tok * TOP_K
    blk = min(EXPERT_BLOCK, n_assign)
    n_blocks = -(-n_assign // blk) + N_EXPERTS
    flat_e = top_idx.reshape(-1)
    flat_tok = jnp.arange(n_assign, dtype=jnp.int32) // TOP_K
    order = jnp.argsort(flat_e)
    sorted_e = flat_e[order]
    counts = jnp.zeros((N_EXPERTS,), jnp.int32).at[flat_e].add(1)
    blocks_per_e = (counts + blk - 1) // blk
    blk_end = jnp.cumsum(blocks_per_e)
    blk_start = blk_end - blocks_per_e
    row_start = jnp.cumsum(counts) - counts
    rank = jnp.arange(n_assign, dtype=jnp.int32) - row_start[sorted_e]
    dest = blk_start[sorted_e] * blk + rank
    rows_tok = jnp.zeros((n_blocks * blk,), jnp.int32).at[dest].set(flat_tok[order])
    rows_gate = jnp.zeros((n_blocks * blk,), F32).at[dest].set(gates.reshape(-1)[order])
    block_e = jnp.minimum(jnp.searchsorted(blk_end, jnp.arange(n_blocks, dtype=jnp.int32), side='right'), N_EXPERTS - 1)
    xb = xt[rows_tok].reshape(n_blocks, blk, D_MODEL)

    def expert_block(args):
        xblk, e = args
        h = xblk @ w_gu[e] + b_gu[e]
        gate = jnp.minimum(h[:, :D_FF], SWIGLU_LIMIT)
        up = jnp.clip(h[:, D_FF:], -SWIGLU_LIMIT, SWIGLU_LIMIT)
        act = (up + 1.0) * gate * jax.nn.sigmoid(SWIGLU_ALPHA * gate)
        return act @ w_down[e] + b_down[e]

    yb = lax.map(expert_block, (xb, block_e))
    y = jax.ops.segment_sum(yb.reshape(-1, D_MODEL).astype(F32) * rows_gate[:, None], rows_tok, num_segments=n_tok)
    return y.astype(x.dtype).reshape(x.shape)


def _decoder_layer(x, pos0, st, lw, fw):
    ln1_g, ln1_b, router_w, router_b, w_gu, b_gu, w_down, b_down, ln2_g, ln2_b = fw
    mix, new_state = _token_mixers(x, pos0, st, lw)
    x = _layer_norm(DN_ALPHA * x + mix, ln1_g, ln1_b)
    x = _layer_norm(DN_ALPHA * x + _moe_ffn(x, router_w, router_b, w_gu, b_gu, w_down, b_down), ln2_g, ln2_b)
    return x, new_state


def kernel(x_prompt, x_sample, cache_moba_k, cache_moba_v, cache_sb_k, cache_sb_v,
           state_gdn_conv, state_gdn_rec, state_ssm_conv, state_ssm_rec, page_table,
           w_in, gdn_conv_w, gdn_a_log, gdn_dt_bias, gdn_norm_w,
           ssm_conv_w, ssm_conv_b, ssm_a_log, ssm_dt_bias, ssm_d, ssm_norm_w,
           w_out, ln1_g, ln1_b, router_w, router_b,
           expert_w_gu, expert_b_gu, expert_w_down, expert_b_down, ln2_g, ln2_b):
    bp = x_prompt.shape[0]
    bs, n_pages = page_table.shape
    past_len = n_pages * PAGE_SIZE
    dtype = x_prompt.dtype

    def gather_pages(pool):
        return pool[page_table].reshape(bs, past_len, N_HEADS, HEAD_DIM)

    empty = jnp.zeros((bp, 0, N_HEADS, HEAD_DIM), dtype)
    prompt_state = (jnp.zeros((bp, CONV_W - 1, GDN_CONV_DIM), dtype),
                    jnp.zeros((bp, N_HEADS, HEAD_DIM, HEAD_DIM), F32),
                    jnp.zeros((bp, CONV_W - 1, SSM_CONV_DIM), dtype),
                    jnp.zeros((bp, N_HEADS, HEAD_DIM, SSM_STATE), F32),
                    empty, empty, empty, empty)
    yp, ys = x_prompt, x_sample
    new_p, new_s = [], []
    for l in range(DEPTH):
        lw = (w_in[l], gdn_conv_w[l], gdn_a_log[l], gdn_dt_bias[l], gdn_norm_w[l], ssm_conv_w[l], ssm_conv_b[l],
              ssm_a_log[l], ssm_dt_bias[l], ssm_d[l], ssm_norm_w[l], w_out[l])
        fw = (ln1_g[l], ln1_b[l], router_w[l], router_b[l], expert_w_gu[l], expert_b_gu[l],
              expert_w_down[l], expert_b_down[l], ln2_g[l], ln2_b[l])
        sample_state = (state_gdn_conv[l], state_gdn_rec[l], state_ssm_conv[l], state_ssm_rec[l],
                        gather_pages(cache_moba_k[l]), gather_pages(cache_moba_v[l]),
                        gather_pages(cache_sb_k[l]), gather_pages(cache_sb_v[l]))
        yp, st_p = _decoder_layer(yp, 0, prompt_state, lw, fw)
        ys, st_s = _decoder_layer(ys, past_len, sample_state, lw, fw)
        new_p.append(st_p)
        new_s.append(st_s)

    def stk(states, i):
        return jnp.stack([s[i] for s in states])

    return (yp, ys,
            stk(new_p, 0), stk(new_p, 1), stk(new_p, 2), stk(new_p, 3),
            stk(new_p, 4), stk(new_p, 5), stk(new_p, 6), stk(new_p, 7),
            stk(new_s, 0), stk(new_s, 1), stk(new_s, 2), stk(new_s, 3),
            stk(new_s, 4), stk(new_s, 5), stk(new_s, 6), stk(new_s, 7))
```

```python
import functools
import math

import jax
import jax.numpy as jnp
import numpy as np
from jax import lax
from jax.experimental import pallas as pl
from jax.experimental.pallas import tpu as pltpu

D_MODEL = 1024
DEPTH = 2
PAGE_SIZE = 128
HEAD_DIM = 64
N_MIXERS = 4
GROUP_WIDTH = D_MODEL // N_MIXERS
N_HEADS = GROUP_WIDTH // HEAD_DIM
MIX_WIDTH = N_MIXERS * GROUP_WIDTH
CONV_W = 4
GDN_CHUNK = 64
GDN_CONV_DIM = 3 * GROUP_WIDTH
SSM_STATE = 128
SSM_NGROUPS = 2
SSM_CHUNK = 64
SSM_CONV_DIM = GROUP_WIDTH + 2 * SSM_NGROUPS * SSM_STATE
MOBA_BLOCK = 256
MOBA_TOPK = 3
MOBA_QUERY_BLOCK = 64
ROPE_THETA = 500000.0
ROPE_DIM = HEAD_DIM // 4
QUERY_BLOCK = 128
N_EXPERTS = 32
TOP_K = 4
D_FF = D_MODEL
SWIGLU_LIMIT = 7.0
SWIGLU_ALPHA = 1.702
EXPERT_BLOCK = 256
N_IN = (3 * GROUP_WIDTH + GROUP_WIDTH + 2 * N_HEADS) + (SSM_CONV_DIM + GROUP_WIDTH + N_HEADS) + 3 * GROUP_WIDTH + 3 * GROUP_WIDTH
DN_ALPHA = (2 * DEPTH) ** 0.25
LN_EPS = 1e-5
RMS_EPS = 1e-6

F32 = jnp.float32
BF16 = jnp.bfloat16

LANE = 128
VMEM_LIMIT = 48 * 1024 * 1024


def _round_up(n, m):
    return -(-n // m) * m


def _matmul_kernel(x_ref, w_ref, o_ref):
    o_ref[...] = jnp.dot(x_ref[...].astype(BF16), w_ref[...], preferred_element_type=F32)


def _matmul(x, w_bf16, tm):
    m, k = x.shape
    n = w_bf16.shape[1]
    tm = min(tm, m)
    return pl.pallas_call(
        _matmul_kernel,
        grid=(m // tm,),
        in_specs=[pl.BlockSpec((tm, k), lambda i: (i, 0)),
                  pl.BlockSpec((k, n), lambda i: (0, 0))],
        out_specs=pl.BlockSpec((tm, n), lambda i: (i, 0)),
        out_shape=jax.ShapeDtypeStruct((m, n), F32),
        compiler_params=pltpu.CompilerParams(dimension_semantics=("arbitrary",), vmem_limit_bytes=VMEM_LIMIT),
        name="dense_proj",
    )(x, w_bf16)


NEG = -1e30


def _softplus(z):
    return jnp.maximum(z, 0.0) + jnp.log1p(jnp.exp(-jnp.abs(z)))


def _split_dot(x, m_bf16):
    hi = x.astype(BF16)
    lo = (x - hi.astype(F32)).astype(BF16)
    return (jnp.dot(hi, m_bf16, preferred_element_type=F32)
            + jnp.dot(lo, m_bf16, preferred_element_type=F32))


def _sb_kernel(q_ref, k_ref, v_ref, o_ref, *, tq):
    i = pl.program_id(2)
    q = (q_ref[0, 0] * (HEAD_DIM ** -0.5)).astype(BF16)
    row = lax.broadcasted_iota(jnp.int32, (tq, tq), 0)
    col = lax.broadcasted_iota(jnp.int32, (tq, tq), 1)
    upper = (row > col).astype(BF16)
    past = col < row

    def block(j, carry, acc, diag):
        kj = k_ref[0, 0, pl.ds(j * tq, tq), :].astype(BF16)
        vj = v_ref[0, 0, pl.ds(j * tq, tq), :].astype(BF16)
        z = lax.dot_general(q, kj, (((1,), (1,)), ((), ())), preferred_element_type=F32)
        ls = -_softplus(z)
        lk = jnp.where(past, ls, 0.0) if diag else ls
        la = _split_dot(lk, upper)
        w = jnp.exp(z + ls + la + carry)
        if diag:
            w = jnp.where(past, w, 0.0)
        acc = acc + jnp.dot(w.astype(BF16), vj, preferred_element_type=F32)
        carry = carry + la[:, 0:1] + lk[:, 0:1]
        return carry, acc

    carry, acc = block(i, jnp.zeros((tq, 1), F32), jnp.zeros((tq, HEAD_DIM), F32), True)
    carry, acc = lax.fori_loop(0, i, lambda t, c: block(i - 1 - t, c[0], c[1], False), (carry, acc))
    o_ref[0, 0] = acc


def _sb_attention(q, k, v, tq=256):
    b, h, l, d = q.shape
    assert l % tq == 0
    return pl.pallas_call(
        functools.partial(_sb_kernel, tq=tq),
        grid=(b, h, l // tq),
        in_specs=[pl.BlockSpec((1, 1, tq, d), lambda bi, hi, i: (bi, hi, i, 0)),
                  pl.BlockSpec((1, 1, l, d), lambda bi, hi, i: (bi, hi, 0, 0)),
                  pl.BlockSpec((1, 1, l, d), lambda bi, hi, i: (bi, hi, 0, 0))],
        out_specs=pl.BlockSpec((1, 1, tq, d), lambda bi, hi, i: (bi, hi, i, 0)),
        out_shape=jax.ShapeDtypeStruct((b, h, l, d), F32),
        compiler_params=pltpu.CompilerParams(dimension_semantics=("arbitrary",) * 3, vmem_limit_bytes=VMEM_LIMIT),
        name="sb_attention",
    )(q, k, v)


def _moba_kernel(q_ref, k_ref, v_ref, o_ref, kmean_ref, *, nb):
    t = MOBA_BLOCK
    i = pl.program_id(2)

    @pl.when(i == 0)
    def _():
        for n in range(nb):
            kmean_ref[n:n + 1, :] = jnp.mean(k_ref[0, 0, n * t:(n + 1) * t, :], axis=0, keepdims=True)

    qf = q_ref[0, 0]
    q = (qf * (HEAD_DIM ** -0.5)).astype(BF16)
    gate = lax.dot_general(qf, kmean_ref[...], (((1,), (1,)), ((), ())),
                           preferred_element_type=F32, precision=lax.Precision.HIGHEST)
    lane = lax.broadcasted_iota(jnp.int32, (t, nb), 1)
    cnt = jnp.zeros((t, nb), F32)
    for n2 in range(nb):
        cn = gate[:, n2:n2 + 1]
        beats = (cn > gate) | ((cn == gate) & (n2 < lane))
        cnt = cnt + jnp.where(beats & (n2 < i), 1.0, 0.0)
    sel = jnp.where((cnt < MOBA_TOPK) & (lane < i), 1.0, 0.0)

    row = lax.broadcasted_iota(jnp.int32, (t, t), 0)
    col = lax.broadcasted_iota(jnp.int32, (t, t), 1)

    ki = k_ref[0, 0, pl.ds(i * t, t), :].astype(BF16)
    vi = v_ref[0, 0, pl.ds(i * t, t), :].astype(BF16)
    s = lax.dot_general(q, ki, (((1,), (1,)), ((), ())), preferred_element_type=F32)
    s = jnp.where(col <= row, s, NEG)
    m = jnp.max(s, axis=-1, keepdims=True)
    p = jnp.exp(s - m)
    l = jnp.sum(p, axis=-1, keepdims=True)
    acc = jnp.dot(p.astype(BF16), vi, preferred_element_type=F32)

    def body(j, c):
        m, l, acc = c
        kj = k_ref[0, 0, pl.ds(j * t, t), :].astype(BF16)
        vj = v_ref[0, 0, pl.ds(j * t, t), :].astype(BF16)
        s = lax.dot_general(q, kj, (((1,), (1,)), ((), ())), preferred_element_type=F32)
        sj = jnp.sum(jnp.where(lane == j, sel, 0.0), axis=-1, keepdims=True)
        s = jnp.where(sj > 0.5, s, NEG)
        m_new = jnp.maximum(m, jnp.max(s, axis=-1, keepdims=True))
        a = jnp.exp(m - m_new)
        p = jnp.exp(s - m_new)
        l = a * l + jnp.sum(p, axis=-1, keepdims=True)
        acc = a * acc + jnp.dot(p.astype(BF16), vj, preferred_element_type=F32)
        return m_new, l, acc

    m, l, acc = lax.fori_loop(0, i, body, (m, l, acc))
    o_ref[0, 0] = acc / l


def _moba_attention(q, k, v):
    b, h, l, d = q.shape
    t = MOBA_BLOCK
    assert l % t == 0
    nb = l // t
    return pl.pallas_call(
        functools.partial(_moba_kernel, nb=nb),
        grid=(b, h, nb),
        in_specs=[pl.BlockSpec((1, 1, t, d), lambda bi, hi, i: (bi, hi, i, 0)),
                  pl.BlockSpec((1, 1, l, d), lambda bi, hi, i: (bi, hi, 0, 0)),
                  pl.BlockSpec((1, 1, l, d), lambda bi, hi, i: (bi, hi, 0, 0))],
        out_specs=pl.BlockSpec((1, 1, t, d), lambda bi, hi, i: (bi, hi, i, 0)),
        out_shape=jax.ShapeDtypeStruct((b, h, l, d), F32),
        scratch_shapes=[pltpu.VMEM((nb, d), F32)],
        compiler_params=pltpu.CompilerParams(dimension_semantics=("arbitrary",) * 3, vmem_limit_bytes=VMEM_LIMIT),
        name="moba_attention",
    )(q, k, v)


def _head_major(t):
    return jnp.transpose(t, (0, 2, 1, 3))


def _layer_norm(x, g, b):
    xf = x.astype(F32)
    mu = xf.mean(-1, keepdims=True)
    var = jnp.square(xf - mu).mean(-1, keepdims=True)
    return ((xf - mu) * lax.rsqrt(var + LN_EPS) * g + b).astype(x.dtype)


def _rms(x):
    xf = x.astype(F32)
    return xf * lax.rsqrt(jnp.mean(xf * xf, -1, keepdims=True) + RMS_EPS)


def _l2norm(x):
    xf = x.astype(F32)
    return xf * lax.rsqrt(jnp.sum(xf * xf, -1, keepdims=True) + RMS_EPS)


def _causal_conv(u, buf, w, b=None):
    length = u.shape[1]
    ext = jnp.concatenate([buf.astype(u.dtype), u], axis=1)
    y = sum(ext[:, i:i + length] * w[i] for i in range(CONV_W))
    if b is not None:
        y = y + b
    return y, ext[:, ext.shape[1] - (CONV_W - 1):]


def _rope(x, pos):
    half = ROPE_DIM // 2
    inv = ROPE_THETA ** (-jnp.arange(half, dtype=F32) * 2.0 / ROPE_DIM)
    ang = pos.astype(F32)[:, None] * inv
    cos = jnp.cos(ang)[None, :, None, :]
    sin = jnp.sin(ang)[None, :, None, :]
    xr = x[..., :ROPE_DIM].astype(F32)
    x1, x2 = xr[..., :half], xr[..., half:]
    rot = jnp.concatenate([x1 * cos - x2 * sin, x2 * cos + x1 * sin], -1)
    return jnp.concatenate([rot.astype(x.dtype), x[..., ROPE_DIM:]], -1)


def _gated_delta_chunked(q, k, v, g, beta, s0):
    bsz, length, nh, dk = q.shape
    dv = v.shape[-1]
    c = min(GDN_CHUNK, length)
    n = -(-length // c)
    pad = n * c - length

    def prep(t):
        t = jnp.pad(t.astype(F32), [(0, 0), (0, pad)] + [(0, 0)] * (t.ndim - 2))
        return jnp.moveaxis(t.reshape((bsz, n, c) + t.shape[2:]), 3, 1)

    q, k, v, g, beta = prep(q) * dk ** -0.5, prep(k), prep(v), prep(g), prep(beta)
    tril = jnp.tril(jnp.ones((c, c), bool))
    strict = jnp.tril(jnp.ones((c, c), bool), -1)
    gc = jnp.cumsum(g, axis=-1)
    diff = gc[..., :, None] - gc[..., None, :]
    decay = jnp.where(tril, jnp.exp(jnp.where(tril, diff, 0.0)), 0.0)
    kb = k * beta[..., None]
    a = jnp.where(strict, jnp.einsum('bhnid,bhnjd->bhnij', kb, k) * decay, 0.0)
    eye = jnp.broadcast_to(jnp.eye(c, dtype=F32), a.shape)
    t_inv = lax.linalg.triangular_solve(a, eye, left_side=True, lower=True, unit_diagonal=True)
    u = jnp.einsum('bhnij,bhnjd->bhnid', t_inv, v * beta[..., None])
    w = jnp.einsum('bhnij,bhnjd->bhnid', t_inv, kb * jnp.exp(gc)[..., None])
    attn = jnp.where(tril, jnp.einsum('bhnid,bhnjd->bhnij', q, k) * decay, 0.0)

    def step(s, inp):
        qi, ki, ui, wi, gi, ai = inp
        v_new = ui - jnp.einsum('bhcd,bhde->bhce', wi, s)
        o = (jnp.einsum('bhcd,bhde->bhce', qi * jnp.exp(gi)[..., None], s)
             + jnp.einsum('bhij,bhje->bhie', ai, v_new))
        g_last = gi[..., -1]
        s = (s * jnp.exp(g_last)[..., None, None]
             + jnp.einsum('bhcd,bhce->bhde', ki * jnp.exp(g_last[..., None] - gi)[..., None], v_new))
        return s, o

    xs = tuple(jnp.moveaxis(t, 2, 0) for t in (q, k, u, w, gc, attn))
    s_final, o = lax.scan(step, s0.astype(F32), xs)
    o = jnp.transpose(o, (1, 0, 3, 2, 4)).reshape(bsz, n * c, nh, dv)[:, :length]
    return o, s_final


def _ssd_chunked(x, dt, a, bm, cm, h0):
    bsz, length, nh, hp = x.shape
    c = min(SSM_CHUNK, length)
    n = -(-length // c)
    pad = n * c - length

    def prep(t):
        t = jnp.pad(t.astype(F32), [(0, 0), (0, pad)] + [(0, 0)] * (t.ndim - 2))
        return t.reshape((bsz, n, c) + t.shape[2:])

    xdt = prep(x * dt[..., None])
    ad = jnp.moveaxis(prep(dt * a), 3, 1)
    bm, cm = prep(bm), prep(cm)
    acs = jnp.cumsum(ad, -1)
    tril = jnp.tril(jnp.ones((c, c), bool))
    seg = acs[..., :, None] - acs[..., None, :]
    lmat = jnp.where(tril, jnp.exp(jnp.where(tril, seg, 0.0)), 0.0)
    cb = jnp.einsum('bnlhd,bnshd->bhnls', cm, bm) * lmat
    y_diag = jnp.einsum('bhnls,bnshp->bnlhp', cb, xdt)
    decay_states = jnp.exp(acs[..., -1:] - acs)
    states = jnp.einsum('bnlhd,bhnl,bnlhp->bnhpd', bm, decay_states, xdt)
    chunk_decay = jnp.exp(acs[..., -1])

    def step(h, inp):
        st, dec = inp
        return h * dec[..., None, None] + st, h

    h_final, h_in = lax.scan(step, h0.astype(F32),
                             (jnp.moveaxis(states, 1, 0), jnp.moveaxis(chunk_decay, 2, 0)))
    h_in = jnp.moveaxis(h_in, 0, 1)
    y_off = jnp.einsum('bnlhd,bnhpd,bhnl->bnlhp', cm, h_in, jnp.exp(acs))
    y = (y_diag + y_off).reshape(bsz, n * c, nh, hp)[:, :length]
    return y, h_final


def _moba_attend(q, k, v, q_pos0):
    bsz, lq, nh, hd = q.shape
    t_len = k.shape[1]
    nb = -(-t_len // MOBA_BLOCK)
    padk = nb * MOBA_BLOCK - t_len
    kb = jnp.pad(k, ((0, 0), (0, padk), (0, 0), (0, 0))).reshape(bsz, nb, MOBA_BLOCK, nh, hd).transpose(0, 3, 1, 2, 4)
    vb = jnp.pad(v, ((0, 0), (0, padk), (0, 0), (0, 0))).reshape(bsz, nb, MOBA_BLOCK, nh, hd).transpose(0, 3, 1, 2, 4)
    kmean = kb.astype(F32).mean(axis=3)
    qbs = min(MOBA_QUERY_BLOCK, lq)
    nq = -(-lq // qbs)
    qp = jnp.pad(q, ((0, 0), (0, nq * qbs - lq), (0, 0), (0, 0))).reshape(bsz, nq, qbs, nh, hd).transpose(1, 0, 3, 2, 4)
    qpos = q_pos0 + jnp.arange(nq * qbs).reshape(nq, qbs)
    topk = min(MOBA_TOPK, nb)
    bi = jnp.arange(bsz)[:, None, None, None]
    hi = jnp.arange(nh)[None, :, None, None]
    scale = hd ** -0.5

    def one_block(args):
        qblk, pos = args
        qf = qblk.astype(F32)
        own = pos // MOBA_BLOCK
        own_idx = jnp.minimum(own, nb - 1)
        gate = jnp.einsum('bhqd,bhnd->bhqn', qf, kmean)
        gate = jnp.where(jnp.arange(nb)[None, :] < own[:, None], gate, -jnp.inf)
        _, sel = lax.top_k(gate, topk)
        idx = jnp.concatenate([sel, jnp.broadcast_to(own_idx[None, None, :, None], (bsz, nh, qbs, 1))], -1)
        kg = kb[bi, hi, idx].astype(F32)
        vg = vb[bi, hi, idx].astype(F32)
        s = jnp.einsum('bhqd,bhqjkd->bhqjk', qf, kg) * scale
        sel_ok = jnp.arange(topk)[None, :] < own[:, None]
        own_ok = (own_idx[:, None] * MOBA_BLOCK + jnp.arange(MOBA_BLOCK)[None, :]) <= pos[:, None]
        mask = jnp.concatenate([jnp.broadcast_to(sel_ok[:, :, None], (qbs, topk, MOBA_BLOCK)), own_ok[:, None, :]], 1)
        s = jnp.where(mask, s, -jnp.inf)
        p = jax.nn.softmax(s.reshape(bsz, nh, qbs, -1), axis=-1).reshape(s.shape)
        return jnp.einsum('bhqjk,bhqjkd->bhqd', p, vg)

    out = lax.map(one_block, (qp, qpos))
    return out.transpose(1, 0, 3, 2, 4).reshape(bsz, nq * qbs, nh, hd)[:, :lq].astype(q.dtype)


def _stick_breaking_attend(q, k, v, q_pos0):
    bsz, lq, nh, hd = q.shape
    t_len = k.shape[1]
    qbs = min(QUERY_BLOCK, lq)
    nq = -(-lq // qbs)
    qp = jnp.pad(q, ((0, 0), (0, nq * qbs - lq), (0, 0), (0, 0))).reshape(bsz, nq, qbs, nh, hd).transpose(1, 0, 3, 2, 4)
    qpos = q_pos0 + jnp.arange(nq * qbs).reshape(nq, qbs)
    kf = jnp.swapaxes(k, 1, 2).astype(F32)
    vf = jnp.swapaxes(v, 1, 2).astype(F32)
    kpos = jnp.arange(t_len)
    scale = hd ** -0.5

    def one_block(args):
        qblk, pos = args
        z = jnp.einsum('bhqd,bhkd->bhqk', qblk.astype(F32), kf) * scale
        past = kpos[None, :] < pos[:, None]
        log_keep = jnp.where(past, jax.nn.log_sigmoid(-z), 0.0)
        log_after = lax.cumsum(log_keep, axis=3, reverse=True) - log_keep
        w = jnp.where(past, jnp.exp(jax.nn.log_sigmoid(z) + log_after), 0.0)
        return jnp.einsum('bhqk,bhkd->bhqd', w, vf)

    out = lax.map(one_block, (qp, qpos))
    return out.transpose(1, 0, 3, 2, 4).reshape(bsz, nq * qbs, nh, hd)[:, :lq].astype(q.dtype)


def _token_mixers(x, pos0, st, lw):
    gdn_conv0, gdn_s0, ssm_conv0, ssm_h0, past_kc, past_vc, past_ks, past_vs = st
    (w_in, gdn_conv_w, gdn_a_log, gdn_dt_bias, gdn_norm_w, ssm_conv_w, ssm_conv_b,
     ssm_a_log, ssm_dt_bias, ssm_d, ssm_norm_w, w_out) = lw
    bsz, length, _ = x.shape
    n_pad = _round_up(N_IN, LANE)
    w_in_p = jnp.pad(w_in, ((0, 0), (0, n_pad - N_IN))).astype(BF16)
    proj = _matmul(x.reshape(bsz * length, D_MODEL), w_in_p, 256)[:, :N_IN].reshape(bsz, length, N_IN)
    sizes = (3 * GROUP_WIDTH, GROUP_WIDTH, N_HEADS, N_HEADS, SSM_CONV_DIM, GROUP_WIDTH, N_HEADS,
             3 * GROUP_WIDTH, 3 * GROUP_WIDTH)
    offs, run = [], 0
    for sz in sizes[:-1]:
        run += sz
        offs.append(run)
    gdn_qkv, gdn_z, gdn_b, gdn_a, ssm_xbc, ssm_z, ssm_dt, moba_qkv, sb_qkv = jnp.split(proj, offs, axis=-1)

    qkv, gdn_conv_new = _causal_conv(gdn_qkv, gdn_conv0, gdn_conv_w)
    qkv = jax.nn.silu(qkv).reshape(bsz, length, 3, N_HEADS, HEAD_DIM)
    gq, gk, gv = _l2norm(qkv[:, :, 0]), _l2norm(qkv[:, :, 1]), qkv[:, :, 2]
    g = -jnp.exp(gdn_a_log.astype(F32)) * jax.nn.softplus(gdn_a.astype(F32) + gdn_dt_bias)
    beta = jax.nn.sigmoid(gdn_b.astype(F32))
    o, gdn_s_new = _gated_delta_chunked(gq, gk, gv, g, beta, gdn_s0)
    gz = gdn_z.reshape(bsz, length, N_HEADS, HEAD_DIM).astype(F32)
    gdn_out = (_rms(o) * gdn_norm_w * jax.nn.silu(gz)).reshape(bsz, length, GROUP_WIDTH).astype(x.dtype)

    xbc, ssm_conv_new = _causal_conv(ssm_xbc, ssm_conv0, ssm_conv_w, ssm_conv_b)
    xbc = jax.nn.silu(xbc)
    xs, bm, cm = jnp.split(xbc, [GROUP_WIDTH, GROUP_WIDTH + SSM_NGROUPS * SSM_STATE], axis=-1)
    hpg = N_HEADS // SSM_NGROUPS
    bm = jnp.repeat(bm.reshape(bsz, length, SSM_NGROUPS, SSM_STATE), hpg, axis=2)
    cm = jnp.repeat(cm.reshape(bsz, length, SSM_NGROUPS, SSM_STATE), hpg, axis=2)
    dt = jax.nn.softplus(ssm_dt.astype(F32) + ssm_dt_bias)
    xh = xs.reshape(bsz, length, N_HEADS, HEAD_DIM).astype(F32)
    y, ssm_h_new = _ssd_chunked(xh, dt, -jnp.exp(ssm_a_log.astype(F32)), bm, cm, ssm_h0)
    y = y + ssm_d[:, None] * xh
    yz = (y.reshape(bsz, length, GROUP_WIDTH) * jax.nn.silu(ssm_z.astype(F32))).reshape(bsz, length, SSM_NGROUPS, -1)
    ssm_out = (_rms(yz).reshape(bsz, length, GROUP_WIDTH) * ssm_norm_w).astype(x.dtype)

    pos = pos0 + jnp.arange(length)
    mqkv = moba_qkv.reshape(bsz, length, 3, N_HEADS, HEAD_DIM)
    mq, mk, mv = _rope(mqkv[:, :, 0], pos), _rope(mqkv[:, :, 1], pos), mqkv[:, :, 2]
    sqkv = sb_qkv.reshape(bsz, length, 3, N_HEADS, HEAD_DIM)
    sq, sk, sv = sqkv[:, :, 0], sqkv[:, :, 1], sqkv[:, :, 2]
    if past_kc.shape[1] == 0:
        moba_out = _head_major(_moba_attention(_head_major(mq), _head_major(mk), _head_major(mv)))
        sb_out = _head_major(_sb_attention(_head_major(sq), _head_major(sk), _head_major(sv)))
    else:
        mk_all = jnp.concatenate([past_kc.astype(x.dtype), mk], 1)
        mv_all = jnp.concatenate([past_vc.astype(x.dtype), mv], 1)
        moba_out = _moba_attend(mq, mk_all, mv_all, pos0)
        sk_all = jnp.concatenate([past_ks.astype(x.dtype), sk], 1)
        sv_all = jnp.concatenate([past_vs.astype(x.dtype), sv], 1)
        sb_out = _stick_breaking_attend(sq, sk_all, sv_all, pos0)
    moba_out = moba_out.reshape(bsz, length, GROUP_WIDTH)
    sb_out = sb_out.reshape(bsz, length, GROUP_WIDTH)

    cat = jnp.concatenate([gdn_out, ssm_out, moba_out, sb_out], -1)
    mix = _matmul(cat.reshape(bsz * length, MIX_WIDTH), w_out.astype(BF16), 256).reshape(bsz, length, D_MODEL)
    new_state = (mk, mv, sk, sv, gdn_conv_new, gdn_s_new.astype(x.dtype), ssm_conv_new, ssm_h_new.astype(x.dtype))
    return mix, new_state


def _moe_ffn(x, router_w, router_b, w_gu, b_gu, w_down, b_down):
    xt = x.reshape(-1, D_MODEL)
    n_tok = xt.shape[0]
    logits = (xt @ router_w + router_b).astype(F32)
    top_val, top_idx = lax.top_k(logits, TOP_K)
    gates = jax.nn.softmax(top_val, axis=-1)
    n_assign = n_tok * TOP_K
    blk = min(EXPERT_BLOCK, n_assign)
    n_blocks = -(-n_assign // blk) + N_EXPERTS
    flat_e = top_idx.reshape(-1)
    flat_tok = jnp.arange(n_assign, dtype=jnp.int32) // TOP_K
    order = jnp.argsort(flat_e)
    sorted_e = flat_e[order]
    counts = jnp.zeros((N_EXPERTS,), jnp.int32).at[flat_e].add(1)
    blocks_per_e = (counts + blk - 1) // blk
    blk_end = jnp.cumsum(blocks_per_e)
    blk_start = blk_end - blocks_per_e
    row_start = jnp.cumsum(counts) - counts
    rank = jnp.arange(n_assign, dtype=jnp.int32) - row_start[sorted_e]
    dest = blk_start[sorted_e] * blk + rank
    rows_tok = jnp.zeros((n_blocks * blk,), jnp.int32).at[dest].set(flat_tok[order])
    rows_gate = jnp.zeros((n_blocks * blk,), F32).at[dest].set(gates.reshape(-1)[order])
    block_e = jnp.minimum(jnp.searchsorted(blk_end, jnp.arange(n_blocks, dtype=jnp.int32), side='right'), N_EXPERTS - 1)
    xb = xt[rows_tok].reshape(n_blocks, blk, D_MODEL)

    def expert_block(args):
        xblk, e = args
        h = xblk @ w_gu[e] + b_gu[e]
        gate = jnp.minimum(h[:, :D_FF], SWIGLU_LIMIT)
        up = jnp.clip(h[:, D_FF:], -SWIGLU_LIMIT, SWIGLU_LIMIT)
        act = (up + 1.0) * gate * jax.nn.sigmoid(SWIGLU_ALPHA * gate)
        return act @ w_down[e] + b_down[e]

    yb = lax.map(expert_block, (xb, block_e))
    y = jax.ops.segment_sum(yb.reshape(-1, D_MODEL).astype(F32) * rows_gate[:, None], rows_tok, num_segments=n_tok)
    return y.astype(x.dtype).reshape(x.shape)


def _decoder_layer(x, pos0, st, lw, fw):
    ln1_g, ln1_b, router_w, router_b, w_gu, b_gu, w_down, b_down, ln2_g, ln2_b = fw
    mix, new_state = _token_mixers(x, pos0, st, lw)
    x = _layer_norm(DN_ALPHA * x + mix, ln1_g, ln1_b)
    x = _layer_norm(DN_ALPHA * x + _moe_ffn(x, router_w, router_b, w_gu, b_gu, w_down, b_down), ln2_g, ln2_b)
    return x, new_state


def kernel(x_prompt, x_sample, cache_moba_k, cache_moba_v, cache_sb_k, cache_sb_v,
           state_gdn_conv, state_gdn_rec, state_ssm_conv, state_ssm_rec, page_table,
           w_in, gdn_conv_w, gdn_a_log, gdn_dt_bias, gdn_norm_w,
           ssm_conv_w, ssm_conv_b, ssm_a_log, ssm_dt_bias, ssm_d, ssm_norm_w,
           w_out, ln1_g, ln1_b, router_w, router_b,
           expert_w_gu, expert_b_gu, expert_w_down, expert_b_down, ln2_g, ln2_b):
    bp = x_prompt.shape[0]
    bs, n_pages = page_table.shape
    past_len = n_pages * PAGE_SIZE
    dtype = x_prompt.dtype

    def gather_pages(pool):
        return pool[page_table].reshape(bs, past_len, N_HEADS, HEAD_DIM)

    empty = jnp.zeros((bp, 0, N_HEADS, HEAD_DIM), dtype)
    prompt_state = (jnp.zeros((bp, CONV_W - 1, GDN_CONV_DIM), dtype),
                    jnp.zeros((bp, N_HEADS, HEAD_DIM, HEAD_DIM), F32),
                    jnp.zeros((bp, CONV_W - 1, SSM_CONV_DIM), dtype),
                    jnp.zeros((bp, N_HEADS, HEAD_DIM, SSM_STATE), F32),
                    empty, empty, empty, empty)
    yp, ys = x_prompt, x_sample
    new_p, new_s = [], []
    for l in range(DEPTH):
        lw = (w_in[l], gdn_conv_w[l], gdn_a_log[l], gdn_dt_bias[l], gdn_norm_w[l], ssm_conv_w[l], ssm_conv_b[l],
              ssm_a_log[l], ssm_dt_bias[l], ssm_d[l], ssm_norm_w[l], w_out[l])
        fw = (ln1_g[l], ln1_b[l], router_w[l], router_b[l], expert_w_gu[l], expert_b_gu[l],
              expert_w_down[l], expert_b_down[l], ln2_g[l], ln2_b[l])
        sample_state = (state_gdn_conv[l], state_gdn_rec[l], state_ssm_conv[l], state_ssm_rec[l],
                        gather_pages(cache_moba_k[l]), gather_pages(cache_moba_v[l]),
                        gather_pages(cache_sb_k[l]), gather_pages(cache_sb_v[l]))
        yp, st_p = _decoder_layer(yp, 0, prompt_state, lw, fw)
        ys, st_s = _decoder_layer(ys, past_len, sample_state, lw, fw)
        new_p.append(st_p)
        new_s.append(st_s)

    def stk(states, i):
        return jnp.stack([s[i] for s in states])

    return (yp, ys,
            stk(new_p, 0), stk(new_p, 1), stk(new_p, 2), stk(new_p, 3),
            stk(new_p, 4), stk(new_p, 5), stk(new_p, 6), stk(new_p, 7),
            stk(new_s, 0), stk(new_s, 1), stk(new_s, 2), stk(new_s, 3),
            stk(new_s, 4), stk(new_s, 5), stk(new_s, 6), stk(new_s, 7))
```

```python
import functools

import jax
import jax.numpy as jnp
from jax import lax
from jax.experimental import pallas as pl
from jax.experimental.pallas import tpu as pltpu

D_MODEL = 1024
DEPTH = 2
PAGE_SIZE = 128
HEAD_DIM = 64
N_MIXERS = 4
GROUP_WIDTH = D_MODEL // N_MIXERS
N_HEADS = GROUP_WIDTH // HEAD_DIM
CONV_W = 4
CHUNK = 64
GDN_CONV_DIM = 3 * GROUP_WIDTH
SSM_STATE = 128
SSM_NGROUPS = 2
SSM_CONV_DIM = GROUP_WIDTH + 2 * SSM_NGROUPS * SSM_STATE
MOBA_BLOCK = 256
MOBA_TOPK = 3
MOBA_QUERY_BLOCK = 64
ROPE_THETA = 500000.0
ROPE_DIM = HEAD_DIM // 4
QUERY_BLOCK = 128
N_EXPERTS = 32
TOP_K = 4
D_FF = D_MODEL
SWIGLU_LIMIT = 7.0
SWIGLU_ALPHA = 1.702
EXPERT_BLOCK = 256
DN_ALPHA = (2 * DEPTH) ** 0.25
LN_EPS = 1e-5
RMS_EPS = 1e-6

F32 = jnp.float32
BF16 = jnp.bfloat16
I32 = jnp.int32
HI = lax.Precision.HIGHEST

LANE = 128
VMEM_LIMIT = 56 * 1024 * 1024
NEG = -1e30

_SEG_SIZES = (GDN_CONV_DIM, GROUP_WIDTH, N_HEADS, N_HEADS, SSM_CONV_DIM, GROUP_WIDTH, N_HEADS, GDN_CONV_DIM, GDN_CONV_DIM)
_SEG_NAMES = ("gdn_qkv", "gdn_z", "gdn_b", "gdn_a", "ssm_xbc", "ssm_z", "ssm_dt", "moba_qkv", "sb_qkv")
_OUT_SEGS = (("gdn_qkv", GDN_CONV_DIM), ("gdn_z", GROUP_WIDTH), ("ssm_xbc", SSM_CONV_DIM), ("ssm_z", GROUP_WIDTH),
             ("moba_qkv", GDN_CONV_DIM), ("sb_qkv", GDN_CONV_DIM), ("small", LANE))


def _softplus(z):
    return jnp.maximum(z, 0.0) + jnp.log1p(jnp.exp(-jnp.abs(z)))


def _sigmoid(x):
    return 1.0 / (1.0 + jnp.exp(-x))


def _silu(x):
    return x * _sigmoid(x)


def _dot(a, b):
    return jnp.dot(a.astype(BF16), b.astype(BF16), preferred_element_type=F32)


def _dot_nt(a, b):
    return lax.dot_general(a.astype(BF16), b.astype(BF16), (((1,), (1,)), ((), ())), preferred_element_type=F32)


def _dot_tn(a, b):
    return lax.dot_general(a.astype(BF16), b.astype(BF16), (((0,), (0,)), ((), ())), preferred_element_type=F32)


def _dot_hi(a, b):
    return jnp.dot(a, b, preferred_element_type=F32, precision=HI)


def _split_dot(x, m_bf16):
    hi = x.astype(BF16)
    lo = (x - hi.astype(F32)).astype(BF16)
    return (jnp.dot(hi, m_bf16, preferred_element_type=F32)
            + jnp.dot(lo, m_bf16, preferred_element_type=F32))


def _ln(h, g, b):
    mu = jnp.mean(h, axis=-1, keepdims=True)
    d = h - mu
    var = jnp.mean(d * d, axis=-1, keepdims=True)
    return d * lax.rsqrt(var + LN_EPS) * g + b


def _in_proj_kernel(x_ref, w_ref, *o_refs):
    y = jnp.dot(x_ref[...].astype(BF16), w_ref[...], preferred_element_type=F32)
    off = 0
    for (_, width), o_ref in zip(_OUT_SEGS, o_refs):
        o_ref[...] = y[:, off:off + width]
        off += width


def _in_proj(x, w_in):
    m = x.shape[0]
    tm = min(256, m)
    cols, off = {}, 0
    for name, sz in zip(_SEG_NAMES, _SEG_SIZES):
        cols[name] = w_in[:, off:off + sz]
        off += sz
    small = jnp.concatenate([cols["gdn_b"], cols["gdn_a"], cols["ssm_dt"],
                             jnp.zeros((D_MODEL, LANE - 3 * N_HEADS), w_in.dtype)], axis=1)
    cols["small"] = small
    w = jnp.concatenate([cols[name] for name, _ in _OUT_SEGS], axis=1).astype(BF16)
    n = w.shape[1]
    outs = pl.pallas_call(
        _in_proj_kernel,
        grid=(m // tm,),
        in_specs=[pl.BlockSpec((tm, D_MODEL), lambda i: (i, 0)),
                  pl.BlockSpec((D_MODEL, n), lambda i: (0, 0))],
        out_specs=[pl.BlockSpec((tm, width), lambda i: (i, 0)) for _, width in _OUT_SEGS],
        out_shape=[jax.ShapeDtypeStruct((m, width), F32) for _, width in _OUT_SEGS],
        compiler_params=pltpu.CompilerParams(dimension_semantics=("arbitrary",), vmem_limit_bytes=VMEM_LIMIT),
        name="in_proj",
    )(x, w)
    return {name: o for (name, _), o in zip(_OUT_SEGS, outs)}


def _sb_kernel(q_ref, k_ref, v_ref, o_ref, *, tq):
    i = pl.program_id(2)
    q = (q_ref[0, 0] * (HEAD_DIM ** -0.5)).astype(BF16)
    row = lax.broadcasted_iota(I32, (tq, tq), 0)
    col = lax.broadcasted_iota(I32, (tq, tq), 1)
    upper = (row > col).astype(BF16)
    past = col < row

    def block(j, carry, acc, diag):
        kj = k_ref[0, 0, pl.ds(j * tq, tq), :].astype(BF16)
        vj = v_ref[0, 0, pl.ds(j * tq, tq), :].astype(BF16)
        z = lax.dot_general(q, kj, (((1,), (1,)), ((), ())), preferred_element_type=F32)
        ls = -_softplus(z)
        lk = jnp.where(past, ls, 0.0) if diag else ls
        la = _split_dot(lk, upper)
        w = jnp.exp(z + ls + la + carry)
        if diag:
            w = jnp.where(past, w, 0.0)
        acc = acc + jnp.dot(w.astype(BF16), vj, preferred_element_type=F32)
        carry = carry + la[:, 0:1] + lk[:, 0:1]
        return carry, acc

    carry, acc = block(i, jnp.zeros((tq, 1), F32), jnp.zeros((tq, HEAD_DIM), F32), True)
    carry, acc = lax.fori_loop(0, i, lambda t, c: block(i - 1 - t, c[0], c[1], False), (carry, acc))
    o_ref[0, 0] = acc


def _sb_attention(q, k, v, tq=256):
    b, h, l, d = q.shape
    assert l % tq == 0
    return pl.pallas_call(
        functools.partial(_sb_kernel, tq=tq),
        grid=(b, h, l // tq),
        in_specs=[pl.BlockSpec((1, 1, tq, d), lambda bi, hi, i: (bi, hi, i, 0)),
                  pl.BlockSpec((1, 1, l, d), lambda bi, hi, i: (bi, hi, 0, 0)),
                  pl.BlockSpec((1, 1, l, d), lambda bi, hi, i: (bi, hi, 0, 0))],
        out_specs=pl.BlockSpec((1, 1, tq, d), lambda bi, hi, i: (bi, hi, i, 0)),
        out_shape=jax.ShapeDtypeStruct((b, h, l, d), F32),
        compiler_params=pltpu.CompilerParams(dimension_semantics=("arbitrary",) * 3, vmem_limit_bytes=VMEM_LIMIT),
        name="sb_attention",
    )(q, k, v)


def _moba_kernel(q_ref, k_ref, v_ref, o_ref, kmean_ref, *, nb):
    t = MOBA_BLOCK
    i = pl.program_id(2)

    @pl.when(i == 0)
    def _():
        for n in range(nb):
            kmean_ref[n:n + 1, :] = jnp.mean(k_ref[0, 0, n * t:(n + 1) * t, :], axis=0, keepdims=True)

    qf = q_ref[0, 0]
    q = (qf * (HEAD_DIM ** -0.5)).astype(BF16)
    gate = lax.dot_general(qf, kmean_ref[...], (((1,), (1,)), ((), ())),
                           preferred_element_type=F32, precision=HI)
    lane = lax.broadcasted_iota(I32, (t, nb), 1)
    cnt = jnp.zeros((t, nb), F32)
    for n2 in range(nb):
        cn = gate[:, n2:n2 + 1]
        beats = (cn > gate) | ((cn == gate) & (n2 < lane))
        cnt = cnt + jnp.where(beats & (n2 < i), 1.0, 0.0)
    sel = jnp.where((cnt < MOBA_TOPK) & (lane < i), 1.0, 0.0)

    row = lax.broadcasted_iota(I32, (t, t), 0)
    col = lax.broadcasted_iota(I32, (t, t), 1)

    ki = k_ref[0, 0, pl.ds(i * t, t), :].astype(BF16)
    vi = v_ref[0, 0, pl.ds(i * t, t), :].astype(BF16)
    s = lax.dot_general(q, ki, (((1,), (1,)), ((), ())), preferred_element_type=F32)
    s = jnp.where(col <= row, s, NEG)
    m = jnp.max(s, axis=-1, keepdims=True)
    p = jnp.exp(s - m)
    l = jnp.sum(p, axis=-1, keepdims=True)
    acc = jnp.dot(p.astype(BF16), vi, preferred_element_type=F32)

    def body(j, c):
        m, l, acc = c
        kj = k_ref[0, 0, pl.ds(j * t, t), :].astype(BF16)
        vj = v_ref[0, 0, pl.ds(j * t, t), :].astype(BF16)
        s = lax.dot_general(q, kj, (((1,), (1,)), ((), ())), preferred_element_type=F32)
        sj = jnp.sum(jnp.where(lane == j, sel, 0.0), axis=-1, keepdims=True)
        s = jnp.where(sj > 0.5, s, NEG)
        m_new = jnp.maximum(m, jnp.max(s, axis=-1, keepdims=True))
        a = jnp.exp(m - m_new)
        p = jnp.exp(s - m_new)
        l = a * l + jnp.sum(p, axis=-1, keepdims=True)
        acc = a * acc + jnp.dot(p.astype(BF16), vj, preferred_element_type=F32)
        return m_new, l, acc

    m, l, acc = lax.fori_loop(0, i, body, (m, l, acc))
    o_ref[0, 0] = acc / l


def _moba_attention(q, k, v):
    b, h, l, d = q.shape
    t = MOBA_BLOCK
    assert l % t == 0
    nb = l // t
    return pl.pallas_call(
        functools.partial(_moba_kernel, nb=nb),
        grid=(b, h, nb),
        in_specs=[pl.BlockSpec((1, 1, t, d), lambda bi, hi, i: (bi, hi, i, 0)),
                  pl.BlockSpec((1, 1, l, d), lambda bi, hi, i: (bi, hi, 0, 0)),
                  pl.BlockSpec((1, 1, l, d), lambda bi, hi, i: (bi, hi, 0, 0))],
        out_specs=pl.BlockSpec((1, 1, t, d), lambda bi, hi, i: (bi, hi, i, 0)),
        out_shape=jax.ShapeDtypeStruct((b, h, l, d), F32),
        scratch_shapes=[pltpu.VMEM((nb, d), F32)],
        compiler_params=pltpu.CompilerParams(dimension_semantics=("arbitrary",) * 3, vmem_limit_bytes=VMEM_LIMIT),
        name="moba_attention",
    )(q, k, v)


def _head_major(t):
    return jnp.transpose(t, (0, 2, 1, 3))


def _conv_tile(ext_ref, u, w_ref, tl):
    ext_ref[8:8 + tl, :] = u
    y = ext_ref[5:5 + tl, :] * w_ref[0:1, :]
    for i in range(1, CONV_W):
        y = y + ext_ref[5 + i:5 + i + tl, :] * w_ref[i:i + 1, :]
    tail = ext_ref[tl + 5:tl + 8, :]
    ext_ref[5:8, :] = tail
    return y, tail


def _gdn_kernel(qkv_ref, z_ref, small_ref, convw_ref, par_ref, normw_ref,
                o_ref, conv_out_ref, s_out_ref, ext_ref, s_ref, *, tl):
    c = CHUNK
    step = pl.program_id(1)

    @pl.when(step == 0)
    def _():
        ext_ref[...] = jnp.zeros_like(ext_ref)
        s_ref[...] = jnp.zeros_like(s_ref)

    y, tail = _conv_tile(ext_ref, qkv_ref[0], convw_ref, tl)
    conv_out_ref[0] = tail
    y = _silu(y)
    small = small_ref[0]
    beta_all = _sigmoid(small[:, 0:N_HEADS])
    g_all = par_ref[0:1, :] * _softplus(small[:, N_HEADS:2 * N_HEADS] + par_ref[1:2, :])

    row = lax.broadcasted_iota(I32, (c, c), 0)
    col = lax.broadcasted_iota(I32, (c, c), 1)
    tril = row >= col
    strict = row > col
    tril_f = tril.astype(F32)
    triu_f = (row <= col).astype(F32)
    eye = (row == col).astype(F32)
    ones = jnp.ones((c, c), F32)

    for ci in range(tl // c):
        r0 = ci * c
        g_c = g_all[r0:r0 + c, :]
        gc_all = _dot_hi(tril_f, g_c)
        for h in range(N_HEADS):
            q = y[r0:r0 + c, h * HEAD_DIM:(h + 1) * HEAD_DIM]
            k = y[r0:r0 + c, GROUP_WIDTH + h * HEAD_DIM:GROUP_WIDTH + (h + 1) * HEAD_DIM]
            v = y[r0:r0 + c, 2 * GROUP_WIDTH + h * HEAD_DIM:2 * GROUP_WIDTH + (h + 1) * HEAD_DIM]
            q = q * lax.rsqrt(jnp.sum(q * q, axis=-1, keepdims=True) + RMS_EPS) * (HEAD_DIM ** -0.5)
            k = k * lax.rsqrt(jnp.sum(k * k, axis=-1, keepdims=True) + RMS_EPS)
            beta = beta_all[r0:r0 + c, h:h + 1]
            gcol = gc_all[:, h:h + 1]
            grow = _dot_hi(ones, g_c[:, h:h + 1] * triu_f)
            decay = jnp.where(tril, jnp.exp(jnp.where(tril, gcol - grow, 0.0)), 0.0)
            kb = k * beta
            a = jnp.where(strict, _dot_nt(kb, k) * decay, 0.0)
            p = -a
            tinv = eye + p
            for _ in range(5):
                p = _dot_hi(p, p)
                tinv = tinv + _dot_hi(tinv, p)
            u = _dot(tinv, v * beta)
            w = _dot(tinv, kb * jnp.exp(gcol))
            attn = jnp.where(tril, _dot_nt(q, k) * decay, 0.0)
            s = s_ref[h]
            v_new = u - _dot(w, s)
            o = _dot(q * jnp.exp(gcol), s) + _dot(attn, v_new)
            g_last = gcol[c - 1:c, :]
            s_ref[h] = s * jnp.exp(g_last) + _dot_tn(k * jnp.exp(g_last - gcol), v_new)
            zz = z_ref[0, r0:r0 + c, h * HEAD_DIM:(h + 1) * HEAD_DIM]
            o = o * lax.rsqrt(jnp.mean(o * o, axis=-1, keepdims=True) + RMS_EPS) * normw_ref[...] * _silu(zz)
            o_ref[0, r0:r0 + c, h * HEAD_DIM:(h + 1) * HEAD_DIM] = o

    @pl.when(step == pl.num_programs(1) - 1)
    def _():
        s_out_ref[0] = s_ref[...]


def _gdn_prompt(qkv, z, small, conv_w, a_log, dt_bias, norm_w, tl=256):
    b, l, _ = qkv.shape
    assert l % tl == 0 and tl % CHUNK == 0
    par = jnp.stack([-jnp.exp(a_log.astype(F32)), dt_bias.astype(F32)])
    return pl.pallas_call(
        functools.partial(_gdn_kernel, tl=tl),
        grid=(b, l // tl),
        in_specs=[pl.BlockSpec((1, tl, GDN_CONV_DIM), lambda bi, i: (bi, i, 0)),
                  pl.BlockSpec((1, tl, GROUP_WIDTH), lambda bi, i: (bi, i, 0)),
                  pl.BlockSpec((1, tl, LANE), lambda bi, i: (bi, i, 0)),
                  pl.BlockSpec((CONV_W, GDN_CONV_DIM), lambda bi, i: (0, 0)),
                  pl.BlockSpec((2, N_HEADS), lambda bi, i: (0, 0)),
                  pl.BlockSpec((1, HEAD_DIM), lambda bi, i: (0, 0))],
        out_specs=[pl.BlockSpec((1, tl, GROUP_WIDTH), lambda bi, i: (bi, i, 0)),
                   pl.BlockSpec((1, CONV_W - 1, GDN_CONV_DIM), lambda bi, i: (bi, 0, 0)),
                   pl.BlockSpec((1, N_HEADS, HEAD_DIM, HEAD_DIM), lambda bi, i: (bi, 0, 0, 0))],
        out_shape=[jax.ShapeDtypeStruct((b, l, GROUP_WIDTH), F32),
                   jax.ShapeDtypeStruct((b, CONV_W - 1, GDN_CONV_DIM), F32),
                   jax.ShapeDtypeStruct((b, N_HEADS, HEAD_DIM, HEAD_DIM), F32)],
        scratch_shapes=[pltpu.VMEM((tl + 8, GDN_CONV_DIM), F32),
                        pltpu.VMEM((N_HEADS, HEAD_DIM, HEAD_DIM), F32)],
        compiler_params=pltpu.CompilerParams(dimension_semantics=("arbitrary", "arbitrary"),
                                             vmem_limit_bytes=VMEM_LIMIT),
        name="gdn_prompt",
    )(qkv, z, small, conv_w, par, norm_w.reshape(1, HEAD_DIM))


def _ssd_kernel(xbc_ref, z_ref, small_ref, convw_ref, convb_ref, par_ref, normw_ref,
                o_ref, conv_out_ref, h_out_ref, ext_ref, h_ref, *, tl):
    c = CHUNK
    step = pl.program_id(1)
    hpg = N_HEADS // SSM_NGROUPS

    @pl.when(step == 0)
    def _():
        ext_ref[...] = jnp.zeros_like(ext_ref)
        h_ref[...] = jnp.zeros_like(h_ref)

    y, tail = _conv_tile(ext_ref, xbc_ref[0], convw_ref, tl)
    conv_out_ref[0] = tail
    xbc = _silu(y + convb_ref[...])
    dt_all = _softplus(small_ref[0][:, 2 * N_HEADS:3 * N_HEADS] + par_ref[1:2, :])
    ad_all = dt_all * par_ref[0:1, :]

    row = lax.broadcasted_iota(I32, (c, c), 0)
    col = lax.broadcasted_iota(I32, (c, c), 1)
    tril = row >= col
    tril_f = tril.astype(F32)
    triu_f = (row <= col).astype(F32)
    ones = jnp.ones((c, c), F32)
    boff = GROUP_WIDTH
    coff = GROUP_WIDTH + SSM_NGROUPS * SSM_STATE

    for ci in range(tl // c):
        r0 = ci * c
        ad_c = ad_all[r0:r0 + c, :]
        acs_all = _dot_hi(tril_f, ad_c)
        ys = []
        for h in range(N_HEADS):
            grp = h // hpg
            x = xbc[r0:r0 + c, h * HEAD_DIM:(h + 1) * HEAD_DIM]
            bm = xbc[r0:r0 + c, boff + grp * SSM_STATE:boff + (grp + 1) * SSM_STATE]
            cm = xbc[r0:r0 + c, coff + grp * SSM_STATE:coff + (grp + 1) * SSM_STATE]
            xdt = x * dt_all[r0:r0 + c, h:h + 1]
            acol = acs_all[:, h:h + 1]
            arow = _dot_hi(ones, ad_c[:, h:h + 1] * triu_f)
            lmat = jnp.where(tril, jnp.exp(jnp.where(tril, acol - arow, 0.0)), 0.0)
            y_diag = _dot(_dot_nt(cm, bm) * lmat, xdt)
            a_last = acol[c - 1:c, :]
            states = _dot_tn(xdt * jnp.exp(a_last - acol), bm)
            hin = h_ref[h]
            y_off = _dot_nt(cm, hin) * jnp.exp(acol)
            h_ref[h] = hin * jnp.exp(a_last) + states
            yh = y_diag + y_off + par_ref[2:3, h:h + 1] * x
            ys.append(yh * _silu(z_ref[0, r0:r0 + c, h * HEAD_DIM:(h + 1) * HEAD_DIM]))
        for grp in range(SSM_NGROUPS):
            grp_ys = ys[grp * hpg:(grp + 1) * hpg]
            ms = sum(jnp.sum(a * a, axis=-1, keepdims=True) for a in grp_ys) / (hpg * HEAD_DIM)
            r = lax.rsqrt(ms + RMS_EPS)
            for j, a in enumerate(grp_ys):
                h = grp * hpg + j
                o_ref[0, r0:r0 + c, h * HEAD_DIM:(h + 1) * HEAD_DIM] = a * r * normw_ref[:, h * HEAD_DIM:(h + 1) * HEAD_DIM]

    @pl.when(step == pl.num_programs(1) - 1)
    def _():
        h_out_ref[0] = h_ref[...]


def _ssd_prompt(xbc, z, small, conv_w, conv_b, a_log, dt_bias, d_skip, norm_w, tl=256):
    b, l, _ = xbc.shape
    assert l % tl == 0 and tl % CHUNK == 0
    par = jnp.stack([-jnp.exp(a_log.astype(F32)), dt_bias.astype(F32), d_skip.astype(F32)])
    return pl.pallas_call(
        functools.partial(_ssd_kernel, tl=tl),
        grid=(b, l // tl),
        in_specs=[pl.BlockSpec((1, tl, SSM_CONV_DIM), lambda bi, i: (bi, i, 0)),
                  pl.BlockSpec((1, tl, GROUP_WIDTH), lambda bi, i: (bi, i, 0)),
                  pl.BlockSpec((1, tl, LANE), lambda bi, i: (bi, i, 0)),
                  pl.BlockSpec((CONV_W, SSM_CONV_DIM), lambda bi, i: (0, 0)),
                  pl.BlockSpec((1, SSM_CONV_DIM), lambda bi, i: (0, 0)),
                  pl.BlockSpec((3, N_HEADS), lambda bi, i: (0, 0)),
                  pl.BlockSpec((1, GROUP_WIDTH), lambda bi, i: (0, 0))],
        out_specs=[pl.BlockSpec((1, tl, GROUP_WIDTH), lambda bi, i: (bi, i, 0)),
                   pl.BlockSpec((1, CONV_W - 1, SSM_CONV_DIM), lambda bi, i: (bi, 0, 0)),
                   pl.BlockSpec((1, N_HEADS, HEAD_DIM, SSM_STATE), lambda bi, i: (bi, 0, 0, 0))],
        out_shape=[jax.ShapeDtypeStruct((b, l, GROUP_WIDTH), F32),
                   jax.ShapeDtypeStruct((b, CONV_W - 1, SSM_CONV_DIM), F32),
                   jax.ShapeDtypeStruct((b, N_HEADS, HEAD_DIM, SSM_STATE), F32)],
        scratch_shapes=[pltpu.VMEM((tl + 8, SSM_CONV_DIM), F32),
                        pltpu.VMEM((N_HEADS, HEAD_DIM, SSM_STATE), F32)],
        compiler_params=pltpu.CompilerParams(dimension_semantics=("arbitrary", "arbitrary"),
                                             vmem_limit_bytes=VMEM_LIMIT),
        name="ssd_prompt",
    )(xbc, z, small, conv_w, conv_b.reshape(1, SSM_CONV_DIM), par, norm_w.reshape(1, GROUP_WIDTH))


def _post_mix_kernel(o0_ref, o1_ref, o2_ref, o3_ref, x_ref, wout_ref, g_ref, b_ref, rw_ref, rb_ref,
                     xn_ref, xnb_ref, idx_ref, gate_ref, rank_ref, cum_ref, cnt_ref, *, tm):
    @pl.when(pl.program_id(0) == 0)
    def _():
        cnt_ref[...] = jnp.zeros_like(cnt_ref)

    mix = jnp.zeros((tm, D_MODEL), F32)
    for gi, o_ref in enumerate((o0_ref, o1_ref, o2_ref, o3_ref)):
        mix = mix + jnp.dot(o_ref[...].astype(BF16), wout_ref[gi * GROUP_WIDTH:(gi + 1) * GROUP_WIDTH, :],
                            preferred_element_type=F32)
    xn = _ln(DN_ALPHA * x_ref[...] + mix, g_ref[...], b_ref[...])
    xn_ref[...] = xn
    xnb_ref[...] = xn.astype(BF16)

    logits = _dot_hi(xn, rw_ref[...]) + rb_ref[...]
    lane_e = lax.broadcasted_iota(I32, (tm, N_EXPERTS), 1).astype(F32)
    lane = lax.broadcasted_iota(I32, (tm, LANE), 1)
    work = logits
    sel = jnp.zeros((tm, N_EXPERTS), F32)
    idx_out = jnp.zeros((tm, LANE), I32)
    val_out = jnp.zeros((tm, LANE), F32)
    hits = []
    for r in range(TOP_K):
        m = jnp.max(work, axis=-1, keepdims=True)
        pick = jnp.min(jnp.where(work == m, lane_e, float(N_EXPERTS)), axis=-1, keepdims=True)
        hit = lane_e == pick
        work = jnp.where(hit, -jnp.inf, work)
        sel = sel + hit.astype(F32)
        hits.append(hit)
        idx_out = jnp.where(lane == r, pick.astype(I32), idx_out)
        val_out = jnp.where(lane == r, m, val_out)
    ex = jnp.where(lane < TOP_K, jnp.exp(val_out - val_out[:, 0:1]), 0.0)
    gate_ref[...] = ex / jnp.sum(ex, axis=-1, keepdims=True)
    idx_ref[...] = idx_out

    row = lax.broadcasted_iota(I32, (tm, tm), 0)
    col = lax.broadcasted_iota(I32, (tm, tm), 1)
    before = jnp.dot((row > col).astype(BF16), sel.astype(BF16), preferred_element_type=F32) + cnt_ref[...]
    rank_out = jnp.zeros((tm, LANE), I32)
    for r in range(TOP_K):
        rk = jnp.sum(jnp.where(hits[r], before, 0.0), axis=-1, keepdims=True)
        rank_out = jnp.where(lane == r, rk.astype(I32), rank_out)
    rank_ref[...] = rank_out
    cum_ref[0] = jnp.broadcast_to(cnt_ref[...], (8, N_EXPERTS))
    cnt_ref[...] = cnt_ref[...] + jnp.sum(sel, axis=0, keepdims=True)


def _post_mix(outs, x, w_out_bf16, ln_g, ln_b, router_w, router_b, tm):
    n = x.shape[0]
    nt = n // tm
    tok = lambda w: pl.BlockSpec((tm, w), lambda i: (i, 0))
    full = lambda a: pl.BlockSpec(a.shape, lambda i: (0,) * a.ndim)
    g2, b2, rb2 = ln_g.reshape(1, -1), ln_b.reshape(1, -1), router_b.reshape(1, -1)
    return pl.pallas_call(
        functools.partial(_post_mix_kernel, tm=tm),
        grid=(nt,),
        in_specs=[tok(GROUP_WIDTH)] * 4 + [tok(D_MODEL), full(w_out_bf16), full(g2), full(b2), full(router_w), full(rb2)],
        out_specs=[tok(D_MODEL), tok(D_MODEL), tok(LANE), tok(LANE), tok(LANE),
                   pl.BlockSpec((1, 8, N_EXPERTS), lambda i: (i, 0, 0))],
        out_shape=[jax.ShapeDtypeStruct((n, D_MODEL), F32), jax.ShapeDtypeStruct((n, D_MODEL), BF16),
                   jax.ShapeDtypeStruct((n, LANE), I32), jax.ShapeDtypeStruct((n, LANE), F32),
                   jax.ShapeDtypeStruct((n, LANE), I32), jax.ShapeDtypeStruct((nt, 8, N_EXPERTS), F32)],
        scratch_shapes=[pltpu.VMEM((1, N_EXPERTS), F32)],
        compiler_params=pltpu.CompilerParams(dimension_semantics=("arbitrary",), vmem_limit_bytes=VMEM_LIMIT),
        name="post_mix",
    )(*outs, x, w_out_bf16, g2, b2, router_w, rb2)


def _moe_dispatch(idx, gates, rank, cum, n_tok, tm, blk):
    nt = n_tok // tm
    n_blocks = -(-n_tok * TOP_K // blk) + N_EXPERTS
    e_idx = idx[:, :TOP_K]
    cum_t = cum[:, 0, :].astype(I32)
    onehot = e_idx[:, :, None] == jnp.arange(N_EXPERTS, dtype=I32)
    counts = cum_t[nt - 1] + jnp.sum(onehot[n_tok - tm:].astype(I32), axis=(0, 1))
    cum_all = jnp.concatenate([cum_t, counts[None]], 0)
    blocks_per_e = (counts + blk - 1) // blk
    blk_start = jnp.cumsum(blocks_per_e) - blocks_per_e
    dest = jnp.sum(jnp.where(onehot, blk_start, 0), axis=-1) * blk + rank[:, :TOP_K]
    dest_pad = jnp.pad(dest, ((0, 0), (0, LANE - TOP_K)), constant_values=-1)
    dest_t = jnp.pad(dest.T, ((0, 8 - TOP_K), (0, 0)), constant_values=-1)
    gate_t = jnp.pad(gates[:, :TOP_K].T, ((0, 8 - TOP_K), (0, 0)))

    lo, hi = cum_all[:-1].T, cum_all[1:].T
    has = hi > lo
    j0 = lo // blk
    j1 = jnp.maximum(hi - 1, 0) // blk
    jj = jnp.stack([j0, j1], -1)
    valid = jnp.stack([has, has & (j1 > j0)], -1)
    blk_id = blk_start[:, None, None] + jj
    e_b = jnp.broadcast_to(jnp.arange(N_EXPERTS, dtype=I32)[:, None, None], jj.shape)
    t_b = jnp.broadcast_to(jnp.arange(nt, dtype=I32)[None, :, None], jj.shape)
    n_steps = min(N_EXPERTS * nt + n_blocks, 2 * N_EXPERTS * nt)
    n_valid = jnp.sum(valid)
    pos = jnp.arange(n_steps)
    v = pos < n_valid

    def build(key):
        order = jnp.argsort(jnp.where(valid, key, jnp.int32(1 << 30)).reshape(-1))[:n_steps]
        src = order[jnp.minimum(pos, n_valid - 1)]
        return blk_id.reshape(-1)[src], t_b.reshape(-1)[src], e_b.reshape(-1)[src]

    def edges(a):
        change = a[1:] != a[:-1]
        first = jnp.concatenate([jnp.ones((1,), bool), change]) & v
        last = (jnp.concatenate([change, jnp.ones((1,), bool)]) | (pos == n_valid - 1)) & v
        return first, last

    b1, t1, e1 = build(blk_id * nt + t_b)
    first_b, last_b = edges(b1)
    first_e, _ = edges(e1)
    flags1 = first_b.astype(I32) + 2 * last_b.astype(I32) + 4 * v.astype(I32) + 8 * first_e.astype(I32)
    b2, t2, _ = build(t_b * (2 * n_blocks) + blk_id)
    first_t, last_t = edges(t2)
    flags2 = first_t.astype(I32) + 2 * last_t.astype(I32) + 4 * v.astype(I32)
    return dict(dest=dest_pad, dest_t=dest_t, gate_t=gate_t, n_rows=n_blocks * blk, n_steps=n_steps,
                gather=(b1.astype(I32), t1.astype(I32), e1.astype(I32), flags1),
                combine=(b2.astype(I32), t2.astype(I32), flags2))


def _expert_kernel(blk_s, tile_s, exp_s, flag_s, x_ref, dest_ref, gate_ref, wgu_ref, bgu_ref, wdn_ref, bdn_ref,
                   yb_ref, xacc_ref, gacc_ref, wgu_bf, wdn_bf, *, blk, tm):
    s = pl.program_id(0)
    flags = flag_s[s]

    @pl.when((flags & 8) != 0)
    def _():
        wgu_bf[...] = wgu_ref[0].astype(BF16)
        wdn_bf[...] = wdn_ref[0].astype(BF16)

    @pl.when((flags & 1) != 0)
    def _():
        xacc_ref[...] = jnp.zeros_like(xacc_ref)
        gacc_ref[...] = jnp.zeros_like(gacc_ref)

    @pl.when((flags & 4) != 0)
    def _():
        rowid = blk_s[s] * blk + lax.broadcasted_iota(I32, (blk, tm), 0)
        d = dest_ref[...]
        g = gate_ref[...]
        p = jnp.zeros((blk, tm), F32)
        pg = jnp.zeros((blk, tm), F32)
        for k in range(TOP_K):
            hit = d[k:k + 1, :] == rowid
            p = jnp.where(hit, 1.0, p)
            pg = jnp.where(hit, g[k:k + 1, :], pg)
        xacc_ref[...] += jnp.dot(p.astype(BF16), x_ref[...], preferred_element_type=F32)
        gacc_ref[...] += jnp.sum(pg, axis=-1, keepdims=True)

    @pl.when((flags & 2) != 0)
    def _():
        h = jnp.dot(xacc_ref[...].astype(BF16), wgu_bf[...], preferred_element_type=F32) + bgu_ref[0]
        gate = jnp.minimum(h[:, :D_FF], SWIGLU_LIMIT)
        up = jnp.clip(h[:, D_FF:], -SWIGLU_LIMIT, SWIGLU_LIMIT)
        act = (up + 1.0) * gate * _sigmoid(SWIGLU_ALPHA * gate)
        y = jnp.dot(act.astype(BF16), wdn_bf[...], preferred_element_type=F32) + bdn_ref[0]
        yb_ref[...] = (y * gacc_ref[...]).astype(yb_ref.dtype)


def _expert_ffn(xn_bf16, disp, w_gu, b_gu, w_down, b_down, tm, blk):
    blk_s, tile_s, exp_s, flag_s = disp["gather"]
    grid_spec = pltpu.PrefetchScalarGridSpec(
        num_scalar_prefetch=4,
        grid=(disp["n_steps"],),
        in_specs=[pl.BlockSpec((tm, D_MODEL), lambda s, b, t, e, f: (t[s], 0)),
                  pl.BlockSpec((8, tm), lambda s, b, t, e, f: (0, t[s])),
                  pl.BlockSpec((8, tm), lambda s, b, t, e, f: (0, t[s])),
                  pl.BlockSpec((1, D_MODEL, 2 * D_FF), lambda s, b, t, e, f: (e[s], 0, 0)),
                  pl.BlockSpec((1, 1, 2 * D_FF), lambda s, b, t, e, f: (e[s], 0, 0)),
                  pl.BlockSpec((1, D_FF, D_MODEL), lambda s, b, t, e, f: (e[s], 0, 0)),
                  pl.BlockSpec((1, 1, D_MODEL), lambda s, b, t, e, f: (e[s], 0, 0))],
        out_specs=pl.BlockSpec((blk, D_MODEL), lambda s, b, t, e, f: (b[s], 0)),
        scratch_shapes=[pltpu.VMEM((blk, D_MODEL), F32), pltpu.VMEM((blk, 1), F32),
                        pltpu.VMEM((D_MODEL, 2 * D_FF), BF16), pltpu.VMEM((D_FF, D_MODEL), BF16)])
    return pl.pallas_call(
        functools.partial(_expert_kernel, blk=blk, tm=tm),
        grid_spec=grid_spec,
        out_shape=jax.ShapeDtypeStruct((disp["n_rows"], D_MODEL), BF16),
        compiler_params=pltpu.CompilerParams(dimension_semantics=("arbitrary",), vmem_limit_bytes=VMEM_LIMIT),
        name="expert_ffn",
    )(blk_s, tile_s, exp_s, flag_s, xn_bf16, disp["dest_t"], disp["gate_t"],
      w_gu, b_gu.reshape(N_EXPERTS, 1, 2 * D_FF), w_down, b_down.reshape(N_EXPERTS, 1, D_MODEL))


def _combine_kernel(blk_s, tile_s, flag_s, yb_ref, dest_ref, xn_ref, g_ref, b_ref, y_ref, acc_ref, *, blk, tm):
    s = pl.program_id(0)
    flags = flag_s[s]

    @pl.when((flags & 1) != 0)
    def _():
        acc_ref[...] = jnp.zeros_like(acc_ref)

    @pl.when((flags & 4) != 0)
    def _():
        rowid = blk_s[s] * blk + lax.broadcasted_iota(I32, (tm, blk), 1)
        d = dest_ref[...]
        p = jnp.zeros((tm, blk), F32)
        for k in range(TOP_K):
            p = jnp.where(d[:, k:k + 1] == rowid, 1.0, p)
        acc_ref[...] += jnp.dot(p.astype(BF16), yb_ref[...], preferred_element_type=F32)

    @pl.when((flags & 2) != 0)
    def _():
        y_ref[...] = _ln(DN_ALPHA * xn_ref[...] + acc_ref[...], g_ref[...], b_ref[...])


def _moe_combine(yb, disp, xn, ln_g, ln_b, tm, blk):
    blk_s, tile_s, flag_s = disp["combine"]
    n = xn.shape[0]
    grid_spec = pltpu.PrefetchScalarGridSpec(
        num_scalar_prefetch=3,
        grid=(disp["n_steps"],),
        in_specs=[pl.BlockSpec((blk, D_MODEL), lambda s, b, t, f: (b[s], 0)),
                  pl.BlockSpec((tm, LANE), lambda s, b, t, f: (t[s], 0)),
                  pl.BlockSpec((tm, D_MODEL), lambda s, b, t, f: (t[s], 0)),
                  pl.BlockSpec((1, D_MODEL), lambda s, b, t, f: (0, 0)),
                  pl.BlockSpec((1, D_MODEL), lambda s, b, t, f: (0, 0))],
        out_specs=pl.BlockSpec((tm, D_MODEL), lambda s, b, t, f: (t[s], 0)),
        scratch_shapes=[pltpu.VMEM((tm, D_MODEL), F32)])
    return pl.pallas_call(
        functools.partial(_combine_kernel, blk=blk, tm=tm),
        grid_spec=grid_spec,
        out_shape=jax.ShapeDtypeStruct((n, D_MODEL), F32),
        compiler_params=pltpu.CompilerParams(dimension_semantics=("arbitrary",), vmem_limit_bytes=VMEM_LIMIT),
        name="moe_combine",
    )(blk_s, tile_s, flag_s, yb, disp["dest"], xn, ln_g.reshape(1, -1), ln_b.reshape(1, -1))


def _post_mix_moe(outs, x, w_out, fw):
    ln1_g, ln1_b, router_w, router_b, w_gu, b_gu, w_down, b_down, ln2_g, ln2_b = fw
    n = x.shape[0]
    tm = min(256, n)
    blk = min(EXPERT_BLOCK, n * TOP_K)
    xn, xnb, idx, gates, rank, cum = _post_mix(outs, x, w_out.astype(BF16), ln1_g, ln1_b, router_w, router_b, tm)
    disp = _moe_dispatch(idx, gates, rank, cum, n, tm, blk)
    yb = _expert_ffn(xnb, disp, w_gu, b_gu, w_down, b_down, tm, blk)
    return _moe_combine(yb, disp, xn, ln2_g, ln2_b, tm, blk)


def _rms(x):
    xf = x.astype(F32)
    return xf * lax.rsqrt(jnp.mean(xf * xf, -1, keepdims=True) + RMS_EPS)


def _l2norm(x):
    xf = x.astype(F32)
    return xf * lax.rsqrt(jnp.sum(xf * xf, -1, keepdims=True) + RMS_EPS)


def _causal_conv(u, buf, w, b=None):
    length = u.shape[1]
    ext = jnp.concatenate([buf.astype(u.dtype), u], axis=1)
    y = sum(ext[:, i:i + length] * w[i] for i in range(CONV_W))
    if b is not None:
        y = y + b
    return y, ext[:, ext.shape[1] - (CONV_W - 1):]


def _rope(x, pos):
    half = ROPE_DIM // 2
    inv = ROPE_THETA ** (-jnp.arange(half, dtype=F32) * 2.0 / ROPE_DIM)
    ang = pos.astype(F32)[:, None] * inv
    cos = jnp.cos(ang)[None, :, None, :]
    sin = jnp.sin(ang)[None, :, None, :]
    xr = x[..., :ROPE_DIM].astype(F32)
    x1, x2 = xr[..., :half], xr[..., half:]
    rot = jnp.concatenate([x1 * cos - x2 * sin, x2 * cos + x1 * sin], -1)
    return jnp.concatenate([rot.astype(x.dtype), x[..., ROPE_DIM:]], -1)


def _gated_delta_step(q, k, v, g, beta, s0):
    q = q * HEAD_DIM ** -0.5
    v_new = v * beta[..., None] - jnp.einsum('bhd,bhde->bhe', k * (beta * jnp.exp(g))[..., None], s0)
    o = jnp.einsum('bhd,bhde->bhe', q * jnp.exp(g)[..., None], s0) + jnp.sum(q * k, -1, keepdims=True) * v_new
    s = s0 * jnp.exp(g)[..., None, None] + k[..., :, None] * v_new[..., None, :]
    return o, s


def _ssd_step(x, dt, a, bm, cm, h0):
    h = h0 * jnp.exp(dt * a)[..., None, None] + (x * dt[..., None])[..., :, None] * bm[..., None, :]
    y = jnp.einsum('bhpn,bhn->bhp', h, cm)
    return y, h


def _moba_attend(q, k, v, q_pos0):
    bsz, lq, nh, hd = q.shape
    t_len = k.shape[1]
    nb = -(-t_len // MOBA_BLOCK)
    padk = nb * MOBA_BLOCK - t_len
    kb = jnp.pad(k, ((0, 0), (0, padk), (0, 0), (0, 0))).reshape(bsz, nb, MOBA_BLOCK, nh, hd).transpose(0, 3, 1, 2, 4)
    vb = jnp.pad(v, ((0, 0), (0, padk), (0, 0), (0, 0))).reshape(bsz, nb, MOBA_BLOCK, nh, hd).transpose(0, 3, 1, 2, 4)
    kmean = kb.astype(F32).mean(axis=3)
    qbs = min(MOBA_QUERY_BLOCK, lq)
    nq = -(-lq // qbs)
    qp = jnp.pad(q, ((0, 0), (0, nq * qbs - lq), (0, 0), (0, 0))).reshape(bsz, nq, qbs, nh, hd).transpose(1, 0, 3, 2, 4)
    qpos = q_pos0 + jnp.arange(nq * qbs).reshape(nq, qbs)
    topk = min(MOBA_TOPK, nb)
    bi = jnp.arange(bsz)[:, None, None, None]
    hi = jnp.arange(nh)[None, :, None, None]
    scale = hd ** -0.5

    def one_block(args):
        qblk, pos = args
        qf = qblk.astype(F32)
        own = pos // MOBA_BLOCK
        own_idx = jnp.minimum(own, nb - 1)
        gate = jnp.einsum('bhqd,bhnd->bhqn', qf, kmean)
        gate = jnp.where(jnp.arange(nb)[None, :] < own[:, None], gate, -jnp.inf)
        _, sel = lax.top_k(gate, topk)
        idx = jnp.concatenate([sel, jnp.broadcast_to(own_idx[None, None, :, None], (bsz, nh, qbs, 1))], -1)
        kg = kb[bi, hi, idx].astype(F32)
        vg = vb[bi, hi, idx].astype(F32)
        s = jnp.einsum('bhqd,bhqjkd->bhqjk', qf, kg) * scale
        sel_ok = jnp.arange(topk)[None, :] < own[:, None]
        own_ok = (own_idx[:, None] * MOBA_BLOCK + jnp.arange(MOBA_BLOCK)[None, :]) <= pos[:, None]
        mask = jnp.concatenate([jnp.broadcast_to(sel_ok[:, :, None], (qbs, topk, MOBA_BLOCK)), own_ok[:, None, :]], 1)
        s = jnp.where(mask, s, -jnp.inf)
        p = jax.nn.softmax(s.reshape(bsz, nh, qbs, -1), axis=-1).reshape(s.shape)
        return jnp.einsum('bhqjk,bhqjkd->bhqd', p, vg)

    out = lax.map(one_block, (qp, qpos))
    return out.transpose(1, 0, 3, 2, 4).reshape(bsz, nq * qbs, nh, hd)[:, :lq].astype(q.dtype)


def _stick_breaking_attend(q, k, v, q_pos0):
    bsz, lq, nh, hd = q.shape
    t_len = k.shape[1]
    qbs = min(QUERY_BLOCK, lq)
    nq = -(-lq // qbs)
    qp = jnp.pad(q, ((0, 0), (0, nq * qbs - lq), (0, 0), (0, 0))).reshape(bsz, nq, qbs, nh, hd).transpose(1, 0, 3, 2, 4)
    qpos = q_pos0 + jnp.arange(nq * qbs).reshape(nq, qbs)
    kf = jnp.swapaxes(k, 1, 2).astype(F32)
    vf = jnp.swapaxes(v, 1, 2).astype(F32)
    kpos = jnp.arange(t_len)
    scale = hd ** -0.5

    def one_block(args):
        qblk, pos = args
        z = jnp.einsum('bhqd,bhkd->bhqk', qblk.astype(F32), kf) * scale
        past = kpos[None, :] < pos[:, None]
        log_keep = jnp.where(past, jax.nn.log_sigmoid(-z), 0.0)
        log_after = lax.cumsum(log_keep, axis=3, reverse=True) - log_keep
        w = jnp.where(past, jnp.exp(jax.nn.log_sigmoid(z) + log_after), 0.0)
        return jnp.einsum('bhqk,bhkd->bhqd', w, vf)

    out = lax.map(one_block, (qp, qpos))
    return out.transpose(1, 0, 3, 2, 4).reshape(bsz, nq * qbs, nh, hd)[:, :lq].astype(q.dtype)


def _split_heads3(t, bsz, length):
    t = t.reshape(bsz, length, 3, N_HEADS, HEAD_DIM)
    return t[:, :, 0], t[:, :, 1], t[:, :, 2]


def _prompt_mixers(x, lw):
    (w_in, gdn_conv_w, gdn_a_log, gdn_dt_bias, gdn_norm_w, ssm_conv_w, ssm_conv_b,
     ssm_a_log, ssm_dt_bias, ssm_d, ssm_norm_w, _) = lw
    bsz, length, _ = x.shape
    seg = _in_proj(x.reshape(bsz * length, D_MODEL), w_in)
    seg = {k: v.reshape(bsz, length, -1) for k, v in seg.items()}

    gdn_out, gdn_conv_new, gdn_s_new = _gdn_prompt(seg["gdn_qkv"], seg["gdn_z"], seg["small"],
                                                   gdn_conv_w, gdn_a_log, gdn_dt_bias, gdn_norm_w)
    ssm_out, ssm_conv_new, ssm_h_new = _ssd_prompt(seg["ssm_xbc"], seg["ssm_z"], seg["small"], ssm_conv_w, ssm_conv_b,
                                                   ssm_a_log, ssm_dt_bias, ssm_d, ssm_norm_w)

    pos = jnp.arange(length)
    mq, mk, mv = _split_heads3(seg["moba_qkv"], bsz, length)
    mq, mk = _rope(mq, pos), _rope(mk, pos)
    moba_out = _head_major(_moba_attention(_head_major(mq), _head_major(mk), _head_major(mv)))
    sq, sk, sv = _split_heads3(seg["sb_qkv"], bsz, length)
    sb_out = _head_major(_sb_attention(_head_major(sq), _head_major(sk), _head_major(sv)))

    n = bsz * length
    outs = [gdn_out.reshape(n, GROUP_WIDTH), ssm_out.reshape(n, GROUP_WIDTH),
            moba_out.reshape(n, GROUP_WIDTH), sb_out.reshape(n, GROUP_WIDTH)]
    return outs, (mk, mv, sk, sv, gdn_conv_new, gdn_s_new, ssm_conv_new, ssm_h_new)


def _sample_mixers(x, pos0, st, lw):
    gdn_conv0, gdn_s0, ssm_conv0, ssm_h0, past_kc, past_vc, past_ks, past_vs = st
    (w_in, gdn_conv_w, gdn_a_log, gdn_dt_bias, gdn_norm_w, ssm_conv_w, ssm_conv_b,
     ssm_a_log, ssm_dt_bias, ssm_d, ssm_norm_w, _) = lw
    bsz, length, _ = x.shape
    assert length == 1
    seg = _in_proj(x.reshape(bsz, D_MODEL), w_in)
    small = seg["small"]
    gdn_b, gdn_a, ssm_dt = small[:, 0:N_HEADS], small[:, N_HEADS:2 * N_HEADS], small[:, 2 * N_HEADS:3 * N_HEADS]

    qkv, gdn_conv_new = _causal_conv(seg["gdn_qkv"][:, None], gdn_conv0, gdn_conv_w)
    qkv = jax.nn.silu(qkv).reshape(bsz, 3, N_HEADS, HEAD_DIM)
    gq, gk, gv = _l2norm(qkv[:, 0]), _l2norm(qkv[:, 1]), qkv[:, 2]
    g = -jnp.exp(gdn_a_log.astype(F32)) * jax.nn.softplus(gdn_a + gdn_dt_bias)
    o, gdn_s_new = _gated_delta_step(gq, gk, gv, g, jax.nn.sigmoid(gdn_b), gdn_s0.astype(F32))
    gz = seg["gdn_z"].reshape(bsz, N_HEADS, HEAD_DIM)
    gdn_out = (_rms(o) * gdn_norm_w * jax.nn.silu(gz)).reshape(bsz, GROUP_WIDTH)

    xbc, ssm_conv_new = _causal_conv(seg["ssm_xbc"][:, None], ssm_conv0, ssm_conv_w, ssm_conv_b)
    xbc = jax.nn.silu(xbc)[:, 0]
    xs, bm, cm = jnp.split(xbc, [GROUP_WIDTH, GROUP_WIDTH + SSM_NGROUPS * SSM_STATE], axis=-1)
    hpg = N_HEADS // SSM_NGROUPS
    bm = jnp.repeat(bm.reshape(bsz, SSM_NGROUPS, SSM_STATE), hpg, axis=1)
    cm = jnp.repeat(cm.reshape(bsz, SSM_NGROUPS, SSM_STATE), hpg, axis=1)
    dt = jax.nn.softplus(ssm_dt + ssm_dt_bias)
    xh = xs.reshape(bsz, N_HEADS, HEAD_DIM)
    y, ssm_h_new = _ssd_step(xh, dt, -jnp.exp(ssm_a_log.astype(F32)), bm, cm, ssm_h0.astype(F32))
    y = y + ssm_d[:, None] * xh
    yz = (y.reshape(bsz, GROUP_WIDTH) * jax.nn.silu(seg["ssm_z"])).reshape(bsz, SSM_NGROUPS, -1)
    ssm_out = _rms(yz).reshape(bsz, GROUP_WIDTH) * ssm_norm_w

    pos = pos0 + jnp.arange(length)
    mq, mk, mv = _split_heads3(seg["moba_qkv"], bsz, length)
    mq, mk = _rope(mq, pos), _rope(mk, pos)
    moba_out = _moba_attend(mq, jnp.concatenate([past_kc, mk], 1), jnp.concatenate([past_vc, mv], 1), pos0)
    sq, sk, sv = _split_heads3(seg["sb_qkv"], bsz, length)
    sb_out = _stick_breaking_attend(sq, jnp.concatenate([past_ks, sk], 1), jnp.concatenate([past_vs, sv], 1), pos0)

    outs = [gdn_out, ssm_out, moba_out.reshape(bsz, GROUP_WIDTH), sb_out.reshape(bsz, GROUP_WIDTH)]
    return outs, (mk, mv, sk, sv, gdn_conv_new, gdn_s_new, ssm_conv_new, ssm_h_new)


def kernel(x_prompt, x_sample, cache_moba_k, cache_moba_v, cache_sb_k, cache_sb_v,
           state_gdn_conv, state_gdn_rec, state_ssm_conv, state_ssm_rec, page_table,
           w_in, gdn_conv_w, gdn_a_log, gdn_dt_bias, gdn_norm_w,
           ssm_conv_w, ssm_conv_b, ssm_a_log, ssm_dt_bias, ssm_d, ssm_norm_w,
           w_out, ln1_g, ln1_b, router_w, router_b,
           expert_w_gu, expert_b_gu, expert_w_down, expert_b_down, ln2_g, ln2_b):
    bs, n_pages = page_table.shape
    past_len = n_pages * PAGE_SIZE

    def gather_pages(pool):
        return pool[page_table].reshape(bs, past_len, N_HEADS, HEAD_DIM)

    yp, ys = x_prompt, x_sample
    new_p, new_s = [], []
    for l in range(DEPTH):
        lw = (w_in[l], gdn_conv_w[l], gdn_a_log[l], gdn_dt_bias[l], gdn_norm_w[l], ssm_conv_w[l], ssm_conv_b[l],
              ssm_a_log[l], ssm_dt_bias[l], ssm_d[l], ssm_norm_w[l], w_out[l])
        fw = (ln1_g[l], ln1_b[l], router_w[l], router_b[l], expert_w_gu[l], expert_b_gu[l],
              expert_w_down[l], expert_b_down[l], ln2_g[l], ln2_b[l])
        sample_state = (state_gdn_conv[l], state_gdn_rec[l], state_ssm_conv[l], state_ssm_rec[l],
                        gather_pages(cache_moba_k[l]), gather_pages(cache_moba_v[l]),
                        gather_pages(cache_sb_k[l]), gather_pages(cache_sb_v[l]))
        outs_p, st_p = _prompt_mixers(yp, lw)
        yp = _post_mix_moe(outs_p, yp.reshape(-1, D_MODEL), w_out[l], fw).reshape(yp.shape)
        outs_s, st_s = _sample_mixers(ys, past_len, sample_state, lw)
        ys = _post_mix_moe(outs_s, ys.reshape(-1, D_MODEL), w_out[l], fw).reshape(ys.shape)
        new_p.append(st_p)
        new_s.append(st_s)

    def stk(states, i):
        return jnp.stack([s[i] for s in states])

    return (yp, ys,
            stk(new_p, 0), stk(new_p, 1), stk(new_p, 2), stk(new_p, 3),
            stk(new_p, 4), stk(new_p, 5), stk(new_p, 6), stk(new_p, 7),
            stk(new_s, 0), stk(new_s, 1), stk(new_s, 2), stk(new_s, 3),
            stk(new_s, 4), stk(new_s, 5), stk(new_s, 6), stk(new_s, 7))
```

```python
import functools

import jax
import jax.numpy as jnp
from jax import lax
from jax.experimental import pallas as pl
from jax.experimental.pallas import tpu as pltpu

D_MODEL = 1024
DEPTH = 2
PAGE_SIZE = 128
HEAD_DIM = 64
N_MIXERS = 4
GROUP_WIDTH = D_MODEL // N_MIXERS
N_HEADS = GROUP_WIDTH // HEAD_DIM
CONV_W = 4
CHUNK = 64
GDN_CONV_DIM = 3 * GROUP_WIDTH
SSM_STATE = 128
SSM_NGROUPS = 2
SSM_CONV_DIM = GROUP_WIDTH + 2 * SSM_NGROUPS * SSM_STATE
MOBA_BLOCK = 256
MOBA_TOPK = 3
MOBA_QUERY_BLOCK = 64
ROPE_THETA = 500000.0
ROPE_DIM = HEAD_DIM // 4
QUERY_BLOCK = 128
N_EXPERTS = 32
TOP_K = 4
D_FF = D_MODEL
SWIGLU_LIMIT = 7.0
SWIGLU_ALPHA = 1.702
EXPERT_BLOCK = 256
DN_ALPHA = (2 * DEPTH) ** 0.25
LN_EPS = 1e-5
RMS_EPS = 1e-6

F32 = jnp.float32
BF16 = jnp.bfloat16
I32 = jnp.int32
HI = lax.Precision.HIGHEST

LANE = 128
VMEM_LIMIT = 56 * 1024 * 1024
NEG = -1e30

_SEG_SIZES = (GDN_CONV_DIM, GROUP_WIDTH, N_HEADS, N_HEADS, SSM_CONV_DIM, GROUP_WIDTH, N_HEADS, GDN_CONV_DIM, GDN_CONV_DIM)
_SEG_NAMES = ("gdn_qkv", "gdn_z", "gdn_b", "gdn_a", "ssm_xbc", "ssm_z", "ssm_dt", "moba_qkv", "sb_qkv")
_OUT_SEGS = (("gdn_qkv", GDN_CONV_DIM), ("gdn_z", GROUP_WIDTH), ("ssm_xbc", SSM_CONV_DIM), ("ssm_z", GROUP_WIDTH),
             ("moba_q", GROUP_WIDTH), ("moba_k", GROUP_WIDTH), ("moba_v", GROUP_WIDTH),
             ("sb_q", GROUP_WIDTH), ("sb_k", GROUP_WIDTH), ("sb_v", GROUP_WIDTH), ("small", LANE))
_ROTARY_SEGS = ("moba_q", "moba_k")


def _softplus(z):
    return jnp.maximum(z, 0.0) + jnp.log1p(jnp.exp(-jnp.abs(z)))


def _sigmoid(x):
    return 1.0 / (1.0 + jnp.exp(-x))


def _silu(x):
    return x * _sigmoid(x)


def _dot(a, b):
    return jnp.dot(a.astype(BF16), b.astype(BF16), preferred_element_type=F32)


def _dot_nt(a, b):
    return lax.dot_general(a.astype(BF16), b.astype(BF16), (((1,), (1,)), ((), ())), preferred_element_type=F32)


def _dot_tn(a, b):
    return lax.dot_general(a.astype(BF16), b.astype(BF16), (((0,), (0,)), ((), ())), preferred_element_type=F32)


def _dot_hi(a, b):
    return jnp.dot(a, b, preferred_element_type=F32, precision=HI)


def _split_dot(x, m_bf16):
    hi = x.astype(BF16)
    lo = (x - hi.astype(F32)).astype(BF16)
    return (jnp.dot(hi, m_bf16, preferred_element_type=F32)
            + jnp.dot(lo, m_bf16, preferred_element_type=F32))


def _ln(h, g, b):
    mu = jnp.mean(h, axis=-1, keepdims=True)
    d = h - mu
    var = jnp.mean(d * d, axis=-1, keepdims=True)
    return d * lax.rsqrt(var + LN_EPS) * g + b


def _in_proj_kernel(x_ref, w_ref, cos_ref, sin_ref, *o_refs):
    y = jnp.dot(x_ref[...].astype(BF16), w_ref[...], preferred_element_type=F32)
    cos = jnp.concatenate([cos_ref[...]] * (GROUP_WIDTH // LANE), axis=1)
    sin = jnp.concatenate([sin_ref[...]] * (GROUP_WIDTH // LANE), axis=1)
    first_half = (lax.broadcasted_iota(I32, (1, GROUP_WIDTH), 1) % HEAD_DIM) < ROPE_DIM // 2
    off = 0
    for (name, width), o_ref in zip(_OUT_SEGS, o_refs):
        seg = y[:, off:off + width]
        if name in _ROTARY_SEGS:
            partner = jnp.where(first_half, pltpu.roll(seg, GROUP_WIDTH - ROPE_DIM // 2, 1), pltpu.roll(seg, ROPE_DIM // 2, 1))
            seg = seg * cos + partner * sin
        o_ref[...] = seg
        off += width


def _rotary_tables(pos):
    half = ROPE_DIM // 2
    inv = ROPE_THETA ** (-jnp.arange(half, dtype=F32) * 2.0 / ROPE_DIM)
    ang = pos.astype(F32)[:, None] * inv
    cc = jnp.arange(LANE) % HEAD_DIM
    cos = jnp.where(cc < ROPE_DIM, jnp.cos(ang)[:, cc % half], 1.0)
    sin = jnp.sin(ang)[:, cc % half]
    sin = jnp.where(cc < half, -sin, jnp.where(cc < ROPE_DIM, sin, 0.0))
    return cos, sin


def _in_proj(x, w_in, pos, seq_len):
    m = x.shape[0]
    tm = min(256, m)
    cols, off = {}, 0
    for name, sz in zip(_SEG_NAMES, _SEG_SIZES):
        cols[name] = w_in[:, off:off + sz]
        off += sz
    for name in ("moba", "sb"):
        for j, part in enumerate("qkv"):
            cols[f"{name}_{part}"] = cols[f"{name}_qkv"][:, j * GROUP_WIDTH:(j + 1) * GROUP_WIDTH]
    cols["small"] = jnp.concatenate([cols["gdn_b"], cols["gdn_a"], cols["ssm_dt"],
                                     jnp.zeros((D_MODEL, LANE - 3 * N_HEADS), w_in.dtype)], axis=1)
    w = jnp.concatenate([cols[name] for name, _ in _OUT_SEGS], axis=1).astype(BF16)
    n = w.shape[1]
    cos, sin = _rotary_tables(pos)
    if seq_len == 1:
        tab_spec = pl.BlockSpec((1, LANE), lambda i: (0, 0))
    else:
        assert seq_len % tm == 0
        tab_spec = pl.BlockSpec((tm, LANE), lambda i: (i % (seq_len // tm), 0))
    outs = pl.pallas_call(
        _in_proj_kernel,
        grid=(m // tm,),
        in_specs=[pl.BlockSpec((tm, D_MODEL), lambda i: (i, 0)),
                  pl.BlockSpec((D_MODEL, n), lambda i: (0, 0)), tab_spec, tab_spec],
        out_specs=[pl.BlockSpec((tm, width), lambda i: (i, 0)) for _, width in _OUT_SEGS],
        out_shape=[jax.ShapeDtypeStruct((m, width), F32) for _, width in _OUT_SEGS],
        compiler_params=pltpu.CompilerParams(dimension_semantics=("arbitrary",), vmem_limit_bytes=VMEM_LIMIT),
        name="in_proj",
    )(x, w, cos, sin)
    return {name: o for (name, _), o in zip(_OUT_SEGS, outs)}


def _sb_kernel(q_ref, k_ref, v_ref, o_ref, *, tq):
    i = pl.program_id(1)
    row = lax.broadcasted_iota(I32, (tq, tq), 0)
    col = lax.broadcasted_iota(I32, (tq, tq), 1)
    upper = (row > col).astype(BF16)
    past = col < row

    for h in range(N_HEADS):
        hs = slice(h * HEAD_DIM, (h + 1) * HEAD_DIM)
        q = (q_ref[0, :, hs] * (HEAD_DIM ** -0.5)).astype(BF16)

        def block(j, carry, acc, diag):
            kj = k_ref[0, pl.ds(j * tq, tq), hs].astype(BF16)
            vj = v_ref[0, pl.ds(j * tq, tq), hs].astype(BF16)
            z = lax.dot_general(q, kj, (((1,), (1,)), ((), ())), preferred_element_type=F32)
            ls = -_softplus(z)
            lk = jnp.where(past, ls, 0.0) if diag else ls
            la = _split_dot(lk, upper)
            w = jnp.exp(z + ls + la + carry)
            if diag:
                w = jnp.where(past, w, 0.0)
            acc = acc + jnp.dot(w.astype(BF16), vj, preferred_element_type=F32)
            carry = carry + la[:, 0:1] + lk[:, 0:1]
            return carry, acc

        carry, acc = block(i, jnp.zeros((tq, 1), F32), jnp.zeros((tq, HEAD_DIM), F32), True)
        carry, acc = lax.fori_loop(0, i, lambda t, c: block(i - 1 - t, c[0], c[1], False), (carry, acc))
        o_ref[0, :, hs] = acc


def _sb_attention(q, k, v, tq=256):
    b, l, d = q.shape
    assert l % tq == 0
    return pl.pallas_call(
        functools.partial(_sb_kernel, tq=tq),
        grid=(b, l // tq),
        in_specs=[pl.BlockSpec((1, tq, d), lambda bi, i: (bi, i, 0)),
                  pl.BlockSpec((1, l, d), lambda bi, i: (bi, 0, 0)),
                  pl.BlockSpec((1, l, d), lambda bi, i: (bi, 0, 0))],
        out_specs=pl.BlockSpec((1, tq, d), lambda bi, i: (bi, i, 0)),
        out_shape=jax.ShapeDtypeStruct((b, l, d), F32),
        compiler_params=pltpu.CompilerParams(dimension_semantics=("arbitrary",) * 2, vmem_limit_bytes=VMEM_LIMIT),
        name="sb_attention",
    )(q, k, v)


def _moba_kernel(q_ref, k_ref, v_ref, o_ref, kmean_ref, *, nb):
    t = MOBA_BLOCK
    i = pl.program_id(1)

    @pl.when(i == 0)
    def _():
        for n in range(nb):
            kmean_ref[n:n + 1, :] = jnp.mean(k_ref[0, n * t:(n + 1) * t, :], axis=0, keepdims=True)

    lane = lax.broadcasted_iota(I32, (t, nb), 1)
    row = lax.broadcasted_iota(I32, (t, t), 0)
    col = lax.broadcasted_iota(I32, (t, t), 1)

    for h in range(N_HEADS):
        hs = slice(h * HEAD_DIM, (h + 1) * HEAD_DIM)
        qf = q_ref[0, :, hs]
        q = (qf * (HEAD_DIM ** -0.5)).astype(BF16)
        gate = lax.dot_general(qf, kmean_ref[:, hs], (((1,), (1,)), ((), ())),
                               preferred_element_type=F32, precision=HI)
        cnt = jnp.zeros((t, nb), F32)
        for n2 in range(nb):
            cn = gate[:, n2:n2 + 1]
            beats = (cn > gate) | ((cn == gate) & (n2 < lane))
            cnt = cnt + jnp.where(beats & (n2 < i), 1.0, 0.0)
        sel = jnp.where((cnt < MOBA_TOPK) & (lane < i), 1.0, 0.0)

        ki = k_ref[0, pl.ds(i * t, t), hs].astype(BF16)
        vi = v_ref[0, pl.ds(i * t, t), hs].astype(BF16)
        s = lax.dot_general(q, ki, (((1,), (1,)), ((), ())), preferred_element_type=F32)
        s = jnp.where(col <= row, s, NEG)
        m = jnp.max(s, axis=-1, keepdims=True)
        p = jnp.exp(s - m)
        l = jnp.sum(p, axis=-1, keepdims=True)
        acc = jnp.dot(p.astype(BF16), vi, preferred_element_type=F32)

        def body(j, c):
            m, l, acc = c
            kj = k_ref[0, pl.ds(j * t, t), hs].astype(BF16)
            vj = v_ref[0, pl.ds(j * t, t), hs].astype(BF16)
            s = lax.dot_general(q, kj, (((1,), (1,)), ((), ())), preferred_element_type=F32)
            sj = jnp.sum(jnp.where(lane == j, sel, 0.0), axis=-1, keepdims=True)
            s = jnp.where(sj > 0.5, s, NEG)
            m_new = jnp.maximum(m, jnp.max(s, axis=-1, keepdims=True))
            a = jnp.exp(m - m_new)
            p = jnp.exp(s - m_new)
            l = a * l + jnp.sum(p, axis=-1, keepdims=True)
            acc = a * acc + jnp.dot(p.astype(BF16), vj, preferred_element_type=F32)
            return m_new, l, acc

        m, l, acc = lax.fori_loop(0, i, body, (m, l, acc))
        o_ref[0, :, hs] = acc / l


def _moba_attention(q, k, v):
    b, l, d = q.shape
    t = MOBA_BLOCK
    assert l % t == 0
    nb = l // t
    return pl.pallas_call(
        functools.partial(_moba_kernel, nb=nb),
        grid=(b, nb),
        in_specs=[pl.BlockSpec((1, t, d), lambda bi, i: (bi, i, 0)),
                  pl.BlockSpec((1, l, d), lambda bi, i: (bi, 0, 0)),
                  pl.BlockSpec((1, l, d), lambda bi, i: (bi, 0, 0))],
        out_specs=pl.BlockSpec((1, t, d), lambda bi, i: (bi, i, 0)),
        out_shape=jax.ShapeDtypeStruct((b, l, d), F32),
        scratch_shapes=[pltpu.VMEM((nb, d), F32)],
        compiler_params=pltpu.CompilerParams(dimension_semantics=("arbitrary",) * 2, vmem_limit_bytes=VMEM_LIMIT),
        name="moba_attention",
    )(q, k, v)


def _conv_tile(ext_ref, u, w_ref, tl):
    ext_ref[8:8 + tl, :] = u
    y = ext_ref[5:5 + tl, :] * w_ref[0:1, :]
    for i in range(1, CONV_W):
        y = y + ext_ref[5 + i:5 + i + tl, :] * w_ref[i:i + 1, :]
    tail = ext_ref[tl + 5:tl + 8, :]
    ext_ref[5:8, :] = tail
    return y, tail


def _gdn_kernel(qkv_ref, z_ref, small_ref, convw_ref, par_ref, normw_ref,
                o_ref, conv_out_ref, s_out_ref, ext_ref, s_ref, *, tl):
    c = CHUNK
    step = pl.program_id(1)

    @pl.when(step == 0)
    def _():
        ext_ref[...] = jnp.zeros_like(ext_ref)
        s_ref[...] = jnp.zeros_like(s_ref)

    y, tail = _conv_tile(ext_ref, qkv_ref[0], convw_ref, tl)
    conv_out_ref[0] = tail
    y = _silu(y)
    small = small_ref[0]
    beta_all = _sigmoid(small[:, 0:N_HEADS])
    g_all = par_ref[0:1, :] * _softplus(small[:, N_HEADS:2 * N_HEADS] + par_ref[1:2, :])

    row = lax.broadcasted_iota(I32, (c, c), 0)
    col = lax.broadcasted_iota(I32, (c, c), 1)
    tril = row >= col
    strict = row > col
    tril_f = tril.astype(F32)
    triu_f = (row <= col).astype(F32)
    eye = (row == col).astype(F32)
    ones = jnp.ones((c, c), F32)

    for ci in range(tl // c):
        r0 = ci * c
        g_c = g_all[r0:r0 + c, :]
        gc_all = _dot_hi(tril_f, g_c)
        for h in range(N_HEADS):
            q = y[r0:r0 + c, h * HEAD_DIM:(h + 1) * HEAD_DIM]
            k = y[r0:r0 + c, GROUP_WIDTH + h * HEAD_DIM:GROUP_WIDTH + (h + 1) * HEAD_DIM]
            v = y[r0:r0 + c, 2 * GROUP_WIDTH + h * HEAD_DIM:2 * GROUP_WIDTH + (h + 1) * HEAD_DIM]
            q = q * lax.rsqrt(jnp.sum(q * q, axis=-1, keepdims=True) + RMS_EPS) * (HEAD_DIM ** -0.5)
            k = k * lax.rsqrt(jnp.sum(k * k, axis=-1, keepdims=True) + RMS_EPS)
            beta = beta_all[r0:r0 + c, h:h + 1]
            gcol = gc_all[:, h:h + 1]
            grow = _dot_hi(ones, g_c[:, h:h + 1] * triu_f)
            decay = jnp.where(tril, jnp.exp(jnp.where(tril, gcol - grow, 0.0)), 0.0)
            kb = k * beta
            a = jnp.where(strict, _dot_nt(kb, k) * decay, 0.0)
            p = -a
            tinv = eye + p
            for _ in range(5):
                p = _dot_hi(p, p)
                tinv = tinv + _dot_hi(tinv, p)
            u = _dot(tinv, v * beta)
            w = _dot(tinv, kb * jnp.exp(gcol))
            attn = jnp.where(tril, _dot_nt(q, k) * decay, 0.0)
            s = s_ref[h]
            v_new = u - _dot(w, s)
            o = _dot(q * jnp.exp(gcol), s) + _dot(attn, v_new)
            g_last = gcol[c - 1:c, :]
            s_ref[h] = s * jnp.exp(g_last) + _dot_tn(k * jnp.exp(g_last - gcol), v_new)
            zz = z_ref[0, r0:r0 + c, h * HEAD_DIM:(h + 1) * HEAD_DIM]
            o = o * lax.rsqrt(jnp.mean(o * o, axis=-1, keepdims=True) + RMS_EPS) * normw_ref[...] * _silu(zz)
            o_ref[0, r0:r0 + c, h * HEAD_DIM:(h + 1) * HEAD_DIM] = o

    @pl.when(step == pl.num_programs(1) - 1)
    def _():
        s_out_ref[0] = s_ref[...]


def _gdn_prompt(qkv, z, small, conv_w, a_log, dt_bias, norm_w, tl=256):
    b, l, _ = qkv.shape
    assert l % tl == 0 and tl % CHUNK == 0
    par = jnp.stack([-jnp.exp(a_log.astype(F32)), dt_bias.astype(F32)])
    return pl.pallas_call(
        functools.partial(_gdn_kernel, tl=tl),
        grid=(b, l // tl),
        in_specs=[pl.BlockSpec((1, tl, GDN_CONV_DIM), lambda bi, i: (bi, i, 0)),
                  pl.BlockSpec((1, tl, GROUP_WIDTH), lambda bi, i: (bi, i, 0)),
                  pl.BlockSpec((1, tl, LANE), lambda bi, i: (bi, i, 0)),
                  pl.BlockSpec((CONV_W, GDN_CONV_DIM), lambda bi, i: (0, 0)),
                  pl.BlockSpec((2, N_HEADS), lambda bi, i: (0, 0)),
                  pl.BlockSpec((1, HEAD_DIM), lambda bi, i: (0, 0))],
        out_specs=[pl.BlockSpec((1, tl, GROUP_WIDTH), lambda bi, i: (bi, i, 0)),
                   pl.BlockSpec((1, CONV_W - 1, GDN_CONV_DIM), lambda bi, i: (bi, 0, 0)),
                   pl.BlockSpec((1, N_HEADS, HEAD_DIM, HEAD_DIM), lambda bi, i: (bi, 0, 0, 0))],
        out_shape=[jax.ShapeDtypeStruct((b, l, GROUP_WIDTH), F32),
                   jax.ShapeDtypeStruct((b, CONV_W - 1, GDN_CONV_DIM), F32),
                   jax.ShapeDtypeStruct((b, N_HEADS, HEAD_DIM, HEAD_DIM), F32)],
        scratch_shapes=[pltpu.VMEM((tl + 8, GDN_CONV_DIM), F32),
                        pltpu.VMEM((N_HEADS, HEAD_DIM, HEAD_DIM), F32)],
        compiler_params=pltpu.CompilerParams(dimension_semantics=("arbitrary", "arbitrary"),
                                             vmem_limit_bytes=VMEM_LIMIT),
        name="gdn_prompt",
    )(qkv, z, small, conv_w, par, norm_w.reshape(1, HEAD_DIM))


def _ssd_kernel(xbc_ref, z_ref, small_ref, convw_ref, convb_ref, par_ref, normw_ref,
                o_ref, conv_out_ref, h_out_ref, ext_ref, h_ref, *, tl):
    c = CHUNK
    step = pl.program_id(1)
    hpg = N_HEADS // SSM_NGROUPS

    @pl.when(step == 0)
    def _():
        ext_ref[...] = jnp.zeros_like(ext_ref)
        h_ref[...] = jnp.zeros_like(h_ref)

    y, tail = _conv_tile(ext_ref, xbc_ref[0], convw_ref, tl)
    conv_out_ref[0] = tail
    xbc = _silu(y + convb_ref[...])
    dt_all = _softplus(small_ref[0][:, 2 * N_HEADS:3 * N_HEADS] + par_ref[1:2, :])
    ad_all = dt_all * par_ref[0:1, :]

    row = lax.broadcasted_iota(I32, (c, c), 0)
    col = lax.broadcasted_iota(I32, (c, c), 1)
    tril = row >= col
    tril_f = tril.astype(F32)
    triu_f = (row <= col).astype(F32)
    ones = jnp.ones((c, c), F32)
    boff = GROUP_WIDTH
    coff = GROUP_WIDTH + SSM_NGROUPS * SSM_STATE

    for ci in range(tl // c):
        r0 = ci * c
        ad_c = ad_all[r0:r0 + c, :]
        acs_all = _dot_hi(tril_f, ad_c)
        ys = []
        for h in range(N_HEADS):
            grp = h // hpg
            x = xbc[r0:r0 + c, h * HEAD_DIM:(h + 1) * HEAD_DIM]
            bm = xbc[r0:r0 + c, boff + grp * SSM_STATE:boff + (grp + 1) * SSM_STATE]
            cm = xbc[r0:r0 + c, coff + grp * SSM_STATE:coff + (grp + 1) * SSM_STATE]
            xdt = x * dt_all[r0:r0 + c, h:h + 1]
            acol = acs_all[:, h:h + 1]
            arow = _dot_hi(ones, ad_c[:, h:h + 1] * triu_f)
            lmat = jnp.where(tril, jnp.exp(jnp.where(tril, acol - arow, 0.0)), 0.0)
            y_diag = _dot(_dot_nt(cm, bm) * lmat, xdt)
            a_last = acol[c - 1:c, :]
            states = _dot_tn(xdt * jnp.exp(a_last - acol), bm)
            hin = h_ref[h]
            y_off = _dot_nt(cm, hin) * jnp.exp(acol)
            h_ref[h] = hin * jnp.exp(a_last) + states
            yh = y_diag + y_off + par_ref[2:3, h:h + 1] * x
            ys.append(yh * _silu(z_ref[0, r0:r0 + c, h * HEAD_DIM:(h + 1) * HEAD_DIM]))
        for grp in range(SSM_NGROUPS):
            grp_ys = ys[grp * hpg:(grp + 1) * hpg]
            ms = sum(jnp.sum(a * a, axis=-1, keepdims=True) for a in grp_ys) / (hpg * HEAD_DIM)
            r = lax.rsqrt(ms + RMS_EPS)
            for j, a in enumerate(grp_ys):
                h = grp * hpg + j
                o_ref[0, r0:r0 + c, h * HEAD_DIM:(h + 1) * HEAD_DIM] = a * r * normw_ref[:, h * HEAD_DIM:(h + 1) * HEAD_DIM]

    @pl.when(step == pl.num_programs(1) - 1)
    def _():
        h_out_ref[0] = h_ref[...]


def _ssd_prompt(xbc, z, small, conv_w, conv_b, a_log, dt_bias, d_skip, norm_w, tl=256):
    b, l, _ = xbc.shape
    assert l % tl == 0 and tl % CHUNK == 0
    par = jnp.stack([-jnp.exp(a_log.astype(F32)), dt_bias.astype(F32), d_skip.astype(F32)])
    return pl.pallas_call(
        functools.partial(_ssd_kernel, tl=tl),
        grid=(b, l // tl),
        in_specs=[pl.BlockSpec((1, tl, SSM_CONV_DIM), lambda bi, i: (bi, i, 0)),
                  pl.BlockSpec((1, tl, GROUP_WIDTH), lambda bi, i: (bi, i, 0)),
                  pl.BlockSpec((1, tl, LANE), lambda bi, i: (bi, i, 0)),
                  pl.BlockSpec((CONV_W, SSM_CONV_DIM), lambda bi, i: (0, 0)),
                  pl.BlockSpec((1, SSM_CONV_DIM), lambda bi, i: (0, 0)),
                  pl.BlockSpec((3, N_HEADS), lambda bi, i: (0, 0)),
                  pl.BlockSpec((1, GROUP_WIDTH), lambda bi, i: (0, 0))],
        out_specs=[pl.BlockSpec((1, tl, GROUP_WIDTH), lambda bi, i: (bi, i, 0)),
                   pl.BlockSpec((1, CONV_W - 1, SSM_CONV_DIM), lambda bi, i: (bi, 0, 0)),
                   pl.BlockSpec((1, N_HEADS, HEAD_DIM, SSM_STATE), lambda bi, i: (bi, 0, 0, 0))],
        out_shape=[jax.ShapeDtypeStruct((b, l, GROUP_WIDTH), F32),
                   jax.ShapeDtypeStruct((b, CONV_W - 1, SSM_CONV_DIM), F32),
                   jax.ShapeDtypeStruct((b, N_HEADS, HEAD_DIM, SSM_STATE), F32)],
        scratch_shapes=[pltpu.VMEM((tl + 8, SSM_CONV_DIM), F32),
                        pltpu.VMEM((N_HEADS, HEAD_DIM, SSM_STATE), F32)],
        compiler_params=pltpu.CompilerParams(dimension_semantics=("arbitrary", "arbitrary"),
                                             vmem_limit_bytes=VMEM_LIMIT),
        name="ssd_prompt",
    )(xbc, z, small, conv_w, conv_b.reshape(1, SSM_CONV_DIM), par, norm_w.reshape(1, GROUP_WIDTH))


def _post_mix_kernel(o0_ref, o1_ref, o2_ref, o3_ref, x_ref, wout_ref, g_ref, b_ref, rw_ref, rb_ref,
                     xn_ref, xnb_ref, idx_ref, gate_ref, rank_ref, cum_ref, cnt_ref, *, tm):
    @pl.when(pl.program_id(0) == 0)
    def _():
        cnt_ref[...] = jnp.zeros_like(cnt_ref)

    mix = jnp.zeros((tm, D_MODEL), F32)
    for gi, o_ref in enumerate((o0_ref, o1_ref, o2_ref, o3_ref)):
        mix = mix + jnp.dot(o_ref[...].astype(BF16), wout_ref[gi * GROUP_WIDTH:(gi + 1) * GROUP_WIDTH, :],
                            preferred_element_type=F32)
    xn = _ln(DN_ALPHA * x_ref[...] + mix, g_ref[...], b_ref[...])
    xn_ref[...] = xn
    xnb_ref[...] = xn.astype(BF16)

    logits = _dot_hi(xn, rw_ref[...]) + rb_ref[...]
    lane_e = lax.broadcasted_iota(I32, (tm, N_EXPERTS), 1).astype(F32)
    lane = lax.broadcasted_iota(I32, (tm, LANE), 1)
    work = logits
    sel = jnp.zeros((tm, N_EXPERTS), F32)
    idx_out = jnp.zeros((tm, LANE), I32)
    val_out = jnp.zeros((tm, LANE), F32)
    hits = []
    for r in range(TOP_K):
        m = jnp.max(work, axis=-1, keepdims=True)
        pick = jnp.min(jnp.where(work == m, lane_e, float(N_EXPERTS)), axis=-1, keepdims=True)
        hit = lane_e == pick
        work = jnp.where(hit, -jnp.inf, work)
        sel = sel + hit.astype(F32)
        hits.append(hit)
        idx_out = jnp.where(lane == r, pick.astype(I32), idx_out)
        val_out = jnp.where(lane == r, m, val_out)
    ex = jnp.where(lane < TOP_K, jnp.exp(val_out - val_out[:, 0:1]), 0.0)
    gate_ref[...] = ex / jnp.sum(ex, axis=-1, keepdims=True)
    idx_ref[...] = idx_out

    row = lax.broadcasted_iota(I32, (tm, tm), 0)
    col = lax.broadcasted_iota(I32, (tm, tm), 1)
    before = jnp.dot((row > col).astype(BF16), sel.astype(BF16), preferred_element_type=F32) + cnt_ref[...]
    rank_out = jnp.zeros((tm, LANE), I32)
    for r in range(TOP_K):
        rk = jnp.sum(jnp.where(hits[r], before, 0.0), axis=-1, keepdims=True)
        rank_out = jnp.where(lane == r, rk.astype(I32), rank_out)
    rank_ref[...] = rank_out
    cum_ref[0] = jnp.broadcast_to(cnt_ref[...], (8, N_EXPERTS))
    cnt_ref[...] = cnt_ref[...] + jnp.sum(sel, axis=0, keepdims=True)


def _post_mix(outs, x, w_out_bf16, ln_g, ln_b, router_w, router_b, tm):
    n = x.shape[0]
    nt = n // tm
    tok = lambda w: pl.BlockSpec((tm, w), lambda i: (i, 0))
    full = lambda a: pl.BlockSpec(a.shape, lambda i: (0,) * a.ndim)
    g2, b2, rb2 = ln_g.reshape(1, -1), ln_b.reshape(1, -1), router_b.reshape(1, -1)
    return pl.pallas_call(
        functools.partial(_post_mix_kernel, tm=tm),
        grid=(nt,),
        in_specs=[tok(GROUP_WIDTH)] * 4 + [tok(D_MODEL), full(w_out_bf16), full(g2), full(b2), full(router_w), full(rb2)],
        out_specs=[tok(D_MODEL), tok(D_MODEL), tok(LANE), tok(LANE), tok(LANE),
                   pl.BlockSpec((1, 8, N_EXPERTS), lambda i: (i, 0, 0))],
        out_shape=[jax.ShapeDtypeStruct((n, D_MODEL), F32), jax.ShapeDtypeStruct((n, D_MODEL), BF16),
                   jax.ShapeDtypeStruct((n, LANE), I32), jax.ShapeDtypeStruct((n, LANE), F32),
                   jax.ShapeDtypeStruct((n, LANE), I32), jax.ShapeDtypeStruct((nt, 8, N_EXPERTS), F32)],
        scratch_shapes=[pltpu.VMEM((1, N_EXPERTS), F32)],
        compiler_params=pltpu.CompilerParams(dimension_semantics=("arbitrary",), vmem_limit_bytes=VMEM_LIMIT),
        name="post_mix",
    )(*outs, x, w_out_bf16, g2, b2, router_w, rb2)


def _moe_dispatch(idx, gates, rank, cum, n_tok, tm_cum, tm, blk):
    nt = n_tok // tm
    n_blocks = -(-n_tok * TOP_K // blk) + N_EXPERTS
    per_pair = -(-tm // blk) + 1
    e_idx = idx[:, :TOP_K]
    cum_i = cum[:, 0, :].astype(I32)
    onehot = e_idx[:, :, None] == jnp.arange(N_EXPERTS, dtype=I32)
    counts = cum_i[-1] + jnp.sum(onehot[n_tok - tm_cum:].astype(I32), axis=(0, 1))
    cum_t = cum_i[::tm // tm_cum]
    cum_all = jnp.concatenate([cum_t, counts[None]], 0)
    blocks_per_e = (counts + blk - 1) // blk
    blk_start = jnp.cumsum(blocks_per_e) - blocks_per_e
    dest = jnp.sum(jnp.where(onehot, blk_start, 0), axis=-1) * blk + rank[:, :TOP_K]
    dest_pad = jnp.pad(dest, ((0, 0), (0, LANE - TOP_K)), constant_values=-1)
    dest_t = jnp.pad(dest.T, ((0, 8 - TOP_K), (0, 0)), constant_values=-1)
    gate_t = jnp.pad(gates[:, :TOP_K].T, ((0, 8 - TOP_K), (0, 0)))

    lo, hi = cum_all[:-1].T, cum_all[1:].T
    has = hi > lo
    j0 = lo // blk
    j1 = jnp.maximum(hi - 1, 0) // blk
    jj = j0[:, :, None] + jnp.arange(per_pair, dtype=I32)
    valid = has[:, :, None] & (jj <= j1[:, :, None])
    blk_id = blk_start[:, None, None] + jj
    e_b = jnp.broadcast_to(jnp.arange(N_EXPERTS, dtype=I32)[:, None, None], jj.shape)
    t_b = jnp.broadcast_to(jnp.arange(nt, dtype=I32)[None, :, None], jj.shape)
    n_steps = min(N_EXPERTS * nt + n_blocks, per_pair * N_EXPERTS * nt)
    n_valid = jnp.sum(valid)
    pos = jnp.arange(n_steps)
    v = pos < n_valid

    def build(key):
        order = jnp.argsort(jnp.where(valid, key, jnp.int32(1 << 30)).reshape(-1))[:n_steps]
        src = order[jnp.minimum(pos, n_valid - 1)]
        return blk_id.reshape(-1)[src], t_b.reshape(-1)[src], e_b.reshape(-1)[src]

    def edges(a):
        change = a[1:] != a[:-1]
        first = jnp.concatenate([jnp.ones((1,), bool), change]) & v
        last = (jnp.concatenate([change, jnp.ones((1,), bool)]) | (pos == n_valid - 1)) & v
        return first, last

    b1, t1, e1 = build(blk_id * nt + t_b)
    first_b, last_b = edges(b1)
    first_e, _ = edges(e1)
    flags1 = first_b.astype(I32) + 2 * last_b.astype(I32) + 4 * v.astype(I32) + 8 * first_e.astype(I32)
    b2, t2, _ = build(t_b * (2 * n_blocks) + blk_id)
    first_t, last_t = edges(t2)
    flags2 = first_t.astype(I32) + 2 * last_t.astype(I32) + 4 * v.astype(I32)
    return dict(dest=dest_pad, dest_t=dest_t, gate_t=gate_t, n_rows=n_blocks * blk, n_steps=n_steps,
                gather=(b1.astype(I32), t1.astype(I32), e1.astype(I32), flags1),
                combine=(b2.astype(I32), t2.astype(I32), flags2))


def _expert_kernel(blk_s, tile_s, exp_s, flag_s, x_ref, dest_ref, gate_ref, wgu_ref, bgu_ref, wdn_ref, bdn_ref,
                   yb_ref, xacc_ref, gacc_ref, wgu_bf, wdn_bf, *, blk, tm):
    s = pl.program_id(0)
    flags = flag_s[s]

    @pl.when((flags & 8) != 0)
    def _():
        wgu_bf[...] = wgu_ref[0].astype(BF16)
        wdn_bf[...] = wdn_ref[0].astype(BF16)

    @pl.when((flags & 1) != 0)
    def _():
        xacc_ref[...] = jnp.zeros_like(xacc_ref)
        gacc_ref[...] = jnp.zeros_like(gacc_ref)

    @pl.when((flags & 4) != 0)
    def _():
        rowid = blk_s[s] * blk + lax.broadcasted_iota(I32, (blk, tm), 0)
        d = dest_ref[...]
        g = gate_ref[...]
        p = jnp.zeros((blk, tm), F32)
        pg = jnp.zeros((blk, tm), F32)
        for k in range(TOP_K):
            hit = d[k:k + 1, :] == rowid
            p = jnp.where(hit, 1.0, p)
            pg = jnp.where(hit, g[k:k + 1, :], pg)
        xacc_ref[...] += jnp.dot(p.astype(BF16), x_ref[...], preferred_element_type=F32)
        gacc_ref[...] += jnp.sum(pg, axis=-1, keepdims=True)

    @pl.when((flags & 2) != 0)
    def _():
        h = jnp.dot(xacc_ref[...].astype(BF16), wgu_bf[...], preferred_element_type=F32) + bgu_ref[0]
        gate = jnp.minimum(h[:, :D_FF], SWIGLU_LIMIT)
        up = jnp.clip(h[:, D_FF:], -SWIGLU_LIMIT, SWIGLU_LIMIT)
        act = (up + 1.0) * gate * _sigmoid(SWIGLU_ALPHA * gate)
        y = jnp.dot(act.astype(BF16), wdn_bf[...], preferred_element_type=F32) + bdn_ref[0]
        yb_ref[...] = (y * gacc_ref[...]).astype(yb_ref.dtype)


def _expert_ffn(xn_bf16, disp, w_gu, b_gu, w_down, b_down, tm, blk):
    blk_s, tile_s, exp_s, flag_s = disp["gather"]
    grid_spec = pltpu.PrefetchScalarGridSpec(
        num_scalar_prefetch=4,
        grid=(disp["n_steps"],),
        in_specs=[pl.BlockSpec((tm, D_MODEL), lambda s, b, t, e, f: (t[s], 0)),
                  pl.BlockSpec((8, tm), lambda s, b, t, e, f: (0, t[s])),
                  pl.BlockSpec((8, tm), lambda s, b, t, e, f: (0, t[s])),
                  pl.BlockSpec((1, D_MODEL, 2 * D_FF), lambda s, b, t, e, f: (e[s], 0, 0)),
                  pl.BlockSpec((1, 1, 2 * D_FF), lambda s, b, t, e, f: (e[s], 0, 0)),
                  pl.BlockSpec((1, D_FF, D_MODEL), lambda s, b, t, e, f: (e[s], 0, 0)),
                  pl.BlockSpec((1, 1, D_MODEL), lambda s, b, t, e, f: (e[s], 0, 0))],
        out_specs=pl.BlockSpec((blk, D_MODEL), lambda s, b, t, e, f: (b[s], 0)),
        scratch_shapes=[pltpu.VMEM((blk, D_MODEL), F32), pltpu.VMEM((blk, 1), F32),
                        pltpu.VMEM((D_MODEL, 2 * D_FF), BF16), pltpu.VMEM((D_FF, D_MODEL), BF16)])
    return pl.pallas_call(
        functools.partial(_expert_kernel, blk=blk, tm=tm),
        grid_spec=grid_spec,
        out_shape=jax.ShapeDtypeStruct((disp["n_rows"], D_MODEL), BF16),
        compiler_params=pltpu.CompilerParams(dimension_semantics=("arbitrary",), vmem_limit_bytes=VMEM_LIMIT),
        name="expert_ffn",
    )(blk_s, tile_s, exp_s, flag_s, xn_bf16, disp["dest_t"], disp["gate_t"],
      w_gu, b_gu.reshape(N_EXPERTS, 1, 2 * D_FF), w_down, b_down.reshape(N_EXPERTS, 1, D_MODEL))


def _combine_kernel(blk_s, tile_s, flag_s, yb_ref, dest_ref, xn_ref, g_ref, b_ref, y_ref, acc_ref, *, blk, tm):
    s = pl.program_id(0)
    flags = flag_s[s]

    @pl.when((flags & 1) != 0)
    def _():
        acc_ref[...] = jnp.zeros_like(acc_ref)

    @pl.when((flags & 4) != 0)
    def _():
        rowid = blk_s[s] * blk + lax.broadcasted_iota(I32, (tm, blk), 1)
        d = dest_ref[...]
        p = jnp.zeros((tm, blk), F32)
        for k in range(TOP_K):
            p = jnp.where(d[:, k:k + 1] == rowid, 1.0, p)
        acc_ref[...] += jnp.dot(p.astype(BF16), yb_ref[...], preferred_element_type=F32)

    @pl.when((flags & 2) != 0)
    def _():
        y_ref[...] = _ln(DN_ALPHA * xn_ref[...] + acc_ref[...], g_ref[...], b_ref[...])


def _moe_combine(yb, disp, xn, ln_g, ln_b, tm, blk):
    blk_s, tile_s, flag_s = disp["combine"]
    n = xn.shape[0]
    grid_spec = pltpu.PrefetchScalarGridSpec(
        num_scalar_prefetch=3,
        grid=(disp["n_steps"],),
        in_specs=[pl.BlockSpec((blk, D_MODEL), lambda s, b, t, f: (b[s], 0)),
                  pl.BlockSpec((tm, LANE), lambda s, b, t, f: (t[s], 0)),
                  pl.BlockSpec((tm, D_MODEL), lambda s, b, t, f: (t[s], 0)),
                  pl.BlockSpec((1, D_MODEL), lambda s, b, t, f: (0, 0)),
                  pl.BlockSpec((1, D_MODEL), lambda s, b, t, f: (0, 0))],
        out_specs=pl.BlockSpec((tm, D_MODEL), lambda s, b, t, f: (t[s], 0)),
        scratch_shapes=[pltpu.VMEM((tm, D_MODEL), F32)])
    return pl.pallas_call(
        functools.partial(_combine_kernel, blk=blk, tm=tm),
        grid_spec=grid_spec,
        out_shape=jax.ShapeDtypeStruct((n, D_MODEL), F32),
        compiler_params=pltpu.CompilerParams(dimension_semantics=("arbitrary",), vmem_limit_bytes=VMEM_LIMIT),
        name="moe_combine",
    )(blk_s, tile_s, flag_s, yb, disp["dest"], xn, ln_g.reshape(1, -1), ln_b.reshape(1, -1))


def _post_mix_moe(outs, x, w_out, fw):
    ln1_g, ln1_b, router_w, router_b, w_gu, b_gu, w_down, b_down, ln2_g, ln2_b = fw
    n = x.shape[0]
    tm_cum = min(256, n)
    tm = min(512, n)
    blk = min(EXPERT_BLOCK, n * TOP_K)
    xn, xnb, idx, gates, rank, cum = _post_mix(outs, x, w_out.astype(BF16), ln1_g, ln1_b, router_w, router_b, tm_cum)
    disp = _moe_dispatch(idx, gates, rank, cum, n, tm_cum, tm, blk)
    yb = _expert_ffn(xnb, disp, w_gu, b_gu, w_down, b_down, tm, blk)
    return _moe_combine(yb, disp, xn, ln2_g, ln2_b, tm, blk)


def _rec_step_kernel(gqkv_ref, gz_ref, xbc_ref, sz_ref, small_ref, gconv0_ref, sconv0_ref, s0_ref, h0_ref,
                     gcw_ref, scw_ref, scb_ref, gpar_ref, spar_ref, gnw_ref, snw_ref,
                     gout_ref, sout_ref, gconv_ref, sconv_ref, s_ref, h_ref):
    small = small_ref[0]

    def conv(u, buf_ref, w_ref, out_ref):
        buf = buf_ref[0]
        y = u * w_ref[CONV_W - 1:CONV_W, :]
        for i in range(CONV_W - 1):
            y = y + buf[i:i + 1, :] * w_ref[i:i + 1, :]
        out_ref[0, 0:CONV_W - 2, :] = buf[1:CONV_W - 1, :]
        out_ref[0, CONV_W - 2:CONV_W - 1, :] = u
        return y

    eye = (lax.broadcasted_iota(I32, (HEAD_DIM, HEAD_DIM), 0)
           == lax.broadcasted_iota(I32, (HEAD_DIM, HEAD_DIM), 1)).astype(F32)

    def to_col(r):
        return jnp.sum(eye * r, axis=-1, keepdims=True)

    y = _silu(conv(gqkv_ref[0], gconv0_ref, gcw_ref, gconv_ref))
    beta_all = _sigmoid(small[:, 0:N_HEADS])
    g_all = gpar_ref[0:1, :] * _softplus(small[:, N_HEADS:2 * N_HEADS] + gpar_ref[1:2, :])
    for h in range(N_HEADS):
        hs = slice(h * HEAD_DIM, (h + 1) * HEAD_DIM)
        q = y[:, h * HEAD_DIM:(h + 1) * HEAD_DIM]
        k = y[:, GROUP_WIDTH + h * HEAD_DIM:GROUP_WIDTH + (h + 1) * HEAD_DIM]
        v = y[:, 2 * GROUP_WIDTH + h * HEAD_DIM:2 * GROUP_WIDTH + (h + 1) * HEAD_DIM]
        q = q * lax.rsqrt(jnp.sum(q * q, axis=-1, keepdims=True) + RMS_EPS) * (HEAD_DIM ** -0.5)
        k = k * lax.rsqrt(jnp.sum(k * k, axis=-1, keepdims=True) + RMS_EPS)
        beta = beta_all[:, h:h + 1]
        eg = jnp.exp(g_all[:, h:h + 1])
        s0 = s0_ref[0, h]
        kc, qc = to_col(k), to_col(q)
        v_new = v * beta - jnp.sum(kc * (beta * eg) * s0, axis=0, keepdims=True)
        o = jnp.sum(qc * eg * s0, axis=0, keepdims=True) + jnp.sum(q * k, axis=-1, keepdims=True) * v_new
        s_ref[0, h] = s0 * eg + kc * v_new
        o = o * lax.rsqrt(jnp.mean(o * o, axis=-1, keepdims=True) + RMS_EPS) * gnw_ref[...] * _silu(gz_ref[0][:, hs])
        gout_ref[0, :, hs] = o

    xbc = _silu(conv(xbc_ref[0], sconv0_ref, scw_ref, sconv_ref) + scb_ref[...])
    dt_all = _softplus(small[:, 2 * N_HEADS:3 * N_HEADS] + spar_ref[1:2, :])
    hpg = N_HEADS // SSM_NGROUPS
    boff, coff = GROUP_WIDTH, GROUP_WIDTH + SSM_NGROUPS * SSM_STATE
    ys = []
    for h in range(N_HEADS):
        grp = h // hpg
        x = xbc[:, h * HEAD_DIM:(h + 1) * HEAD_DIM]
        bm = xbc[:, boff + grp * SSM_STATE:boff + (grp + 1) * SSM_STATE]
        cm = xbc[:, coff + grp * SSM_STATE:coff + (grp + 1) * SSM_STATE]
        dt = dt_all[:, h:h + 1]
        hn = h0_ref[0, h] * jnp.exp(dt * spar_ref[0:1, h:h + 1]) + to_col(x * dt) * bm
        h_ref[0, h] = hn
        yc = jnp.sum(hn * cm, axis=-1, keepdims=True)
        yh = jnp.sum(eye * yc, axis=0, keepdims=True) + spar_ref[2:3, h:h + 1] * x
        ys.append(yh * _silu(sz_ref[0][:, h * HEAD_DIM:(h + 1) * HEAD_DIM]))
    for grp in range(SSM_NGROUPS):
        grp_ys = ys[grp * hpg:(grp + 1) * hpg]
        ms = sum(jnp.sum(a * a, axis=-1, keepdims=True) for a in grp_ys) / (hpg * HEAD_DIM)
        r = lax.rsqrt(ms + RMS_EPS)
        for j, a in enumerate(grp_ys):
            h = grp * hpg + j
            sout_ref[0, :, h * HEAD_DIM:(h + 1) * HEAD_DIM] = a * r * snw_ref[:, h * HEAD_DIM:(h + 1) * HEAD_DIM]


def _rec_step(seg, gconv0, s0, sconv0, h0, lw):
    (_, gdn_conv_w, gdn_a_log, gdn_dt_bias, gdn_norm_w, ssm_conv_w, ssm_conv_b,
     ssm_a_log, ssm_dt_bias, ssm_d, ssm_norm_w, _) = lw
    b = seg["small"].shape[0]
    gpar = jnp.stack([-jnp.exp(gdn_a_log.astype(F32)), gdn_dt_bias.astype(F32)])
    spar = jnp.stack([-jnp.exp(ssm_a_log.astype(F32)), ssm_dt_bias.astype(F32), ssm_d.astype(F32)])
    row = lambda w: pl.BlockSpec((1, 1, w), lambda i: (i, 0, 0))
    full = lambda a: pl.BlockSpec(a.shape, lambda i: (0,) * a.ndim)
    per_seq = lambda *dims: pl.BlockSpec((1,) + dims, lambda i: (i,) + (0,) * len(dims))
    r3 = lambda a: a.reshape(b, 1, -1)
    consts = [gdn_conv_w, ssm_conv_w, ssm_conv_b.reshape(1, -1), gpar, spar,
              gdn_norm_w.reshape(1, HEAD_DIM), ssm_norm_w.reshape(1, GROUP_WIDTH)]
    state_specs = [per_seq(CONV_W - 1, GDN_CONV_DIM), per_seq(CONV_W - 1, SSM_CONV_DIM),
                   per_seq(N_HEADS, HEAD_DIM, HEAD_DIM), per_seq(N_HEADS, HEAD_DIM, SSM_STATE)]
    go, so, gc, sc, s, h = pl.pallas_call(
        _rec_step_kernel,
        grid=(b,),
        in_specs=[row(GDN_CONV_DIM), row(GROUP_WIDTH), row(SSM_CONV_DIM), row(GROUP_WIDTH), row(LANE)]
                 + state_specs + [full(c) for c in consts],
        out_specs=[row(GROUP_WIDTH), row(GROUP_WIDTH)] + state_specs,
        out_shape=[jax.ShapeDtypeStruct((b, 1, GROUP_WIDTH), F32), jax.ShapeDtypeStruct((b, 1, GROUP_WIDTH), F32),
                   jax.ShapeDtypeStruct((b, CONV_W - 1, GDN_CONV_DIM), F32),
                   jax.ShapeDtypeStruct((b, CONV_W - 1, SSM_CONV_DIM), F32),
                   jax.ShapeDtypeStruct((b, N_HEADS, HEAD_DIM, HEAD_DIM), F32),
                   jax.ShapeDtypeStruct((b, N_HEADS, HEAD_DIM, SSM_STATE), F32)],
        compiler_params=pltpu.CompilerParams(dimension_semantics=("arbitrary",), vmem_limit_bytes=VMEM_LIMIT),
        name="rec_step",
    )(r3(seg["gdn_qkv"]), r3(seg["gdn_z"]), r3(seg["ssm_xbc"]), r3(seg["ssm_z"]), r3(seg["small"]),
      gconv0, sconv0, s0, h0, *consts)
    return go.reshape(b, GROUP_WIDTH), so.reshape(b, GROUP_WIDTH), gc, sc, s, h


PAGES_PER_STEP = 4


def _head_rows(q_row, scale):
    r = lax.broadcasted_iota(I32, (8, GROUP_WIDTH), 0)
    c = lax.broadcasted_iota(I32, (8, GROUP_WIDTH), 1)
    return jnp.where(c // HEAD_DIM == r, q_row * scale, 0.0)


def _decode_pass_kernel(pt_ref, sq_ref, mq_ref, *refs, n_steps):
    pp = PAGES_PER_STEP
    sk_refs, sv_refs, mk_refs = refs[0:pp], refs[pp:2 * pp], refs[2 * pp:3 * pp]
    sb_out_ref, gp_ref, acc_ref, carry_ref = refs[3 * pp:]
    p = pl.program_id(1)

    @pl.when(p == 0)
    def _():
        acc_ref[...] = jnp.zeros_like(acc_ref)
        carry_ref[...] = jnp.zeros_like(carry_ref)

    qs = _head_rows(sq_ref[0], HEAD_DIM ** -0.5).astype(BF16)
    qm = _head_rows(mq_ref[0], 1.0 / MOBA_BLOCK)
    row = lax.broadcasted_iota(I32, (PAGE_SIZE, PAGE_SIZE), 0)
    col = lax.broadcasted_iota(I32, (PAGE_SIZE, PAGE_SIZE), 1)
    later = (col > row).astype(BF16)
    diag8 = lax.broadcasted_iota(I32, (8, LANE), 0) == lax.broadcasted_iota(I32, (8, LANE), 1)
    acc = acc_ref[...]
    carry = carry_ref[...]
    for i in range(pp):
        k = sk_refs[i][0, 0].astype(BF16)
        v = sv_refs[i][0, 0].astype(BF16)
        z = lax.dot_general(k, qs, (((1,), (1,)), ((), ())), preferred_element_type=F32)
        ls = -_softplus(z)
        hi = ls.astype(BF16)
        lo = (ls - hi.astype(F32)).astype(BF16)
        la = jnp.dot(later, hi, preferred_element_type=F32) + jnp.dot(later, lo, preferred_element_type=F32)
        w = jnp.exp(z + ls + la + carry)
        acc = acc + lax.dot_general(w.astype(BF16), v, (((0,), (0,)), ((), ())), preferred_element_type=F32)
        carry = carry + la[0:1, :] + ls[0:1, :]
        ksum = jnp.sum(mk_refs[i][0, 0], axis=0, keepdims=True)
        gp = jnp.sum(qm * ksum, axis=-1, keepdims=True)
        gp_ref[0, pl.ds(p * pp + i, 1), :] = jnp.sum(jnp.where(diag8, gp, 0.0), axis=0, keepdims=True)
    acc_ref[...] = acc
    carry_ref[...] = carry

    @pl.when(p == n_steps - 1)
    def _():
        r = lax.broadcasted_iota(I32, (8, GROUP_WIDTH), 0)
        c = lax.broadcasted_iota(I32, (8, GROUP_WIDTH), 1)
        sb_out_ref[0] = jnp.sum(jnp.where(c // HEAD_DIM == r, acc, 0.0), axis=0, keepdims=True)


def _decode_pass(layer, page_table, sq, mq, cache_sb_k, cache_sb_v, cache_moba_k):
    b, n_pages = page_table.shape
    pp = PAGES_PER_STEP
    assert n_pages % pp == 0
    n_steps = n_pages // pp

    def page_spec(i):
        return pl.BlockSpec((1, 1, PAGE_SIZE, GROUP_WIDTH),
                            lambda bi, p, pt: (layer, pt[bi, n_pages - 1 - (p * pp + i)], 0, 0))

    qspec = pl.BlockSpec((1, 1, GROUP_WIDTH), lambda bi, p, pt: (bi, 0, 0))
    grid_spec = pltpu.PrefetchScalarGridSpec(
        num_scalar_prefetch=1,
        grid=(b, n_steps),
        in_specs=[qspec, qspec] + [page_spec(i) for i in range(pp)] * 3,
        out_specs=[pl.BlockSpec((1, 1, GROUP_WIDTH), lambda bi, p, pt: (bi, 0, 0)),
                   pl.BlockSpec((1, n_pages, LANE), lambda bi, p, pt: (bi, 0, 0))],
        scratch_shapes=[pltpu.VMEM((8, GROUP_WIDTH), F32), pltpu.VMEM((1, 8), F32)])
    sb_out, gp = pl.pallas_call(
        functools.partial(_decode_pass_kernel, n_steps=n_steps),
        grid_spec=grid_spec,
        out_shape=[jax.ShapeDtypeStruct((b, 1, GROUP_WIDTH), F32), jax.ShapeDtypeStruct((b, n_pages, LANE), F32)],
        compiler_params=pltpu.CompilerParams(dimension_semantics=("arbitrary", "arbitrary"), vmem_limit_bytes=VMEM_LIMIT),
        name="decode_pass",
    )(page_table, sq.reshape(b, 1, GROUP_WIDTH), mq.reshape(b, 1, GROUP_WIDTH),
      *([cache_sb_k] * pp), *([cache_sb_v] * pp), *([cache_moba_k] * pp))
    return sb_out.reshape(b, GROUP_WIDTH), gp


def _moba_decode_kernel(pg_ref, mq_ref, mk_ref, mv_ref, kp_ref, vp_ref, o_ref, m_ref, l_ref, acc_ref, *, n_sel):
    h = pl.program_id(1)
    j = pl.program_id(2)
    lane = lax.broadcasted_iota(I32, (1, GROUP_WIDTH), 1)
    in_head = lane // HEAD_DIM == h
    q = jnp.where(in_head, mq_ref[0] * (HEAD_DIM ** -0.5), 0.0)

    @pl.when(j == 0)
    def _():
        m_ref[...] = jnp.sum(q * mk_ref[0], axis=-1, keepdims=True)
        l_ref[...] = jnp.ones_like(l_ref)
        acc_ref[...] = mv_ref[0]

    @pl.when((j == 0) & (h == 0))
    def _():
        o_ref[...] = jnp.zeros_like(o_ref)

    q8 = jnp.where(lax.broadcasted_iota(I32, (8, GROUP_WIDTH), 0) == 0, q, 0.0).astype(BF16)
    k = kp_ref[0, 0].astype(BF16)
    v = vp_ref[0, 0].astype(BF16)
    s = lax.dot_general(k, q8, (((1,), (1,)), ((), ())), preferred_element_type=F32)[:, 0:1]
    m_old = m_ref[...]
    m_new = jnp.maximum(m_old, jnp.max(s, axis=0, keepdims=True))
    a = jnp.exp(m_old - m_new)
    p = jnp.exp(s - m_new)
    l_ref[...] = a * l_ref[...] + jnp.sum(p, axis=0, keepdims=True)
    p8 = jnp.where(lax.broadcasted_iota(I32, (PAGE_SIZE, 8), 1) == 0, p, 0.0).astype(BF16)
    pv = lax.dot_general(p8, v, (((0,), (0,)), ((), ())), preferred_element_type=F32)[0:1, :]
    acc_ref[...] = a * acc_ref[...] + pv
    m_ref[...] = m_new

    @pl.when(j == n_sel - 1)
    def _():
        o_ref[0] = jnp.where(in_head, acc_ref[...] / l_ref[...], o_ref[0])


def _moba_decode(layer, sel_pages, mq, mk_new, mv_new, cache_moba_k, cache_moba_v):
    b, nh, n_sel = sel_pages.shape
    row = pl.BlockSpec((1, 1, GROUP_WIDTH), lambda bi, h, j, pg: (bi, 0, 0))
    page = pl.BlockSpec((1, 1, PAGE_SIZE, GROUP_WIDTH), lambda bi, h, j, pg: (layer, pg[(bi * nh + h) * n_sel + j], 0, 0))
    grid_spec = pltpu.PrefetchScalarGridSpec(
        num_scalar_prefetch=1,
        grid=(b, nh, n_sel),
        in_specs=[row, row, row, page, page],
        out_specs=pl.BlockSpec((1, 1, GROUP_WIDTH), lambda bi, h, j, pg: (bi, 0, 0)),
        scratch_shapes=[pltpu.VMEM((1, 1), F32), pltpu.VMEM((1, 1), F32), pltpu.VMEM((1, GROUP_WIDTH), F32)])
    r3 = lambda a: a.reshape(b, 1, GROUP_WIDTH)
    out = pl.pallas_call(
        functools.partial(_moba_decode_kernel, n_sel=n_sel),
        grid_spec=grid_spec,
        out_shape=jax.ShapeDtypeStruct((b, 1, GROUP_WIDTH), F32),
        compiler_params=pltpu.CompilerParams(dimension_semantics=("arbitrary",) * 3, vmem_limit_bytes=VMEM_LIMIT),
        name="moba_decode",
    )(sel_pages.reshape(-1).astype(I32), r3(mq), r3(mk_new), r3(mv_new), cache_moba_k, cache_moba_v)
    return out.reshape(b, GROUP_WIDTH)


def _sample_attention(layer, page_table, seg, caches):
    cache_moba_k, cache_moba_v, cache_sb_k, cache_sb_v = caches
    b, n_pages = page_table.shape
    ppb = MOBA_BLOCK // PAGE_SIZE
    n_past_blocks = n_pages // ppb
    assert n_pages % ppb == 0 and n_past_blocks >= MOBA_TOPK
    sb_out, gp = _decode_pass(layer, page_table, seg["sb_q"], seg["moba_q"], cache_sb_k, cache_sb_v, cache_moba_k)
    gates = gp[:, ::-1, :N_HEADS].reshape(b, n_past_blocks, ppb, N_HEADS).sum(axis=2)
    _, sel = lax.top_k(jnp.swapaxes(gates, 1, 2), MOBA_TOPK)
    pages = sel[..., None] * ppb + jnp.arange(ppb)
    sel_pages = jnp.take_along_axis(page_table[:, None, :], pages.reshape(b, N_HEADS, -1), axis=2)
    moba_out = _moba_decode(layer, sel_pages, seg["moba_q"], seg["moba_k"], seg["moba_v"], cache_moba_k, cache_moba_v)
    return moba_out, sb_out


def _kv_rows(seg, bsz, length):
    return tuple(seg[name].reshape(bsz, length, N_HEADS, HEAD_DIM) for name in ("moba_k", "moba_v", "sb_k", "sb_v"))


def _prompt_mixers(x, lw):
    (w_in, gdn_conv_w, gdn_a_log, gdn_dt_bias, gdn_norm_w, ssm_conv_w, ssm_conv_b,
     ssm_a_log, ssm_dt_bias, ssm_d, ssm_norm_w, _) = lw
    bsz, length, _ = x.shape
    n = bsz * length
    flat = _in_proj(x.reshape(n, D_MODEL), w_in, jnp.arange(length), length)
    seg = {k: v.reshape(bsz, length, -1) for k, v in flat.items()}

    gdn_out, gdn_conv_new, gdn_s_new = _gdn_prompt(seg["gdn_qkv"], seg["gdn_z"], seg["small"],
                                                   gdn_conv_w, gdn_a_log, gdn_dt_bias, gdn_norm_w)
    ssm_out, ssm_conv_new, ssm_h_new = _ssd_prompt(seg["ssm_xbc"], seg["ssm_z"], seg["small"], ssm_conv_w, ssm_conv_b,
                                                   ssm_a_log, ssm_dt_bias, ssm_d, ssm_norm_w)
    moba_out = _moba_attention(seg["moba_q"], seg["moba_k"], seg["moba_v"])
    sb_out = _sb_attention(seg["sb_q"], seg["sb_k"], seg["sb_v"])

    outs = [o.reshape(n, GROUP_WIDTH) for o in (gdn_out, ssm_out, moba_out, sb_out)]
    return outs, _kv_rows(seg, bsz, length) + (gdn_conv_new, gdn_s_new, ssm_conv_new, ssm_h_new)


def _sample_mixers(x, layer, page_table, rec_state, caches, lw):
    gdn_conv0, gdn_s0, ssm_conv0, ssm_h0 = rec_state
    bsz, length, _ = x.shape
    assert length == 1
    pos0 = page_table.shape[1] * PAGE_SIZE
    seg = _in_proj(x.reshape(bsz, D_MODEL), lw[0], jnp.full((1,), pos0, I32), 1)
    gdn_out, ssm_out, gdn_conv_new, ssm_conv_new, gdn_s_new, ssm_h_new = _rec_step(
        seg, gdn_conv0, gdn_s0, ssm_conv0, ssm_h0, lw)
    moba_out, sb_out = _sample_attention(layer, page_table, seg, caches)
    outs = [gdn_out, ssm_out, moba_out, sb_out]
    return outs, _kv_rows(seg, bsz, length) + (gdn_conv_new, gdn_s_new, ssm_conv_new, ssm_h_new)


def kernel(x_prompt, x_sample, cache_moba_k, cache_moba_v, cache_sb_k, cache_sb_v,
           state_gdn_conv, state_gdn_rec, state_ssm_conv, state_ssm_rec, page_table,
           w_in, gdn_conv_w, gdn_a_log, gdn_dt_bias, gdn_norm_w,
           ssm_conv_w, ssm_conv_b, ssm_a_log, ssm_dt_bias, ssm_d, ssm_norm_w,
           w_out, ln1_g, ln1_b, router_w, router_b,
           expert_w_gu, expert_b_gu, expert_w_down, expert_b_down, ln2_g, ln2_b):
    caches = tuple(c.reshape(c.shape[0], c.shape[1], PAGE_SIZE, GROUP_WIDTH)
                   for c in (cache_moba_k, cache_moba_v, cache_sb_k, cache_sb_v))
    yp, ys = x_prompt, x_sample
    new_p, new_s = [], []
    for l in range(DEPTH):
        lw = (w_in[l], gdn_conv_w[l], gdn_a_log[l], gdn_dt_bias[l], gdn_norm_w[l], ssm_conv_w[l], ssm_conv_b[l],
              ssm_a_log[l], ssm_dt_bias[l], ssm_d[l], ssm_norm_w[l], w_out[l])
        fw = (ln1_g[l], ln1_b[l], router_w[l], router_b[l], expert_w_gu[l], expert_b_gu[l],
              expert_w_down[l], expert_b_down[l], ln2_g[l], ln2_b[l])
        rec_state = (state_gdn_conv[l], state_gdn_rec[l], state_ssm_conv[l], state_ssm_rec[l])
        outs_p, st_p = _prompt_mixers(yp, lw)
        yp = _post_mix_moe(outs_p, yp.reshape(-1, D_MODEL), w_out[l], fw).reshape(yp.shape)
        outs_s, st_s = _sample_mixers(ys, l, page_table, rec_state, caches, lw)
        ys = _post_mix_moe(outs_s, ys.reshape(-1, D_MODEL), w_out[l], fw).reshape(ys.shape)
        new_p.append(st_p)
        new_s.append(st_s)

    def stk(states, i):
        return jnp.stack([s[i] for s in states])

    return (yp, ys,
            stk(new_p, 0), stk(new_p, 1), stk(new_p, 2), stk(new_p, 3),
            stk(new_p, 4), stk(new_p, 5), stk(new_p, 6), stk(new_p, 7),
            stk(new_s, 0), stk(new_s, 1), stk(new_s, 2), stk(new_s, 3),
            stk(new_s, 4), stk(new_s, 5), stk(new_s, 6), stk(new_s, 7))
```

```python
import functools

import jax
import jax.numpy as jnp
from jax import lax
from jax.experimental import pallas as pl
from jax.experimental.pallas import tpu as pltpu

D_MODEL = 1024
DEPTH = 2
PAGE_SIZE = 128
HEAD_DIM = 64
N_MIXERS = 4
GROUP_WIDTH = D_MODEL // N_MIXERS
N_HEADS = GROUP_WIDTH // HEAD_DIM
CONV_W = 4
CHUNK = 64
GDN_CONV_DIM = 3 * GROUP_WIDTH
SSM_STATE = 128
SSM_NGROUPS = 2
SSM_CONV_DIM = GROUP_WIDTH + 2 * SSM_NGROUPS * SSM_STATE
MOBA_BLOCK = 256
MOBA_TOPK = 3
MOBA_QUERY_BLOCK = 64
ROPE_THETA = 500000.0
ROPE_DIM = HEAD_DIM // 4
QUERY_BLOCK = 128
N_EXPERTS = 32
TOP_K = 4
D_FF = D_MODEL
SWIGLU_LIMIT = 7.0
SWIGLU_ALPHA = 1.702
EXPERT_BLOCK = 256
DN_ALPHA = (2 * DEPTH) ** 0.25
LN_EPS = 1e-5
RMS_EPS = 1e-6

F32 = jnp.float32
BF16 = jnp.bfloat16
I32 = jnp.int32
HI = lax.Precision.HIGHEST

LANE = 128
VMEM_LIMIT = 56 * 1024 * 1024
NEG = -1e30

_SEG_SIZES = (GDN_CONV_DIM, GROUP_WIDTH, N_HEADS, N_HEADS, SSM_CONV_DIM, GROUP_WIDTH, N_HEADS, GDN_CONV_DIM, GDN_CONV_DIM)
_SEG_NAMES = ("gdn_qkv", "gdn_z", "gdn_b", "gdn_a", "ssm_xbc", "ssm_z", "ssm_dt", "moba_qkv", "sb_qkv")
_OUT_SEGS = (("gdn_qkv", GDN_CONV_DIM), ("gdn_z", GROUP_WIDTH), ("ssm_xbc", SSM_CONV_DIM), ("ssm_z", GROUP_WIDTH),
             ("moba_q", GROUP_WIDTH), ("moba_k", GROUP_WIDTH), ("moba_v", GROUP_WIDTH),
             ("sb_q", GROUP_WIDTH), ("sb_k", GROUP_WIDTH), ("sb_v", GROUP_WIDTH), ("small", LANE))
_ROTARY_SEGS = ("moba_q", "moba_k")


def _softplus(z):
    return jnp.maximum(z, 0.0) + jnp.log1p(jnp.exp(-jnp.abs(z)))


def _sigmoid(x):
    return 1.0 / (1.0 + jnp.exp(-x))


def _silu(x):
    return x * _sigmoid(x)


def _dot(a, b):
    return jnp.dot(a.astype(BF16), b.astype(BF16), preferred_element_type=F32)


def _dot_nt(a, b):
    return lax.dot_general(a.astype(BF16), b.astype(BF16), (((1,), (1,)), ((), ())), preferred_element_type=F32)


def _dot_tn(a, b):
    return lax.dot_general(a.astype(BF16), b.astype(BF16), (((0,), (0,)), ((), ())), preferred_element_type=F32)


def _dot_hi(a, b):
    return jnp.dot(a, b, preferred_element_type=F32, precision=HI)


def _split_dot(x, m_bf16):
    hi = x.astype(BF16)
    lo = (x - hi.astype(F32)).astype(BF16)
    return (jnp.dot(hi, m_bf16, preferred_element_type=F32)
            + jnp.dot(lo, m_bf16, preferred_element_type=F32))


def _split3(x):
    hi = x.astype(BF16)
    r = x - hi.astype(F32)
    mid = r.astype(BF16)
    return hi, mid, (r - mid.astype(F32)).astype(BF16)


def _dot3(a, b):
    a_hi = a.astype(BF16)
    a_lo = (a - a_hi.astype(F32)).astype(BF16)
    b_hi = b.astype(BF16)
    b_lo = (b - b_hi.astype(F32)).astype(BF16)
    return (jnp.dot(a_hi, b_hi, preferred_element_type=F32) + jnp.dot(a_hi, b_lo, preferred_element_type=F32)
            + jnp.dot(a_lo, b_hi, preferred_element_type=F32))


def _ln(h, g, b):
    mu = jnp.mean(h, axis=-1, keepdims=True)
    d = h - mu
    var = jnp.mean(d * d, axis=-1, keepdims=True)
    return d * lax.rsqrt(var + LN_EPS) * g + b


def _in_proj_kernel(x_ref, w_ref, cos_ref, sin_ref, *o_refs):
    y = jnp.dot(x_ref[...].astype(BF16), w_ref[...], preferred_element_type=F32)
    cos = jnp.concatenate([cos_ref[...]] * (GROUP_WIDTH // LANE), axis=1)
    sin = jnp.concatenate([sin_ref[...]] * (GROUP_WIDTH // LANE), axis=1)
    first_half = (lax.broadcasted_iota(I32, (1, GROUP_WIDTH), 1) % HEAD_DIM) < ROPE_DIM // 2
    off = 0
    for (name, width), o_ref in zip(_OUT_SEGS, o_refs):
        seg = y[:, off:off + width]
        if name in _ROTARY_SEGS:
            partner = jnp.where(first_half, pltpu.roll(seg, GROUP_WIDTH - ROPE_DIM // 2, 1), pltpu.roll(seg, ROPE_DIM // 2, 1))
            seg = seg * cos + partner * sin
        o_ref[...] = seg
        off += width


def _rotary_tables(pos):
    half = ROPE_DIM // 2
    inv = ROPE_THETA ** (-jnp.arange(half, dtype=F32) * 2.0 / ROPE_DIM)
    ang = pos.astype(F32)[:, None] * inv
    cc = jnp.arange(LANE) % HEAD_DIM
    cos = jnp.where(cc < ROPE_DIM, jnp.cos(ang)[:, cc % half], 1.0)
    sin = jnp.sin(ang)[:, cc % half]
    sin = jnp.where(cc < half, -sin, jnp.where(cc < ROPE_DIM, sin, 0.0))
    return cos, sin


def _in_proj(x, w_in, pos, seq_len):
    m = x.shape[0]
    tm = min(256, m)
    cols, off = {}, 0
    for name, sz in zip(_SEG_NAMES, _SEG_SIZES):
        cols[name] = w_in[:, off:off + sz]
        off += sz
    for name in ("moba", "sb"):
        for j, part in enumerate("qkv"):
            cols[f"{name}_{part}"] = cols[f"{name}_qkv"][:, j * GROUP_WIDTH:(j + 1) * GROUP_WIDTH]
    cols["small"] = jnp.concatenate([cols["gdn_b"], cols["gdn_a"], cols["ssm_dt"],
                                     jnp.zeros((D_MODEL, LANE - 3 * N_HEADS), w_in.dtype)], axis=1)
    w = jnp.concatenate([cols[name] for name, _ in _OUT_SEGS], axis=1).astype(BF16)
    n = w.shape[1]
    cos, sin = _rotary_tables(pos)
    if seq_len == 1:
        tab_spec = pl.BlockSpec((1, LANE), lambda i: (0, 0))
    else:
        assert seq_len % tm == 0
        tab_spec = pl.BlockSpec((tm, LANE), lambda i: (i % (seq_len // tm), 0))
    outs = pl.pallas_call(
        _in_proj_kernel,
        grid=(m // tm,),
        in_specs=[pl.BlockSpec((tm, D_MODEL), lambda i: (i, 0)),
                  pl.BlockSpec((D_MODEL, n), lambda i: (0, 0)), tab_spec, tab_spec],
        out_specs=[pl.BlockSpec((tm, width), lambda i: (i, 0)) for _, width in _OUT_SEGS],
        out_shape=[jax.ShapeDtypeStruct((m, width), F32) for _, width in _OUT_SEGS],
        compiler_params=pltpu.CompilerParams(dimension_semantics=("arbitrary",), vmem_limit_bytes=VMEM_LIMIT),
        name="in_proj",
    )(x, w, cos, sin)
    return {name: o for (name, _), o in zip(_OUT_SEGS, outs)}


def _sb_kernel(q_ref, k_ref, v_ref, o_ref, *, tq):
    i = pl.program_id(1)
    row = lax.broadcasted_iota(I32, (tq, tq), 0)
    col = lax.broadcasted_iota(I32, (tq, tq), 1)
    upper = (row > col).astype(BF16)
    past = col < row

    heads = [slice(h * HEAD_DIM, (h + 1) * HEAD_DIM) for h in range(N_HEADS)]
    qs = [(q_ref[0, :, hs] * (HEAD_DIM ** -0.5)).astype(BF16) for hs in heads]

    def block(j, state, diag):
        out = []
        for hs, q, (carry, acc) in zip(heads, qs, state):
            kj = k_ref[0, pl.ds(j * tq, tq), hs].astype(BF16)
            vj = v_ref[0, pl.ds(j * tq, tq), hs].astype(BF16)
            z = lax.dot_general(q, kj, (((1,), (1,)), ((), ())), preferred_element_type=F32)
            ls = -_softplus(z)
            lk = jnp.where(past, ls, 0.0) if diag else ls
            la = _split_dot(lk, upper)
            w = jnp.exp(z + ls + la + carry)
            if diag:
                w = jnp.where(past, w, 0.0)
            acc = acc + jnp.dot(w.astype(BF16), vj, preferred_element_type=F32)
            out.append((carry + la[:, 0:1] + lk[:, 0:1], acc))
        return tuple(out)

    state = tuple((jnp.zeros((tq, 1), F32), jnp.zeros((tq, HEAD_DIM), F32)) for _ in heads)
    state = block(i, state, True)
    state = lax.fori_loop(0, i, lambda t, c: block(i - 1 - t, c, False), state)
    for hs, (_, acc) in zip(heads, state):
        o_ref[0, :, hs] = acc


def _sb_attention(q, k, v, tq=256):
    b, l, d = q.shape
    assert l % tq == 0
    return pl.pallas_call(
        functools.partial(_sb_kernel, tq=tq),
        grid=(b, l // tq),
        in_specs=[pl.BlockSpec((1, tq, d), lambda bi, i: (bi, i, 0)),
                  pl.BlockSpec((1, l, d), lambda bi, i: (bi, 0, 0)),
                  pl.BlockSpec((1, l, d), lambda bi, i: (bi, 0, 0))],
        out_specs=pl.BlockSpec((1, tq, d), lambda bi, i: (bi, i, 0)),
        out_shape=jax.ShapeDtypeStruct((b, l, d), F32),
        compiler_params=pltpu.CompilerParams(dimension_semantics=("arbitrary",) * 2, vmem_limit_bytes=VMEM_LIMIT),
        name="sb_attention",
    )(q, k, v)


def _moba_kernel(q_ref, k_ref, v_ref, o_ref, kmean_ref, *, nb):
    t = MOBA_BLOCK
    i = pl.program_id(1)

    @pl.when(i == 0)
    def _():
        for n in range(nb):
            kmean_ref[n:n + 1, :] = jnp.mean(k_ref[0, n * t:(n + 1) * t, :], axis=0, keepdims=True)

    blk_id = lax.broadcasted_iota(I32, (nb, t), 0)
    row = lax.broadcasted_iota(I32, (t, t), 0)
    col = lax.broadcasted_iota(I32, (t, t), 1)

    heads = [slice(h * HEAD_DIM, (h + 1) * HEAD_DIM) for h in range(N_HEADS)]
    qs, sels, state = [], [], []
    for hs in heads:
        qf = q_ref[0, :, hs]
        q = (qf * (HEAD_DIM ** -0.5)).astype(BF16)
        gate = lax.dot_general(kmean_ref[:, hs], qf, (((1,), (1,)), ((), ())),
                               preferred_element_type=F32, precision=HI)
        cnt = jnp.zeros((nb, t), F32)
        for n2 in range(nb):
            cn = gate[n2:n2 + 1, :]
            beats = (cn > gate) | ((cn == gate) & (n2 < blk_id))
            cnt = cnt + jnp.where(beats & (n2 < i), 1.0, 0.0)
        sels.append(jnp.where((cnt < MOBA_TOPK) & (blk_id < i), 1.0, 0.0).astype(BF16))
        qs.append(q)

        ki = k_ref[0, pl.ds(i * t, t), hs].astype(BF16)
        vi = v_ref[0, pl.ds(i * t, t), hs].astype(BF16)
        s = lax.dot_general(q, ki, (((1,), (1,)), ((), ())), preferred_element_type=F32)
        s = jnp.where(col <= row, s, NEG)
        m = jnp.max(s, axis=-1, keepdims=True)
        p = jnp.exp(s - m)
        state.append((m, jnp.sum(p, axis=-1, keepdims=True), jnp.dot(p.astype(BF16), vi, preferred_element_type=F32)))

    def body(j, state):
        pick = ((lax.broadcasted_iota(I32, (nb, 8), 0) == j) & (lax.broadcasted_iota(I32, (nb, 8), 1) == 0)).astype(BF16)
        out = []
        for hs, q, sel, (m, l, acc) in zip(heads, qs, sels, state):
            kj = k_ref[0, pl.ds(j * t, t), hs].astype(BF16)
            vj = v_ref[0, pl.ds(j * t, t), hs].astype(BF16)
            s = lax.dot_general(q, kj, (((1,), (1,)), ((), ())), preferred_element_type=F32)
            sj = lax.dot_general(sel, pick, (((0,), (0,)), ((), ())), preferred_element_type=F32)[:, 0:1]
            s = jnp.where(sj > 0.5, s, NEG)
            m_new = jnp.maximum(m, jnp.max(s, axis=-1, keepdims=True))
            a = jnp.exp(m - m_new)
            p = jnp.exp(s - m_new)
            out.append((m_new, a * l + jnp.sum(p, axis=-1, keepdims=True),
                        a * acc + jnp.dot(p.astype(BF16), vj, preferred_element_type=F32)))
        return tuple(out)

    state = lax.fori_loop(0, i, body, tuple(state))
    for hs, (_, l, acc) in zip(heads, state):
        o_ref[0, :, hs] = acc / l


def _moba_attention(q, k, v):
    b, l, d = q.shape
    t = MOBA_BLOCK
    assert l % t == 0
    nb = l // t
    return pl.pallas_call(
        functools.partial(_moba_kernel, nb=nb),
        grid=(b, nb),
        in_specs=[pl.BlockSpec((1, t, d), lambda bi, i: (bi, i, 0)),
                  pl.BlockSpec((1, l, d), lambda bi, i: (bi, 0, 0)),
                  pl.BlockSpec((1, l, d), lambda bi, i: (bi, 0, 0))],
        out_specs=pl.BlockSpec((1, t, d), lambda bi, i: (bi, i, 0)),
        out_shape=jax.ShapeDtypeStruct((b, l, d), F32),
        scratch_shapes=[pltpu.VMEM((nb, d), F32)],
        compiler_params=pltpu.CompilerParams(dimension_semantics=("arbitrary",) * 2, vmem_limit_bytes=VMEM_LIMIT),
        name="moba_attention",
    )(q, k, v)


def _conv_tile(ext_ref, u, w_ref, tl):
    ext_ref[8:8 + tl, :] = u
    y = ext_ref[5:5 + tl, :] * w_ref[0:1, :]
    for i in range(1, CONV_W):
        y = y + ext_ref[5 + i:5 + i + tl, :] * w_ref[i:i + 1, :]
    tail = ext_ref[tl + 5:tl + 8, :]
    ext_ref[5:8, :] = tail
    return y, tail


def _gdn_kernel(qkv_ref, z_ref, small_ref, convw_ref, par_ref, normw_ref,
                o_ref, conv_out_ref, s_out_ref, ext_ref, s_ref, *, tl):
    c = CHUNK
    step = pl.program_id(1)

    @pl.when(step == 0)
    def _():
        ext_ref[...] = jnp.zeros_like(ext_ref)
        s_ref[...] = jnp.zeros_like(s_ref)

    y, tail = _conv_tile(ext_ref, qkv_ref[0], convw_ref, tl)
    conv_out_ref[0] = tail
    y = _silu(y)
    small = small_ref[0]
    beta_all = _sigmoid(small[:, 0:N_HEADS])
    g_all = par_ref[0:1, :] * _softplus(small[:, N_HEADS:2 * N_HEADS] + par_ref[1:2, :])

    row = lax.broadcasted_iota(I32, (tl, tl), 0)
    col = lax.broadcasted_iota(I32, (tl, tl), 1)
    same = (row // c) == (col // c)
    tril = same & (row >= col)
    strict = same & (row > col)
    tril_b = tril.astype(BF16)
    triu_b = (same & (row <= col)).astype(BF16)
    eye = (row == col).astype(F32)

    g8 = jnp.concatenate([g_all, jnp.zeros((tl, 8 - N_HEADS), F32)], axis=1)
    g_parts = _split3(g8)
    gc_cols = sum(jnp.dot(tril_b, part, preferred_element_type=F32) for part in g_parts)
    gc_rows = sum(lax.dot_general(part, triu_b, (((0,), (0,)), ((), ())), preferred_element_type=F32) for part in g_parts)

    for h in range(N_HEADS):
        hs = slice(h * HEAD_DIM, (h + 1) * HEAD_DIM)
        q = y[:, h * HEAD_DIM:(h + 1) * HEAD_DIM]
        k = y[:, GROUP_WIDTH + h * HEAD_DIM:GROUP_WIDTH + (h + 1) * HEAD_DIM]
        v = y[:, 2 * GROUP_WIDTH + h * HEAD_DIM:2 * GROUP_WIDTH + (h + 1) * HEAD_DIM]
        q = q * lax.rsqrt(jnp.sum(q * q, axis=-1, keepdims=True) + RMS_EPS) * (HEAD_DIM ** -0.5)
        k = k * lax.rsqrt(jnp.sum(k * k, axis=-1, keepdims=True) + RMS_EPS)
        beta = beta_all[:, h:h + 1]
        gcol = gc_cols[:, h:h + 1]
        grow = gc_rows[h:h + 1, :]
        decay = jnp.where(tril, jnp.exp(jnp.where(tril, gcol - grow, 0.0)), 0.0)
        kb = k * beta
        a = jnp.where(strict, _dot_nt(kb, k) * decay, 0.0)
        p = -a
        tinv = eye + p
        for _ in range(5):
            p = _dot3(p, p)
            tinv = tinv + _dot3(tinv, p)
        u = _dot(tinv, v * beta)
        w = _dot(tinv, kb * jnp.exp(gcol))
        attn = jnp.where(tril, _dot_nt(q, k) * decay, 0.0)
        qe = q * jnp.exp(gcol)
        s = s_ref[h]
        for ci in range(tl // c):
            rs = slice(ci * c, (ci + 1) * c)
            g_last = gcol[(ci + 1) * c - 1:(ci + 1) * c, :]
            v_new = u[rs] - _dot(w[rs], s)
            o = _dot(qe[rs], s) + _dot(attn[rs, rs], v_new)
            s = s * jnp.exp(g_last) + _dot_tn(k[rs] * jnp.exp(g_last - gcol[rs]), v_new)
            o = o * lax.rsqrt(jnp.mean(o * o, axis=-1, keepdims=True) + RMS_EPS) * normw_ref[...] * _silu(z_ref[0, rs, hs])
            o_ref[0, rs, hs] = o
        s_ref[h] = s

    @pl.when(step == pl.num_programs(1) - 1)
    def _():
        s_out_ref[0] = s_ref[...]


def _gdn_prompt(qkv, z, small, conv_w, a_log, dt_bias, norm_w, tl=256):
    b, l, _ = qkv.shape
    assert l % tl == 0 and tl % CHUNK == 0
    par = jnp.stack([-jnp.exp(a_log.astype(F32)), dt_bias.astype(F32)])
    return pl.pallas_call(
        functools.partial(_gdn_kernel, tl=tl),
        grid=(b, l // tl),
        in_specs=[pl.BlockSpec((1, tl, GDN_CONV_DIM), lambda bi, i: (bi, i, 0)),
                  pl.BlockSpec((1, tl, GROUP_WIDTH), lambda bi, i: (bi, i, 0)),
                  pl.BlockSpec((1, tl, LANE), lambda bi, i: (bi, i, 0)),
                  pl.BlockSpec((CONV_W, GDN_CONV_DIM), lambda bi, i: (0, 0)),
                  pl.BlockSpec((2, N_HEADS), lambda bi, i: (0, 0)),
                  pl.BlockSpec((1, HEAD_DIM), lambda bi, i: (0, 0))],
        out_specs=[pl.BlockSpec((1, tl, GROUP_WIDTH), lambda bi, i: (bi, i, 0)),
                   pl.BlockSpec((1, CONV_W - 1, GDN_CONV_DIM), lambda bi, i: (bi, 0, 0)),
                   pl.BlockSpec((1, N_HEADS, HEAD_DIM, HEAD_DIM), lambda bi, i: (bi, 0, 0, 0))],
        out_shape=[jax.ShapeDtypeStruct((b, l, GROUP_WIDTH), F32),
                   jax.ShapeDtypeStruct((b, CONV_W - 1, GDN_CONV_DIM), F32),
                   jax.ShapeDtypeStruct((b, N_HEADS, HEAD_DIM, HEAD_DIM), F32)],
        scratch_shapes=[pltpu.VMEM((tl + 8, GDN_CONV_DIM), F32),
                        pltpu.VMEM((N_HEADS, HEAD_DIM, HEAD_DIM), F32)],
        compiler_params=pltpu.CompilerParams(dimension_semantics=("arbitrary", "arbitrary"),
                                             vmem_limit_bytes=VMEM_LIMIT),
        name="gdn_prompt",
    )(qkv, z, small, conv_w, par, norm_w.reshape(1, HEAD_DIM))


def _ssd_kernel(xbc_ref, z_ref, small_ref, convw_ref, convb_ref, par_ref, normw_ref,
                o_ref, conv_out_ref, h_out_ref, ext_ref, h_ref, *, tl):
    c = CHUNK
    step = pl.program_id(1)
    hpg = N_HEADS // SSM_NGROUPS

    @pl.when(step == 0)
    def _():
        ext_ref[...] = jnp.zeros_like(ext_ref)
        h_ref[...] = jnp.zeros_like(h_ref)

    y, tail = _conv_tile(ext_ref, xbc_ref[0], convw_ref, tl)
    conv_out_ref[0] = tail
    xbc = _silu(y + convb_ref[...])
    dt_all = _softplus(small_ref[0][:, 2 * N_HEADS:3 * N_HEADS] + par_ref[1:2, :])
    ad_all = dt_all * par_ref[0:1, :]

    row = lax.broadcasted_iota(I32, (c, c), 0)
    col = lax.broadcasted_iota(I32, (c, c), 1)
    tril = row >= col
    tril_f = tril.astype(F32)
    triu_f = (row <= col).astype(F32)
    ones = jnp.ones((c, c), F32)
    boff = GROUP_WIDTH
    coff = GROUP_WIDTH + SSM_NGROUPS * SSM_STATE

    for ci in range(tl // c):
        r0 = ci * c
        ad_c = ad_all[r0:r0 + c, :]
        acs_all = _dot_hi(tril_f, ad_c)
        ys = []
        for h in range(N_HEADS):
            grp = h // hpg
            x = xbc[r0:r0 + c, h * HEAD_DIM:(h + 1) * HEAD_DIM]
            bm = xbc[r0:r0 + c, boff + grp * SSM_STATE:boff + (grp + 1) * SSM_STATE]
            cm = xbc[r0:r0 + c, coff + grp * SSM_STATE:coff + (grp + 1) * SSM_STATE]
            xdt = x * dt_all[r0:r0 + c, h:h + 1]
            acol = acs_all[:, h:h + 1]
            arow = _dot_hi(ones, ad_c[:, h:h + 1] * triu_f)
            lmat = jnp.where(tril, jnp.exp(jnp.where(tril, acol - arow, 0.0)), 0.0)
            y_diag = _dot(_dot_nt(cm, bm) * lmat, xdt)
            a_last = acol[c - 1:c, :]
            states = _dot_tn(xdt * jnp.exp(a_last - acol), bm)
            hin = h_ref[h]
            y_off = _dot_nt(cm, hin) * jnp.exp(acol)
            h_ref[h] = hin * jnp.exp(a_last) + states
            yh = y_diag + y_off + par_ref[2:3, h:h + 1] * x
            ys.append(yh * _silu(z_ref[0, r0:r0 + c, h * HEAD_DIM:(h + 1) * HEAD_DIM]))
        for grp in range(SSM_NGROUPS):
            grp_ys = ys[grp * hpg:(grp + 1) * hpg]
            ms = sum(jnp.sum(a * a, axis=-1, keepdims=True) for a in grp_ys) / (hpg * HEAD_DIM)
            r = lax.rsqrt(ms + RMS_EPS)
            for j, a in enumerate(grp_ys):
                h = grp * hpg + j
                o_ref[0, r0:r0 + c, h * HEAD_DIM:(h + 1) * HEAD_DIM] = a * r * normw_ref[:, h * HEAD_DIM:(h + 1) * HEAD_DIM]

    @pl.when(step == pl.num_programs(1) - 1)
    def _():
        h_out_ref[0] = h_ref[...]


def _ssd_prompt(xbc, z, small, conv_w, conv_b, a_log, dt_bias, d_skip, norm_w, tl=256):
    b, l, _ = xbc.shape
    assert l % tl == 0 and tl % CHUNK == 0
    par = jnp.stack([-jnp.exp(a_log.astype(F32)), dt_bias.astype(F32), d_skip.astype(F32)])
    return pl.pallas_call(
        functools.partial(_ssd_kernel, tl=tl),
        grid=(b, l // tl),
        in_specs=[pl.BlockSpec((1, tl, SSM_CONV_DIM), lambda bi, i: (bi, i, 0)),
                  pl.BlockSpec((1, tl, GROUP_WIDTH), lambda bi, i: (bi, i, 0)),
                  pl.BlockSpec((1, tl, LANE), lambda bi, i: (bi, i, 0)),
                  pl.BlockSpec((CONV_W, SSM_CONV_DIM), lambda bi, i: (0, 0)),
                  pl.BlockSpec((1, SSM_CONV_DIM), lambda bi, i: (0, 0)),
                  pl.BlockSpec((3, N_HEADS), lambda bi, i: (0, 0)),
                  pl.BlockSpec((1, GROUP_WIDTH), lambda bi, i: (0, 0))],
        out_specs=[pl.BlockSpec((1, tl, GROUP_WIDTH), lambda bi, i: (bi, i, 0)),
                   pl.BlockSpec((1, CONV_W - 1, SSM_CONV_DIM), lambda bi, i: (bi, 0, 0)),
                   pl.BlockSpec((1, N_HEADS, HEAD_DIM, SSM_STATE), lambda bi, i: (bi, 0, 0, 0))],
        out_shape=[jax.ShapeDtypeStruct((b, l, GROUP_WIDTH), F32),
                   jax.ShapeDtypeStruct((b, CONV_W - 1, SSM_CONV_DIM), F32),
                   jax.ShapeDtypeStruct((b, N_HEADS, HEAD_DIM, SSM_STATE), F32)],
        scratch_shapes=[pltpu.VMEM((tl + 8, SSM_CONV_DIM), F32),
                        pltpu.VMEM((N_HEADS, HEAD_DIM, SSM_STATE), F32)],
        compiler_params=pltpu.CompilerParams(dimension_semantics=("arbitrary", "arbitrary"),
                                             vmem_limit_bytes=VMEM_LIMIT),
        name="ssd_prompt",
    )(xbc, z, small, conv_w, conv_b.reshape(1, SSM_CONV_DIM), par, norm_w.reshape(1, GROUP_WIDTH))


def _post_mix_kernel(o0_ref, o1_ref, o2_ref, o3_ref, x_ref, wout_ref, g_ref, b_ref, rw_ref, rb_ref,
                     xn_ref, xnb_ref, idx_ref, gate_ref, rank_ref, cum_ref, cnt_ref, *, tm):
    @pl.when(pl.program_id(0) == 0)
    def _():
        cnt_ref[...] = jnp.zeros_like(cnt_ref)

    mix = jnp.zeros((tm, D_MODEL), F32)
    for gi, o_ref in enumerate((o0_ref, o1_ref, o2_ref, o3_ref)):
        mix = mix + jnp.dot(o_ref[...].astype(BF16), wout_ref[gi * GROUP_WIDTH:(gi + 1) * GROUP_WIDTH, :],
                            preferred_element_type=F32)
    xn = _ln(DN_ALPHA * x_ref[...] + mix, g_ref[...], b_ref[...])
    xn_ref[...] = xn
    xnb_ref[...] = xn.astype(BF16)

    logits = _dot_hi(xn, rw_ref[...]) + rb_ref[...]
    lane_e = lax.broadcasted_iota(I32, (tm, N_EXPERTS), 1).astype(F32)
    lane = lax.broadcasted_iota(I32, (tm, LANE), 1)
    work = logits
    sel = jnp.zeros((tm, N_EXPERTS), F32)
    idx_out = jnp.zeros((tm, LANE), I32)
    val_out = jnp.zeros((tm, LANE), F32)
    hits = []
    for r in range(TOP_K):
        m = jnp.max(work, axis=-1, keepdims=True)
        pick = jnp.min(jnp.where(work == m, lane_e, float(N_EXPERTS)), axis=-1, keepdims=True)
        hit = lane_e == pick
        work = jnp.where(hit, -jnp.inf, work)
        sel = sel + hit.astype(F32)
        hits.append(hit)
        idx_out = jnp.where(lane == r, pick.astype(I32), idx_out)
        val_out = jnp.where(lane == r, m, val_out)
    ex = jnp.where(lane < TOP_K, jnp.exp(val_out - val_out[:, 0:1]), 0.0)
    gate_ref[...] = ex / jnp.sum(ex, axis=-1, keepdims=True)
    idx_ref[...] = idx_out

    row = lax.broadcasted_iota(I32, (tm, tm), 0)
    col = lax.broadcasted_iota(I32, (tm, tm), 1)
    before = jnp.dot((row > col).astype(BF16), sel.astype(BF16), preferred_element_type=F32) + cnt_ref[...]
    rank_out = jnp.zeros((tm, LANE), I32)
    for r in range(TOP_K):
        rk = jnp.sum(jnp.where(hits[r], before, 0.0), axis=-1, keepdims=True)
        rank_out = jnp.where(lane == r, rk.astype(I32), rank_out)
    rank_ref[...] = rank_out
    cum_ref[0] = jnp.broadcast_to(cnt_ref[...], (8, N_EXPERTS))
    cnt_ref[...] = cnt_ref[...] + jnp.sum(sel, axis=0, keepdims=True)


def _post_mix(outs, x, w_out_bf16, ln_g, ln_b, router_w, router_b, tm):
    n = x.shape[0]
    nt = n // tm
    tok = lambda w: pl.BlockSpec((tm, w), lambda i: (i, 0))
    full = lambda a: pl.BlockSpec(a.shape, lambda i: (0,) * a.ndim)
    g2, b2, rb2 = ln_g.reshape(1, -1), ln_b.reshape(1, -1), router_b.reshape(1, -1)
    return pl.pallas_call(
        functools.partial(_post_mix_kernel, tm=tm),
        grid=(nt,),
        in_specs=[tok(GROUP_WIDTH)] * 4 + [tok(D_MODEL), full(w_out_bf16), full(g2), full(b2), full(router_w), full(rb2)],
        out_specs=[tok(D_MODEL), tok(D_MODEL), tok(LANE), tok(LANE), tok(LANE),
                   pl.BlockSpec((1, 8, N_EXPERTS), lambda i: (i, 0, 0))],
        out_shape=[jax.ShapeDtypeStruct((n, D_MODEL), F32), jax.ShapeDtypeStruct((n, D_MODEL), BF16),
                   jax.ShapeDtypeStruct((n, LANE), I32), jax.ShapeDtypeStruct((n, LANE), F32),
                   jax.ShapeDtypeStruct((n, LANE), I32), jax.ShapeDtypeStruct((nt, 8, N_EXPERTS), F32)],
        scratch_shapes=[pltpu.VMEM((1, N_EXPERTS), F32)],
        compiler_params=pltpu.CompilerParams(dimension_semantics=("arbitrary",), vmem_limit_bytes=VMEM_LIMIT),
        name="post_mix",
    )(*outs, x, w_out_bf16, g2, b2, router_w, rb2)


def _moe_dispatch(idx, gates, rank, cum, n_tok, tm_cum, tm, blk):
    nt = n_tok // tm
    n_blocks = -(-n_tok * TOP_K // blk) + N_EXPERTS
    per_pair = -(-tm // blk) + 1
    e_idx = idx[:, :TOP_K]
    cum_i = cum[:, 0, :].astype(I32)
    onehot = e_idx[:, :, None] == jnp.arange(N_EXPERTS, dtype=I32)
    counts = cum_i[-1] + jnp.sum(onehot[n_tok - tm_cum:].astype(I32), axis=(0, 1))
    cum_t = cum_i[::tm // tm_cum]
    cum_all = jnp.concatenate([cum_t, counts[None]], 0)
    blocks_per_e = (counts + blk - 1) // blk
    blk_start = jnp.cumsum(blocks_per_e) - blocks_per_e
    dest = jnp.sum(jnp.where(onehot, blk_start, 0), axis=-1) * blk + rank[:, :TOP_K]
    dest_pad = jnp.pad(dest, ((0, 0), (0, LANE - TOP_K)), constant_values=-1)
    dest_t = jnp.pad(dest.T, ((0, 8 - TOP_K), (0, 0)), constant_values=-1)
    gate_t = jnp.pad(gates[:, :TOP_K].T, ((0, 8 - TOP_K), (0, 0)))

    lo, hi = cum_all[:-1].T, cum_all[1:].T
    has = hi > lo
    j0 = lo // blk
    j1 = jnp.maximum(hi - 1, 0) // blk
    jj = j0[:, :, None] + jnp.arange(per_pair, dtype=I32)
    valid = has[:, :, None] & (jj <= j1[:, :, None])
    blk_id = blk_start[:, None, None] + jj
    e_b = jnp.broadcast_to(jnp.arange(N_EXPERTS, dtype=I32)[:, None, None], jj.shape)
    t_b = jnp.broadcast_to(jnp.arange(nt, dtype=I32)[None, :, None], jj.shape)
    n_steps = min(N_EXPERTS * nt + n_blocks, per_pair * N_EXPERTS * nt)
    n_valid = jnp.sum(valid)
    pos = jnp.arange(n_steps)
    v = pos < n_valid

    def build(key):
        order = jnp.argsort(jnp.where(valid, key, jnp.int32(1 << 30)).reshape(-1))[:n_steps]
        src = order[jnp.minimum(pos, n_valid - 1)]
        return blk_id.reshape(-1)[src], t_b.reshape(-1)[src], e_b.reshape(-1)[src]

    def edges(a):
        change = a[1:] != a[:-1]
        first = jnp.concatenate([jnp.ones((1,), bool), change]) & v
        last = (jnp.concatenate([change, jnp.ones((1,), bool)]) | (pos == n_valid - 1)) & v
        return first, last

    b1, t1, e1 = build(blk_id * nt + t_b)
    first_b, last_b = edges(b1)
    first_e, _ = edges(e1)
    flags1 = first_b.astype(I32) + 2 * last_b.astype(I32) + 4 * v.astype(I32) + 8 * first_e.astype(I32)
    b2, t2, _ = build(t_b * (2 * n_blocks) + blk_id)
    first_t, last_t = edges(t2)
    flags2 = first_t.astype(I32) + 2 * last_t.astype(I32) + 4 * v.astype(I32)
    return dict(dest=dest_pad, dest_t=dest_t, gate_t=gate_t, n_rows=n_blocks * blk, n_steps=n_steps,
                gather=(b1.astype(I32), t1.astype(I32), e1.astype(I32), flags1),
                combine=(b2.astype(I32), t2.astype(I32), flags2))


def _expert_kernel(blk_s, tile_s, exp_s, flag_s, x_ref, dest_ref, gate_ref, wgu_ref, bgu_ref, wdn_ref, bdn_ref,
                   yb_ref, xacc_ref, gacc_ref, wgu_bf, wdn_bf, *, blk, tm):
    s = pl.program_id(0)
    flags = flag_s[s]

    @pl.when((flags & 8) != 0)
    def _():
        wgu_bf[...] = wgu_ref[0].astype(BF16)
        wdn_bf[...] = wdn_ref[0].astype(BF16)

    @pl.when((flags & 1) != 0)
    def _():
        xacc_ref[...] = jnp.zeros_like(xacc_ref)
        gacc_ref[...] = jnp.zeros_like(gacc_ref)

    @pl.when((flags & 4) != 0)
    def _():
        rowid = blk_s[s] * blk + lax.broadcasted_iota(I32, (blk, tm), 0)
        d = dest_ref[...]
        g = gate_ref[...]
        p = jnp.zeros((blk, tm), F32)
        pg = jnp.zeros((blk, tm), F32)
        for k in range(TOP_K):
            hit = d[k:k + 1, :] == rowid
            p = jnp.where(hit, 1.0, p)
            pg = jnp.where(hit, g[k:k + 1, :], pg)
        xacc_ref[...] += jnp.dot(p.astype(BF16), x_ref[...], preferred_element_type=F32)
        gacc_ref[...] += jnp.sum(pg, axis=-1, keepdims=True)

    @pl.when((flags & 2) != 0)
    def _():
        h = jnp.dot(xacc_ref[...].astype(BF16), wgu_bf[...], preferred_element_type=F32) + bgu_ref[0]
        gate = jnp.minimum(h[:, :D_FF], SWIGLU_LIMIT)
        up = jnp.clip(h[:, D_FF:], -SWIGLU_LIMIT, SWIGLU_LIMIT)
        act = (up + 1.0) * gate * _sigmoid(SWIGLU_ALPHA * gate)
        y = jnp.dot(act.astype(BF16), wdn_bf[...], preferred_element_type=F32) + bdn_ref[0]
        yb_ref[...] = (y * gacc_ref[...]).astype(yb_ref.dtype)


def _expert_ffn(xn_bf16, disp, w_gu, b_gu, w_down, b_down, tm, blk):
    blk_s, tile_s, exp_s, flag_s = disp["gather"]
    grid_spec = pltpu.PrefetchScalarGridSpec(
        num_scalar_prefetch=4,
        grid=(disp["n_steps"],),
        in_specs=[pl.BlockSpec((tm, D_MODEL), lambda s, b, t, e, f: (t[s], 0)),
                  pl.BlockSpec((8, tm), lambda s, b, t, e, f: (0, t[s])),
                  pl.BlockSpec((8, tm), lambda s, b, t, e, f: (0, t[s])),
                  pl.BlockSpec((1, D_MODEL, 2 * D_FF), lambda s, b, t, e, f: (e[s], 0, 0)),
                  pl.BlockSpec((1, 1, 2 * D_FF), lambda s, b, t, e, f: (e[s], 0, 0)),
                  pl.BlockSpec((1, D_FF, D_MODEL), lambda s, b, t, e, f: (e[s], 0, 0)),
                  pl.BlockSpec((1, 1, D_MODEL), lambda s, b, t, e, f: (e[s], 0, 0))],
        out_specs=pl.BlockSpec((blk, D_MODEL), lambda s, b, t, e, f: (b[s], 0)),
        scratch_shapes=[pltpu.VMEM((blk, D_MODEL), F32), pltpu.VMEM((blk, 1), F32),
                        pltpu.VMEM((D_MODEL, 2 * D_FF), BF16), pltpu.VMEM((D_FF, D_MODEL), BF16)])
    return pl.pallas_call(
        functools.partial(_expert_kernel, blk=blk, tm=tm),
        grid_spec=grid_spec,
        out_shape=jax.ShapeDtypeStruct((disp["n_rows"], D_MODEL), BF16),
        compiler_params=pltpu.CompilerParams(dimension_semantics=("arbitrary",), vmem_limit_bytes=VMEM_LIMIT),
        name="expert_ffn",
    )(blk_s, tile_s, exp_s, flag_s, xn_bf16, disp["dest_t"], disp["gate_t"],
      w_gu, b_gu.reshape(N_EXPERTS, 1, 2 * D_FF), w_down, b_down.reshape(N_EXPERTS, 1, D_MODEL))


def _combine_kernel(blk_s, tile_s, flag_s, yb_ref, dest_ref, xn_ref, g_ref, b_ref, y_ref, acc_ref, *, blk, tm):
    s = pl.program_id(0)
    flags = flag_s[s]

    @pl.when((flags & 1) != 0)
    def _():
        acc_ref[...] = jnp.zeros_like(acc_ref)

    @pl.when((flags & 4) != 0)
    def _():
        rowid = blk_s[s] * blk + lax.broadcasted_iota(I32, (tm, blk), 1)
        d = dest_ref[...]
        p = jnp.zeros((tm, blk), F32)
        for k in range(TOP_K):
            p = jnp.where(d[:, k:k + 1] == rowid, 1.0, p)
        acc_ref[...] += jnp.dot(p.astype(BF16), yb_ref[...], preferred_element_type=F32)

    @pl.when((flags & 2) != 0)
    def _():
        y_ref[...] = _ln(DN_ALPHA * xn_ref[...] + acc_ref[...], g_ref[...], b_ref[...])


def _moe_combine(yb, disp, xn, ln_g, ln_b, tm, blk):
    blk_s, tile_s, flag_s = disp["combine"]
    n = xn.shape[0]
    grid_spec = pltpu.PrefetchScalarGridSpec(
        num_scalar_prefetch=3,
        grid=(disp["n_steps"],),
        in_specs=[pl.BlockSpec((blk, D_MODEL), lambda s, b, t, f: (b[s], 0)),
                  pl.BlockSpec((tm, LANE), lambda s, b, t, f: (t[s], 0)),
                  pl.BlockSpec((tm, D_MODEL), lambda s, b, t, f: (t[s], 0)),
                  pl.BlockSpec((1, D_MODEL), lambda s, b, t, f: (0, 0)),
                  pl.BlockSpec((1, D_MODEL), lambda s, b, t, f: (0, 0))],
        out_specs=pl.BlockSpec((tm, D_MODEL), lambda s, b, t, f: (t[s], 0)),
        scratch_shapes=[pltpu.VMEM((tm, D_MODEL), F32)])
    return pl.pallas_call(
        functools.partial(_combine_kernel, blk=blk, tm=tm),
        grid_spec=grid_spec,
        out_shape=jax.ShapeDtypeStruct((n, D_MODEL), F32),
        compiler_params=pltpu.CompilerParams(dimension_semantics=("arbitrary",), vmem_limit_bytes=VMEM_LIMIT),
        name="moe_combine",
    )(blk_s, tile_s, flag_s, yb, disp["dest"], xn, ln_g.reshape(1, -1), ln_b.reshape(1, -1))


def _post_mix_moe(outs, x, w_out, fw):
    ln1_g, ln1_b, router_w, router_b, w_gu, b_gu, w_down, b_down, ln2_g, ln2_b = fw
    n = x.shape[0]
    tm_cum = min(256, n)
    tm = min(512, n)
    blk = min(EXPERT_BLOCK, n * TOP_K)
    xn, xnb, idx, gates, rank, cum = _post_mix(outs, x, w_out.astype(BF16), ln1_g, ln1_b, router_w, router_b, tm_cum)
    disp = _moe_dispatch(idx, gates, rank, cum, n, tm_cum, tm, blk)
    yb = _expert_ffn(xnb, disp, w_gu, b_gu, w_down, b_down, tm, blk)
    return _moe_combine(yb, disp, xn, ln2_g, ln2_b, tm, blk)


def _rec_step_kernel(gqkv_ref, gz_ref, xbc_ref, sz_ref, small_ref, gconv0_ref, sconv0_ref, s0_ref, h0_ref,
                     gcw_ref, scw_ref, scb_ref, gpar_ref, spar_ref, gnw_ref, snw_ref,
                     gout_ref, sout_ref, gconv_ref, sconv_ref, s_ref, h_ref):
    small = small_ref[0]

    def conv(u, buf_ref, w_ref, out_ref):
        buf = buf_ref[0]
        y = u * w_ref[CONV_W - 1:CONV_W, :]
        for i in range(CONV_W - 1):
            y = y + buf[i:i + 1, :] * w_ref[i:i + 1, :]
        out_ref[0, 0:CONV_W - 2, :] = buf[1:CONV_W - 1, :]
        out_ref[0, CONV_W - 2:CONV_W - 1, :] = u
        return y

    eye = (lax.broadcasted_iota(I32, (HEAD_DIM, HEAD_DIM), 0)
           == lax.broadcasted_iota(I32, (HEAD_DIM, HEAD_DIM), 1)).astype(F32)

    def to_col(r):
        return jnp.sum(eye * r, axis=-1, keepdims=True)

    y = _silu(conv(gqkv_ref[0], gconv0_ref, gcw_ref, gconv_ref))
    beta_all = _sigmoid(small[:, 0:N_HEADS])
    g_all = gpar_ref[0:1, :] * _softplus(small[:, N_HEADS:2 * N_HEADS] + gpar_ref[1:2, :])
    for h in range(N_HEADS):
        hs = slice(h * HEAD_DIM, (h + 1) * HEAD_DIM)
        q = y[:, h * HEAD_DIM:(h + 1) * HEAD_DIM]
        k = y[:, GROUP_WIDTH + h * HEAD_DIM:GROUP_WIDTH + (h + 1) * HEAD_DIM]
        v = y[:, 2 * GROUP_WIDTH + h * HEAD_DIM:2 * GROUP_WIDTH + (h + 1) * HEAD_DIM]
        q = q * lax.rsqrt(jnp.sum(q * q, axis=-1, keepdims=True) + RMS_EPS) * (HEAD_DIM ** -0.5)
        k = k * lax.rsqrt(jnp.sum(k * k, axis=-1, keepdims=True) + RMS_EPS)
        beta = beta_all[:, h:h + 1]
        eg = jnp.exp(g_all[:, h:h + 1])
        s0 = s0_ref[0, h]
        kc, qc = to_col(k), to_col(q)
        v_new = v * beta - jnp.sum(kc * (beta * eg) * s0, axis=0, keepdims=True)
        o = jnp.sum(qc * eg * s0, axis=0, keepdims=True) + jnp.sum(q * k, axis=-1, keepdims=True) * v_new
        s_ref[0, h] = s0 * eg + kc * v_new
        o = o * lax.rsqrt(jnp.mean(o * o, axis=-1, keepdims=True) + RMS_EPS) * gnw_ref[...] * _silu(gz_ref[0][:, hs])
        gout_ref[0, :, hs] = o

    xbc = _silu(conv(xbc_ref[0], sconv0_ref, scw_ref, sconv_ref) + scb_ref[...])
    dt_all = _softplus(small[:, 2 * N_HEADS:3 * N_HEADS] + spar_ref[1:2, :])
    hpg = N_HEADS // SSM_NGROUPS
    boff, coff = GROUP_WIDTH, GROUP_WIDTH + SSM_NGROUPS * SSM_STATE
    ys = []
    for h in range(N_HEADS):
        grp = h // hpg
        x = xbc[:, h * HEAD_DIM:(h + 1) * HEAD_DIM]
        bm = xbc[:, boff + grp * SSM_STATE:boff + (grp + 1) * SSM_STATE]
        cm = xbc[:, coff + grp * SSM_STATE:coff + (grp + 1) * SSM_STATE]
        dt = dt_all[:, h:h + 1]
        hn = h0_ref[0, h] * jnp.exp(dt * spar_ref[0:1, h:h + 1]) + to_col(x * dt) * bm
        h_ref[0, h] = hn
        yc = jnp.sum(hn * cm, axis=-1, keepdims=True)
        yh = jnp.sum(eye * yc, axis=0, keepdims=True) + spar_ref[2:3, h:h + 1] * x
        ys.append(yh * _silu(sz_ref[0][:, h * HEAD_DIM:(h + 1) * HEAD_DIM]))
    for grp in range(SSM_NGROUPS):
        grp_ys = ys[grp * hpg:(grp + 1) * hpg]
        ms = sum(jnp.sum(a * a, axis=-1, keepdims=True) for a in grp_ys) / (hpg * HEAD_DIM)
        r = lax.rsqrt(ms + RMS_EPS)
        for j, a in enumerate(grp_ys):
            h = grp * hpg + j
            sout_ref[0, :, h * HEAD_DIM:(h + 1) * HEAD_DIM] = a * r * snw_ref[:, h * HEAD_DIM:(h + 1) * HEAD_DIM]


def _rec_step(seg, gconv0, s0, sconv0, h0, lw):
    (_, gdn_conv_w, gdn_a_log, gdn_dt_bias, gdn_norm_w, ssm_conv_w, ssm_conv_b,
     ssm_a_log, ssm_dt_bias, ssm_d, ssm_norm_w, _) = lw
    b = seg["small"].shape[0]
    gpar = jnp.stack([-jnp.exp(gdn_a_log.astype(F32)), gdn_dt_bias.astype(F32)])
    spar = jnp.stack([-jnp.exp(ssm_a_log.astype(F32)), ssm_dt_bias.astype(F32), ssm_d.astype(F32)])
    row = lambda w: pl.BlockSpec((1, 1, w), lambda i: (i, 0, 0))
    full = lambda a: pl.BlockSpec(a.shape, lambda i: (0,) * a.ndim)
    per_seq = lambda *dims: pl.BlockSpec((1,) + dims, lambda i: (i,) + (0,) * len(dims))
    r3 = lambda a: a.reshape(b, 1, -1)
    consts = [gdn_conv_w, ssm_conv_w, ssm_conv_b.reshape(1, -1), gpar, spar,
              gdn_norm_w.reshape(1, HEAD_DIM), ssm_norm_w.reshape(1, GROUP_WIDTH)]
    state_specs = [per_seq(CONV_W - 1, GDN_CONV_DIM), per_seq(CONV_W - 1, SSM_CONV_DIM),
                   per_seq(N_HEADS, HEAD_DIM, HEAD_DIM), per_seq(N_HEADS, HEAD_DIM, SSM_STATE)]
    go, so, gc, sc, s, h = pl.pallas_call(
        _rec_step_kernel,
        grid=(b,),
        in_specs=[row(GDN_CONV_DIM), row(GROUP_WIDTH), row(SSM_CONV_DIM), row(GROUP_WIDTH), row(LANE)]
                 + state_specs + [full(c) for c in consts],
        out_specs=[row(GROUP_WIDTH), row(GROUP_WIDTH)] + state_specs,
        out_shape=[jax.ShapeDtypeStruct((b, 1, GROUP_WIDTH), F32), jax.ShapeDtypeStruct((b, 1, GROUP_WIDTH), F32),
                   jax.ShapeDtypeStruct((b, CONV_W - 1, GDN_CONV_DIM), F32),
                   jax.ShapeDtypeStruct((b, CONV_W - 1, SSM_CONV_DIM), F32),
                   jax.ShapeDtypeStruct((b, N_HEADS, HEAD_DIM, HEAD_DIM), F32),
                   jax.ShapeDtypeStruct((b, N_HEADS, HEAD_DIM, SSM_STATE), F32)],
        compiler_params=pltpu.CompilerParams(dimension_semantics=("arbitrary",), vmem_limit_bytes=VMEM_LIMIT),
        name="rec_step",
    )(r3(seg["gdn_qkv"]), r3(seg["gdn_z"]), r3(seg["ssm_xbc"]), r3(seg["ssm_z"]), r3(seg["small"]),
      gconv0, sconv0, s0, h0, *consts)
    return go.reshape(b, GROUP_WIDTH), so.reshape(b, GROUP_WIDTH), gc, sc, s, h


PAGES_PER_STEP = 8


def _head_rows(q_row, scale):
    r = lax.broadcasted_iota(I32, (8, GROUP_WIDTH), 0)
    c = lax.broadcasted_iota(I32, (8, GROUP_WIDTH), 1)
    return jnp.where(c // HEAD_DIM == r, q_row * scale, 0.0)


def _decode_pass_kernel(pt_ref, sq_ref, mq_ref, *refs, n_steps):
    pp = PAGES_PER_STEP
    sk_refs, sv_refs, mk_refs = refs[0:pp], refs[pp:2 * pp], refs[2 * pp:3 * pp]
    sb_out_ref, gp_ref, acc_ref, carry_ref = refs[3 * pp:]
    p = pl.program_id(1)

    @pl.when(p == 0)
    def _():
        acc_ref[...] = jnp.zeros_like(acc_ref)
        carry_ref[...] = jnp.zeros_like(carry_ref)

    qs = _head_rows(sq_ref[0], HEAD_DIM ** -0.5).astype(BF16)
    qm = _head_rows(mq_ref[0], 1.0 / MOBA_BLOCK)
    row = lax.broadcasted_iota(I32, (PAGE_SIZE, PAGE_SIZE), 0)
    col = lax.broadcasted_iota(I32, (PAGE_SIZE, PAGE_SIZE), 1)
    later = (col > row).astype(BF16)
    diag8 = lax.broadcasted_iota(I32, (8, LANE), 0) == lax.broadcasted_iota(I32, (8, LANE), 1)
    acc = acc_ref[...]
    carry = carry_ref[...]
    for i in range(pp):
        k = sk_refs[i][0, 0].astype(BF16)
        v = sv_refs[i][0, 0].astype(BF16)
        z = lax.dot_general(k, qs, (((1,), (1,)), ((), ())), preferred_element_type=F32)
        ls = -_softplus(z)
        hi = ls.astype(BF16)
        lo = (ls - hi.astype(F32)).astype(BF16)
        la = jnp.dot(later, hi, preferred_element_type=F32) + jnp.dot(later, lo, preferred_element_type=F32)
        w = jnp.exp(z + ls + la + carry)
        acc = acc + lax.dot_general(w.astype(BF16), v, (((0,), (0,)), ((), ())), preferred_element_type=F32)
        carry = carry + la[0:1, :] + ls[0:1, :]
        ksum = jnp.sum(mk_refs[i][0, 0], axis=0, keepdims=True)
        gp = jnp.sum(qm * ksum, axis=-1, keepdims=True)
        gp_ref[0, pl.ds(p * pp + i, 1), :] = jnp.sum(jnp.where(diag8, gp, 0.0), axis=0, keepdims=True)
    acc_ref[...] = acc
    carry_ref[...] = carry

    @pl.when(p == n_steps - 1)
    def _():
        r = lax.broadcasted_iota(I32, (8, GROUP_WIDTH), 0)
        c = lax.broadcasted_iota(I32, (8, GROUP_WIDTH), 1)
        sb_out_ref[0] = jnp.sum(jnp.where(c // HEAD_DIM == r, acc, 0.0), axis=0, keepdims=True)


def _decode_pass(layer, page_table, sq, mq, cache_sb_k, cache_sb_v, cache_moba_k):
    b, n_pages = page_table.shape
    pp = PAGES_PER_STEP
    assert n_pages % pp == 0
    n_steps = n_pages // pp

    def page_spec(i):
        return pl.BlockSpec((1, 1, PAGE_SIZE, GROUP_WIDTH),
                            lambda bi, p, pt: (layer, pt[bi, n_pages - 1 - (p * pp + i)], 0, 0))

    qspec = pl.BlockSpec((1, 1, GROUP_WIDTH), lambda bi, p, pt: (bi, 0, 0))
    grid_spec = pltpu.PrefetchScalarGridSpec(
        num_scalar_prefetch=1,
        grid=(b, n_steps),
        in_specs=[qspec, qspec] + [page_spec(i) for i in range(pp)] * 3,
        out_specs=[pl.BlockSpec((1, 1, GROUP_WIDTH), lambda bi, p, pt: (bi, 0, 0)),
                   pl.BlockSpec((1, n_pages, LANE), lambda bi, p, pt: (bi, 0, 0))],
        scratch_shapes=[pltpu.VMEM((8, GROUP_WIDTH), F32), pltpu.VMEM((1, 8), F32)])
    sb_out, gp = pl.pallas_call(
        functools.partial(_decode_pass_kernel, n_steps=n_steps),
        grid_spec=grid_spec,
        out_shape=[jax.ShapeDtypeStruct((b, 1, GROUP_WIDTH), F32), jax.ShapeDtypeStruct((b, n_pages, LANE), F32)],
        compiler_params=pltpu.CompilerParams(dimension_semantics=("arbitrary", "arbitrary"), vmem_limit_bytes=VMEM_LIMIT),
        name="decode_pass",
    )(page_table, sq.reshape(b, 1, GROUP_WIDTH), mq.reshape(b, 1, GROUP_WIDTH),
      *([cache_sb_k] * pp), *([cache_sb_v] * pp), *([cache_moba_k] * pp))
    return sb_out.reshape(b, GROUP_WIDTH), gp


def _moba_decode_kernel(pg_ref, mq_ref, mk_ref, mv_ref, *refs, n_sel):
    kp_refs, vp_refs, o_ref = refs[:n_sel], refs[n_sel:2 * n_sel], refs[2 * n_sel]
    h = pl.program_id(1)
    lane = lax.broadcasted_iota(I32, (1, GROUP_WIDTH), 1)
    in_head = lane // HEAD_DIM == h
    q = jnp.where(in_head, mq_ref[0] * (HEAD_DIM ** -0.5), 0.0)

    @pl.when(h == 0)
    def _():
        o_ref[...] = jnp.zeros_like(o_ref)

    q8 = jnp.where(lax.broadcasted_iota(I32, (8, GROUP_WIDTH), 0) == 0, q, 0.0).astype(BF16)
    s_self = jnp.sum(q * mk_ref[0], axis=-1, keepdims=True)
    scores = [lax.dot_general(kp[0, 0].astype(BF16), q8, (((1,), (1,)), ((), ())),
                              preferred_element_type=F32)[:, 0:1] for kp in kp_refs]
    m = s_self
    for s in scores:
        m = jnp.maximum(m, jnp.max(s, axis=0, keepdims=True))
    p_self = jnp.exp(s_self - m)
    l = p_self
    acc = p_self * mv_ref[0]
    first_col = lax.broadcasted_iota(I32, (PAGE_SIZE, 8), 1) == 0
    for s, vp in zip(scores, vp_refs):
        p = jnp.exp(s - m)
        l = l + jnp.sum(p, axis=0, keepdims=True)
        p8 = jnp.where(first_col, p, 0.0).astype(BF16)
        acc = acc + lax.dot_general(p8, vp[0, 0].astype(BF16), (((0,), (0,)), ((), ())),
                                    preferred_element_type=F32)[0:1, :]
    o_ref[0] = jnp.where(in_head, acc / l, o_ref[0])


def _moba_decode(layer, sel_pages, mq, mk_new, mv_new, cache_moba_k, cache_moba_v):
    b, nh, n_sel = sel_pages.shape
    row = pl.BlockSpec((1, 1, GROUP_WIDTH), lambda bi, h, pg: (bi, 0, 0))

    def page(j):
        return pl.BlockSpec((1, 1, PAGE_SIZE, GROUP_WIDTH),
                            lambda bi, h, pg: (layer, pg[(bi * nh + h) * n_sel + j], 0, 0))

    grid_spec = pltpu.PrefetchScalarGridSpec(
        num_scalar_prefetch=1,
        grid=(b, nh),
        in_specs=[row, row, row] + [page(j) for j in range(n_sel)] * 2,
        out_specs=pl.BlockSpec((1, 1, GROUP_WIDTH), lambda bi, h, pg: (bi, 0, 0)))
    r3 = lambda a: a.reshape(b, 1, GROUP_WIDTH)
    out = pl.pallas_call(
        functools.partial(_moba_decode_kernel, n_sel=n_sel),
        grid_spec=grid_spec,
        out_shape=jax.ShapeDtypeStruct((b, 1, GROUP_WIDTH), F32),
        compiler_params=pltpu.CompilerParams(dimension_semantics=("arbitrary",) * 2, vmem_limit_bytes=VMEM_LIMIT),
        name="moba_decode",
    )(sel_pages.reshape(-1).astype(I32), r3(mq), r3(mk_new), r3(mv_new),
      *([cache_moba_k] * n_sel), *([cache_moba_v] * n_sel))
    return out.reshape(b, GROUP_WIDTH)


def _sample_attention(layer, page_table, seg, caches):
    cache_moba_k, cache_moba_v, cache_sb_k, cache_sb_v = caches
    b, n_pages = page_table.shape
    ppb = MOBA_BLOCK // PAGE_SIZE
    n_past_blocks = n_pages // ppb
    assert n_pages % ppb == 0 and n_past_blocks >= MOBA_TOPK
    sb_out, gp = _decode_pass(layer, page_table, seg["sb_q"], seg["moba_q"], cache_sb_k, cache_sb_v, cache_moba_k)
    gates = gp[:, ::-1, :N_HEADS].reshape(b, n_past_blocks, ppb, N_HEADS).sum(axis=2)
    _, sel = lax.top_k(jnp.swapaxes(gates, 1, 2), MOBA_TOPK)
    pages = sel[..., None] * ppb + jnp.arange(ppb)
    sel_pages = jnp.take_along_axis(page_table[:, None, :], pages.reshape(b, N_HEADS, -1), axis=2)
    moba_out = _moba_decode(layer, sel_pages, seg["moba_q"], seg["moba_k"], seg["moba_v"], cache_moba_k, cache_moba_v)
    return moba_out, sb_out


def _kv_rows(seg, bsz, length):
    return tuple(seg[name].reshape(bsz, length, N_HEADS, HEAD_DIM) for name in ("moba_k", "moba_v", "sb_k", "sb_v"))


def _prompt_mixers(x, lw):
    (w_in, gdn_conv_w, gdn_a_log, gdn_dt_bias, gdn_norm_w, ssm_conv_w, ssm_conv_b,
     ssm_a_log, ssm_dt_bias, ssm_d, ssm_norm_w, _) = lw
    bsz, length, _ = x.shape
    n = bsz * length
    flat = _in_proj(x.reshape(n, D_MODEL), w_in, jnp.arange(length), length)
    seg = {k: v.reshape(bsz, length, -1) for k, v in flat.items()}

    gdn_out, gdn_conv_new, gdn_s_new = _gdn_prompt(seg["gdn_qkv"], seg["gdn_z"], seg["small"],
                                                   gdn_conv_w, gdn_a_log, gdn_dt_bias, gdn_norm_w)
    ssm_out, ssm_conv_new, ssm_h_new = _ssd_prompt(seg["ssm_xbc"], seg["ssm_z"], seg["small"], ssm_conv_w, ssm_conv_b,
                                                   ssm_a_log, ssm_dt_bias, ssm_d, ssm_norm_w)
    moba_out = _moba_attention(seg["moba_q"], seg["moba_k"], seg["moba_v"])
    sb_out = _sb_attention(seg["sb_q"], seg["sb_k"], seg["sb_v"])

    outs = [o.reshape(n, GROUP_WIDTH) for o in (gdn_out, ssm_out, moba_out, sb_out)]
    return outs, _kv_rows(seg, bsz, length) + (gdn_conv_new, gdn_s_new, ssm_conv_new, ssm_h_new)


def _sample_mixers(x, layer, page_table, rec_state, caches, lw):
    gdn_conv0, gdn_s0, ssm_conv0, ssm_h0 = rec_state
    bsz, length, _ = x.shape
    assert length == 1
    pos0 = page_table.shape[1] * PAGE_SIZE
    seg = _in_proj(x.reshape(bsz, D_MODEL), lw[0], jnp.full((1,), pos0, I32), 1)
    gdn_out, ssm_out, gdn_conv_new, ssm_conv_new, gdn_s_new, ssm_h_new = _rec_step(
        seg, gdn_conv0, gdn_s0, ssm_conv0, ssm_h0, lw)
    moba_out, sb_out = _sample_attention(layer, page_table, seg, caches)
    outs = [gdn_out, ssm_out, moba_out, sb_out]
    return outs, _kv_rows(seg, bsz, length) + (gdn_conv_new, gdn_s_new, ssm_conv_new, ssm_h_new)


def kernel(x_prompt, x_sample, cache_moba_k, cache_moba_v, cache_sb_k, cache_sb_v,
           state_gdn_conv, state_gdn_rec, state_ssm_conv, state_ssm_rec, page_table,
           w_in, gdn_conv_w, gdn_a_log, gdn_dt_bias, gdn_norm_w,
           ssm_conv_w, ssm_conv_b, ssm_a_log, ssm_dt_bias, ssm_d, ssm_norm_w,
           w_out, ln1_g, ln1_b, router_w, router_b,
           expert_w_gu, expert_b_gu, expert_w_down, expert_b_down, ln2_g, ln2_b):
    caches = tuple(c.reshape(c.shape[0], c.shape[1], PAGE_SIZE, GROUP_WIDTH)
                   for c in (cache_moba_k, cache_moba_v, cache_sb_k, cache_sb_v))
    yp, ys = x_prompt, x_sample
    new_p, new_s = [], []
    for l in range(DEPTH):
        lw = (w_in[l], gdn_conv_w[l], gdn_a_log[l], gdn_dt_bias[l], gdn_norm_w[l], ssm_conv_w[l], ssm_conv_b[l],
              ssm_a_log[l], ssm_dt_bias[l], ssm_d[l], ssm_norm_w[l], w_out[l])
        fw = (ln1_g[l], ln1_b[l], router_w[l], router_b[l], expert_w_gu[l], expert_b_gu[l],
              expert_w_down[l], expert_b_down[l], ln2_g[l], ln2_b[l])
        rec_state = (state_gdn_conv[l], state_gdn_rec[l], state_ssm_conv[l], state_ssm_rec[l])
        outs_p, st_p = _prompt_mixers(yp, lw)
        yp = _post_mix_moe(outs_p, yp.reshape(-1, D_MODEL), w_out[l], fw).reshape(yp.shape)
        outs_s, st_s = _sample_mixers(ys, l, page_table, rec_state, caches, lw)
        ys = _post_mix_moe(outs_s, ys.reshape(-1, D_MODEL), w_out[l], fw).reshape(ys.shape)
        new_p.append(st_p)
        new_s.append(st_s)

    def stk(states, i):
        return jnp.stack([s[i] for s in states])

    return (yp, ys,
            stk(new_p, 0), stk(new_p, 1), stk(new_p, 2), stk(new_p, 3),
            stk(new_p, 4), stk(new_p, 5), stk(new_p, 6), stk(new_p, 7),
            stk(new_s, 0), stk(new_s, 1), stk(new_s, 2), stk(new_s, 3),
            stk(new_s, 4), stk(new_s, 5), stk(new_s, 6), stk(new_s, 7))
```

```python
import functools

import jax
import jax.numpy as jnp
from jax import lax
from jax.experimental import pallas as pl
from jax.experimental.pallas import tpu as pltpu

D_MODEL = 1024
DEPTH = 2
PAGE_SIZE = 128
HEAD_DIM = 64
N_MIXERS = 4
GROUP_WIDTH = D_MODEL // N_MIXERS
N_HEADS = GROUP_WIDTH // HEAD_DIM
CONV_W = 4
CHUNK = 64
GDN_CONV_DIM = 3 * GROUP_WIDTH
SSM_STATE = 128
SSM_NGROUPS = 2
SSM_CONV_DIM = GROUP_WIDTH + 2 * SSM_NGROUPS * SSM_STATE
MOBA_BLOCK = 256
MOBA_TOPK = 3
MOBA_QUERY_BLOCK = 64
ROPE_THETA = 500000.0
ROPE_DIM = HEAD_DIM // 4
QUERY_BLOCK = 128
N_EXPERTS = 32
TOP_K = 4
D_FF = D_MODEL
SWIGLU_LIMIT = 7.0
SWIGLU_ALPHA = 1.702
EXPERT_BLOCK = 256
DN_ALPHA = (2 * DEPTH) ** 0.25
LN_EPS = 1e-5
RMS_EPS = 1e-6

F32 = jnp.float32
BF16 = jnp.bfloat16
I32 = jnp.int32
HI = lax.Precision.HIGHEST

LANE = 128
VMEM_LIMIT = 56 * 1024 * 1024
NEG = -1e30

_SEG_SIZES = (GDN_CONV_DIM, GROUP_WIDTH, N_HEADS, N_HEADS, SSM_CONV_DIM, GROUP_WIDTH, N_HEADS, GDN_CONV_DIM, GDN_CONV_DIM)
_SEG_NAMES = ("gdn_qkv", "gdn_z", "gdn_b", "gdn_a", "ssm_xbc", "ssm_z", "ssm_dt", "moba_qkv", "sb_qkv")
_OUT_SEGS = (("gdn_qkv", GDN_CONV_DIM), ("gdn_z", GROUP_WIDTH), ("ssm_xbc", SSM_CONV_DIM), ("ssm_z", GROUP_WIDTH),
             ("moba_q", GROUP_WIDTH), ("moba_k", GROUP_WIDTH), ("moba_v", GROUP_WIDTH),
             ("sb_q", GROUP_WIDTH), ("sb_k", GROUP_WIDTH), ("sb_v", GROUP_WIDTH), ("small", LANE))
_ROTARY_SEGS = ("moba_q", "moba_k")


def _softplus(z):
    return jnp.maximum(z, 0.0) + jnp.log1p(jnp.exp(-jnp.abs(z)))


def _sigmoid(x):
    return 1.0 / (1.0 + jnp.exp(-x))


def _silu(x):
    return x * _sigmoid(x)


def _dot(a, b):
    return jnp.dot(a.astype(BF16), b.astype(BF16), preferred_element_type=F32)


def _dot_nt(a, b):
    return lax.dot_general(a.astype(BF16), b.astype(BF16), (((1,), (1,)), ((), ())), preferred_element_type=F32)


def _dot_tn(a, b):
    return lax.dot_general(a.astype(BF16), b.astype(BF16), (((0,), (0,)), ((), ())), preferred_element_type=F32)


def _dot_hi(a, b):
    return jnp.dot(a, b, preferred_element_type=F32, precision=HI)


def _split_dot(x, m_bf16):
    hi = x.astype(BF16)
    lo = (x - hi.astype(F32)).astype(BF16)
    return (jnp.dot(hi, m_bf16, preferred_element_type=F32)
            + jnp.dot(lo, m_bf16, preferred_element_type=F32))


def _split3(x):
    hi = x.astype(BF16)
    r = x - hi.astype(F32)
    mid = r.astype(BF16)
    return hi, mid, (r - mid.astype(F32)).astype(BF16)


def _dot3(a, b):
    a_hi = a.astype(BF16)
    a_lo = (a - a_hi.astype(F32)).astype(BF16)
    b_hi = b.astype(BF16)
    b_lo = (b - b_hi.astype(F32)).astype(BF16)
    return (jnp.dot(a_hi, b_hi, preferred_element_type=F32) + jnp.dot(a_hi, b_lo, preferred_element_type=F32)
            + jnp.dot(a_lo, b_hi, preferred_element_type=F32))


def _ln(h, g, b):
    mu = jnp.mean(h, axis=-1, keepdims=True)
    d = h - mu
    var = jnp.mean(d * d, axis=-1, keepdims=True)
    return d * lax.rsqrt(var + LN_EPS) * g + b


def _in_proj_kernel(x_ref, w_ref, cos_ref, sin_ref, *o_refs):
    y = jnp.dot(x_ref[...].astype(BF16), w_ref[...], preferred_element_type=F32)
    cos = jnp.concatenate([cos_ref[...]] * (GROUP_WIDTH // LANE), axis=1)
    sin = jnp.concatenate([sin_ref[...]] * (GROUP_WIDTH // LANE), axis=1)
    first_half = (lax.broadcasted_iota(I32, (1, GROUP_WIDTH), 1) % HEAD_DIM) < ROPE_DIM // 2
    off = 0
    for (name, width), o_ref in zip(_OUT_SEGS, o_refs):
        seg = y[:, off:off + width]
        if name in _ROTARY_SEGS:
            partner = jnp.where(first_half, pltpu.roll(seg, GROUP_WIDTH - ROPE_DIM // 2, 1), pltpu.roll(seg, ROPE_DIM // 2, 1))
            seg = seg * cos + partner * sin
        o_ref[...] = seg
        off += width


def _rotary_tables(pos):
    half = ROPE_DIM // 2
    inv = ROPE_THETA ** (-jnp.arange(half, dtype=F32) * 2.0 / ROPE_DIM)
    ang = pos.astype(F32)[:, None] * inv
    cc = jnp.arange(LANE) % HEAD_DIM
    cos = jnp.where(cc < ROPE_DIM, jnp.cos(ang)[:, cc % half], 1.0)
    sin = jnp.sin(ang)[:, cc % half]
    sin = jnp.where(cc < half, -sin, jnp.where(cc < ROPE_DIM, sin, 0.0))
    return cos, sin


def _in_proj(x, w_in, pos, seq_len):
    m = x.shape[0]
    tm = min(256, m)
    cols, off = {}, 0
    for name, sz in zip(_SEG_NAMES, _SEG_SIZES):
        cols[name] = w_in[:, off:off + sz]
        off += sz
    for name in ("moba", "sb"):
        for j, part in enumerate("qkv"):
            cols[f"{name}_{part}"] = cols[f"{name}_qkv"][:, j * GROUP_WIDTH:(j + 1) * GROUP_WIDTH]
    cols["small"] = jnp.concatenate([cols["gdn_b"], cols["gdn_a"], cols["ssm_dt"],
                                     jnp.zeros((D_MODEL, LANE - 3 * N_HEADS), w_in.dtype)], axis=1)
    w = jnp.concatenate([cols[name] for name, _ in _OUT_SEGS], axis=1).astype(BF16)
    n = w.shape[1]
    cos, sin = _rotary_tables(pos)
    if seq_len == 1:
        tab_spec = pl.BlockSpec((1, LANE), lambda i: (0, 0))
    else:
        assert seq_len % tm == 0
        tab_spec = pl.BlockSpec((tm, LANE), lambda i: (i % (seq_len // tm), 0))
    outs = pl.pallas_call(
        _in_proj_kernel,
        grid=(m // tm,),
        in_specs=[pl.BlockSpec((tm, D_MODEL), lambda i: (i, 0)),
                  pl.BlockSpec((D_MODEL, n), lambda i: (0, 0)), tab_spec, tab_spec],
        out_specs=[pl.BlockSpec((tm, width), lambda i: (i, 0)) for _, width in _OUT_SEGS],
        out_shape=[jax.ShapeDtypeStruct((m, width), F32) for _, width in _OUT_SEGS],
        compiler_params=pltpu.CompilerParams(dimension_semantics=("arbitrary",), vmem_limit_bytes=VMEM_LIMIT),
        name="in_proj",
    )(x, w, cos, sin)
    return {name: o for (name, _), o in zip(_OUT_SEGS, outs)}


def _sb_kernel(q_ref, k_ref, v_ref, o_ref, *, tq):
    i = pl.program_id(1)
    row = lax.broadcasted_iota(I32, (tq, tq), 0)
    col = lax.broadcasted_iota(I32, (tq, tq), 1)
    upper = (row > col).astype(BF16)
    past = col < row

    heads = [slice(h * HEAD_DIM, (h + 1) * HEAD_DIM) for h in range(N_HEADS)]
    qs = [(q_ref[0, :, hs] * (HEAD_DIM ** -0.5)).astype(BF16) for hs in heads]

    def block(j, state, diag):
        out = []
        for hs, q, (carry, acc) in zip(heads, qs, state):
            kj = k_ref[0, pl.ds(j * tq, tq), hs].astype(BF16)
            vj = v_ref[0, pl.ds(j * tq, tq), hs].astype(BF16)
            z = lax.dot_general(q, kj, (((1,), (1,)), ((), ())), preferred_element_type=F32)
            ls = -_softplus(z)
            lk = jnp.where(past, ls, 0.0) if diag else ls
            la = _split_dot(lk, upper)
            w = jnp.exp(z + ls + la + carry)
            if diag:
                w = jnp.where(past, w, 0.0)
            acc = acc + jnp.dot(w.astype(BF16), vj, preferred_element_type=F32)
            out.append((carry + la[:, 0:1] + lk[:, 0:1], acc))
        return tuple(out)

    state = tuple((jnp.zeros((tq, 1), F32), jnp.zeros((tq, HEAD_DIM), F32)) for _ in heads)
    state = block(i, state, True)
    state = lax.fori_loop(0, i, lambda t, c: block(i - 1 - t, c, False), state)
    for hs, (_, acc) in zip(heads, state):
        o_ref[0, :, hs] = acc


def _sb_attention(q, k, v, tq=256):
    b, l, d = q.shape
    assert l % tq == 0
    return pl.pallas_call(
        functools.partial(_sb_kernel, tq=tq),
        grid=(b, l // tq),
        in_specs=[pl.BlockSpec((1, tq, d), lambda bi, i: (bi, i, 0)),
                  pl.BlockSpec((1, l, d), lambda bi, i: (bi, 0, 0)),
                  pl.BlockSpec((1, l, d), lambda bi, i: (bi, 0, 0))],
        out_specs=pl.BlockSpec((1, tq, d), lambda bi, i: (bi, i, 0)),
        out_shape=jax.ShapeDtypeStruct((b, l, d), F32),
        compiler_params=pltpu.CompilerParams(dimension_semantics=("arbitrary",) * 2, vmem_limit_bytes=VMEM_LIMIT),
        name="sb_attention",
    )(q, k, v)


def _moba_kernel(q_ref, k_ref, v_ref, o_ref, kmean_ref, *, nb):
    t = MOBA_BLOCK
    i = pl.program_id(1)

    @pl.when(i == 0)
    def _():
        for n in range(nb):
            kmean_ref[n:n + 1, :] = jnp.mean(k_ref[0, n * t:(n + 1) * t, :], axis=0, keepdims=True)

    blk_id = lax.broadcasted_iota(I32, (nb, t), 0)
    row = lax.broadcasted_iota(I32, (t, t), 0)
    col = lax.broadcasted_iota(I32, (t, t), 1)

    heads = [slice(h * HEAD_DIM, (h + 1) * HEAD_DIM) for h in range(N_HEADS)]
    qs, sels, state = [], [], []
    for hs in heads:
        qf = q_ref[0, :, hs]
        q = (qf * (HEAD_DIM ** -0.5)).astype(BF16)
        gate = lax.dot_general(kmean_ref[:, hs], qf, (((1,), (1,)), ((), ())),
                               preferred_element_type=F32, precision=HI)
        cnt = jnp.zeros((nb, t), F32)
        for n2 in range(nb):
            cn = gate[n2:n2 + 1, :]
            beats = (cn > gate) | ((cn == gate) & (n2 < blk_id))
            cnt = cnt + jnp.where(beats & (n2 < i), 1.0, 0.0)
        sels.append(jnp.where((cnt < MOBA_TOPK) & (blk_id < i), 1.0, 0.0).astype(BF16))
        qs.append(q)

        ki = k_ref[0, pl.ds(i * t, t), hs].astype(BF16)
        vi = v_ref[0, pl.ds(i * t, t), hs].astype(BF16)
        s = lax.dot_general(q, ki, (((1,), (1,)), ((), ())), preferred_element_type=F32)
        s = jnp.where(col <= row, s, NEG)
        m = jnp.max(s, axis=-1, keepdims=True)
        p = jnp.exp(s - m)
        state.append((m, jnp.sum(p, axis=-1, keepdims=True), jnp.dot(p.astype(BF16), vi, preferred_element_type=F32)))

    def body(j, state):
        pick = ((lax.broadcasted_iota(I32, (nb, 8), 0) == j) & (lax.broadcasted_iota(I32, (nb, 8), 1) == 0)).astype(BF16)
        out = []
        for hs, q, sel, (m, l, acc) in zip(heads, qs, sels, state):
            kj = k_ref[0, pl.ds(j * t, t), hs].astype(BF16)
            vj = v_ref[0, pl.ds(j * t, t), hs].astype(BF16)
            s = lax.dot_general(q, kj, (((1,), (1,)), ((), ())), preferred_element_type=F32)
            sj = lax.dot_general(sel, pick, (((0,), (0,)), ((), ())), preferred_element_type=F32)[:, 0:1]
            s = jnp.where(sj > 0.5, s, NEG)
            m_new = jnp.maximum(m, jnp.max(s, axis=-1, keepdims=True))
            a = jnp.exp(m - m_new)
            p = jnp.exp(s - m_new)
            out.append((m_new, a * l + jnp.sum(p, axis=-1, keepdims=True),
                        a * acc + jnp.dot(p.astype(BF16), vj, preferred_element_type=F32)))
        return tuple(out)

    state = lax.fori_loop(0, i, body, tuple(state))
    for hs, (_, l, acc) in zip(heads, state):
        o_ref[0, :, hs] = acc / l


def _moba_attention(q, k, v):
    b, l, d = q.shape
    t = MOBA_BLOCK
    assert l % t == 0
    nb = l // t
    return pl.pallas_call(
        functools.partial(_moba_kernel, nb=nb),
        grid=(b, nb),
        in_specs=[pl.BlockSpec((1, t, d), lambda bi, i: (bi, i, 0)),
                  pl.BlockSpec((1, l, d), lambda bi, i: (bi, 0, 0)),
                  pl.BlockSpec((1, l, d), lambda bi, i: (bi, 0, 0))],
        out_specs=pl.BlockSpec((1, t, d), lambda bi, i: (bi, i, 0)),
        out_shape=jax.ShapeDtypeStruct((b, l, d), F32),
        scratch_shapes=[pltpu.VMEM((nb, d), F32)],
        compiler_params=pltpu.CompilerParams(dimension_semantics=("arbitrary",) * 2, vmem_limit_bytes=VMEM_LIMIT),
        name="moba_attention",
    )(q, k, v)


def _conv_tile(ext_ref, u, w_ref, tl):
    ext_ref[8:8 + tl, :] = u
    y = ext_ref[5:5 + tl, :] * w_ref[0:1, :]
    for i in range(1, CONV_W):
        y = y + ext_ref[5 + i:5 + i + tl, :] * w_ref[i:i + 1, :]
    tail = ext_ref[tl + 5:tl + 8, :]
    ext_ref[5:8, :] = tail
    return y, tail


def _gdn_kernel(qkv_ref, z_ref, small_ref, convw_ref, par_ref, normw_ref,
                o_ref, conv_out_ref, s_out_ref, ext_ref, s_ref, *, tl):
    c = CHUNK
    step = pl.program_id(1)

    @pl.when(step == 0)
    def _():
        ext_ref[...] = jnp.zeros_like(ext_ref)
        s_ref[...] = jnp.zeros_like(s_ref)

    y, tail = _conv_tile(ext_ref, qkv_ref[0], convw_ref, tl)
    conv_out_ref[0] = tail
    y = _silu(y)
    small = small_ref[0]
    beta_all = _sigmoid(small[:, 0:N_HEADS])
    g_all = par_ref[0:1, :] * _softplus(small[:, N_HEADS:2 * N_HEADS] + par_ref[1:2, :])

    row = lax.broadcasted_iota(I32, (tl, tl), 0)
    col = lax.broadcasted_iota(I32, (tl, tl), 1)
    same = (row // c) == (col // c)
    tril = same & (row >= col)
    strict = same & (row > col)
    tril_b = tril.astype(BF16)
    triu_b = (same & (row <= col)).astype(BF16)
    eye = (row == col).astype(F32)

    g8 = jnp.concatenate([g_all, jnp.zeros((tl, 8 - N_HEADS), F32)], axis=1)
    g_parts = _split3(g8)
    gc_cols = sum(jnp.dot(tril_b, part, preferred_element_type=F32) for part in g_parts)
    gc_rows = sum(lax.dot_general(part, triu_b, (((0,), (0,)), ((), ())), preferred_element_type=F32) for part in g_parts)

    for h in range(N_HEADS):
        hs = slice(h * HEAD_DIM, (h + 1) * HEAD_DIM)
        q = y[:, h * HEAD_DIM:(h + 1) * HEAD_DIM]
        k = y[:, GROUP_WIDTH + h * HEAD_DIM:GROUP_WIDTH + (h + 1) * HEAD_DIM]
        v = y[:, 2 * GROUP_WIDTH + h * HEAD_DIM:2 * GROUP_WIDTH + (h + 1) * HEAD_DIM]
        q = q * lax.rsqrt(jnp.sum(q * q, axis=-1, keepdims=True) + RMS_EPS) * (HEAD_DIM ** -0.5)
        k = k * lax.rsqrt(jnp.sum(k * k, axis=-1, keepdims=True) + RMS_EPS)
        beta = beta_all[:, h:h + 1]
        gcol = gc_cols[:, h:h + 1]
        grow = gc_rows[h:h + 1, :]
        decay = jnp.where(tril, jnp.exp(jnp.where(tril, gcol - grow, 0.0)), 0.0)
        kb = k * beta
        a = jnp.where(strict, _dot_nt(kb, k) * decay, 0.0)
        p = -a
        tinv = eye + p
        for _ in range(5):
            p = _dot3(p, p)
            tinv = tinv + _dot3(tinv, p)
        u = _dot(tinv, v * beta)
        w = _dot(tinv, kb * jnp.exp(gcol))
        attn = jnp.where(tril, _dot_nt(q, k) * decay, 0.0)
        qe = q * jnp.exp(gcol)
        s = s_ref[h]
        for ci in range(tl // c):
            rs = slice(ci * c, (ci + 1) * c)
            g_last = gcol[(ci + 1) * c - 1:(ci + 1) * c, :]
            v_new = u[rs] - _dot(w[rs], s)
            o = _dot(qe[rs], s) + _dot(attn[rs, rs], v_new)
            s = s * jnp.exp(g_last) + _dot_tn(k[rs] * jnp.exp(g_last - gcol[rs]), v_new)
            o = o * lax.rsqrt(jnp.mean(o * o, axis=-1, keepdims=True) + RMS_EPS) * normw_ref[...] * _silu(z_ref[0, rs, hs])
            o_ref[0, rs, hs] = o
        s_ref[h] = s

    @pl.when(step == pl.num_programs(1) - 1)
    def _():
        s_out_ref[0] = s_ref[...]


def _gdn_prompt(qkv, z, small, conv_w, a_log, dt_bias, norm_w, tl=256):
    b, l, _ = qkv.shape
    assert l % tl == 0 and tl % CHUNK == 0
    par = jnp.stack([-jnp.exp(a_log.astype(F32)), dt_bias.astype(F32)])
    return pl.pallas_call(
        functools.partial(_gdn_kernel, tl=tl),
        grid=(b, l // tl),
        in_specs=[pl.BlockSpec((1, tl, GDN_CONV_DIM), lambda bi, i: (bi, i, 0)),
                  pl.BlockSpec((1, tl, GROUP_WIDTH), lambda bi, i: (bi, i, 0)),
                  pl.BlockSpec((1, tl, LANE), lambda bi, i: (bi, i, 0)),
                  pl.BlockSpec((CONV_W, GDN_CONV_DIM), lambda bi, i: (0, 0)),
                  pl.BlockSpec((2, N_HEADS), lambda bi, i: (0, 0)),
                  pl.BlockSpec((1, HEAD_DIM), lambda bi, i: (0, 0))],
        out_specs=[pl.BlockSpec((1, tl, GROUP_WIDTH), lambda bi, i: (bi, i, 0)),
                   pl.BlockSpec((1, CONV_W - 1, GDN_CONV_DIM), lambda bi, i: (bi, 0, 0)),
                   pl.BlockSpec((1, N_HEADS, HEAD_DIM, HEAD_DIM), lambda bi, i: (bi, 0, 0, 0))],
        out_shape=[jax.ShapeDtypeStruct((b, l, GROUP_WIDTH), F32),
                   jax.ShapeDtypeStruct((b, CONV_W - 1, GDN_CONV_DIM), F32),
                   jax.ShapeDtypeStruct((b, N_HEADS, HEAD_DIM, HEAD_DIM), F32)],
        scratch_shapes=[pltpu.VMEM((tl + 8, GDN_CONV_DIM), F32),
                        pltpu.VMEM((N_HEADS, HEAD_DIM, HEAD_DIM), F32)],
        compiler_params=pltpu.CompilerParams(dimension_semantics=("arbitrary", "arbitrary"),
                                             vmem_limit_bytes=VMEM_LIMIT),
        name="gdn_prompt",
    )(qkv, z, small, conv_w, par, norm_w.reshape(1, HEAD_DIM))


def _ssd_kernel(xbc_ref, z_ref, small_ref, convw_ref, convb_ref, par_ref, normw_ref,
                o_ref, conv_out_ref, h_out_ref, ext_ref, h_ref, *, tl):
    c = CHUNK
    step = pl.program_id(1)
    hpg = N_HEADS // SSM_NGROUPS

    @pl.when(step == 0)
    def _():
        ext_ref[...] = jnp.zeros_like(ext_ref)
        h_ref[...] = jnp.zeros_like(h_ref)

    y, tail = _conv_tile(ext_ref, xbc_ref[0], convw_ref, tl)
    conv_out_ref[0] = tail
    xbc = _silu(y + convb_ref[...])
    dt_all = _softplus(small_ref[0][:, 2 * N_HEADS:3 * N_HEADS] + par_ref[1:2, :])
    ad_all = dt_all * par_ref[0:1, :]

    row = lax.broadcasted_iota(I32, (c, c), 0)
    col = lax.broadcasted_iota(I32, (c, c), 1)
    tril = row >= col
    tril_f = tril.astype(F32)
    triu_f = (row <= col).astype(F32)
    ones = jnp.ones((c, c), F32)
    boff = GROUP_WIDTH
    coff = GROUP_WIDTH + SSM_NGROUPS * SSM_STATE

    for ci in range(tl // c):
        r0 = ci * c
        ad_c = ad_all[r0:r0 + c, :]
        acs_all = _dot_hi(tril_f, ad_c)
        ys = []
        for h in range(N_HEADS):
            grp = h // hpg
            x = xbc[r0:r0 + c, h * HEAD_DIM:(h + 1) * HEAD_DIM]
            bm = xbc[r0:r0 + c, boff + grp * SSM_STATE:boff + (grp + 1) * SSM_STATE]
            cm = xbc[r0:r0 + c, coff + grp * SSM_STATE:coff + (grp + 1) * SSM_STATE]
            xdt = x * dt_all[r0:r0 + c, h:h + 1]
            acol = acs_all[:, h:h + 1]
            arow = _dot_hi(ones, ad_c[:, h:h + 1] * triu_f)
            lmat = jnp.where(tril, jnp.exp(jnp.where(tril, acol - arow, 0.0)), 0.0)
            y_diag = _dot(_dot_nt(cm, bm) * lmat, xdt)
            a_last = acol[c - 1:c, :]
            states = _dot_tn(xdt * jnp.exp(a_last - acol), bm)
            hin = h_ref[h]
            y_off = _dot_nt(cm, hin) * jnp.exp(acol)
            h_ref[h] = hin * jnp.exp(a_last) + states
            yh = y_diag + y_off + par_ref[2:3, h:h + 1] * x
            ys.append(yh * _silu(z_ref[0, r0:r0 + c, h * HEAD_DIM:(h + 1) * HEAD_DIM]))
        for grp in range(SSM_NGROUPS):
            grp_ys = ys[grp * hpg:(grp + 1) * hpg]
            ms = sum(jnp.sum(a * a, axis=-1, keepdims=True) for a in grp_ys) / (hpg * HEAD_DIM)
            r = lax.rsqrt(ms + RMS_EPS)
            for j, a in enumerate(grp_ys):
                h = grp * hpg + j
                o_ref[0, r0:r0 + c, h * HEAD_DIM:(h + 1) * HEAD_DIM] = a * r * normw_ref[:, h * HEAD_DIM:(h + 1) * HEAD_DIM]

    @pl.when(step == pl.num_programs(1) - 1)
    def _():
        h_out_ref[0] = h_ref[...]


def _ssd_prompt(xbc, z, small, conv_w, conv_b, a_log, dt_bias, d_skip, norm_w, tl=256):
    b, l, _ = xbc.shape
    assert l % tl == 0 and tl % CHUNK == 0
    par = jnp.stack([-jnp.exp(a_log.astype(F32)), dt_bias.astype(F32), d_skip.astype(F32)])
    return pl.pallas_call(
        functools.partial(_ssd_kernel, tl=tl),
        grid=(b, l // tl),
        in_specs=[pl.BlockSpec((1, tl, SSM_CONV_DIM), lambda bi, i: (bi, i, 0)),
                  pl.BlockSpec((1, tl, GROUP_WIDTH), lambda bi, i: (bi, i, 0)),
                  pl.BlockSpec((1, tl, LANE), lambda bi, i: (bi, i, 0)),
                  pl.BlockSpec((CONV_W, SSM_CONV_DIM), lambda bi, i: (0, 0)),
                  pl.BlockSpec((1, SSM_CONV_DIM), lambda bi, i: (0, 0)),
                  pl.BlockSpec((3, N_HEADS), lambda bi, i: (0, 0)),
                  pl.BlockSpec((1, GROUP_WIDTH), lambda bi, i: (0, 0))],
        out_specs=[pl.BlockSpec((1, tl, GROUP_WIDTH), lambda bi, i: (bi, i, 0)),
                   pl.BlockSpec((1, CONV_W - 1, SSM_CONV_DIM), lambda bi, i: (bi, 0, 0)),
                   pl.BlockSpec((1, N_HEADS, HEAD_DIM, SSM_STATE), lambda bi, i: (bi, 0, 0, 0))],
        out_shape=[jax.ShapeDtypeStruct((b, l, GROUP_WIDTH), F32),
                   jax.ShapeDtypeStruct((b, CONV_W - 1, SSM_CONV_DIM), F32),
                   jax.ShapeDtypeStruct((b, N_HEADS, HEAD_DIM, SSM_STATE), F32)],
        scratch_shapes=[pltpu.VMEM((tl + 8, SSM_CONV_DIM), F32),
                        pltpu.VMEM((N_HEADS, HEAD_DIM, SSM_STATE), F32)],
        compiler_params=pltpu.CompilerParams(dimension_semantics=("arbitrary", "arbitrary"),
                                             vmem_limit_bytes=VMEM_LIMIT),
        name="ssd_prompt",
    )(xbc, z, small, conv_w, conv_b.reshape(1, SSM_CONV_DIM), par, norm_w.reshape(1, GROUP_WIDTH))


def _post_mix_kernel(o0_ref, o1_ref, o2_ref, o3_ref, x_ref, wout_ref, g_ref, b_ref, rw_ref, rb_ref,
                     xn_ref, xnb_ref, idx_ref, gate_ref, rank_ref, cum_ref, cnt_ref, *, tm):
    @pl.when(pl.program_id(0) == 0)
    def _():
        cnt_ref[...] = jnp.zeros_like(cnt_ref)

    mix = jnp.zeros((tm, D_MODEL), F32)
    for gi, o_ref in enumerate((o0_ref, o1_ref, o2_ref, o3_ref)):
        mix = mix + jnp.dot(o_ref[...].astype(BF16), wout_ref[gi * GROUP_WIDTH:(gi + 1) * GROUP_WIDTH, :],
                            preferred_element_type=F32)
    xn = _ln(DN_ALPHA * x_ref[...] + mix, g_ref[...], b_ref[...])
    xn_ref[...] = xn
    xnb_ref[...] = xn.astype(BF16)

    logits = _dot_hi(xn, rw_ref[...]) + rb_ref[...]
    lane_e = lax.broadcasted_iota(I32, (tm, N_EXPERTS), 1).astype(F32)
    lane = lax.broadcasted_iota(I32, (tm, LANE), 1)
    work = logits
    sel = jnp.zeros((tm, N_EXPERTS), F32)
    idx_out = jnp.zeros((tm, LANE), I32)
    val_out = jnp.zeros((tm, LANE), F32)
    hits = []
    for r in range(TOP_K):
        m = jnp.max(work, axis=-1, keepdims=True)
        pick = jnp.min(jnp.where(work == m, lane_e, float(N_EXPERTS)), axis=-1, keepdims=True)
        hit = lane_e == pick
        work = jnp.where(hit, -jnp.inf, work)
        sel = sel + hit.astype(F32)
        hits.append(hit)
        idx_out = jnp.where(lane == r, pick.astype(I32), idx_out)
        val_out = jnp.where(lane == r, m, val_out)
    ex = jnp.where(lane < TOP_K, jnp.exp(val_out - val_out[:, 0:1]), 0.0)
    gate_ref[...] = ex / jnp.sum(ex, axis=-1, keepdims=True)
    idx_ref[...] = idx_out

    row = lax.broadcasted_iota(I32, (tm, tm), 0)
    col = lax.broadcasted_iota(I32, (tm, tm), 1)
    before = jnp.dot((row > col).astype(BF16), sel.astype(BF16), preferred_element_type=F32) + cnt_ref[...]
    rank_out = jnp.zeros((tm, LANE), I32)
    for r in range(TOP_K):
        rk = jnp.sum(jnp.where(hits[r], before, 0.0), axis=-1, keepdims=True)
        rank_out = jnp.where(lane == r, rk.astype(I32), rank_out)
    rank_ref[...] = rank_out
    cum_ref[0] = jnp.broadcast_to(cnt_ref[...], (8, N_EXPERTS))
    cnt_ref[...] = cnt_ref[...] + jnp.sum(sel, axis=0, keepdims=True)


def _post_mix(outs, x, w_out_bf16, ln_g, ln_b, router_w, router_b, tm):
    n = x.shape[0]
    nt = n // tm
    tok = lambda w: pl.BlockSpec((tm, w), lambda i: (i, 0))
    full = lambda a: pl.BlockSpec(a.shape, lambda i: (0,) * a.ndim)
    g2, b2, rb2 = ln_g.reshape(1, -1), ln_b.reshape(1, -1), router_b.reshape(1, -1)
    return pl.pallas_call(
        functools.partial(_post_mix_kernel, tm=tm),
        grid=(nt,),
        in_specs=[tok(GROUP_WIDTH)] * 4 + [tok(D_MODEL), full(w_out_bf16), full(g2), full(b2), full(router_w), full(rb2)],
        out_specs=[tok(D_MODEL), tok(D_MODEL), tok(LANE), tok(LANE), tok(LANE),
                   pl.BlockSpec((1, 8, N_EXPERTS), lambda i: (i, 0, 0))],
        out_shape=[jax.ShapeDtypeStruct((n, D_MODEL), F32), jax.ShapeDtypeStruct((n, D_MODEL), BF16),
                   jax.ShapeDtypeStruct((n, LANE), I32), jax.ShapeDtypeStruct((n, LANE), F32),
                   jax.ShapeDtypeStruct((n, LANE), I32), jax.ShapeDtypeStruct((nt, 8, N_EXPERTS), F32)],
        scratch_shapes=[pltpu.VMEM((1, N_EXPERTS), F32)],
        compiler_params=pltpu.CompilerParams(dimension_semantics=("arbitrary",), vmem_limit_bytes=VMEM_LIMIT),
        name="post_mix",
    )(*outs, x, w_out_bf16, g2, b2, router_w, rb2)


def _moe_dispatch(idx, gates, rank, cum, n_tok, tm_cum, tm, blk):
    nt = n_tok // tm
    n_blocks = -(-n_tok * TOP_K // blk) + N_EXPERTS
    per_pair = -(-tm // blk) + 1
    e_idx = idx[:, :TOP_K]
    cum_i = cum[:, 0, :].astype(I32)
    onehot = e_idx[:, :, None] == jnp.arange(N_EXPERTS, dtype=I32)
    counts = cum_i[-1] + jnp.sum(onehot[n_tok - tm_cum:].astype(I32), axis=(0, 1))
    cum_t = cum_i[::tm // tm_cum]
    cum_all = jnp.concatenate([cum_t, counts[None]], 0)
    blocks_per_e = (counts + blk - 1) // blk
    blk_start = jnp.cumsum(blocks_per_e) - blocks_per_e
    dest = jnp.sum(jnp.where(onehot, blk_start, 0), axis=-1) * blk + rank[:, :TOP_K]
    dest_f = dest.astype(F32)
    dest_pad = jnp.pad(dest_f, ((0, 0), (0, LANE - TOP_K)), constant_values=-1.0)
    dest_t = jnp.pad(dest_f.T, ((0, 8 - TOP_K), (0, 0)), constant_values=-1.0)
    gate_t = jnp.pad(gates[:, :TOP_K].T, ((0, 8 - TOP_K), (0, 0)))
    eid_t = jnp.pad(e_idx.T, ((0, 8 - TOP_K), (0, 0)), constant_values=-1)

    lo, hi = cum_all[:-1].T, cum_all[1:].T
    has = hi > lo
    j0 = lo // blk
    j1 = jnp.maximum(hi - 1, 0) // blk
    jj = j0[:, :, None] + jnp.arange(per_pair, dtype=I32)
    valid = has[:, :, None] & (jj <= j1[:, :, None])
    blk_id = blk_start[:, None, None] + jj
    e_b = jnp.broadcast_to(jnp.arange(N_EXPERTS, dtype=I32)[:, None, None], jj.shape)
    t_b = jnp.broadcast_to(jnp.arange(nt, dtype=I32)[None, :, None], jj.shape)
    n_steps = min(N_EXPERTS * nt + n_blocks, per_pair * N_EXPERTS * nt)
    n_valid = jnp.sum(valid)
    pos = jnp.arange(n_steps)
    v = pos < n_valid

    def build(key):
        order = jnp.argsort(jnp.where(valid, key, jnp.int32(1 << 30)).reshape(-1))[:n_steps]
        src = order[jnp.minimum(pos, n_valid - 1)]
        return blk_id.reshape(-1)[src], t_b.reshape(-1)[src], e_b.reshape(-1)[src]

    def edges(a):
        change = a[1:] != a[:-1]
        first = jnp.concatenate([jnp.ones((1,), bool), change]) & v
        last = (jnp.concatenate([change, jnp.ones((1,), bool)]) | (pos == n_valid - 1)) & v
        return first, last

    b1, t1, e1 = build(blk_id * nt + t_b)
    first_b, last_b = edges(b1)
    first_e, _ = edges(e1)
    flags1 = first_b.astype(I32) + 2 * last_b.astype(I32) + 4 * v.astype(I32) + 8 * first_e.astype(I32)
    b2, t2, e2 = build(t_b * (2 * n_blocks) + blk_id)
    first_t, last_t = edges(t2)
    flags2 = first_t.astype(I32) + 2 * last_t.astype(I32) + 4 * v.astype(I32)
    return dict(dest=dest_pad, eid=idx, dest_t=dest_t, gate_t=gate_t, eid_t=eid_t,
                n_rows=n_blocks * blk, n_steps=n_steps,
                gather=(b1.astype(I32), t1.astype(I32), e1.astype(I32), flags1),
                combine=(b2.astype(I32), t2.astype(I32), e2.astype(I32), flags2))


def _expert_kernel(blk_s, tile_s, exp_s, flag_s, x_ref, eid_ref, dest_ref, gate_ref, wgu_ref, bgu_ref, wdn_ref, bdn_ref,
                   yb_ref, xacc_ref, gacc_ref, wgu_bf, wdn_bf, *, blk, tm):
    s = pl.program_id(0)
    flags = flag_s[s]

    @pl.when((flags & 8) != 0)
    def _():
        wgu_bf[...] = wgu_ref[0, 0].astype(BF16)
        wdn_bf[...] = wdn_ref[0, 0].astype(BF16)

    @pl.when((flags & 1) != 0)
    def _():
        xacc_ref[...] = jnp.zeros_like(xacc_ref)
        gacc_ref[...] = jnp.zeros_like(gacc_ref)

    @pl.when((flags & 4) != 0)
    def _():
        mine = eid_ref[...] == exp_s[s]
        d_sel = jnp.sum(jnp.where(mine, dest_ref[...] + 1.0, 0.0), axis=0, keepdims=True) - 1.0
        g_sel = jnp.sum(jnp.where(mine, gate_ref[...], 0.0), axis=0, keepdims=True)
        rowid = (blk_s[s] * blk + lax.broadcasted_iota(I32, (blk, tm), 0)).astype(F32)
        hit = d_sel == rowid
        p = jnp.where(hit, 1.0, 0.0).astype(BF16)
        xacc_ref[...] += jnp.dot(p, x_ref[...], preferred_element_type=F32)
        gacc_ref[...] += jnp.sum(jnp.where(hit, g_sel, 0.0), axis=-1, keepdims=True)

    @pl.when((flags & 2) != 0)
    def _():
        h = jnp.dot(xacc_ref[...].astype(BF16), wgu_bf[...], preferred_element_type=F32) + bgu_ref[0, 0]
        gate = jnp.minimum(h[:, :D_FF], SWIGLU_LIMIT)
        up = jnp.clip(h[:, D_FF:], -SWIGLU_LIMIT, SWIGLU_LIMIT)
        act = (up + 1.0) * gate * _sigmoid(SWIGLU_ALPHA * gate)
        y = jnp.dot(act.astype(BF16), wdn_bf[...], preferred_element_type=F32) + bdn_ref[0, 0]
        yb_ref[...] = (y * gacc_ref[...]).astype(yb_ref.dtype)


def _expert_ffn(layer, xn_bf16, disp, w_gu, b_gu, w_down, b_down, tm, blk):
    blk_s, tile_s, exp_s, flag_s = disp["gather"]
    depth = w_gu.shape[0]
    tok_t = pl.BlockSpec((8, tm), lambda s, b, t, e, f: (0, t[s]))
    grid_spec = pltpu.PrefetchScalarGridSpec(
        num_scalar_prefetch=4,
        grid=(disp["n_steps"],),
        in_specs=[pl.BlockSpec((tm, D_MODEL), lambda s, b, t, e, f: (t[s], 0)), tok_t, tok_t, tok_t,
                  pl.BlockSpec((1, 1, D_MODEL, 2 * D_FF), lambda s, b, t, e, f: (layer, e[s], 0, 0)),
                  pl.BlockSpec((1, 1, 1, 2 * D_FF), lambda s, b, t, e, f: (layer, e[s], 0, 0)),
                  pl.BlockSpec((1, 1, D_FF, D_MODEL), lambda s, b, t, e, f: (layer, e[s], 0, 0)),
                  pl.BlockSpec((1, 1, 1, D_MODEL), lambda s, b, t, e, f: (layer, e[s], 0, 0))],
        out_specs=pl.BlockSpec((blk, D_MODEL), lambda s, b, t, e, f: (b[s], 0)),
        scratch_shapes=[pltpu.VMEM((blk, D_MODEL), F32), pltpu.VMEM((blk, 1), F32),
                        pltpu.VMEM((D_MODEL, 2 * D_FF), BF16), pltpu.VMEM((D_FF, D_MODEL), BF16)])
    return pl.pallas_call(
        functools.partial(_expert_kernel, blk=blk, tm=tm),
        grid_spec=grid_spec,
        out_shape=jax.ShapeDtypeStruct((disp["n_rows"], D_MODEL), BF16),
        compiler_params=pltpu.CompilerParams(dimension_semantics=("arbitrary",), vmem_limit_bytes=VMEM_LIMIT),
        name="expert_ffn",
    )(blk_s, tile_s, exp_s, flag_s, xn_bf16, disp["eid_t"], disp["dest_t"], disp["gate_t"],
      w_gu, b_gu.reshape(depth, N_EXPERTS, 1, 2 * D_FF), w_down, b_down.reshape(depth, N_EXPERTS, 1, D_MODEL))


def _combine_kernel(blk_s, tile_s, exp_s, flag_s, yb_ref, eid_ref, dest_ref, xn_ref, g_ref, b_ref, y_ref, acc_ref,
                    *, blk, tm):
    s = pl.program_id(0)
    flags = flag_s[s]

    @pl.when((flags & 1) != 0)
    def _():
        acc_ref[...] = jnp.zeros_like(acc_ref)

    @pl.when((flags & 4) != 0)
    def _():
        lane = lax.broadcasted_iota(I32, (tm, LANE), 1)
        mine = (eid_ref[...] == exp_s[s]) & (lane < TOP_K)
        d_sel = jnp.sum(jnp.where(mine, dest_ref[...] + 1.0, 0.0), axis=-1, keepdims=True) - 1.0
        rowid = (blk_s[s] * blk + lax.broadcasted_iota(I32, (tm, blk), 1)).astype(F32)
        p = jnp.where(d_sel == rowid, 1.0, 0.0).astype(BF16)
        acc_ref[...] += jnp.dot(p, yb_ref[...], preferred_element_type=F32)

    @pl.when((flags & 2) != 0)
    def _():
        y_ref[...] = _ln(DN_ALPHA * xn_ref[...] + acc_ref[...], g_ref[...], b_ref[...])


def _moe_combine(yb, disp, xn, ln_g, ln_b, tm, blk):
    blk_s, tile_s, exp_s, flag_s = disp["combine"]
    n = xn.shape[0]
    tok = pl.BlockSpec((tm, LANE), lambda s, b, t, e, f: (t[s], 0))
    grid_spec = pltpu.PrefetchScalarGridSpec(
        num_scalar_prefetch=4,
        grid=(disp["n_steps"],),
        in_specs=[pl.BlockSpec((blk, D_MODEL), lambda s, b, t, e, f: (b[s], 0)), tok, tok,
                  pl.BlockSpec((tm, D_MODEL), lambda s, b, t, e, f: (t[s], 0)),
                  pl.BlockSpec((1, D_MODEL), lambda s, b, t, e, f: (0, 0)),
                  pl.BlockSpec((1, D_MODEL), lambda s, b, t, e, f: (0, 0))],
        out_specs=pl.BlockSpec((tm, D_MODEL), lambda s, b, t, e, f: (t[s], 0)),
        scratch_shapes=[pltpu.VMEM((tm, D_MODEL), F32)])
    return pl.pallas_call(
        functools.partial(_combine_kernel, blk=blk, tm=tm),
        grid_spec=grid_spec,
        out_shape=jax.ShapeDtypeStruct((n, D_MODEL), F32),
        compiler_params=pltpu.CompilerParams(dimension_semantics=("arbitrary",), vmem_limit_bytes=VMEM_LIMIT),
        name="moe_combine",
    )(blk_s, tile_s, exp_s, flag_s, yb, disp["eid"], disp["dest"], xn, ln_g.reshape(1, -1), ln_b.reshape(1, -1))


def _post_mix_moe(layer, outs, x, w_out, fw):
    ln1_g, ln1_b, router_w, router_b, w_gu, b_gu, w_down, b_down, ln2_g, ln2_b = fw
    n = x.shape[0]
    tm_cum = min(256, n)
    tm = min(512, n)
    blk = min(EXPERT_BLOCK, n * TOP_K)
    xn, xnb, idx, gates, rank, cum = _post_mix(outs, x, w_out.astype(BF16), ln1_g, ln1_b, router_w, router_b, tm_cum)
    disp = _moe_dispatch(idx, gates, rank, cum, n, tm_cum, tm, blk)
    yb = _expert_ffn(layer, xnb, disp, w_gu, b_gu, w_down, b_down, tm, blk)
    return _moe_combine(yb, disp, xn, ln2_g, ln2_b, tm, blk)


def _rec_step_kernel(gqkv_ref, gz_ref, xbc_ref, sz_ref, small_ref, gconv0_ref, sconv0_ref, s0_ref, h0_ref,
                     gcw_ref, scw_ref, scb_ref, gpar_ref, spar_ref, gnw_ref, snw_ref,
                     gout_ref, sout_ref, gconv_ref, sconv_ref, s_ref, h_ref):
    small = small_ref[0]

    def conv(u, buf_ref, w_ref, out_ref):
        buf = buf_ref[0]
        y = u * w_ref[CONV_W - 1:CONV_W, :]
        for i in range(CONV_W - 1):
            y = y + buf[i:i + 1, :] * w_ref[i:i + 1, :]
        out_ref[0, 0:CONV_W - 2, :] = buf[1:CONV_W - 1, :]
        out_ref[0, CONV_W - 2:CONV_W - 1, :] = u
        return y

    eye = (lax.broadcasted_iota(I32, (HEAD_DIM, HEAD_DIM), 0)
           == lax.broadcasted_iota(I32, (HEAD_DIM, HEAD_DIM), 1)).astype(F32)

    def to_col(r):
        return jnp.sum(eye * r, axis=-1, keepdims=True)

    y = _silu(conv(gqkv_ref[0], gconv0_ref, gcw_ref, gconv_ref))
    beta_all = _sigmoid(small[:, 0:N_HEADS])
    g_all = gpar_ref[0:1, :] * _softplus(small[:, N_HEADS:2 * N_HEADS] + gpar_ref[1:2, :])
    for h in range(N_HEADS):
        hs = slice(h * HEAD_DIM, (h + 1) * HEAD_DIM)
        q = y[:, h * HEAD_DIM:(h + 1) * HEAD_DIM]
        k = y[:, GROUP_WIDTH + h * HEAD_DIM:GROUP_WIDTH + (h + 1) * HEAD_DIM]
        v = y[:, 2 * GROUP_WIDTH + h * HEAD_DIM:2 * GROUP_WIDTH + (h + 1) * HEAD_DIM]
        q = q * lax.rsqrt(jnp.sum(q * q, axis=-1, keepdims=True) + RMS_EPS) * (HEAD_DIM ** -0.5)
        k = k * lax.rsqrt(jnp.sum(k * k, axis=-1, keepdims=True) + RMS_EPS)
        beta = beta_all[:, h:h + 1]
        eg = jnp.exp(g_all[:, h:h + 1])
        s0 = s0_ref[0, h]
        kc, qc = to_col(k), to_col(q)
        v_new = v * beta - jnp.sum(kc * (beta * eg) * s0, axis=0, keepdims=True)
        o = jnp.sum(qc * eg * s0, axis=0, keepdims=True) + jnp.sum(q * k, axis=-1, keepdims=True) * v_new
        s_ref[0, h] = s0 * eg + kc * v_new
        o = o * lax.rsqrt(jnp.mean(o * o, axis=-1, keepdims=True) + RMS_EPS) * gnw_ref[...] * _silu(gz_ref[0][:, hs])
        gout_ref[0, :, hs] = o

    xbc = _silu(conv(xbc_ref[0], sconv0_ref, scw_ref, sconv_ref) + scb_ref[...])
    dt_all = _softplus(small[:, 2 * N_HEADS:3 * N_HEADS] + spar_ref[1:2, :])
    hpg = N_HEADS // SSM_NGROUPS
    boff, coff = GROUP_WIDTH, GROUP_WIDTH + SSM_NGROUPS * SSM_STATE
    ys = []
    for h in range(N_HEADS):
        grp = h // hpg
        x = xbc[:, h * HEAD_DIM:(h + 1) * HEAD_DIM]
        bm = xbc[:, boff + grp * SSM_STATE:boff + (grp + 1) * SSM_STATE]
        cm = xbc[:, coff + grp * SSM_STATE:coff + (grp + 1) * SSM_STATE]
        dt = dt_all[:, h:h + 1]
        hn = h0_ref[0, h] * jnp.exp(dt * spar_ref[0:1, h:h + 1]) + to_col(x * dt) * bm
        h_ref[0, h] = hn
        yc = jnp.sum(hn * cm, axis=-1, keepdims=True)
        yh = jnp.sum(eye * yc, axis=0, keepdims=True) + spar_ref[2:3, h:h + 1] * x
        ys.append(yh * _silu(sz_ref[0][:, h * HEAD_DIM:(h + 1) * HEAD_DIM]))
    for grp in range(SSM_NGROUPS):
        grp_ys = ys[grp * hpg:(grp + 1) * hpg]
        ms = sum(jnp.sum(a * a, axis=-1, keepdims=True) for a in grp_ys) / (hpg * HEAD_DIM)
        r = lax.rsqrt(ms + RMS_EPS)
        for j, a in enumerate(grp_ys):
            h = grp * hpg + j
            sout_ref[0, :, h * HEAD_DIM:(h + 1) * HEAD_DIM] = a * r * snw_ref[:, h * HEAD_DIM:(h + 1) * HEAD_DIM]


def _rec_step(seg, gconv0, s0, sconv0, h0, lw):
    (_, gdn_conv_w, gdn_a_log, gdn_dt_bias, gdn_norm_w, ssm_conv_w, ssm_conv_b,
     ssm_a_log, ssm_dt_bias, ssm_d, ssm_norm_w, _) = lw
    b = seg["small"].shape[0]
    gpar = jnp.stack([-jnp.exp(gdn_a_log.astype(F32)), gdn_dt_bias.astype(F32)])
    spar = jnp.stack([-jnp.exp(ssm_a_log.astype(F32)), ssm_dt_bias.astype(F32), ssm_d.astype(F32)])
    row = lambda w: pl.BlockSpec((1, 1, w), lambda i: (i, 0, 0))
    full = lambda a: pl.BlockSpec(a.shape, lambda i: (0,) * a.ndim)
    per_seq = lambda *dims: pl.BlockSpec((1,) + dims, lambda i: (i,) + (0,) * len(dims))
    r3 = lambda a: a.reshape(b, 1, -1)
    consts = [gdn_conv_w, ssm_conv_w, ssm_conv_b.reshape(1, -1), gpar, spar,
              gdn_norm_w.reshape(1, HEAD_DIM), ssm_norm_w.reshape(1, GROUP_WIDTH)]
    state_specs = [per_seq(CONV_W - 1, GDN_CONV_DIM), per_seq(CONV_W - 1, SSM_CONV_DIM),
                   per_seq(N_HEADS, HEAD_DIM, HEAD_DIM), per_seq(N_HEADS, HEAD_DIM, SSM_STATE)]
    go, so, gc, sc, s, h = pl.pallas_call(
        _rec_step_kernel,
        grid=(b,),
        in_specs=[row(GDN_CONV_DIM), row(GROUP_WIDTH), row(SSM_CONV_DIM), row(GROUP_WIDTH), row(LANE)]
                 + state_specs + [full(c) for c in consts],
        out_specs=[row(GROUP_WIDTH), row(GROUP_WIDTH)] + state_specs,
        out_shape=[jax.ShapeDtypeStruct((b, 1, GROUP_WIDTH), F32), jax.ShapeDtypeStruct((b, 1, GROUP_WIDTH), F32),
                   jax.ShapeDtypeStruct((b, CONV_W - 1, GDN_CONV_DIM), F32),
                   jax.ShapeDtypeStruct((b, CONV_W - 1, SSM_CONV_DIM), F32),
                   jax.ShapeDtypeStruct((b, N_HEADS, HEAD_DIM, HEAD_DIM), F32),
                   jax.ShapeDtypeStruct((b, N_HEADS, HEAD_DIM, SSM_STATE), F32)],
        compiler_params=pltpu.CompilerParams(dimension_semantics=("arbitrary",), vmem_limit_bytes=VMEM_LIMIT),
        name="rec_step",
    )(r3(seg["gdn_qkv"]), r3(seg["gdn_z"]), r3(seg["ssm_xbc"]), r3(seg["ssm_z"]), r3(seg["small"]),
      gconv0, sconv0, s0, h0, *consts)
    return go.reshape(b, GROUP_WIDTH), so.reshape(b, GROUP_WIDTH), gc, sc, s, h


PAGES_PER_STEP = 8


def _head_rows(q_row, scale):
    r = lax.broadcasted_iota(I32, (8, GROUP_WIDTH), 0)
    c = lax.broadcasted_iota(I32, (8, GROUP_WIDTH), 1)
    return jnp.where(c // HEAD_DIM == r, q_row * scale, 0.0)


def _decode_pass_kernel(pt_ref, sq_ref, mq_ref, *refs, n_steps):
    pp = PAGES_PER_STEP
    sk_refs, sv_refs, mk_refs = refs[0:pp], refs[pp:2 * pp], refs[2 * pp:3 * pp]
    sb_out_ref, gp_ref, acc_ref, carry_ref = refs[3 * pp:]
    p = pl.program_id(1)

    @pl.when(p == 0)
    def _():
        acc_ref[...] = jnp.zeros_like(acc_ref)
        carry_ref[...] = jnp.zeros_like(carry_ref)

    qs = _head_rows(sq_ref[0], HEAD_DIM ** -0.5).astype(BF16)
    qm = _head_rows(mq_ref[0], 1.0 / MOBA_BLOCK)
    row = lax.broadcasted_iota(I32, (PAGE_SIZE, PAGE_SIZE), 0)
    col = lax.broadcasted_iota(I32, (PAGE_SIZE, PAGE_SIZE), 1)
    upper = (row > col).astype(BF16)
    diag8 = lax.broadcasted_iota(I32, (8, LANE), 0) == lax.broadcasted_iota(I32, (8, LANE), 1)
    acc = acc_ref[...]
    carry = carry_ref[...]
    for i in range(pp):
        kt = sk_refs[i][0, 0].astype(BF16)
        vt = sv_refs[i][0, 0].astype(BF16)
        z = jnp.dot(qs, kt, preferred_element_type=F32)
        ls = -_softplus(z)
        la = _split_dot(ls, upper)
        w = jnp.exp(z + ls + la + carry)
        acc = acc + lax.dot_general(w.astype(BF16), vt, (((1,), (1,)), ((), ())), preferred_element_type=F32)
        carry = carry + la[:, 0:1] + ls[:, 0:1]
        gp = jnp.sum(_dot3(qm, mk_refs[i][0, 0]), axis=-1, keepdims=True)
        gp_ref[0, pl.ds(p * pp + i, 1), :] = jnp.sum(jnp.where(diag8, gp, 0.0), axis=0, keepdims=True)
    acc_ref[...] = acc
    carry_ref[...] = carry

    @pl.when(p == n_steps - 1)
    def _():
        r = lax.broadcasted_iota(I32, (8, GROUP_WIDTH), 0)
        c = lax.broadcasted_iota(I32, (8, GROUP_WIDTH), 1)
        sb_out_ref[0] = jnp.sum(jnp.where(c // HEAD_DIM == r, acc, 0.0), axis=0, keepdims=True)


def _decode_pass(layer, page_table, sq, mq, cache_sb_k, cache_sb_v, cache_moba_k):
    b, n_pages = page_table.shape
    pp = PAGES_PER_STEP
    assert n_pages % pp == 0
    n_steps = n_pages // pp

    def page_spec(i):
        return pl.BlockSpec((1, 1, GROUP_WIDTH, PAGE_SIZE),
                            lambda bi, p, pt: (layer, pt[bi, n_pages - 1 - (p * pp + i)], 0, 0))

    qspec = pl.BlockSpec((1, 1, GROUP_WIDTH), lambda bi, p, pt: (bi, 0, 0))
    grid_spec = pltpu.PrefetchScalarGridSpec(
        num_scalar_prefetch=1,
        grid=(b, n_steps),
        in_specs=[qspec, qspec] + [page_spec(i) for i in range(pp)] * 3,
        out_specs=[pl.BlockSpec((1, 1, GROUP_WIDTH), lambda bi, p, pt: (bi, 0, 0)),
                   pl.BlockSpec((1, n_pages, LANE), lambda bi, p, pt: (bi, 0, 0))],
        scratch_shapes=[pltpu.VMEM((8, GROUP_WIDTH), F32), pltpu.VMEM((8, 1), F32)])
    sb_out, gp = pl.pallas_call(
        functools.partial(_decode_pass_kernel, n_steps=n_steps),
        grid_spec=grid_spec,
        out_shape=[jax.ShapeDtypeStruct((b, 1, GROUP_WIDTH), F32), jax.ShapeDtypeStruct((b, n_pages, LANE), F32)],
        compiler_params=pltpu.CompilerParams(dimension_semantics=("arbitrary", "arbitrary"), vmem_limit_bytes=VMEM_LIMIT),
        name="decode_pass",
    )(page_table, sq.reshape(b, 1, GROUP_WIDTH), mq.reshape(b, 1, GROUP_WIDTH),
      *([cache_sb_k] * pp), *([cache_sb_v] * pp), *([cache_moba_k] * pp))
    return sb_out.reshape(b, GROUP_WIDTH), gp


def _moba_decode_kernel(pg_ref, mq_ref, mk_ref, mv_ref, *refs, n_sel):
    kp_refs, vp_refs, o_ref = refs[:n_sel], refs[n_sel:2 * n_sel], refs[2 * n_sel]
    h = pl.program_id(1)
    lane = lax.broadcasted_iota(I32, (1, GROUP_WIDTH), 1)
    in_head = lane // HEAD_DIM == h
    q = jnp.where(in_head, mq_ref[0] * (HEAD_DIM ** -0.5), 0.0)

    @pl.when(h == 0)
    def _():
        o_ref[...] = jnp.zeros_like(o_ref)

    q8 = jnp.where(lax.broadcasted_iota(I32, (8, GROUP_WIDTH), 0) == 0, q, 0.0).astype(BF16)
    s_self = jnp.sum(q * mk_ref[0], axis=-1, keepdims=True)
    scores = [jnp.dot(q8, kp[0, 0].astype(BF16), preferred_element_type=F32)[0:1, :] for kp in kp_refs]
    m = s_self
    for s in scores:
        m = jnp.maximum(m, jnp.max(s, axis=-1, keepdims=True))
    p_self = jnp.exp(s_self - m)
    l = p_self
    acc = p_self * mv_ref[0]
    first_row = lax.broadcasted_iota(I32, (8, PAGE_SIZE), 0) == 0
    for s, vp in zip(scores, vp_refs):
        p = jnp.exp(s - m)
        l = l + jnp.sum(p, axis=-1, keepdims=True)
        p8 = jnp.where(first_row, p, 0.0).astype(BF16)
        acc = acc + lax.dot_general(p8, vp[0, 0].astype(BF16), (((1,), (1,)), ((), ())),
                                    preferred_element_type=F32)[0:1, :]
    o_ref[0] = jnp.where(in_head, acc / l, o_ref[0])


def _moba_decode(layer, sel_pages, mq, mk_new, mv_new, cache_moba_k, cache_moba_v):
    b, nh, n_sel = sel_pages.shape
    row = pl.BlockSpec((1, 1, GROUP_WIDTH), lambda bi, h, pg: (bi, 0, 0))

    def page(j):
        return pl.BlockSpec((1, 1, GROUP_WIDTH, PAGE_SIZE),
                            lambda bi, h, pg: (layer, pg[(bi * nh + h) * n_sel + j], 0, 0))

    grid_spec = pltpu.PrefetchScalarGridSpec(
        num_scalar_prefetch=1,
        grid=(b, nh),
        in_specs=[row, row, row] + [page(j) for j in range(n_sel)] * 2,
        out_specs=pl.BlockSpec((1, 1, GROUP_WIDTH), lambda bi, h, pg: (bi, 0, 0)))
    r3 = lambda a: a.reshape(b, 1, GROUP_WIDTH)
    out = pl.pallas_call(
        functools.partial(_moba_decode_kernel, n_sel=n_sel),
        grid_spec=grid_spec,
        out_shape=jax.ShapeDtypeStruct((b, 1, GROUP_WIDTH), F32),
        compiler_params=pltpu.CompilerParams(dimension_semantics=("arbitrary",) * 2, vmem_limit_bytes=VMEM_LIMIT),
        name="moba_decode",
    )(sel_pages.reshape(-1).astype(I32), r3(mq), r3(mk_new), r3(mv_new),
      *([cache_moba_k] * n_sel), *([cache_moba_v] * n_sel))
    return out.reshape(b, GROUP_WIDTH)


def _sample_attention(layer, page_table, seg, caches):
    cache_moba_k, cache_moba_v, cache_sb_k, cache_sb_v = caches
    b, n_pages = page_table.shape
    ppb = MOBA_BLOCK // PAGE_SIZE
    n_past_blocks = n_pages // ppb
    assert n_pages % ppb == 0 and n_past_blocks >= MOBA_TOPK
    sb_out, gp = _decode_pass(layer, page_table, seg["sb_q"], seg["moba_q"], cache_sb_k, cache_sb_v, cache_moba_k)
    gates = gp[:, ::-1, :N_HEADS].reshape(b, n_past_blocks, ppb, N_HEADS).sum(axis=2)
    _, sel = lax.top_k(jnp.swapaxes(gates, 1, 2), MOBA_TOPK)
    pages = sel[..., None] * ppb + jnp.arange(ppb)
    sel_pages = jnp.take_along_axis(page_table[:, None, :], pages.reshape(b, N_HEADS, -1), axis=2)
    moba_out = _moba_decode(layer, sel_pages, seg["moba_q"], seg["moba_k"], seg["moba_v"], cache_moba_k, cache_moba_v)
    return moba_out, sb_out


def _kv_rows(seg, bsz, length):
    return tuple(seg[name].reshape(bsz, length, N_HEADS, HEAD_DIM) for name in ("moba_k", "moba_v", "sb_k", "sb_v"))


def _prompt_mixers(x, lw):
    (w_in, gdn_conv_w, gdn_a_log, gdn_dt_bias, gdn_norm_w, ssm_conv_w, ssm_conv_b,
     ssm_a_log, ssm_dt_bias, ssm_d, ssm_norm_w, _) = lw
    bsz, length, _ = x.shape
    n = bsz * length
    flat = _in_proj(x.reshape(n, D_MODEL), w_in, jnp.arange(length), length)
    seg = {k: v.reshape(bsz, length, -1) for k, v in flat.items()}

    gdn_out, gdn_conv_new, gdn_s_new = _gdn_prompt(seg["gdn_qkv"], seg["gdn_z"], seg["small"],
                                                   gdn_conv_w, gdn_a_log, gdn_dt_bias, gdn_norm_w)
    ssm_out, ssm_conv_new, ssm_h_new = _ssd_prompt(seg["ssm_xbc"], seg["ssm_z"], seg["small"], ssm_conv_w, ssm_conv_b,
                                                   ssm_a_log, ssm_dt_bias, ssm_d, ssm_norm_w)
    moba_out = _moba_attention(seg["moba_q"], seg["moba_k"], seg["moba_v"])
    sb_out = _sb_attention(seg["sb_q"], seg["sb_k"], seg["sb_v"])

    outs = [o.reshape(n, GROUP_WIDTH) for o in (gdn_out, ssm_out, moba_out, sb_out)]
    return outs, _kv_rows(seg, bsz, length) + (gdn_conv_new, gdn_s_new, ssm_conv_new, ssm_h_new)


def _sample_mixers(x, layer, page_table, rec_state, caches, lw):
    gdn_conv0, gdn_s0, ssm_conv0, ssm_h0 = rec_state
    bsz, length, _ = x.shape
    assert length == 1
    pos0 = page_table.shape[1] * PAGE_SIZE
    seg = _in_proj(x.reshape(bsz, D_MODEL), lw[0], jnp.full((1,), pos0, I32), 1)
    gdn_out, ssm_out, gdn_conv_new, ssm_conv_new, gdn_s_new, ssm_h_new = _rec_step(
        seg, gdn_conv0, gdn_s0, ssm_conv0, ssm_h0, lw)
    moba_out, sb_out = _sample_attention(layer, page_table, seg, caches)
    outs = [gdn_out, ssm_out, moba_out, sb_out]
    return outs, _kv_rows(seg, bsz, length) + (gdn_conv_new, gdn_s_new, ssm_conv_new, ssm_h_new)


def kernel(x_prompt, x_sample, cache_moba_k, cache_moba_v, cache_sb_k, cache_sb_v,
           state_gdn_conv, state_gdn_rec, state_ssm_conv, state_ssm_rec, page_table,
           w_in, gdn_conv_w, gdn_a_log, gdn_dt_bias, gdn_norm_w,
           ssm_conv_w, ssm_conv_b, ssm_a_log, ssm_dt_bias, ssm_d, ssm_norm_w,
           w_out, ln1_g, ln1_b, router_w, router_b,
           expert_w_gu, expert_b_gu, expert_w_down, expert_b_down, ln2_g, ln2_b):
    caches = tuple(jnp.transpose(c, (0, 1, 3, 4, 2)).reshape(c.shape[0], c.shape[1], GROUP_WIDTH, PAGE_SIZE)
                   for c in (cache_moba_k, cache_moba_v, cache_sb_k, cache_sb_v))
    yp, ys = x_prompt, x_sample
    new_p, new_s = [], []
    for l in range(DEPTH):
        lw = (w_in[l], gdn_conv_w[l], gdn_a_log[l], gdn_dt_bias[l], gdn_norm_w[l], ssm_conv_w[l], ssm_conv_b[l],
              ssm_a_log[l], ssm_dt_bias[l], ssm_d[l], ssm_norm_w[l], w_out[l])
        fw = (ln1_g[l], ln1_b[l], router_w[l], router_b[l], expert_w_gu, expert_b_gu,
              expert_w_down, expert_b_down, ln2_g[l], ln2_b[l])
        rec_state = (state_gdn_conv[l], state_gdn_rec[l], state_ssm_conv[l], state_ssm_rec[l])
        outs_p, st_p = _prompt_mixers(yp, lw)
        yp = _post_mix_moe(l, outs_p, yp.reshape(-1, D_MODEL), w_out[l], fw).reshape(yp.shape)
        outs_s, st_s = _sample_mixers(ys, l, page_table, rec_state, caches, lw)
        ys = _post_mix_moe(l, outs_s, ys.reshape(-1, D_MODEL), w_out[l], fw).reshape(ys.shape)
        new_p.append(st_p)
        new_s.append(st_s)

    def stk(states, i):
        return jnp.stack([s[i] for s in states])

    return (yp, ys,
            stk(new_p, 0), stk(new_p, 1), stk(new_p, 2), stk(new_p, 3),
            stk(new_p, 4), stk(new_p, 5), stk(new_p, 6), stk(new_p, 7),
            stk(new_s, 0), stk(new_s, 1), stk(new_s, 2), stk(new_s, 3),
            stk(new_s, 4), stk(new_s, 5), stk(new_s, 6), stk(new_s, 7))
```

```python
import functools

import jax
import jax.numpy as jnp
from jax import lax
from jax.experimental import pallas as pl
from jax.experimental.pallas import tpu as pltpu

D_MODEL = 1024
DEPTH = 2
PAGE_SIZE = 128
HEAD_DIM = 64
N_MIXERS = 4
GROUP_WIDTH = D_MODEL // N_MIXERS
N_HEADS = GROUP_WIDTH // HEAD_DIM
CONV_W = 4
CHUNK = 64
GDN_CONV_DIM = 3 * GROUP_WIDTH
SSM_STATE = 128
SSM_NGROUPS = 2
SSM_CONV_DIM = GROUP_WIDTH + 2 * SSM_NGROUPS * SSM_STATE
MOBA_BLOCK = 256
MOBA_TOPK = 3
MOBA_QUERY_BLOCK = 64
ROPE_THETA = 500000.0
ROPE_DIM = HEAD_DIM // 4
QUERY_BLOCK = 128
N_EXPERTS = 32
TOP_K = 4
D_FF = D_MODEL
SWIGLU_LIMIT = 7.0
SWIGLU_ALPHA = 1.702
EXPERT_BLOCK = 256
DN_ALPHA = (2 * DEPTH) ** 0.25
LN_EPS = 1e-5
RMS_EPS = 1e-6

F32 = jnp.float32
BF16 = jnp.bfloat16
I32 = jnp.int32
HI = lax.Precision.HIGHEST

LANE = 128
VMEM_LIMIT = 56 * 1024 * 1024
NEG = -1e30

_SEG_SIZES = (GDN_CONV_DIM, GROUP_WIDTH, N_HEADS, N_HEADS, SSM_CONV_DIM, GROUP_WIDTH, N_HEADS, GDN_CONV_DIM, GDN_CONV_DIM)
_SEG_NAMES = ("gdn_qkv", "gdn_z", "gdn_b", "gdn_a", "ssm_xbc", "ssm_z", "ssm_dt", "moba_qkv", "sb_qkv")
_OUT_SEGS = (("gdn_qkv", GDN_CONV_DIM), ("gdn_z", GROUP_WIDTH), ("ssm_xbc", SSM_CONV_DIM), ("ssm_z", GROUP_WIDTH),
             ("moba_q", GROUP_WIDTH), ("moba_k", GROUP_WIDTH), ("moba_v", GROUP_WIDTH),
             ("sb_q", GROUP_WIDTH), ("sb_k", GROUP_WIDTH), ("sb_v", GROUP_WIDTH), ("small", LANE))
_ROTARY_SEGS = ("moba_q", "moba_k")


def _softplus(z):
    return jnp.maximum(z, 0.0) + jnp.log1p(jnp.exp(-jnp.abs(z)))


def _sigmoid(x):
    return 1.0 / (1.0 + jnp.exp(-x))


def _silu(x):
    return x * _sigmoid(x)


def _dot(a, b):
    return jnp.dot(a.astype(BF16), b.astype(BF16), preferred_element_type=F32)


def _dot_nt(a, b):
    return lax.dot_general(a.astype(BF16), b.astype(BF16), (((1,), (1,)), ((), ())), preferred_element_type=F32)


def _dot_tn(a, b):
    return lax.dot_general(a.astype(BF16), b.astype(BF16), (((0,), (0,)), ((), ())), preferred_element_type=F32)


def _dot_hi(a, b):
    return jnp.dot(a, b, preferred_element_type=F32, precision=HI)


def _split_dot(x, m_bf16):
    hi = x.astype(BF16)
    lo = (x - hi.astype(F32)).astype(BF16)
    return (jnp.dot(hi, m_bf16, preferred_element_type=F32)
            + jnp.dot(lo, m_bf16, preferred_element_type=F32))


def _split3(x):
    hi = x.astype(BF16)
    r = x - hi.astype(F32)
    mid = r.astype(BF16)
    return hi, mid, (r - mid.astype(F32)).astype(BF16)


def _dot3(a, b):
    a_hi = a.astype(BF16)
    a_lo = (a - a_hi.astype(F32)).astype(BF16)
    b_hi = b.astype(BF16)
    b_lo = (b - b_hi.astype(F32)).astype(BF16)
    return (jnp.dot(a_hi, b_hi, preferred_element_type=F32) + jnp.dot(a_hi, b_lo, preferred_element_type=F32)
            + jnp.dot(a_lo, b_hi, preferred_element_type=F32))


def _ln(h, g, b):
    mu = jnp.mean(h, axis=-1, keepdims=True)
    d = h - mu
    var = jnp.mean(d * d, axis=-1, keepdims=True)
    return d * lax.rsqrt(var + LN_EPS) * g + b


def _in_proj_kernel(x_ref, w_ref, cos_ref, sin_ref, *o_refs):
    y = jnp.dot(x_ref[...].astype(BF16), w_ref[...], preferred_element_type=F32)
    cos = jnp.concatenate([cos_ref[...]] * (GROUP_WIDTH // LANE), axis=1)
    sin = jnp.concatenate([sin_ref[...]] * (GROUP_WIDTH // LANE), axis=1)
    first_half = (lax.broadcasted_iota(I32, (1, GROUP_WIDTH), 1) % HEAD_DIM) < ROPE_DIM // 2
    off = 0
    for (name, width), o_ref in zip(_OUT_SEGS, o_refs):
        seg = y[:, off:off + width]
        if name in _ROTARY_SEGS:
            partner = jnp.where(first_half, pltpu.roll(seg, GROUP_WIDTH - ROPE_DIM // 2, 1), pltpu.roll(seg, ROPE_DIM // 2, 1))
            seg = seg * cos + partner * sin
        o_ref[...] = seg
        off += width


def _rotary_tables(pos):
    half = ROPE_DIM // 2
    inv = ROPE_THETA ** (-jnp.arange(half, dtype=F32) * 2.0 / ROPE_DIM)
    ang = pos.astype(F32)[:, None] * inv
    cc = jnp.arange(LANE) % HEAD_DIM
    cos = jnp.where(cc < ROPE_DIM, jnp.cos(ang)[:, cc % half], 1.0)
    sin = jnp.sin(ang)[:, cc % half]
    sin = jnp.where(cc < half, -sin, jnp.where(cc < ROPE_DIM, sin, 0.0))
    return cos, sin


def _in_proj(x, w_in, pos, seq_len):
    m = x.shape[0]
    tm = min(256, m)
    w_in = w_in.astype(BF16)
    cols, off = {}, 0
    for name, sz in zip(_SEG_NAMES, _SEG_SIZES):
        cols[name] = w_in[:, off:off + sz]
        off += sz
    for name in ("moba", "sb"):
        for j, part in enumerate("qkv"):
            cols[f"{name}_{part}"] = cols[f"{name}_qkv"][:, j * GROUP_WIDTH:(j + 1) * GROUP_WIDTH]
    cols["small"] = jnp.concatenate([cols["gdn_b"], cols["gdn_a"], cols["ssm_dt"],
                                     jnp.zeros((D_MODEL, LANE - 3 * N_HEADS), w_in.dtype)], axis=1)
    w = jnp.concatenate([cols[name] for name, _ in _OUT_SEGS], axis=1)
    n = w.shape[1]
    cos, sin = _rotary_tables(pos)
    if seq_len == 1:
        tab_spec = pl.BlockSpec((1, LANE), lambda i: (0, 0))
    else:
        assert seq_len % tm == 0
        tab_spec = pl.BlockSpec((tm, LANE), lambda i: (i % (seq_len // tm), 0))
    outs = pl.pallas_call(
        _in_proj_kernel,
        grid=(m // tm,),
        in_specs=[pl.BlockSpec((tm, D_MODEL), lambda i: (i, 0)),
                  pl.BlockSpec((D_MODEL, n), lambda i: (0, 0)), tab_spec, tab_spec],
        out_specs=[pl.BlockSpec((tm, width), lambda i: (i, 0)) for _, width in _OUT_SEGS],
        out_shape=[jax.ShapeDtypeStruct((m, width), F32) for _, width in _OUT_SEGS],
        compiler_params=pltpu.CompilerParams(dimension_semantics=("arbitrary",), vmem_limit_bytes=VMEM_LIMIT),
        name="in_proj",
    )(x, w, cos, sin)
    return {name: o for (name, _), o in zip(_OUT_SEGS, outs)}


def _sb_kernel(q_ref, k_ref, v_ref, o_ref, *, tq):
    i = pl.program_id(1)
    row = lax.broadcasted_iota(I32, (tq, tq), 0)
    col = lax.broadcasted_iota(I32, (tq, tq), 1)
    upper = (row > col).astype(BF16)
    past = col < row

    heads = [slice(h * HEAD_DIM, (h + 1) * HEAD_DIM) for h in range(N_HEADS)]
    qs = [(q_ref[0, :, hs] * (HEAD_DIM ** -0.5)).astype(BF16) for hs in heads]

    def block(j, state, diag):
        out = []
        for hs, q, (carry, acc) in zip(heads, qs, state):
            kj = k_ref[0, pl.ds(j * tq, tq), hs].astype(BF16)
            vj = v_ref[0, pl.ds(j * tq, tq), hs].astype(BF16)
            z = lax.dot_general(q, kj, (((1,), (1,)), ((), ())), preferred_element_type=F32)
            ls = -_softplus(z)
            lk = jnp.where(past, ls, 0.0) if diag else ls
            la = _split_dot(lk, upper)
            w = jnp.exp(z + ls + la + carry)
            if diag:
                w = jnp.where(past, w, 0.0)
            acc = acc + jnp.dot(w.astype(BF16), vj, preferred_element_type=F32)
            out.append((carry + la[:, 0:1] + lk[:, 0:1], acc))
        return tuple(out)

    state = tuple((jnp.zeros((tq, 1), F32), jnp.zeros((tq, HEAD_DIM), F32)) for _ in heads)
    state = block(i, state, True)
    state = lax.fori_loop(0, i, lambda t, c: block(i - 1 - t, c, False), state)
    for hs, (_, acc) in zip(heads, state):
        o_ref[0, :, hs] = acc


def _sb_attention(q, k, v, tq=256):
    b, l, d = q.shape
    assert l % tq == 0
    return pl.pallas_call(
        functools.partial(_sb_kernel, tq=tq),
        grid=(b, l // tq),
        in_specs=[pl.BlockSpec((1, tq, d), lambda bi, i: (bi, i, 0)),
                  pl.BlockSpec((1, l, d), lambda bi, i: (bi, 0, 0)),
                  pl.BlockSpec((1, l, d), lambda bi, i: (bi, 0, 0))],
        out_specs=pl.BlockSpec((1, tq, d), lambda bi, i: (bi, i, 0)),
        out_shape=jax.ShapeDtypeStruct((b, l, d), F32),
        compiler_params=pltpu.CompilerParams(dimension_semantics=("arbitrary",) * 2, vmem_limit_bytes=VMEM_LIMIT),
        name="sb_attention",
    )(q, k, v)


def _moba_kernel(q_ref, k_ref, v_ref, o_ref, kmean_ref, *, nb):
    t = MOBA_BLOCK
    i = pl.program_id(1)

    @pl.when(i == 0)
    def _():
        for n in range(nb):
            kmean_ref[n:n + 1, :] = jnp.mean(k_ref[0, n * t:(n + 1) * t, :], axis=0, keepdims=True)

    blk_id = lax.broadcasted_iota(I32, (nb, t), 0)
    row = lax.broadcasted_iota(I32, (t, t), 0)
    col = lax.broadcasted_iota(I32, (t, t), 1)

    heads = [slice(h * HEAD_DIM, (h + 1) * HEAD_DIM) for h in range(N_HEADS)]
    qs, sels, state = [], [], []
    for hs in heads:
        qf = q_ref[0, :, hs]
        q = (qf * (HEAD_DIM ** -0.5)).astype(BF16)
        gate = lax.dot_general(kmean_ref[:, hs], qf, (((1,), (1,)), ((), ())),
                               preferred_element_type=F32, precision=HI)
        cnt = jnp.zeros((nb, t), F32)
        for n2 in range(nb):
            cn = gate[n2:n2 + 1, :]
            beats = (cn > gate) | ((cn == gate) & (n2 < blk_id))
            cnt = cnt + jnp.where(beats & (n2 < i), 1.0, 0.0)
        sels.append(jnp.where((cnt < MOBA_TOPK) & (blk_id < i), 1.0, 0.0).astype(BF16))
        qs.append(q)

        ki = k_ref[0, pl.ds(i * t, t), hs].astype(BF16)
        vi = v_ref[0, pl.ds(i * t, t), hs].astype(BF16)
        s = lax.dot_general(q, ki, (((1,), (1,)), ((), ())), preferred_element_type=F32)
        s = jnp.where(col <= row, s, NEG)
        m = jnp.max(s, axis=-1, keepdims=True)
        p = jnp.exp(s - m)
        state.append((m, jnp.sum(p, axis=-1, keepdims=True), jnp.dot(p.astype(BF16), vi, preferred_element_type=F32)))

    def body(j, state):
        pick = ((lax.broadcasted_iota(I32, (nb, 8), 0) == j) & (lax.broadcasted_iota(I32, (nb, 8), 1) == 0)).astype(BF16)
        out = []
        for hs, q, sel, (m, l, acc) in zip(heads, qs, sels, state):
            kj = k_ref[0, pl.ds(j * t, t), hs].astype(BF16)
            vj = v_ref[0, pl.ds(j * t, t), hs].astype(BF16)
            s = lax.dot_general(q, kj, (((1,), (1,)), ((), ())), preferred_element_type=F32)
            sj = lax.dot_general(sel, pick, (((0,), (0,)), ((), ())), preferred_element_type=F32)[:, 0:1]
            s = jnp.where(sj > 0.5, s, NEG)
            m_new = jnp.maximum(m, jnp.max(s, axis=-1, keepdims=True))
            a = jnp.exp(m - m_new)
            p = jnp.exp(s - m_new)
            out.append((m_new, a * l + jnp.sum(p, axis=-1, keepdims=True),
                        a * acc + jnp.dot(p.astype(BF16), vj, preferred_element_type=F32)))
        return tuple(out)

    state = lax.fori_loop(0, i, body, tuple(state))
    for hs, (_, l, acc) in zip(heads, state):
        o_ref[0, :, hs] = acc / l


def _moba_attention(q, k, v):
    b, l, d = q.shape
    t = MOBA_BLOCK
    assert l % t == 0
    nb = l // t
    return pl.pallas_call(
        functools.partial(_moba_kernel, nb=nb),
        grid=(b, nb),
        in_specs=[pl.BlockSpec((1, t, d), lambda bi, i: (bi, i, 0)),
                  pl.BlockSpec((1, l, d), lambda bi, i: (bi, 0, 0)),
                  pl.BlockSpec((1, l, d), lambda bi, i: (bi, 0, 0))],
        out_specs=pl.BlockSpec((1, t, d), lambda bi, i: (bi, i, 0)),
        out_shape=jax.ShapeDtypeStruct((b, l, d), F32),
        scratch_shapes=[pltpu.VMEM((nb, d), F32)],
        compiler_params=pltpu.CompilerParams(dimension_semantics=("arbitrary",) * 2, vmem_limit_bytes=VMEM_LIMIT),
        name="moba_attention",
    )(q, k, v)


def _conv_tile(ext_ref, u, w_ref, tl):
    ext_ref[8:8 + tl, :] = u
    y = ext_ref[5:5 + tl, :] * w_ref[0:1, :]
    for i in range(1, CONV_W):
        y = y + ext_ref[5 + i:5 + i + tl, :] * w_ref[i:i + 1, :]
    tail = ext_ref[tl + 5:tl + 8, :]
    ext_ref[5:8, :] = tail
    return y, tail


def _gdn_kernel(qkv_ref, z_ref, small_ref, convw_ref, par_ref, normw_ref,
                o_ref, conv_out_ref, s_out_ref, ext_ref, s_ref, *, tl):
    c = CHUNK
    step = pl.program_id(1)

    @pl.when(step == 0)
    def _():
        ext_ref[...] = jnp.zeros_like(ext_ref)
        s_ref[...] = jnp.zeros_like(s_ref)

    y, tail = _conv_tile(ext_ref, qkv_ref[0], convw_ref, tl)
    conv_out_ref[0] = tail
    y = _silu(y)
    small = small_ref[0]
    beta_all = _sigmoid(small[:, 0:N_HEADS])
    g_all = par_ref[0:1, :] * _softplus(small[:, N_HEADS:2 * N_HEADS] + par_ref[1:2, :])

    row = lax.broadcasted_iota(I32, (tl, tl), 0)
    col = lax.broadcasted_iota(I32, (tl, tl), 1)
    same = (row // c) == (col // c)
    tril = same & (row >= col)
    strict = same & (row > col)
    tril_b = tril.astype(BF16)
    triu_b = (same & (row <= col)).astype(BF16)
    eye = (row == col).astype(F32)

    g8 = jnp.concatenate([g_all, jnp.zeros((tl, 8 - N_HEADS), F32)], axis=1)
    g_parts = _split3(g8)
    gc_cols = sum(jnp.dot(tril_b, part, preferred_element_type=F32) for part in g_parts)
    gc_rows = sum(lax.dot_general(part, triu_b, (((0,), (0,)), ((), ())), preferred_element_type=F32) for part in g_parts)

    for h in range(N_HEADS):
        hs = slice(h * HEAD_DIM, (h + 1) * HEAD_DIM)
        q = y[:, h * HEAD_DIM:(h + 1) * HEAD_DIM]
        k = y[:, GROUP_WIDTH + h * HEAD_DIM:GROUP_WIDTH + (h + 1) * HEAD_DIM]
        v = y[:, 2 * GROUP_WIDTH + h * HEAD_DIM:2 * GROUP_WIDTH + (h + 1) * HEAD_DIM]
        q = q * lax.rsqrt(jnp.sum(q * q, axis=-1, keepdims=True) + RMS_EPS) * (HEAD_DIM ** -0.5)
        k = k * lax.rsqrt(jnp.sum(k * k, axis=-1, keepdims=True) + RMS_EPS)
        beta = beta_all[:, h:h + 1]
        gcol = gc_cols[:, h:h + 1]
        grow = gc_rows[h:h + 1, :]
        decay = jnp.where(tril, jnp.exp(jnp.where(tril, gcol - grow, 0.0)), 0.0)
        kb = k * beta
        a = jnp.where(strict, _dot_nt(kb, k) * decay, 0.0)
        p = -a
        tinv = eye + p
        for _ in range(5):
            p = _dot3(p, p)
            tinv = tinv + _dot3(tinv, p)
        u = _dot(tinv, v * beta)
        w = _dot(tinv, kb * jnp.exp(gcol))
        attn = jnp.where(tril, _dot_nt(q, k) * decay, 0.0)
        qe = q * jnp.exp(gcol)
        s = s_ref[h]
        for ci in range(tl // c):
            rs = slice(ci * c, (ci + 1) * c)
            g_last = gcol[(ci + 1) * c - 1:(ci + 1) * c, :]
            v_new = u[rs] - _dot(w[rs], s)
            o = _dot(qe[rs], s) + _dot(attn[rs, rs], v_new)
            s = s * jnp.exp(g_last) + _dot_tn(k[rs] * jnp.exp(g_last - gcol[rs]), v_new)
            o = o * lax.rsqrt(jnp.mean(o * o, axis=-1, keepdims=True) + RMS_EPS) * normw_ref[...] * _silu(z_ref[0, rs, hs])
            o_ref[0, rs, hs] = o
        s_ref[h] = s

    @pl.when(step == pl.num_programs(1) - 1)
    def _():
        s_out_ref[0] = s_ref[...]


def _gdn_prompt(qkv, z, small, conv_w, a_log, dt_bias, norm_w, tl=256):
    b, l, _ = qkv.shape
    assert l % tl == 0 and tl % CHUNK == 0
    par = jnp.stack([-jnp.exp(a_log.astype(F32)), dt_bias.astype(F32)])
    return pl.pallas_call(
        functools.partial(_gdn_kernel, tl=tl),
        grid=(b, l // tl),
        in_specs=[pl.BlockSpec((1, tl, GDN_CONV_DIM), lambda bi, i: (bi, i, 0)),
                  pl.BlockSpec((1, tl, GROUP_WIDTH), lambda bi, i: (bi, i, 0)),
                  pl.BlockSpec((1, tl, LANE), lambda bi, i: (bi, i, 0)),
                  pl.BlockSpec((CONV_W, GDN_CONV_DIM), lambda bi, i: (0, 0)),
                  pl.BlockSpec((2, N_HEADS), lambda bi, i: (0, 0)),
                  pl.BlockSpec((1, HEAD_DIM), lambda bi, i: (0, 0))],
        out_specs=[pl.BlockSpec((1, tl, GROUP_WIDTH), lambda bi, i: (bi, i, 0)),
                   pl.BlockSpec((1, CONV_W - 1, GDN_CONV_DIM), lambda bi, i: (bi, 0, 0)),
                   pl.BlockSpec((1, N_HEADS, HEAD_DIM, HEAD_DIM), lambda bi, i: (bi, 0, 0, 0))],
        out_shape=[jax.ShapeDtypeStruct((b, l, GROUP_WIDTH), F32),
                   jax.ShapeDtypeStruct((b, CONV_W - 1, GDN_CONV_DIM), F32),
                   jax.ShapeDtypeStruct((b, N_HEADS, HEAD_DIM, HEAD_DIM), F32)],
        scratch_shapes=[pltpu.VMEM((tl + 8, GDN_CONV_DIM), F32),
                        pltpu.VMEM((N_HEADS, HEAD_DIM, HEAD_DIM), F32)],
        compiler_params=pltpu.CompilerParams(dimension_semantics=("arbitrary", "arbitrary"),
                                             vmem_limit_bytes=VMEM_LIMIT),
        name="gdn_prompt",
    )(qkv, z, small, conv_w, par, norm_w.reshape(1, HEAD_DIM))


def _ssd_kernel(xbc_ref, z_ref, small_ref, convw_ref, convb_ref, par_ref, normw_ref,
                o_ref, conv_out_ref, h_out_ref, ext_ref, h_ref, *, tl):
    c = CHUNK
    step = pl.program_id(1)
    hpg = N_HEADS // SSM_NGROUPS

    @pl.when(step == 0)
    def _():
        ext_ref[...] = jnp.zeros_like(ext_ref)
        h_ref[...] = jnp.zeros_like(h_ref)

    y, tail = _conv_tile(ext_ref, xbc_ref[0], convw_ref, tl)
    conv_out_ref[0] = tail
    xbc = _silu(y + convb_ref[...])
    dt_all = _softplus(small_ref[0][:, 2 * N_HEADS:3 * N_HEADS] + par_ref[1:2, :])
    ad_all = dt_all * par_ref[0:1, :]

    row = lax.broadcasted_iota(I32, (c, c), 0)
    col = lax.broadcasted_iota(I32, (c, c), 1)
    tril = row >= col
    tril_f = tril.astype(F32)
    triu_f = (row <= col).astype(F32)
    ones = jnp.ones((c, c), F32)
    boff = GROUP_WIDTH
    coff = GROUP_WIDTH + SSM_NGROUPS * SSM_STATE

    for ci in range(tl // c):
        r0 = ci * c
        ad_c = ad_all[r0:r0 + c, :]
        acs_all = _dot_hi(tril_f, ad_c)
        ys = []
        for h in range(N_HEADS):
            grp = h // hpg
            x = xbc[r0:r0 + c, h * HEAD_DIM:(h + 1) * HEAD_DIM]
            bm = xbc[r0:r0 + c, boff + grp * SSM_STATE:boff + (grp + 1) * SSM_STATE]
            cm = xbc[r0:r0 + c, coff + grp * SSM_STATE:coff + (grp + 1) * SSM_STATE]
            xdt = x * dt_all[r0:r0 + c, h:h + 1]
            acol = acs_all[:, h:h + 1]
            arow = _dot_hi(ones, ad_c[:, h:h + 1] * triu_f)
            lmat = jnp.where(tril, jnp.exp(jnp.where(tril, acol - arow, 0.0)), 0.0)
            y_diag = _dot(_dot_nt(cm, bm) * lmat, xdt)
            a_last = acol[c - 1:c, :]
            states = _dot_tn(xdt * jnp.exp(a_last - acol), bm)
            hin = h_ref[h]
            y_off = _dot_nt(cm, hin) * jnp.exp(acol)
            h_ref[h] = hin * jnp.exp(a_last) + states
            yh = y_diag + y_off + par_ref[2:3, h:h + 1] * x
            ys.append(yh * _silu(z_ref[0, r0:r0 + c, h * HEAD_DIM:(h + 1) * HEAD_DIM]))
        for grp in range(SSM_NGROUPS):
            grp_ys = ys[grp * hpg:(grp + 1) * hpg]
            ms = sum(jnp.sum(a * a, axis=-1, keepdims=True) for a in grp_ys) / (hpg * HEAD_DIM)
            r = lax.rsqrt(ms + RMS_EPS)
            for j, a in enumerate(grp_ys):
                h = grp * hpg + j
                o_ref[0, r0:r0 + c, h * HEAD_DIM:(h + 1) * HEAD_DIM] = a * r * normw_ref[:, h * HEAD_DIM:(h + 1) * HEAD_DIM]

    @pl.when(step == pl.num_programs(1) - 1)
    def _():
        h_out_ref[0] = h_ref[...]


def _ssd_prompt(xbc, z, small, conv_w, conv_b, a_log, dt_bias, d_skip, norm_w, tl=256):
    b, l, _ = xbc.shape
    assert l % tl == 0 and tl % CHUNK == 0
    par = jnp.stack([-jnp.exp(a_log.astype(F32)), dt_bias.astype(F32), d_skip.astype(F32)])
    return pl.pallas_call(
        functools.partial(_ssd_kernel, tl=tl),
        grid=(b, l // tl),
        in_specs=[pl.BlockSpec((1, tl, SSM_CONV_DIM), lambda bi, i: (bi, i, 0)),
                  pl.BlockSpec((1, tl, GROUP_WIDTH), lambda bi, i: (bi, i, 0)),
                  pl.BlockSpec((1, tl, LANE), lambda bi, i: (bi, i, 0)),
                  pl.BlockSpec((CONV_W, SSM_CONV_DIM), lambda bi, i: (0, 0)),
                  pl.BlockSpec((1, SSM_CONV_DIM), lambda bi, i: (0, 0)),
                  pl.BlockSpec((3, N_HEADS), lambda bi, i: (0, 0)),
                  pl.BlockSpec((1, GROUP_WIDTH), lambda bi, i: (0, 0))],
        out_specs=[pl.BlockSpec((1, tl, GROUP_WIDTH), lambda bi, i: (bi, i, 0)),
                   pl.BlockSpec((1, CONV_W - 1, SSM_CONV_DIM), lambda bi, i: (bi, 0, 0)),
                   pl.BlockSpec((1, N_HEADS, HEAD_DIM, SSM_STATE), lambda bi, i: (bi, 0, 0, 0))],
        out_shape=[jax.ShapeDtypeStruct((b, l, GROUP_WIDTH), F32),
                   jax.ShapeDtypeStruct((b, CONV_W - 1, SSM_CONV_DIM), F32),
                   jax.ShapeDtypeStruct((b, N_HEADS, HEAD_DIM, SSM_STATE), F32)],
        scratch_shapes=[pltpu.VMEM((tl + 8, SSM_CONV_DIM), F32),
                        pltpu.VMEM((N_HEADS, HEAD_DIM, SSM_STATE), F32)],
        compiler_params=pltpu.CompilerParams(dimension_semantics=("arbitrary", "arbitrary"),
                                             vmem_limit_bytes=VMEM_LIMIT),
        name="ssd_prompt",
    )(xbc, z, small, conv_w, conv_b.reshape(1, SSM_CONV_DIM), par, norm_w.reshape(1, GROUP_WIDTH))


def _post_mix_kernel(o0_ref, o1_ref, o2_ref, o3_ref, x_ref, wout_ref, g_ref, b_ref, rw_ref, rb_ref,
                     xn_ref, xnb_ref, idx_ref, gate_ref, rank_ref, cum_ref, cnt_ref, *, tm):
    @pl.when(pl.program_id(0) == 0)
    def _():
        cnt_ref[...] = jnp.zeros_like(cnt_ref)

    mix = jnp.zeros((tm, D_MODEL), F32)
    for gi, o_ref in enumerate((o0_ref, o1_ref, o2_ref, o3_ref)):
        mix = mix + jnp.dot(o_ref[...].astype(BF16), wout_ref[gi * GROUP_WIDTH:(gi + 1) * GROUP_WIDTH, :],
                            preferred_element_type=F32)
    xn = _ln(DN_ALPHA * x_ref[...] + mix, g_ref[...], b_ref[...])
    xn_ref[...] = xn
    xnb_ref[...] = xn.astype(BF16)

    logits = _dot_hi(xn, rw_ref[...]) + rb_ref[...]
    lane_e = lax.broadcasted_iota(I32, (tm, N_EXPERTS), 1).astype(F32)
    lane = lax.broadcasted_iota(I32, (tm, LANE), 1)
    work = logits
    sel = jnp.zeros((tm, N_EXPERTS), F32)
    idx_out = jnp.zeros((tm, LANE), I32)
    val_out = jnp.zeros((tm, LANE), F32)
    hits = []
    for r in range(TOP_K):
        m = jnp.max(work, axis=-1, keepdims=True)
        pick = jnp.min(jnp.where(work == m, lane_e, float(N_EXPERTS)), axis=-1, keepdims=True)
        hit = lane_e == pick
        work = jnp.where(hit, -jnp.inf, work)
        sel = sel + hit.astype(F32)
        hits.append(hit)
        idx_out = jnp.where(lane == r, pick.astype(I32), idx_out)
        val_out = jnp.where(lane == r, m, val_out)
    ex = jnp.where(lane < TOP_K, jnp.exp(val_out - val_out[:, 0:1]), 0.0)
    gate_ref[...] = ex / jnp.sum(ex, axis=-1, keepdims=True)
    idx_ref[...] = idx_out

    row = lax.broadcasted_iota(I32, (tm, tm), 0)
    col = lax.broadcasted_iota(I32, (tm, tm), 1)
    before = jnp.dot((row > col).astype(BF16), sel.astype(BF16), preferred_element_type=F32) + cnt_ref[...]
    rank_out = jnp.zeros((tm, LANE), I32)
    for r in range(TOP_K):
        rk = jnp.sum(jnp.where(hits[r], before, 0.0), axis=-1, keepdims=True)
        rank_out = jnp.where(lane == r, rk.astype(I32), rank_out)
    rank_ref[...] = rank_out
    cum_ref[0] = jnp.broadcast_to(cnt_ref[...], (8, N_EXPERTS))
    cnt_ref[...] = cnt_ref[...] + jnp.sum(sel, axis=0, keepdims=True)


def _post_mix(outs, x, w_out_bf16, ln_g, ln_b, router_w, router_b, tm):
    n = x.shape[0]
    nt = n // tm
    tok = lambda w: pl.BlockSpec((tm, w), lambda i: (i, 0))
    full = lambda a: pl.BlockSpec(a.shape, lambda i: (0,) * a.ndim)
    g2, b2, rb2 = ln_g.reshape(1, -1), ln_b.reshape(1, -1), router_b.reshape(1, -1)
    return pl.pallas_call(
        functools.partial(_post_mix_kernel, tm=tm),
        grid=(nt,),
        in_specs=[tok(GROUP_WIDTH)] * 4 + [tok(D_MODEL), full(w_out_bf16), full(g2), full(b2), full(router_w), full(rb2)],
        out_specs=[tok(D_MODEL), tok(D_MODEL), tok(LANE), tok(LANE), tok(LANE),
                   pl.BlockSpec((1, 8, N_EXPERTS), lambda i: (i, 0, 0))],
        out_shape=[jax.ShapeDtypeStruct((n, D_MODEL), F32), jax.ShapeDtypeStruct((n, D_MODEL), BF16),
                   jax.ShapeDtypeStruct((n, LANE), I32), jax.ShapeDtypeStruct((n, LANE), F32),
                   jax.ShapeDtypeStruct((n, LANE), I32), jax.ShapeDtypeStruct((nt, 8, N_EXPERTS), F32)],
        scratch_shapes=[pltpu.VMEM((1, N_EXPERTS), F32)],
        compiler_params=pltpu.CompilerParams(dimension_semantics=("arbitrary",), vmem_limit_bytes=VMEM_LIMIT),
        name="post_mix",
    )(*outs, x, w_out_bf16, g2, b2, router_w, rb2)


def _moe_dispatch(idx, gates, rank, cum, n_tok, tm_cum, tm, blk):
    nt = n_tok // tm
    n_blocks = -(-n_tok * TOP_K // blk) + N_EXPERTS
    per_pair = -(-tm // blk) + 1
    e_idx = idx[:, :TOP_K]
    cum_i = cum[:, 0, :].astype(I32)
    onehot = e_idx[:, :, None] == jnp.arange(N_EXPERTS, dtype=I32)
    counts = cum_i[-1] + jnp.sum(onehot[n_tok - tm_cum:].astype(I32), axis=(0, 1))
    cum_t = cum_i[::tm // tm_cum]
    cum_all = jnp.concatenate([cum_t, counts[None]], 0)
    blocks_per_e = (counts + blk - 1) // blk
    blk_start = jnp.cumsum(blocks_per_e) - blocks_per_e
    dest = jnp.sum(jnp.where(onehot, blk_start, 0), axis=-1) * blk + rank[:, :TOP_K]
    dest_f = dest.astype(F32)
    dest_pad = jnp.pad(dest_f, ((0, 0), (0, LANE - TOP_K)), constant_values=-1.0)
    dest_t = jnp.pad(dest_f.T, ((0, 8 - TOP_K), (0, 0)), constant_values=-1.0)
    gate_t = jnp.pad(gates[:, :TOP_K].T, ((0, 8 - TOP_K), (0, 0)))
    eid_t = jnp.pad(e_idx.T, ((0, 8 - TOP_K), (0, 0)), constant_values=-1)

    lo, hi = cum_all[:-1].T, cum_all[1:].T
    has = hi > lo
    j0 = lo // blk
    j1 = jnp.maximum(hi - 1, 0) // blk
    jj = j0[:, :, None] + jnp.arange(per_pair, dtype=I32)
    valid = has[:, :, None] & (jj <= j1[:, :, None])
    blk_id = blk_start[:, None, None] + jj
    e_b = jnp.broadcast_to(jnp.arange(N_EXPERTS, dtype=I32)[:, None, None], jj.shape)
    t_b = jnp.broadcast_to(jnp.arange(nt, dtype=I32)[None, :, None], jj.shape)
    n_steps = min(N_EXPERTS * nt + n_blocks, per_pair * N_EXPERTS * nt)
    n_valid = jnp.sum(valid)
    pos = jnp.arange(n_steps)
    v = pos < n_valid

    def build(key):
        order = jnp.argsort(jnp.where(valid, key, jnp.int32(1 << 30)).reshape(-1))[:n_steps]
        src = order[jnp.minimum(pos, n_valid - 1)]
        return blk_id.reshape(-1)[src], t_b.reshape(-1)[src], e_b.reshape(-1)[src]

    def edges(a):
        change = a[1:] != a[:-1]
        first = jnp.concatenate([jnp.ones((1,), bool), change]) & v
        last = (jnp.concatenate([change, jnp.ones((1,), bool)]) | (pos == n_valid - 1)) & v
        return first, last

    b1, t1, e1 = build(blk_id * nt + t_b)
    first_b, last_b = edges(b1)
    first_e, _ = edges(e1)
    flags1 = first_b.astype(I32) + 2 * last_b.astype(I32) + 4 * v.astype(I32) + 8 * first_e.astype(I32)
    b2, t2, e2 = build(t_b * (2 * n_blocks) + blk_id)
    first_t, last_t = edges(t2)
    flags2 = first_t.astype(I32) + 2 * last_t.astype(I32) + 4 * v.astype(I32)
    return dict(dest=dest_pad, eid=idx, dest_t=dest_t, gate_t=gate_t, eid_t=eid_t,
                n_rows=n_blocks * blk, n_steps=n_steps,
                gather=(b1.astype(I32), t1.astype(I32), e1.astype(I32), flags1),
                combine=(b2.astype(I32), t2.astype(I32), e2.astype(I32), flags2))


def _expert_kernel(blk_s, tile_s, exp_s, flag_s, x_ref, eid_ref, dest_ref, gate_ref, wgu_ref, bgu_ref, wdn_ref, bdn_ref,
                   yb_ref, xacc_ref, gacc_ref, wgu_bf, wdn_bf, *, blk, tm):
    s = pl.program_id(0)
    flags = flag_s[s]

    @pl.when((flags & 8) != 0)
    def _():
        wgu_bf[...] = wgu_ref[0, 0].astype(BF16)
        wdn_bf[...] = wdn_ref[0, 0].astype(BF16)

    @pl.when((flags & 1) != 0)
    def _():
        xacc_ref[...] = jnp.zeros_like(xacc_ref)
        gacc_ref[...] = jnp.zeros_like(gacc_ref)

    @pl.when((flags & 4) != 0)
    def _():
        mine = eid_ref[...] == exp_s[s]
        d_sel = jnp.sum(jnp.where(mine, dest_ref[...] + 1.0, 0.0), axis=0, keepdims=True) - 1.0
        g_sel = jnp.sum(jnp.where(mine, gate_ref[...], 0.0), axis=0, keepdims=True)
        rowid = (blk_s[s] * blk + lax.broadcasted_iota(I32, (blk, tm), 0)).astype(F32)
        hit = d_sel == rowid
        p = jnp.where(hit, 1.0, 0.0).astype(BF16)
        xacc_ref[...] += jnp.dot(p, x_ref[...], preferred_element_type=F32)
        gacc_ref[...] += jnp.sum(jnp.where(hit, g_sel, 0.0), axis=-1, keepdims=True)

    @pl.when((flags & 2) != 0)
    def _():
        h = jnp.dot(xacc_ref[...].astype(BF16), wgu_bf[...], preferred_element_type=F32) + bgu_ref[0, 0]
        gate = jnp.minimum(h[:, :D_FF], SWIGLU_LIMIT)
        up = jnp.clip(h[:, D_FF:], -SWIGLU_LIMIT, SWIGLU_LIMIT)
        act = (up + 1.0) * gate * _sigmoid(SWIGLU_ALPHA * gate)
        y = jnp.dot(act.astype(BF16), wdn_bf[...], preferred_element_type=F32) + bdn_ref[0, 0]
        yb_ref[...] = (y * gacc_ref[...]).astype(yb_ref.dtype)


def _expert_ffn(layer, xn_bf16, disp, w_gu, b_gu, w_down, b_down, tm, blk):
    blk_s, tile_s, exp_s, flag_s = disp["gather"]
    depth = w_gu.shape[0]
    tok_t = pl.BlockSpec((8, tm), lambda s, b, t, e, f: (0, t[s]))
    grid_spec = pltpu.PrefetchScalarGridSpec(
        num_scalar_prefetch=4,
        grid=(disp["n_steps"],),
        in_specs=[pl.BlockSpec((tm, D_MODEL), lambda s, b, t, e, f: (t[s], 0)), tok_t, tok_t, tok_t,
                  pl.BlockSpec((1, 1, D_MODEL, 2 * D_FF), lambda s, b, t, e, f: (layer, e[s], 0, 0)),
                  pl.BlockSpec((1, 1, 1, 2 * D_FF), lambda s, b, t, e, f: (layer, e[s], 0, 0)),
                  pl.BlockSpec((1, 1, D_FF, D_MODEL), lambda s, b, t, e, f: (layer, e[s], 0, 0)),
                  pl.BlockSpec((1, 1, 1, D_MODEL), lambda s, b, t, e, f: (layer, e[s], 0, 0))],
        out_specs=pl.BlockSpec((blk, D_MODEL), lambda s, b, t, e, f: (b[s], 0)),
        scratch_shapes=[pltpu.VMEM((blk, D_MODEL), F32), pltpu.VMEM((blk, 1), F32),
                        pltpu.VMEM((D_MODEL, 2 * D_FF), BF16), pltpu.VMEM((D_FF, D_MODEL), BF16)])
    return pl.pallas_call(
        functools.partial(_expert_kernel, blk=blk, tm=tm),
        grid_spec=grid_spec,
        out_shape=jax.ShapeDtypeStruct((disp["n_rows"], D_MODEL), BF16),
        compiler_params=pltpu.CompilerParams(dimension_semantics=("arbitrary",), vmem_limit_bytes=VMEM_LIMIT),
        name="expert_ffn",
    )(blk_s, tile_s, exp_s, flag_s, xn_bf16, disp["eid_t"], disp["dest_t"], disp["gate_t"],
      w_gu, b_gu.reshape(depth, N_EXPERTS, 1, 2 * D_FF), w_down, b_down.reshape(depth, N_EXPERTS, 1, D_MODEL))


def _combine_kernel(blk_s, tile_s, exp_s, flag_s, yb_ref, eid_ref, dest_ref, xn_ref, g_ref, b_ref, y_ref, acc_ref,
                    *, blk, tm):
    s = pl.program_id(0)
    flags = flag_s[s]

    @pl.when((flags & 1) != 0)
    def _():
        acc_ref[...] = jnp.zeros_like(acc_ref)

    @pl.when((flags & 4) != 0)
    def _():
        lane = lax.broadcasted_iota(I32, (tm, LANE), 1)
        mine = (eid_ref[...] == exp_s[s]) & (lane < TOP_K)
        d_sel = jnp.sum(jnp.where(mine, dest_ref[...] + 1.0, 0.0), axis=-1, keepdims=True) - 1.0
        rowid = (blk_s[s] * blk + lax.broadcasted_iota(I32, (tm, blk), 1)).astype(F32)
        p = jnp.where(d_sel == rowid, 1.0, 0.0).astype(BF16)
        acc_ref[...] += jnp.dot(p, yb_ref[...], preferred_element_type=F32)

    @pl.when((flags & 2) != 0)
    def _():
        y_ref[...] = _ln(DN_ALPHA * xn_ref[...] + acc_ref[...], g_ref[...], b_ref[...])


def _moe_combine(yb, disp, xn, ln_g, ln_b, tm, blk):
    blk_s, tile_s, exp_s, flag_s = disp["combine"]
    n = xn.shape[0]
    tok = pl.BlockSpec((tm, LANE), lambda s, b, t, e, f: (t[s], 0))
    grid_spec = pltpu.PrefetchScalarGridSpec(
        num_scalar_prefetch=4,
        grid=(disp["n_steps"],),
        in_specs=[pl.BlockSpec((blk, D_MODEL), lambda s, b, t, e, f: (b[s], 0)), tok, tok,
                  pl.BlockSpec((tm, D_MODEL), lambda s, b, t, e, f: (t[s], 0)),
                  pl.BlockSpec((1, D_MODEL), lambda s, b, t, e, f: (0, 0)),
                  pl.BlockSpec((1, D_MODEL), lambda s, b, t, e, f: (0, 0))],
        out_specs=pl.BlockSpec((tm, D_MODEL), lambda s, b, t, e, f: (t[s], 0)),
        scratch_shapes=[pltpu.VMEM((tm, D_MODEL), F32)])
    return pl.pallas_call(
        functools.partial(_combine_kernel, blk=blk, tm=tm),
        grid_spec=grid_spec,
        out_shape=jax.ShapeDtypeStruct((n, D_MODEL), F32),
        compiler_params=pltpu.CompilerParams(dimension_semantics=("arbitrary",), vmem_limit_bytes=VMEM_LIMIT),
        name="moe_combine",
    )(blk_s, tile_s, exp_s, flag_s, yb, disp["eid"], disp["dest"], xn, ln_g.reshape(1, -1), ln_b.reshape(1, -1))


def _post_mix_moe(layer, outs, x, w_out, fw):
    ln1_g, ln1_b, router_w, router_b, w_gu, b_gu, w_down, b_down, ln2_g, ln2_b = fw
    n = x.shape[0]
    tm_cum = min(256, n)
    tm = min(512, n)
    blk = min(EXPERT_BLOCK, n * TOP_K)
    xn, xnb, idx, gates, rank, cum = _post_mix(outs, x, w_out.astype(BF16), ln1_g, ln1_b, router_w, router_b, tm_cum)
    disp = _moe_dispatch(idx, gates, rank, cum, n, tm_cum, tm, blk)
    yb = _expert_ffn(layer, xnb, disp, w_gu, b_gu, w_down, b_down, tm, blk)
    return _moe_combine(yb, disp, xn, ln2_g, ln2_b, tm, blk)


def _rec_step_kernel(gqkv_ref, gz_ref, xbc_ref, sz_ref, small_ref, gconv0_ref, sconv0_ref, s0_ref, h0_ref,
                     gcw_ref, scw_ref, scb_ref, gpar_ref, spar_ref, gnw_ref, snw_ref,
                     gout_ref, sout_ref, gconv_ref, sconv_ref, s_ref, h_ref):
    small = small_ref[0]

    def conv(u, buf_ref, w_ref, out_ref):
        buf = buf_ref[0]
        y = u * w_ref[CONV_W - 1:CONV_W, :]
        for i in range(CONV_W - 1):
            y = y + buf[i:i + 1, :] * w_ref[i:i + 1, :]
        out_ref[0, 0:CONV_W - 2, :] = buf[1:CONV_W - 1, :]
        out_ref[0, CONV_W - 2:CONV_W - 1, :] = u
        return y

    eye = (lax.broadcasted_iota(I32, (HEAD_DIM, HEAD_DIM), 0)
           == lax.broadcasted_iota(I32, (HEAD_DIM, HEAD_DIM), 1)).astype(F32)

    def to_col(r):
        return jnp.sum(eye * r, axis=-1, keepdims=True)

    y = _silu(conv(gqkv_ref[0], gconv0_ref, gcw_ref, gconv_ref))
    beta_all = _sigmoid(small[:, 0:N_HEADS])
    g_all = gpar_ref[0:1, :] * _softplus(small[:, N_HEADS:2 * N_HEADS] + gpar_ref[1:2, :])
    for h in range(N_HEADS):
        hs = slice(h * HEAD_DIM, (h + 1) * HEAD_DIM)
        q = y[:, h * HEAD_DIM:(h + 1) * HEAD_DIM]
        k = y[:, GROUP_WIDTH + h * HEAD_DIM:GROUP_WIDTH + (h + 1) * HEAD_DIM]
        v = y[:, 2 * GROUP_WIDTH + h * HEAD_DIM:2 * GROUP_WIDTH + (h + 1) * HEAD_DIM]
        q = q * lax.rsqrt(jnp.sum(q * q, axis=-1, keepdims=True) + RMS_EPS) * (HEAD_DIM ** -0.5)
        k = k * lax.rsqrt(jnp.sum(k * k, axis=-1, keepdims=True) + RMS_EPS)
        beta = beta_all[:, h:h + 1]
        eg = jnp.exp(g_all[:, h:h + 1])
        s0 = s0_ref[0, h]
        kc, qc = to_col(k), to_col(q)
        v_new = v * beta - jnp.sum(kc * (beta * eg) * s0, axis=0, keepdims=True)
        o = jnp.sum(qc * eg * s0, axis=0, keepdims=True) + jnp.sum(q * k, axis=-1, keepdims=True) * v_new
        s_ref[0, h] = s0 * eg + kc * v_new
        o = o * lax.rsqrt(jnp.mean(o * o, axis=-1, keepdims=True) + RMS_EPS) * gnw_ref[...] * _silu(gz_ref[0][:, hs])
        gout_ref[0, :, hs] = o

    xbc = _silu(conv(xbc_ref[0], sconv0_ref, scw_ref, sconv_ref) + scb_ref[...])
    dt_all = _softplus(small[:, 2 * N_HEADS:3 * N_HEADS] + spar_ref[1:2, :])
    hpg = N_HEADS // SSM_NGROUPS
    boff, coff = GROUP_WIDTH, GROUP_WIDTH + SSM_NGROUPS * SSM_STATE
    ys = []
    for h in range(N_HEADS):
        grp = h // hpg
        x = xbc[:, h * HEAD_DIM:(h + 1) * HEAD_DIM]
        bm = xbc[:, boff + grp * SSM_STATE:boff + (grp + 1) * SSM_STATE]
        cm = xbc[:, coff + grp * SSM_STATE:coff + (grp + 1) * SSM_STATE]
        dt = dt_all[:, h:h + 1]
        hn = h0_ref[0, h] * jnp.exp(dt * spar_ref[0:1, h:h + 1]) + to_col(x * dt) * bm
        h_ref[0, h] = hn
        yc = jnp.sum(hn * cm, axis=-1, keepdims=True)
        yh = jnp.sum(eye * yc, axis=0, keepdims=True) + spar_ref[2:3, h:h + 1] * x
        ys.append(yh * _silu(sz_ref[0][:, h * HEAD_DIM:(h + 1) * HEAD_DIM]))
    for grp in range(SSM_NGROUPS):
        grp_ys = ys[grp * hpg:(grp + 1) * hpg]
        ms = sum(jnp.sum(a * a, axis=-1, keepdims=True) for a in grp_ys) / (hpg * HEAD_DIM)
        r = lax.rsqrt(ms + RMS_EPS)
        for j, a in enumerate(grp_ys):
            h = grp * hpg + j
            sout_ref[0, :, h * HEAD_DIM:(h + 1) * HEAD_DIM] = a * r * snw_ref[:, h * HEAD_DIM:(h + 1) * HEAD_DIM]


def _rec_step(seg, gconv0, s0, sconv0, h0, lw):
    (_, gdn_conv_w, gdn_a_log, gdn_dt_bias, gdn_norm_w, ssm_conv_w, ssm_conv_b,
     ssm_a_log, ssm_dt_bias, ssm_d, ssm_norm_w, _) = lw
    b = seg["small"].shape[0]
    gpar = jnp.stack([-jnp.exp(gdn_a_log.astype(F32)), gdn_dt_bias.astype(F32)])
    spar = jnp.stack([-jnp.exp(ssm_a_log.astype(F32)), ssm_dt_bias.astype(F32), ssm_d.astype(F32)])
    row = lambda w: pl.BlockSpec((1, 1, w), lambda i: (i, 0, 0))
    full = lambda a: pl.BlockSpec(a.shape, lambda i: (0,) * a.ndim)
    per_seq = lambda *dims: pl.BlockSpec((1,) + dims, lambda i: (i,) + (0,) * len(dims))
    r3 = lambda a: a.reshape(b, 1, -1)
    consts = [gdn_conv_w, ssm_conv_w, ssm_conv_b.reshape(1, -1), gpar, spar,
              gdn_norm_w.reshape(1, HEAD_DIM), ssm_norm_w.reshape(1, GROUP_WIDTH)]
    state_specs = [per_seq(CONV_W - 1, GDN_CONV_DIM), per_seq(CONV_W - 1, SSM_CONV_DIM),
                   per_seq(N_HEADS, HEAD_DIM, HEAD_DIM), per_seq(N_HEADS, HEAD_DIM, SSM_STATE)]
    go, so, gc, sc, s, h = pl.pallas_call(
        _rec_step_kernel,
        grid=(b,),
        in_specs=[row(GDN_CONV_DIM), row(GROUP_WIDTH), row(SSM_CONV_DIM), row(GROUP_WIDTH), row(LANE)]
                 + state_specs + [full(c) for c in consts],
        out_specs=[row(GROUP_WIDTH), row(GROUP_WIDTH)] + state_specs,
        out_shape=[jax.ShapeDtypeStruct((b, 1, GROUP_WIDTH), F32), jax.ShapeDtypeStruct((b, 1, GROUP_WIDTH), F32),
                   jax.ShapeDtypeStruct((b, CONV_W - 1, GDN_CONV_DIM), F32),
                   jax.ShapeDtypeStruct((b, CONV_W - 1, SSM_CONV_DIM), F32),
                   jax.ShapeDtypeStruct((b, N_HEADS, HEAD_DIM, HEAD_DIM), F32),
                   jax.ShapeDtypeStruct((b, N_HEADS, HEAD_DIM, SSM_STATE), F32)],
        compiler_params=pltpu.CompilerParams(dimension_semantics=("arbitrary",), vmem_limit_bytes=VMEM_LIMIT),
        name="rec_step",
    )(r3(seg["gdn_qkv"]), r3(seg["gdn_z"]), r3(seg["ssm_xbc"]), r3(seg["ssm_z"]), r3(seg["small"]),
      gconv0, sconv0, s0, h0, *consts)
    return go.reshape(b, GROUP_WIDTH), so.reshape(b, GROUP_WIDTH), gc, sc, s, h


PAGES_PER_STEP = 8


def _decode_pass_kernel(pt_ref, sq_ref, mq_ref, *refs, n_steps):
    pp = PAGES_PER_STEP
    sk_refs, sv_refs, mk_refs = refs[0:pp], refs[pp:2 * pp], refs[2 * pp:3 * pp]
    sb_out_ref, gp_ref, accv_ref, carry_ref = refs[3 * pp:]
    p = pl.program_id(1)

    @pl.when(p == 0)
    def _():
        accv_ref[...] = jnp.zeros_like(accv_ref)
        carry_ref[...] = jnp.zeros_like(carry_ref)

    qs = sq_ref[0] * (HEAD_DIM ** -0.5)
    qm = mq_ref[0] * (1.0 / MOBA_BLOCK)
    row = lax.broadcasted_iota(I32, (PAGE_SIZE, PAGE_SIZE), 0)
    col = lax.broadcasted_iota(I32, (PAGE_SIZE, PAGE_SIZE), 1)
    upper = (row > col).astype(BF16)
    sub8 = lax.broadcasted_iota(I32, (8, LANE), 0)
    diag8 = sub8 == lax.broadcasted_iota(I32, (8, LANE), 1)
    heads = [slice(h * HEAD_DIM, (h + 1) * HEAD_DIM) for h in range(N_HEADS)]

    def head_rows(x):
        out = jnp.zeros((8, LANE), F32)
        for h, hs in enumerate(heads):
            out = jnp.where(sub8 == h, jnp.sum(x[hs, :], axis=0, keepdims=True), out)
        return out

    zs = [head_rows(sk_refs[i][0, 0] * qs) for i in range(pp)]
    ls = [-_softplus(z) for z in zs]
    las = _split_dot(jnp.concatenate(ls, axis=0), upper)
    accv = [accv_ref[hs, :] for hs in heads]
    carry = carry_ref[...]
    for i in range(pp):
        la = las[i * 8:(i + 1) * 8, :]
        w = jnp.exp(zs[i] + ls[i] + la + carry)
        vt = sv_refs[i][0, 0]
        accv = [a + vt[hs, :] * w[h:h + 1, :] for h, (hs, a) in enumerate(zip(heads, accv))]
        carry = carry + la[:, 0:1] + ls[i][:, 0:1]
        gp = jnp.sum(head_rows(mk_refs[i][0, 0] * qm), axis=-1, keepdims=True)
        gp_ref[0, pl.ds(p * pp + i, 1), :] = jnp.sum(jnp.where(diag8, gp, 0.0), axis=0, keepdims=True)
    for hs, a in zip(heads, accv):
        accv_ref[hs, :] = a
    carry_ref[...] = carry

    @pl.when(p == n_steps - 1)
    def _():
        sb_out_ref[0] = jnp.sum(accv_ref[...], axis=-1, keepdims=True)


def _decode_pass(layer, page_table, sq, mq, cache_sb_k, cache_sb_v, cache_moba_k):
    b, n_pages = page_table.shape
    pp = PAGES_PER_STEP
    assert n_pages % pp == 0
    n_steps = n_pages // pp

    def page_spec(i):
        return pl.BlockSpec((1, 1, GROUP_WIDTH, PAGE_SIZE),
                            lambda bi, p, pt: (layer, pt[bi, n_pages - 1 - (p * pp + i)], 0, 0))

    colspec = pl.BlockSpec((1, GROUP_WIDTH, 1), lambda bi, p, pt: (bi, 0, 0))
    grid_spec = pltpu.PrefetchScalarGridSpec(
        num_scalar_prefetch=1,
        grid=(b, n_steps),
        in_specs=[colspec, colspec] + [page_spec(i) for i in range(pp)] * 3,
        out_specs=[colspec, pl.BlockSpec((1, n_pages, LANE), lambda bi, p, pt: (bi, 0, 0))],
        scratch_shapes=[pltpu.VMEM((GROUP_WIDTH, PAGE_SIZE), F32), pltpu.VMEM((8, 1), F32)])
    sb_out, gp = pl.pallas_call(
        functools.partial(_decode_pass_kernel, n_steps=n_steps),
        grid_spec=grid_spec,
        out_shape=[jax.ShapeDtypeStruct((b, GROUP_WIDTH, 1), F32), jax.ShapeDtypeStruct((b, n_pages, LANE), F32)],
        compiler_params=pltpu.CompilerParams(dimension_semantics=("arbitrary", "arbitrary"), vmem_limit_bytes=VMEM_LIMIT),
        name="decode_pass",
    )(page_table, sq.reshape(b, GROUP_WIDTH, 1), mq.reshape(b, GROUP_WIDTH, 1),
      *([cache_sb_k] * pp), *([cache_sb_v] * pp), *([cache_moba_k] * pp))
    return sb_out.reshape(b, GROUP_WIDTH), gp


def _moba_decode_kernel(pg_ref, mq_ref, mk_ref, mv_ref, *refs, n_sel):
    kp_refs, vp_refs, o_ref = refs[:n_sel], refs[n_sel:2 * n_sel], refs[2 * n_sel]
    h = pl.program_id(1)
    lane = lax.broadcasted_iota(I32, (1, GROUP_WIDTH), 1)
    in_head = lane // HEAD_DIM == h
    q = jnp.where(in_head, mq_ref[0] * (HEAD_DIM ** -0.5), 0.0)

    @pl.when(h == 0)
    def _():
        o_ref[...] = jnp.zeros_like(o_ref)

    q8 = jnp.where(lax.broadcasted_iota(I32, (8, GROUP_WIDTH), 0) == 0, q, 0.0).astype(BF16)
    s_self = jnp.sum(q * mk_ref[0], axis=-1, keepdims=True)
    scores = [jnp.dot(q8, kp[0, 0].astype(BF16), preferred_element_type=F32)[0:1, :] for kp in kp_refs]
    m = s_self
    for s in scores:
        m = jnp.maximum(m, jnp.max(s, axis=-1, keepdims=True))
    p_self = jnp.exp(s_self - m)
    l = p_self
    acc = p_self * mv_ref[0]
    first_row = lax.broadcasted_iota(I32, (8, PAGE_SIZE), 0) == 0
    for s, vp in zip(scores, vp_refs):
        p = jnp.exp(s - m)
        l = l + jnp.sum(p, axis=-1, keepdims=True)
        p8 = jnp.where(first_row, p, 0.0).astype(BF16)
        acc = acc + lax.dot_general(p8, vp[0, 0].astype(BF16), (((1,), (1,)), ((), ())),
                                    preferred_element_type=F32)[0:1, :]
    o_ref[0] = jnp.where(in_head, acc / l, o_ref[0])


def _moba_decode(layer, sel_pages, mq, mk_new, mv_new, cache_moba_k, cache_moba_v):
    b, nh, n_sel = sel_pages.shape
    row = pl.BlockSpec((1, 1, GROUP_WIDTH), lambda bi, h, pg: (bi, 0, 0))

    def page(j):
        return pl.BlockSpec((1, 1, GROUP_WIDTH, PAGE_SIZE),
                            lambda bi, h, pg: (layer, pg[(bi * nh + h) * n_sel + j], 0, 0))

    grid_spec = pltpu.PrefetchScalarGridSpec(
        num_scalar_prefetch=1,
        grid=(b, nh),
        in_specs=[row, row, row] + [page(j) for j in range(n_sel)] * 2,
        out_specs=pl.BlockSpec((1, 1, GROUP_WIDTH), lambda bi, h, pg: (bi, 0, 0)))
    r3 = lambda a: a.reshape(b, 1, GROUP_WIDTH)
    out = pl.pallas_call(
        functools.partial(_moba_decode_kernel, n_sel=n_sel),
        grid_spec=grid_spec,
        out_shape=jax.ShapeDtypeStruct((b, 1, GROUP_WIDTH), F32),
        compiler_params=pltpu.CompilerParams(dimension_semantics=("arbitrary",) * 2, vmem_limit_bytes=VMEM_LIMIT),
        name="moba_decode",
    )(sel_pages.reshape(-1).astype(I32), r3(mq), r3(mk_new), r3(mv_new),
      *([cache_moba_k] * n_sel), *([cache_moba_v] * n_sel))
    return out.reshape(b, GROUP_WIDTH)


def _sample_attention(layer, page_table, seg, caches):
    cache_moba_k, cache_moba_v, cache_sb_k, cache_sb_v = caches
    b, n_pages = page_table.shape
    ppb = MOBA_BLOCK // PAGE_SIZE
    n_past_blocks = n_pages // ppb
    assert n_pages % ppb == 0 and n_past_blocks >= MOBA_TOPK
    sb_out, gp = _decode_pass(layer, page_table, seg["sb_q"], seg["moba_q"], cache_sb_k, cache_sb_v, cache_moba_k)
    gates = gp[:, ::-1, :N_HEADS].reshape(b, n_past_blocks, ppb, N_HEADS).sum(axis=2)
    _, sel = lax.top_k(jnp.swapaxes(gates, 1, 2), MOBA_TOPK)
    pages = sel[..., None] * ppb + jnp.arange(ppb)
    sel_pages = jnp.take_along_axis(page_table[:, None, :], pages.reshape(b, N_HEADS, -1), axis=2)
    moba_out = _moba_decode(layer, sel_pages, seg["moba_q"], seg["moba_k"], seg["moba_v"], cache_moba_k, cache_moba_v)
    return moba_out, sb_out


def _kv_rows(seg, bsz, length):
    return tuple(seg[name].reshape(bsz, length, N_HEADS, HEAD_DIM) for name in ("moba_k", "moba_v", "sb_k", "sb_v"))


def _prompt_mixers(x, lw):
    (w_in, gdn_conv_w, gdn_a_log, gdn_dt_bias, gdn_norm_w, ssm_conv_w, ssm_conv_b,
     ssm_a_log, ssm_dt_bias, ssm_d, ssm_norm_w, _) = lw
    bsz, length, _ = x.shape
    n = bsz * length
    flat = _in_proj(x.reshape(n, D_MODEL), w_in, jnp.arange(length), length)
    seg = {k: v.reshape(bsz, length, -1) for k, v in flat.items()}

    gdn_out, gdn_conv_new, gdn_s_new = _gdn_prompt(seg["gdn_qkv"], seg["gdn_z"], seg["small"],
                                                   gdn_conv_w, gdn_a_log, gdn_dt_bias, gdn_norm_w)
    ssm_out, ssm_conv_new, ssm_h_new = _ssd_prompt(seg["ssm_xbc"], seg["ssm_z"], seg["small"], ssm_conv_w, ssm_conv_b,
                                                   ssm_a_log, ssm_dt_bias, ssm_d, ssm_norm_w)
    moba_out = _moba_attention(seg["moba_q"], seg["moba_k"], seg["moba_v"])
    sb_out = _sb_attention(seg["sb_q"], seg["sb_k"], seg["sb_v"])

    outs = [o.reshape(n, GROUP_WIDTH) for o in (gdn_out, ssm_out, moba_out, sb_out)]
    return outs, _kv_rows(seg, bsz, length) + (gdn_conv_new, gdn_s_new, ssm_conv_new, ssm_h_new)


def _sample_mixers(x, layer, page_table, rec_state, caches, lw):
    gdn_conv0, gdn_s0, ssm_conv0, ssm_h0 = rec_state
    bsz, length, _ = x.shape
    assert length == 1
    pos0 = page_table.shape[1] * PAGE_SIZE
    seg = _in_proj(x.reshape(bsz, D_MODEL), lw[0], jnp.full((1,), pos0, I32), 1)
    gdn_out, ssm_out, gdn_conv_new, ssm_conv_new, gdn_s_new, ssm_h_new = _rec_step(
        seg, gdn_conv0, gdn_s0, ssm_conv0, ssm_h0, lw)
    moba_out, sb_out = _sample_attention(layer, page_table, seg, caches)
    outs = [gdn_out, ssm_out, moba_out, sb_out]
    return outs, _kv_rows(seg, bsz, length) + (gdn_conv_new, gdn_s_new, ssm_conv_new, ssm_h_new)


def kernel(x_prompt, x_sample, cache_moba_k, cache_moba_v, cache_sb_k, cache_sb_v,
           state_gdn_conv, state_gdn_rec, state_ssm_conv, state_ssm_rec, page_table,
           w_in, gdn_conv_w, gdn_a_log, gdn_dt_bias, gdn_norm_w,
           ssm_conv_w, ssm_conv_b, ssm_a_log, ssm_dt_bias, ssm_d, ssm_norm_w,
           w_out, ln1_g, ln1_b, router_w, router_b,
           expert_w_gu, expert_b_gu, expert_w_down, expert_b_down, ln2_g, ln2_b):
    caches = tuple(jnp.transpose(c, (0, 1, 3, 4, 2)).reshape(c.shape[0], c.shape[1], GROUP_WIDTH, PAGE_SIZE)
                   for c in (cache_moba_k, cache_moba_v, cache_sb_k, cache_sb_v))
    yp, ys = x_prompt, x_sample
    new_p, new_s = [], []
    for l in range(DEPTH):
        lw = (w_in[l], gdn_conv_w[l], gdn_a_log[l], gdn_dt_bias[l], gdn_norm_w[l], ssm_conv_w[l], ssm_conv_b[l],
              ssm_a_log[l], ssm_dt_bias[l], ssm_d[l], ssm_norm_w[l], w_out[l])
        fw = (ln1_g[l], ln1_b[l], router_w[l], router_b[l], expert_w_gu, expert_b_gu,
              expert_w_down, expert_b_down, ln2_g[l], ln2_b[l])
        rec_state = (state_gdn_conv[l], state_gdn_rec[l], state_ssm_conv[l], state_ssm_rec[l])
        outs_p, st_p = _prompt_mixers(yp, lw)
        yp = _post_mix_moe(l, outs_p, yp.reshape(-1, D_MODEL), w_out[l], fw).reshape(yp.shape)
        outs_s, st_s = _sample_mixers(ys, l, page_table, rec_state, caches, lw)
        ys = _post_mix_moe(l, outs_s, ys.reshape(-1, D_MODEL), w_out[l], fw).reshape(ys.shape)
        new_p.append(st_p)
        new_s.append(st_s)

    def stk(states, i):
        return jnp.stack([s[i] for s in states])

    return (yp, ys,
            stk(new_p, 0), stk(new_p, 1), stk(new_p, 2), stk(new_p, 3),
            stk(new_p, 4), stk(new_p, 5), stk(new_p, 6), stk(new_p, 7),
            stk(new_s, 0), stk(new_s, 1), stk(new_s, 2), stk(new_s, 3),
            stk(new_s, 4), stk(new_s, 5), stk(new_s, 6), stk(new_s, 7))
```

```python
import functools

import jax
import jax.numpy as jnp
from jax import lax
from jax.experimental import pallas as pl
from jax.experimental.pallas import tpu as pltpu

D_MODEL = 1024
DEPTH = 2
PAGE_SIZE = 128
HEAD_DIM = 64
N_MIXERS = 4
GROUP_WIDTH = D_MODEL // N_MIXERS
N_HEADS = GROUP_WIDTH // HEAD_DIM
CONV_W = 4
CHUNK = 64
GDN_CONV_DIM = 3 * GROUP_WIDTH
SSM_STATE = 128
SSM_NGROUPS = 2
SSM_CONV_DIM = GROUP_WIDTH + 2 * SSM_NGROUPS * SSM_STATE
MOBA_BLOCK = 256
MOBA_TOPK = 3
MOBA_QUERY_BLOCK = 64
ROPE_THETA = 500000.0
ROPE_DIM = HEAD_DIM // 4
QUERY_BLOCK = 128
N_EXPERTS = 32
TOP_K = 4
D_FF = D_MODEL
SWIGLU_LIMIT = 7.0
SWIGLU_ALPHA = 1.702
EXPERT_BLOCK = 256
DN_ALPHA = (2 * DEPTH) ** 0.25
LN_EPS = 1e-5
RMS_EPS = 1e-6

F32 = jnp.float32
BF16 = jnp.bfloat16
I32 = jnp.int32
HI = lax.Precision.HIGHEST

LANE = 128
VMEM_LIMIT = 56 * 1024 * 1024
NEG = -1e30
MOE_ROW_WINDOW = 128
GDN_PRECISE_SQUARINGS = 2

_SEG_SIZES = (GDN_CONV_DIM, GROUP_WIDTH, N_HEADS, N_HEADS, SSM_CONV_DIM, GROUP_WIDTH, N_HEADS, GDN_CONV_DIM, GDN_CONV_DIM)
_SEG_NAMES = ("gdn_qkv", "gdn_z", "gdn_b", "gdn_a", "ssm_xbc", "ssm_z", "ssm_dt", "moba_qkv", "sb_qkv")
_OUT_SEGS = (("gdn_qkv", GDN_CONV_DIM), ("gdn_z", GROUP_WIDTH), ("ssm_xbc", SSM_CONV_DIM), ("ssm_z", GROUP_WIDTH),
             ("moba_q", GROUP_WIDTH), ("moba_k", GROUP_WIDTH), ("moba_v", GROUP_WIDTH),
             ("sb_q", GROUP_WIDTH), ("sb_k", GROUP_WIDTH), ("sb_v", GROUP_WIDTH), ("small", LANE))
_ROTARY_SEGS = ("moba_q", "moba_k")


def _softplus(z):
    return jnp.maximum(z, 0.0) + jnp.log1p(jnp.exp(-jnp.abs(z)))


def _sigmoid(x):
    return 1.0 / (1.0 + jnp.exp(-x))


def _silu(x):
    return x * _sigmoid(x)


def _dot(a, b):
    return jnp.dot(a.astype(BF16), b.astype(BF16), preferred_element_type=F32)


def _dot_nt(a, b):
    return lax.dot_general(a.astype(BF16), b.astype(BF16), (((1,), (1,)), ((), ())), preferred_element_type=F32)


def _dot_tn(a, b):
    return lax.dot_general(a.astype(BF16), b.astype(BF16), (((0,), (0,)), ((), ())), preferred_element_type=F32)


def _dot_hi(a, b):
    return jnp.dot(a, b, preferred_element_type=F32, precision=HI)


def _split_dot(x, m_bf16):
    hi = x.astype(BF16)
    lo = (x - hi.astype(F32)).astype(BF16)
    return (jnp.dot(hi, m_bf16, preferred_element_type=F32)
            + jnp.dot(lo, m_bf16, preferred_element_type=F32))


def _split3(x):
    hi = x.astype(BF16)
    r = x - hi.astype(F32)
    mid = r.astype(BF16)
    return hi, mid, (r - mid.astype(F32)).astype(BF16)


def _dot3(a, b):
    a_hi = a.astype(BF16)
    a_lo = (a - a_hi.astype(F32)).astype(BF16)
    b_hi = b.astype(BF16)
    b_lo = (b - b_hi.astype(F32)).astype(BF16)
    return (jnp.dot(a_hi, b_hi, preferred_element_type=F32) + jnp.dot(a_hi, b_lo, preferred_element_type=F32)
            + jnp.dot(a_lo, b_hi, preferred_element_type=F32))


def _ln(h, g, b):
    mu = jnp.mean(h, axis=-1, keepdims=True)
    d = h - mu
    var = jnp.mean(d * d, axis=-1, keepdims=True)
    return d * lax.rsqrt(var + LN_EPS) * g + b


def _in_proj_kernel(x_ref, w_ref, cos_ref, sin_ref, *o_refs):
    y = jnp.dot(x_ref[...].astype(BF16), w_ref[...], preferred_element_type=F32)
    cos = jnp.concatenate([cos_ref[...]] * (GROUP_WIDTH // LANE), axis=1)
    sin = jnp.concatenate([sin_ref[...]] * (GROUP_WIDTH // LANE), axis=1)
    first_half = (lax.broadcasted_iota(I32, (1, GROUP_WIDTH), 1) % HEAD_DIM) < ROPE_DIM // 2
    off = 0
    for (name, width), o_ref in zip(_OUT_SEGS, o_refs):
        seg = y[:, off:off + width]
        if name in _ROTARY_SEGS:
            partner = jnp.where(first_half, pltpu.roll(seg, GROUP_WIDTH - ROPE_DIM // 2, 1), pltpu.roll(seg, ROPE_DIM // 2, 1))
            seg = seg * cos + partner * sin
        o_ref[...] = seg
        off += width


def _rotary_tables(pos):
    half = ROPE_DIM // 2
    inv = ROPE_THETA ** (-jnp.arange(half, dtype=F32) * 2.0 / ROPE_DIM)
    ang = pos.astype(F32)[:, None] * inv
    cc = jnp.arange(LANE) % HEAD_DIM
    cos = jnp.where(cc < ROPE_DIM, jnp.cos(ang)[:, cc % half], 1.0)
    sin = jnp.sin(ang)[:, cc % half]
    sin = jnp.where(cc < half, -sin, jnp.where(cc < ROPE_DIM, sin, 0.0))
    return cos, sin


def _in_proj(x, w_in, pos, seq_len):
    m = x.shape[0]
    tm = min(256, m)
    w_in = w_in.astype(BF16)
    cols, off = {}, 0
    for name, sz in zip(_SEG_NAMES, _SEG_SIZES):
        cols[name] = w_in[:, off:off + sz]
        off += sz
    for name in ("moba", "sb"):
        for j, part in enumerate("qkv"):
            cols[f"{name}_{part}"] = cols[f"{name}_qkv"][:, j * GROUP_WIDTH:(j + 1) * GROUP_WIDTH]
    cols["small"] = jnp.concatenate([cols["gdn_b"], cols["gdn_a"], cols["ssm_dt"],
                                     jnp.zeros((D_MODEL, LANE - 3 * N_HEADS), w_in.dtype)], axis=1)
    w = jnp.concatenate([cols[name] for name, _ in _OUT_SEGS], axis=1)
    n = w.shape[1]
    cos, sin = _rotary_tables(pos)
    if seq_len == 1:
        tab_spec = pl.BlockSpec((1, LANE), lambda i: (0, 0))
    else:
        assert seq_len % tm == 0
        tab_spec = pl.BlockSpec((tm, LANE), lambda i: (i % (seq_len // tm), 0))
    outs = pl.pallas_call(
        _in_proj_kernel,
        grid=(m // tm,),
        in_specs=[pl.BlockSpec((tm, D_MODEL), lambda i: (i, 0)),
                  pl.BlockSpec((D_MODEL, n), lambda i: (0, 0)), tab_spec, tab_spec],
        out_specs=[pl.BlockSpec((tm, width), lambda i: (i, 0)) for _, width in _OUT_SEGS],
        out_shape=[jax.ShapeDtypeStruct((m, width), F32) for _, width in _OUT_SEGS],
        compiler_params=pltpu.CompilerParams(dimension_semantics=("arbitrary",), vmem_limit_bytes=VMEM_LIMIT),
        name="in_proj",
    )(x, w, cos, sin)
    return {name: o for (name, _), o in zip(_OUT_SEGS, outs)}


def _sb_kernel(q_ref, k_ref, v_ref, o_ref, *, tq):
    i = pl.program_id(1)
    row = lax.broadcasted_iota(I32, (tq, tq), 0)
    col = lax.broadcasted_iota(I32, (tq, tq), 1)
    upper = (row > col).astype(BF16)
    past = col < row

    heads = [slice(h * HEAD_DIM, (h + 1) * HEAD_DIM) for h in range(N_HEADS)]
    qs = [(q_ref[0, :, hs] * (HEAD_DIM ** -0.5)).astype(BF16) for hs in heads]

    def block(j, state, diag):
        out = []
        for hs, q, (carry, acc) in zip(heads, qs, state):
            kj = k_ref[0, pl.ds(j * tq, tq), hs].astype(BF16)
            vj = v_ref[0, pl.ds(j * tq, tq), hs].astype(BF16)
            z = lax.dot_general(q, kj, (((1,), (1,)), ((), ())), preferred_element_type=F32)
            ls = -_softplus(z)
            lk = jnp.where(past, ls, 0.0) if diag else ls
            la = _split_dot(lk, upper)
            w = jnp.exp(z + ls + la + carry)
            if diag:
                w = jnp.where(past, w, 0.0)
            acc = acc + jnp.dot(w.astype(BF16), vj, preferred_element_type=F32)
            out.append((carry + la[:, 0:1] + lk[:, 0:1], acc))
        return tuple(out)

    state = tuple((jnp.zeros((tq, 1), F32), jnp.zeros((tq, HEAD_DIM), F32)) for _ in heads)
    state = block(i, state, True)
    state = lax.fori_loop(0, i, lambda t, c: block(i - 1 - t, c, False), state)
    for hs, (_, acc) in zip(heads, state):
        o_ref[0, :, hs] = acc


def _sb_attention(q, k, v, tq=256):
    b, l, d = q.shape
    assert l % tq == 0
    return pl.pallas_call(
        functools.partial(_sb_kernel, tq=tq),
        grid=(b, l // tq),
        in_specs=[pl.BlockSpec((1, tq, d), lambda bi, i: (bi, i, 0)),
                  pl.BlockSpec((1, l, d), lambda bi, i: (bi, 0, 0)),
                  pl.BlockSpec((1, l, d), lambda bi, i: (bi, 0, 0))],
        out_specs=pl.BlockSpec((1, tq, d), lambda bi, i: (bi, i, 0)),
        out_shape=jax.ShapeDtypeStruct((b, l, d), F32),
        compiler_params=pltpu.CompilerParams(dimension_semantics=("arbitrary",) * 2, vmem_limit_bytes=VMEM_LIMIT),
        name="sb_attention",
    )(q, k, v)


def _moba_kernel(q_ref, k_ref, v_ref, o_ref, kmean_ref, *, nb):
    t = MOBA_BLOCK
    i = pl.program_id(1)

    @pl.when(i == 0)
    def _():
        for n in range(nb):
            kmean_ref[n:n + 1, :] = jnp.mean(k_ref[0, n * t:(n + 1) * t, :], axis=0, keepdims=True)

    blk_id = lax.broadcasted_iota(I32, (nb, t), 0)
    row = lax.broadcasted_iota(I32, (t, t), 0)
    col = lax.broadcasted_iota(I32, (t, t), 1)

    heads = [slice(h * HEAD_DIM, (h + 1) * HEAD_DIM) for h in range(N_HEADS)]
    qs, sels, state = [], [], []
    for hs in heads:
        qf = q_ref[0, :, hs]
        q = (qf * (HEAD_DIM ** -0.5)).astype(BF16)
        gate = lax.dot_general(kmean_ref[:, hs], qf, (((1,), (1,)), ((), ())),
                               preferred_element_type=F32, precision=HI)
        cnt = jnp.zeros((nb, t), F32)
        for n2 in range(nb):
            cn = gate[n2:n2 + 1, :]
            beats = (cn > gate) | ((cn == gate) & (n2 < blk_id))
            cnt = cnt + jnp.where(beats & (n2 < i), 1.0, 0.0)
        sels.append(jnp.where((cnt < MOBA_TOPK) & (blk_id < i), 1.0, 0.0).astype(BF16))
        qs.append(q)

        ki = k_ref[0, pl.ds(i * t, t), hs].astype(BF16)
        vi = v_ref[0, pl.ds(i * t, t), hs].astype(BF16)
        s = lax.dot_general(q, ki, (((1,), (1,)), ((), ())), preferred_element_type=F32)
        s = jnp.where(col <= row, s, NEG)
        m = jnp.max(s, axis=-1, keepdims=True)
        p = jnp.exp(s - m)
        state.append((m, jnp.sum(p, axis=-1, keepdims=True), jnp.dot(p.astype(BF16), vi, preferred_element_type=F32)))

    def body(j, state):
        pick = ((lax.broadcasted_iota(I32, (nb, 8), 0) == j) & (lax.broadcasted_iota(I32, (nb, 8), 1) == 0)).astype(BF16)
        out = []
        for hs, q, sel, (m, l, acc) in zip(heads, qs, sels, state):
            kj = k_ref[0, pl.ds(j * t, t), hs].astype(BF16)
            vj = v_ref[0, pl.ds(j * t, t), hs].astype(BF16)
            s = lax.dot_general(q, kj, (((1,), (1,)), ((), ())), preferred_element_type=F32)
            sj = lax.dot_general(sel, pick, (((0,), (0,)), ((), ())), preferred_element_type=F32)[:, 0:1]
            s = jnp.where(sj > 0.5, s, NEG)
            m_new = jnp.maximum(m, jnp.max(s, axis=-1, keepdims=True))
            a = jnp.exp(m - m_new)
            p = jnp.exp(s - m_new)
            out.append((m_new, a * l + jnp.sum(p, axis=-1, keepdims=True),
                        a * acc + jnp.dot(p.astype(BF16), vj, preferred_element_type=F32)))
        return tuple(out)

    state = lax.fori_loop(0, i, body, tuple(state))
    for hs, (_, l, acc) in zip(heads, state):
        o_ref[0, :, hs] = acc / l


def _moba_attention(q, k, v):
    b, l, d = q.shape
    t = MOBA_BLOCK
    assert l % t == 0
    nb = l // t
    return pl.pallas_call(
        functools.partial(_moba_kernel, nb=nb),
        grid=(b, nb),
        in_specs=[pl.BlockSpec((1, t, d), lambda bi, i: (bi, i, 0)),
                  pl.BlockSpec((1, l, d), lambda bi, i: (bi, 0, 0)),
                  pl.BlockSpec((1, l, d), lambda bi, i: (bi, 0, 0))],
        out_specs=pl.BlockSpec((1, t, d), lambda bi, i: (bi, i, 0)),
        out_shape=jax.ShapeDtypeStruct((b, l, d), F32),
        scratch_shapes=[pltpu.VMEM((nb, d), F32)],
        compiler_params=pltpu.CompilerParams(dimension_semantics=("arbitrary",) * 2, vmem_limit_bytes=VMEM_LIMIT),
        name="moba_attention",
    )(q, k, v)


def _conv_tile(ext_ref, u, w_ref, tl):
    ext_ref[8:8 + tl, :] = u
    y = ext_ref[5:5 + tl, :] * w_ref[0:1, :]
    for i in range(1, CONV_W):
        y = y + ext_ref[5 + i:5 + i + tl, :] * w_ref[i:i + 1, :]
    tail = ext_ref[tl + 5:tl + 8, :]
    ext_ref[5:8, :] = tail
    return y, tail


def _gdn_kernel(qkv_ref, z_ref, small_ref, convw_ref, par_ref, normw_ref,
                o_ref, conv_out_ref, s_out_ref, ext_ref, s_ref, *, tl):
    c = CHUNK
    step = pl.program_id(1)

    @pl.when(step == 0)
    def _():
        ext_ref[...] = jnp.zeros_like(ext_ref)
        s_ref[...] = jnp.zeros_like(s_ref)

    y, tail = _conv_tile(ext_ref, qkv_ref[0], convw_ref, tl)
    conv_out_ref[0] = tail
    y = _silu(y)
    small = small_ref[0]
    beta_all = _sigmoid(small[:, 0:N_HEADS])
    g_all = par_ref[0:1, :] * _softplus(small[:, N_HEADS:2 * N_HEADS] + par_ref[1:2, :])

    row = lax.broadcasted_iota(I32, (tl, tl), 0)
    col = lax.broadcasted_iota(I32, (tl, tl), 1)
    same = (row // c) == (col // c)
    tril = same & (row >= col)
    strict = same & (row > col)
    tril_b = tril.astype(BF16)
    triu_b = (same & (row <= col)).astype(BF16)
    eye = (row == col).astype(F32)

    g8 = jnp.concatenate([g_all, jnp.zeros((tl, 8 - N_HEADS), F32)], axis=1)
    g_parts = _split3(g8)
    gc_cols = sum(jnp.dot(tril_b, part, preferred_element_type=F32) for part in g_parts)
    gc_rows = sum(lax.dot_general(part, triu_b, (((0,), (0,)), ((), ())), preferred_element_type=F32) for part in g_parts)

    for h in range(N_HEADS):
        hs = slice(h * HEAD_DIM, (h + 1) * HEAD_DIM)
        q = y[:, h * HEAD_DIM:(h + 1) * HEAD_DIM]
        k = y[:, GROUP_WIDTH + h * HEAD_DIM:GROUP_WIDTH + (h + 1) * HEAD_DIM]
        v = y[:, 2 * GROUP_WIDTH + h * HEAD_DIM:2 * GROUP_WIDTH + (h + 1) * HEAD_DIM]
        q = q * lax.rsqrt(jnp.sum(q * q, axis=-1, keepdims=True) + RMS_EPS) * (HEAD_DIM ** -0.5)
        k = k * lax.rsqrt(jnp.sum(k * k, axis=-1, keepdims=True) + RMS_EPS)
        beta = beta_all[:, h:h + 1]
        gcol = gc_cols[:, h:h + 1]
        grow = gc_rows[h:h + 1, :]
        decay = jnp.where(tril, jnp.exp(jnp.where(tril, gcol - grow, 0.0)), 0.0)
        kb = k * beta
        a = jnp.where(strict, _dot_nt(kb, k) * decay, 0.0)
        p = -a
        tinv = eye + p
        for m in range(5):
            mm = _dot3 if m < GDN_PRECISE_SQUARINGS else _dot
            p = mm(p, p)
            tinv = tinv + mm(tinv, p)
        u = _dot(tinv, v * beta)
        w = _dot(tinv, kb * jnp.exp(gcol))
        attn = jnp.where(tril, _dot_nt(q, k) * decay, 0.0)
        qe = q * jnp.exp(gcol)
        s = s_ref[h]
        for ci in range(tl // c):
            rs = slice(ci * c, (ci + 1) * c)
            g_last = gcol[(ci + 1) * c - 1:(ci + 1) * c, :]
            v_new = u[rs] - _dot(w[rs], s)
            o = _dot(qe[rs], s) + _dot(attn[rs, rs], v_new)
            s = s * jnp.exp(g_last) + _dot_tn(k[rs] * jnp.exp(g_last - gcol[rs]), v_new)
            o = o * lax.rsqrt(jnp.mean(o * o, axis=-1, keepdims=True) + RMS_EPS) * normw_ref[...] * _silu(z_ref[0, rs, hs])
            o_ref[0, rs, hs] = o
        s_ref[h] = s

    @pl.when(step == pl.num_programs(1) - 1)
    def _():
        s_out_ref[0] = s_ref[...]


def _gdn_prompt(qkv, z, small, conv_w, a_log, dt_bias, norm_w, tl=256):
    b, l, _ = qkv.shape
    assert l % tl == 0 and tl % CHUNK == 0
    par = jnp.stack([-jnp.exp(a_log.astype(F32)), dt_bias.astype(F32)])
    return pl.pallas_call(
        functools.partial(_gdn_kernel, tl=tl),
        grid=(b, l // tl),
        in_specs=[pl.BlockSpec((1, tl, GDN_CONV_DIM), lambda bi, i: (bi, i, 0)),
                  pl.BlockSpec((1, tl, GROUP_WIDTH), lambda bi, i: (bi, i, 0)),
                  pl.BlockSpec((1, tl, LANE), lambda bi, i: (bi, i, 0)),
                  pl.BlockSpec((CONV_W, GDN_CONV_DIM), lambda bi, i: (0, 0)),
                  pl.BlockSpec((2, N_HEADS), lambda bi, i: (0, 0)),
                  pl.BlockSpec((1, HEAD_DIM), lambda bi, i: (0, 0))],
        out_specs=[pl.BlockSpec((1, tl, GROUP_WIDTH), lambda bi, i: (bi, i, 0)),
                   pl.BlockSpec((1, CONV_W - 1, GDN_CONV_DIM), lambda bi, i: (bi, 0, 0)),
                   pl.BlockSpec((1, N_HEADS, HEAD_DIM, HEAD_DIM), lambda bi, i: (bi, 0, 0, 0))],
        out_shape=[jax.ShapeDtypeStruct((b, l, GROUP_WIDTH), F32),
                   jax.ShapeDtypeStruct((b, CONV_W - 1, GDN_CONV_DIM), F32),
                   jax.ShapeDtypeStruct((b, N_HEADS, HEAD_DIM, HEAD_DIM), F32)],
        scratch_shapes=[pltpu.VMEM((tl + 8, GDN_CONV_DIM), F32),
                        pltpu.VMEM((N_HEADS, HEAD_DIM, HEAD_DIM), F32)],
        compiler_params=pltpu.CompilerParams(dimension_semantics=("arbitrary", "arbitrary"),
                                             vmem_limit_bytes=VMEM_LIMIT),
        name="gdn_prompt",
    )(qkv, z, small, conv_w, par, norm_w.reshape(1, HEAD_DIM))


def _ssd_kernel(xbc_ref, z_ref, small_ref, convw_ref, convb_ref, par_ref, normw_ref,
                o_ref, conv_out_ref, h_out_ref, ext_ref, h_ref, *, tl):
    c = CHUNK
    step = pl.program_id(1)
    hpg = N_HEADS // SSM_NGROUPS

    @pl.when(step == 0)
    def _():
        ext_ref[...] = jnp.zeros_like(ext_ref)
        h_ref[...] = jnp.zeros_like(h_ref)

    y, tail = _conv_tile(ext_ref, xbc_ref[0], convw_ref, tl)
    conv_out_ref[0] = tail
    xbc = _silu(y + convb_ref[...])
    dt_all = _softplus(small_ref[0][:, 2 * N_HEADS:3 * N_HEADS] + par_ref[1:2, :])
    ad_all = dt_all * par_ref[0:1, :]

    row = lax.broadcasted_iota(I32, (c, c), 0)
    col = lax.broadcasted_iota(I32, (c, c), 1)
    tril = row >= col
    tril_f = tril.astype(F32)
    triu_f = (row <= col).astype(F32)
    ones = jnp.ones((c, c), F32)
    boff = GROUP_WIDTH
    coff = GROUP_WIDTH + SSM_NGROUPS * SSM_STATE

    for ci in range(tl // c):
        r0 = ci * c
        ad_c = ad_all[r0:r0 + c, :]
        acs_all = _dot_hi(tril_f, ad_c)
        ys = []
        for h in range(N_HEADS):
            grp = h // hpg
            x = xbc[r0:r0 + c, h * HEAD_DIM:(h + 1) * HEAD_DIM]
            bm = xbc[r0:r0 + c, boff + grp * SSM_STATE:boff + (grp + 1) * SSM_STATE]
            cm = xbc[r0:r0 + c, coff + grp * SSM_STATE:coff + (grp + 1) * SSM_STATE]
            xdt = x * dt_all[r0:r0 + c, h:h + 1]
            acol = acs_all[:, h:h + 1]
            arow = _dot_hi(ones, ad_c[:, h:h + 1] * triu_f)
            lmat = jnp.where(tril, jnp.exp(jnp.where(tril, acol - arow, 0.0)), 0.0)
            y_diag = _dot(_dot_nt(cm, bm) * lmat, xdt)
            a_last = acol[c - 1:c, :]
            states = _dot_tn(xdt * jnp.exp(a_last - acol), bm)
            hin = h_ref[h]
            y_off = _dot_nt(cm, hin) * jnp.exp(acol)
            h_ref[h] = hin * jnp.exp(a_last) + states
            yh = y_diag + y_off + par_ref[2:3, h:h + 1] * x
            ys.append(yh * _silu(z_ref[0, r0:r0 + c, h * HEAD_DIM:(h + 1) * HEAD_DIM]))
        for grp in range(SSM_NGROUPS):
            grp_ys = ys[grp * hpg:(grp + 1) * hpg]
            ms = sum(jnp.sum(a * a, axis=-1, keepdims=True) for a in grp_ys) / (hpg * HEAD_DIM)
            r = lax.rsqrt(ms + RMS_EPS)
            for j, a in enumerate(grp_ys):
                h = grp * hpg + j
                o_ref[0, r0:r0 + c, h * HEAD_DIM:(h + 1) * HEAD_DIM] = a * r * normw_ref[:, h * HEAD_DIM:(h + 1) * HEAD_DIM]

    @pl.when(step == pl.num_programs(1) - 1)
    def _():
        h_out_ref[0] = h_ref[...]


def _ssd_prompt(xbc, z, small, conv_w, conv_b, a_log, dt_bias, d_skip, norm_w, tl=256):
    b, l, _ = xbc.shape
    assert l % tl == 0 and tl % CHUNK == 0
    par = jnp.stack([-jnp.exp(a_log.astype(F32)), dt_bias.astype(F32), d_skip.astype(F32)])
    return pl.pallas_call(
        functools.partial(_ssd_kernel, tl=tl),
        grid=(b, l // tl),
        in_specs=[pl.BlockSpec((1, tl, SSM_CONV_DIM), lambda bi, i: (bi, i, 0)),
                  pl.BlockSpec((1, tl, GROUP_WIDTH), lambda bi, i: (bi, i, 0)),
                  pl.BlockSpec((1, tl, LANE), lambda bi, i: (bi, i, 0)),
                  pl.BlockSpec((CONV_W, SSM_CONV_DIM), lambda bi, i: (0, 0)),
                  pl.BlockSpec((1, SSM_CONV_DIM), lambda bi, i: (0, 0)),
                  pl.BlockSpec((3, N_HEADS), lambda bi, i: (0, 0)),
                  pl.BlockSpec((1, GROUP_WIDTH), lambda bi, i: (0, 0))],
        out_specs=[pl.BlockSpec((1, tl, GROUP_WIDTH), lambda bi, i: (bi, i, 0)),
                   pl.BlockSpec((1, CONV_W - 1, SSM_CONV_DIM), lambda bi, i: (bi, 0, 0)),
                   pl.BlockSpec((1, N_HEADS, HEAD_DIM, SSM_STATE), lambda bi, i: (bi, 0, 0, 0))],
        out_shape=[jax.ShapeDtypeStruct((b, l, GROUP_WIDTH), F32),
                   jax.ShapeDtypeStruct((b, CONV_W - 1, SSM_CONV_DIM), F32),
                   jax.ShapeDtypeStruct((b, N_HEADS, HEAD_DIM, SSM_STATE), F32)],
        scratch_shapes=[pltpu.VMEM((tl + 8, SSM_CONV_DIM), F32),
                        pltpu.VMEM((N_HEADS, HEAD_DIM, SSM_STATE), F32)],
        compiler_params=pltpu.CompilerParams(dimension_semantics=("arbitrary", "arbitrary"),
                                             vmem_limit_bytes=VMEM_LIMIT),
        name="ssd_prompt",
    )(xbc, z, small, conv_w, conv_b.reshape(1, SSM_CONV_DIM), par, norm_w.reshape(1, GROUP_WIDTH))


def _post_mix_kernel(o0_ref, o1_ref, o2_ref, o3_ref, x_ref, wout_ref, g_ref, b_ref, rw_ref, rb_ref,
                     xn_ref, xnb_ref, idx_ref, gate_ref, rank_ref, cum_ref, cnt_ref, *, tm):
    @pl.when(pl.program_id(0) == 0)
    def _():
        cnt_ref[...] = jnp.zeros_like(cnt_ref)

    mix = jnp.zeros((tm, D_MODEL), F32)
    for gi, o_ref in enumerate((o0_ref, o1_ref, o2_ref, o3_ref)):
        mix = mix + jnp.dot(o_ref[...].astype(BF16), wout_ref[gi * GROUP_WIDTH:(gi + 1) * GROUP_WIDTH, :],
                            preferred_element_type=F32)
    xn = _ln(DN_ALPHA * x_ref[...] + mix, g_ref[...], b_ref[...])
    xn_ref[...] = xn
    xnb_ref[...] = xn.astype(BF16)

    logits = _dot_hi(xn, rw_ref[...]) + rb_ref[...]
    lane_e = lax.broadcasted_iota(I32, (tm, N_EXPERTS), 1).astype(F32)
    lane = lax.broadcasted_iota(I32, (tm, LANE), 1)
    work = logits
    sel = jnp.zeros((tm, N_EXPERTS), F32)
    idx_out = jnp.zeros((tm, LANE), I32)
    val_out = jnp.zeros((tm, LANE), F32)
    hits = []
    for r in range(TOP_K):
        m = jnp.max(work, axis=-1, keepdims=True)
        pick = jnp.min(jnp.where(work == m, lane_e, float(N_EXPERTS)), axis=-1, keepdims=True)
        hit = lane_e == pick
        work = jnp.where(hit, -jnp.inf, work)
        sel = sel + hit.astype(F32)
        hits.append(hit)
        idx_out = jnp.where(lane == r, pick.astype(I32), idx_out)
        val_out = jnp.where(lane == r, m, val_out)
    ex = jnp.where(lane < TOP_K, jnp.exp(val_out - val_out[:, 0:1]), 0.0)
    gate_ref[...] = ex / jnp.sum(ex, axis=-1, keepdims=True)
    idx_ref[...] = idx_out

    row = lax.broadcasted_iota(I32, (tm, tm), 0)
    col = lax.broadcasted_iota(I32, (tm, tm), 1)
    before = jnp.dot((row > col).astype(BF16), sel.astype(BF16), preferred_element_type=F32) + cnt_ref[...]
    rank_out = jnp.zeros((tm, LANE), I32)
    for r in range(TOP_K):
        rk = jnp.sum(jnp.where(hits[r], before, 0.0), axis=-1, keepdims=True)
        rank_out = jnp.where(lane == r, rk.astype(I32), rank_out)
    rank_ref[...] = rank_out
    cum_ref[0] = jnp.broadcast_to(cnt_ref[...], (8, N_EXPERTS))
    cnt_ref[...] = cnt_ref[...] + jnp.sum(sel, axis=0, keepdims=True)


def _post_mix(outs, x, w_out_bf16, ln_g, ln_b, router_w, router_b, tm):
    n = x.shape[0]
    nt = n // tm
    tok = lambda w: pl.BlockSpec((tm, w), lambda i: (i, 0))
    full = lambda a: pl.BlockSpec(a.shape, lambda i: (0,) * a.ndim)
    g2, b2, rb2 = ln_g.reshape(1, -1), ln_b.reshape(1, -1), router_b.reshape(1, -1)
    return pl.pallas_call(
        functools.partial(_post_mix_kernel, tm=tm),
        grid=(nt,),
        in_specs=[tok(GROUP_WIDTH)] * 4 + [tok(D_MODEL), full(w_out_bf16), full(g2), full(b2), full(router_w), full(rb2)],
        out_specs=[tok(D_MODEL), tok(D_MODEL), tok(LANE), tok(LANE), tok(LANE),
                   pl.BlockSpec((1, 8, N_EXPERTS), lambda i: (i, 0, 0))],
        out_shape=[jax.ShapeDtypeStruct((n, D_MODEL), F32), jax.ShapeDtypeStruct((n, D_MODEL), BF16),
                   jax.ShapeDtypeStruct((n, LANE), I32), jax.ShapeDtypeStruct((n, LANE), F32),
                   jax.ShapeDtypeStruct((n, LANE), I32), jax.ShapeDtypeStruct((nt, 8, N_EXPERTS), F32)],
        scratch_shapes=[pltpu.VMEM((1, N_EXPERTS), F32)],
        compiler_params=pltpu.CompilerParams(dimension_semantics=("arbitrary",), vmem_limit_bytes=VMEM_LIMIT),
        name="post_mix",
    )(*outs, x, w_out_bf16, g2, b2, router_w, rb2)


def _moe_dispatch(idx, gates, rank, cum, n_tok, tm_cum, tm, blk):
    nt = n_tok // tm
    n_blocks = -(-n_tok * TOP_K // blk) + N_EXPERTS
    per_pair = -(-tm // blk) + 1
    e_idx = idx[:, :TOP_K]
    cum_i = cum[:, 0, :].astype(I32)
    onehot = e_idx[:, :, None] == jnp.arange(N_EXPERTS, dtype=I32)
    counts = cum_i[-1] + jnp.sum(onehot[n_tok - tm_cum:].astype(I32), axis=(0, 1))
    cum_t = cum_i[::tm // tm_cum]
    cum_all = jnp.concatenate([cum_t, counts[None]], 0)
    blocks_per_e = (counts + blk - 1) // blk
    blk_start = jnp.cumsum(blocks_per_e) - blocks_per_e
    dest = jnp.sum(jnp.where(onehot, blk_start, 0), axis=-1) * blk + rank[:, :TOP_K]
    dest_f = dest.astype(F32)
    dest_pad = jnp.pad(dest_f, ((0, 0), (0, LANE - TOP_K)), constant_values=-1.0)
    dest_t = jnp.pad(dest_f.T, ((0, 8 - TOP_K), (0, 0)), constant_values=-1.0)
    gate_t = jnp.pad(gates[:, :TOP_K].T, ((0, 8 - TOP_K), (0, 0)))
    eid_t = jnp.pad(e_idx.T, ((0, 8 - TOP_K), (0, 0)), constant_values=-1)

    lo, hi = cum_all[:-1].T, cum_all[1:].T
    has = hi > lo
    j0 = lo // blk
    j1 = jnp.maximum(hi - 1, 0) // blk
    jj = j0[:, :, None] + jnp.arange(per_pair, dtype=I32)
    valid = has[:, :, None] & (jj <= j1[:, :, None])
    blk_id = blk_start[:, None, None] + jj
    e_b = jnp.broadcast_to(jnp.arange(N_EXPERTS, dtype=I32)[:, None, None], jj.shape)
    t_b = jnp.broadcast_to(jnp.arange(nt, dtype=I32)[None, :, None], jj.shape)
    n_steps = min(N_EXPERTS * nt + n_blocks, per_pair * N_EXPERTS * nt)
    n_valid = jnp.sum(valid)
    pos = jnp.arange(n_steps)
    v = pos < n_valid

    win = min(MOE_ROW_WINDOW, blk)
    r_lo = jnp.maximum(lo[:, :, None], jj * blk) - jj * blk
    r_hi = jnp.minimum(hi[:, :, None], (jj + 1) * blk) - jj * blk
    w0 = (r_lo // 16) * 16
    n_win = jnp.maximum((r_hi - w0 + win - 1) // win, 1)
    win_bits = (n_win << 4) + (w0 << 8)

    def build(key):
        order = jnp.argsort(jnp.where(valid, key, jnp.int32(1 << 30)).reshape(-1))[:n_steps]
        src = order[jnp.minimum(pos, n_valid - 1)]
        return blk_id.reshape(-1)[src], t_b.reshape(-1)[src], e_b.reshape(-1)[src], win_bits.reshape(-1)[src]

    def edges(a):
        change = a[1:] != a[:-1]
        first = jnp.concatenate([jnp.ones((1,), bool), change]) & v
        last = (jnp.concatenate([change, jnp.ones((1,), bool)]) | (pos == n_valid - 1)) & v
        return first, last

    b1, t1, e1, wb1 = build(blk_id * nt + t_b)
    first_b, last_b = edges(b1)
    first_e, _ = edges(e1)
    flags1 = first_b.astype(I32) + 2 * last_b.astype(I32) + 4 * v.astype(I32) + 8 * first_e.astype(I32) + wb1
    b2, t2, e2, wb2 = build(t_b * (2 * n_blocks) + blk_id)
    first_t, last_t = edges(t2)
    flags2 = first_t.astype(I32) + 2 * last_t.astype(I32) + 4 * v.astype(I32) + wb2
    return dict(dest=dest_pad, eid=idx, dest_t=dest_t, gate_t=gate_t, eid_t=eid_t,
                n_rows=n_blocks * blk, n_steps=n_steps,
                gather=(b1.astype(I32), t1.astype(I32), e1.astype(I32), flags1),
                combine=(b2.astype(I32), t2.astype(I32), e2.astype(I32), flags2))


def _expert_kernel(blk_s, tile_s, exp_s, flag_s, x_ref, eid_ref, dest_ref, gate_ref, wgu_ref, bgu_ref, wdn_ref, bdn_ref,
                   yb_ref, xacc_ref, gacc_ref, wgu_bf, wdn_bf, *, blk, tm):
    s = pl.program_id(0)
    flags = flag_s[s]

    @pl.when((flags & 8) != 0)
    def _():
        wgu_bf[...] = wgu_ref[0, 0].astype(BF16)
        wdn_bf[...] = wdn_ref[0, 0].astype(BF16)

    @pl.when((flags & 1) != 0)
    def _():
        xacc_ref[...] = jnp.zeros_like(xacc_ref)
        gacc_ref[...] = jnp.zeros_like(gacc_ref)

    @pl.when((flags & 4) != 0)
    def _():
        mine = eid_ref[...] == exp_s[s]
        d_sel = jnp.sum(jnp.where(mine, dest_ref[...] + 1.0, 0.0), axis=0, keepdims=True) - 1.0
        g_sel = jnp.sum(jnp.where(mine, gate_ref[...], 0.0), axis=0, keepdims=True)
        win = min(MOE_ROW_WINDOW, blk)
        w0 = flags >> 8
        base = blk_s[s] * blk

        def window(i, carry):
            start = w0 + i * win
            r0 = pl.multiple_of(jnp.minimum(start, blk - win), 16)
            local = r0 + lax.broadcasted_iota(I32, (win, tm), 0)
            hit = (d_sel == (base + local).astype(F32)) & (local >= start)
            p = jnp.where(hit, 1.0, 0.0).astype(BF16)
            xacc_ref[pl.ds(r0, win), :] += jnp.dot(p, x_ref[...], preferred_element_type=F32)
            gacc_ref[pl.ds(r0, win), :] += jnp.sum(jnp.where(hit, g_sel, 0.0), axis=-1, keepdims=True)
            return carry

        lax.fori_loop(0, (flags >> 4) & 15, window, 0)

    @pl.when((flags & 2) != 0)
    def _():
        h = jnp.dot(xacc_ref[...].astype(BF16), wgu_bf[...], preferred_element_type=F32) + bgu_ref[0, 0]
        gate = jnp.minimum(h[:, :D_FF], SWIGLU_LIMIT)
        up = jnp.clip(h[:, D_FF:], -SWIGLU_LIMIT, SWIGLU_LIMIT)
        act = (up + 1.0) * gate * _sigmoid(SWIGLU_ALPHA * gate)
        y = jnp.dot(act.astype(BF16), wdn_bf[...], preferred_element_type=F32) + bdn_ref[0, 0]
        yb_ref[...] = (y * gacc_ref[...]).astype(yb_ref.dtype)


def _expert_ffn(layer, xn_bf16, disp, w_gu, b_gu, w_down, b_down, tm, blk):
    blk_s, tile_s, exp_s, flag_s = disp["gather"]
    depth = w_gu.shape[0]
    tok_t = pl.BlockSpec((8, tm), lambda s, b, t, e, f: (0, t[s]))
    grid_spec = pltpu.PrefetchScalarGridSpec(
        num_scalar_prefetch=4,
        grid=(disp["n_steps"],),
        in_specs=[pl.BlockSpec((tm, D_MODEL), lambda s, b, t, e, f: (t[s], 0)), tok_t, tok_t, tok_t,
                  pl.BlockSpec((1, 1, D_MODEL, 2 * D_FF), lambda s, b, t, e, f: (layer, e[s], 0, 0)),
                  pl.BlockSpec((1, 1, 1, 2 * D_FF), lambda s, b, t, e, f: (layer, e[s], 0, 0)),
                  pl.BlockSpec((1, 1, D_FF, D_MODEL), lambda s, b, t, e, f: (layer, e[s], 0, 0)),
                  pl.BlockSpec((1, 1, 1, D_MODEL), lambda s, b, t, e, f: (layer, e[s], 0, 0))],
        out_specs=pl.BlockSpec((blk, D_MODEL), lambda s, b, t, e, f: (b[s], 0)),
        scratch_shapes=[pltpu.VMEM((blk, D_MODEL), F32), pltpu.VMEM((blk, 1), F32),
                        pltpu.VMEM((D_MODEL, 2 * D_FF), BF16), pltpu.VMEM((D_FF, D_MODEL), BF16)])
    return pl.pallas_call(
        functools.partial(_expert_kernel, blk=blk, tm=tm),
        grid_spec=grid_spec,
        out_shape=jax.ShapeDtypeStruct((disp["n_rows"], D_MODEL), BF16),
        compiler_params=pltpu.CompilerParams(dimension_semantics=("arbitrary",), vmem_limit_bytes=VMEM_LIMIT),
        name="expert_ffn",
    )(blk_s, tile_s, exp_s, flag_s, xn_bf16, disp["eid_t"], disp["dest_t"], disp["gate_t"],
      w_gu, b_gu.reshape(depth, N_EXPERTS, 1, 2 * D_FF), w_down, b_down.reshape(depth, N_EXPERTS, 1, D_MODEL))


def _combine_kernel(blk_s, tile_s, exp_s, flag_s, yb_ref, eid_ref, dest_ref, xn_ref, g_ref, b_ref, y_ref, acc_ref,
                    *, blk, tm):
    s = pl.program_id(0)
    flags = flag_s[s]

    @pl.when((flags & 1) != 0)
    def _():
        acc_ref[...] = jnp.zeros_like(acc_ref)

    @pl.when((flags & 4) != 0)
    def _():
        lane = lax.broadcasted_iota(I32, (tm, LANE), 1)
        mine = (eid_ref[...] == exp_s[s]) & (lane < TOP_K)
        d_sel = jnp.sum(jnp.where(mine, dest_ref[...] + 1.0, 0.0), axis=-1, keepdims=True) - 1.0
        win = min(MOE_ROW_WINDOW, blk)
        w0 = flags >> 8
        base = blk_s[s] * blk

        def window(i, carry):
            start = w0 + i * win
            r0 = pl.multiple_of(jnp.minimum(start, blk - win), 16)
            local = r0 + lax.broadcasted_iota(I32, (tm, win), 1)
            hit = (d_sel == (base + local).astype(F32)) & (local >= start)
            acc_ref[...] += jnp.dot(jnp.where(hit, 1.0, 0.0).astype(BF16), yb_ref[pl.ds(r0, win), :],
                                    preferred_element_type=F32)
            return carry

        lax.fori_loop(0, (flags >> 4) & 15, window, 0)

    @pl.when((flags & 2) != 0)
    def _():
        y_ref[...] = _ln(DN_ALPHA * xn_ref[...] + acc_ref[...], g_ref[...], b_ref[...])


def _moe_combine(yb, disp, xn, ln_g, ln_b, tm, blk):
    blk_s, tile_s, exp_s, flag_s = disp["combine"]
    n = xn.shape[0]
    tok = pl.BlockSpec((tm, LANE), lambda s, b, t, e, f: (t[s], 0))
    grid_spec = pltpu.PrefetchScalarGridSpec(
        num_scalar_prefetch=4,
        grid=(disp["n_steps"],),
        in_specs=[pl.BlockSpec((blk, D_MODEL), lambda s, b, t, e, f: (b[s], 0)), tok, tok,
                  pl.BlockSpec((tm, D_MODEL), lambda s, b, t, e, f: (t[s], 0)),
                  pl.BlockSpec((1, D_MODEL), lambda s, b, t, e, f: (0, 0)),
                  pl.BlockSpec((1, D_MODEL), lambda s, b, t, e, f: (0, 0))],
        out_specs=pl.BlockSpec((tm, D_MODEL), lambda s, b, t, e, f: (t[s], 0)),
        scratch_shapes=[pltpu.VMEM((tm, D_MODEL), F32)])
    return pl.pallas_call(
        functools.partial(_combine_kernel, blk=blk, tm=tm),
        grid_spec=grid_spec,
        out_shape=jax.ShapeDtypeStruct((n, D_MODEL), F32),
        compiler_params=pltpu.CompilerParams(dimension_semantics=("arbitrary",), vmem_limit_bytes=VMEM_LIMIT),
        name="moe_combine",
    )(blk_s, tile_s, exp_s, flag_s, yb, disp["eid"], disp["dest"], xn, ln_g.reshape(1, -1), ln_b.reshape(1, -1))


def _post_mix_moe(layer, outs, x, w_out, fw):
    ln1_g, ln1_b, router_w, router_b, w_gu, b_gu, w_down, b_down, ln2_g, ln2_b = fw
    n = x.shape[0]
    tm_cum = min(256, n)
    tm = min(512, n)
    blk = min(EXPERT_BLOCK, n * TOP_K)
    xn, xnb, idx, gates, rank, cum = _post_mix(outs, x, w_out.astype(BF16), ln1_g, ln1_b, router_w, router_b, tm_cum)
    disp = _moe_dispatch(idx, gates, rank, cum, n, tm_cum, tm, blk)
    yb = _expert_ffn(layer, xnb, disp, w_gu, b_gu, w_down, b_down, tm, blk)
    return _moe_combine(yb, disp, xn, ln2_g, ln2_b, tm, blk)


def _rec_step_kernel(gqkv_ref, gz_ref, xbc_ref, sz_ref, small_ref, gconv0_ref, sconv0_ref, s0_ref, h0_ref,
                     gcw_ref, scw_ref, scb_ref, gpar_ref, spar_ref, gnw_ref, snw_ref,
                     gout_ref, sout_ref, gconv_ref, sconv_ref, s_ref, h_ref):
    small = small_ref[0]

    def conv(u, buf_ref, w_ref, out_ref):
        buf = buf_ref[0]
        y = u * w_ref[CONV_W - 1:CONV_W, :]
        for i in range(CONV_W - 1):
            y = y + buf[i:i + 1, :] * w_ref[i:i + 1, :]
        out_ref[0, 0:CONV_W - 2, :] = buf[1:CONV_W - 1, :]
        out_ref[0, CONV_W - 2:CONV_W - 1, :] = u
        return y

    eye = (lax.broadcasted_iota(I32, (HEAD_DIM, HEAD_DIM), 0)
           == lax.broadcasted_iota(I32, (HEAD_DIM, HEAD_DIM), 1)).astype(F32)

    def to_col(r):
        return jnp.sum(eye * r, axis=-1, keepdims=True)

    y = _silu(conv(gqkv_ref[0], gconv0_ref, gcw_ref, gconv_ref))
    beta_all = _sigmoid(small[:, 0:N_HEADS])
    g_all = gpar_ref[0:1, :] * _softplus(small[:, N_HEADS:2 * N_HEADS] + gpar_ref[1:2, :])
    for h in range(N_HEADS):
        hs = slice(h * HEAD_DIM, (h + 1) * HEAD_DIM)
        q = y[:, h * HEAD_DIM:(h + 1) * HEAD_DIM]
        k = y[:, GROUP_WIDTH + h * HEAD_DIM:GROUP_WIDTH + (h + 1) * HEAD_DIM]
        v = y[:, 2 * GROUP_WIDTH + h * HEAD_DIM:2 * GROUP_WIDTH + (h + 1) * HEAD_DIM]
        q = q * lax.rsqrt(jnp.sum(q * q, axis=-1, keepdims=True) + RMS_EPS) * (HEAD_DIM ** -0.5)
        k = k * lax.rsqrt(jnp.sum(k * k, axis=-1, keepdims=True) + RMS_EPS)
        beta = beta_all[:, h:h + 1]
        eg = jnp.exp(g_all[:, h:h + 1])
        s0 = s0_ref[0, h]
        kc, qc = to_col(k), to_col(q)
        v_new = v * beta - jnp.sum(kc * (beta * eg) * s0, axis=0, keepdims=True)
        o = jnp.sum(qc * eg * s0, axis=0, keepdims=True) + jnp.sum(q * k, axis=-1, keepdims=True) * v_new
        s_ref[0, h] = s0 * eg + kc * v_new
        o = o * lax.rsqrt(jnp.mean(o * o, axis=-1, keepdims=True) + RMS_EPS) * gnw_ref[...] * _silu(gz_ref[0][:, hs])
        gout_ref[0, :, hs] = o

    xbc = _silu(conv(xbc_ref[0], sconv0_ref, scw_ref, sconv_ref) + scb_ref[...])
    dt_all = _softplus(small[:, 2 * N_HEADS:3 * N_HEADS] + spar_ref[1:2, :])
    hpg = N_HEADS // SSM_NGROUPS
    boff, coff = GROUP_WIDTH, GROUP_WIDTH + SSM_NGROUPS * SSM_STATE
    ys = []
    for h in range(N_HEADS):
        grp = h // hpg
        x = xbc[:, h * HEAD_DIM:(h + 1) * HEAD_DIM]
        bm = xbc[:, boff + grp * SSM_STATE:boff + (grp + 1) * SSM_STATE]
        cm = xbc[:, coff + grp * SSM_STATE:coff + (grp + 1) * SSM_STATE]
        dt = dt_all[:, h:h + 1]
        hn = h0_ref[0, h] * jnp.exp(dt * spar_ref[0:1, h:h + 1]) + to_col(x * dt) * bm
        h_ref[0, h] = hn
        yc = jnp.sum(hn * cm, axis=-1, keepdims=True)
        yh = jnp.sum(eye * yc, axis=0, keepdims=True) + spar_ref[2:3, h:h + 1] * x
        ys.append(yh * _silu(sz_ref[0][:, h * HEAD_DIM:(h + 1) * HEAD_DIM]))
    for grp in range(SSM_NGROUPS):
        grp_ys = ys[grp * hpg:(grp + 1) * hpg]
        ms = sum(jnp.sum(a * a, axis=-1, keepdims=True) for a in grp_ys) / (hpg * HEAD_DIM)
        r = lax.rsqrt(ms + RMS_EPS)
        for j, a in enumerate(grp_ys):
            h = grp * hpg + j
            sout_ref[0, :, h * HEAD_DIM:(h + 1) * HEAD_DIM] = a * r * snw_ref[:, h * HEAD_DIM:(h + 1) * HEAD_DIM]


def _rec_step(seg, gconv0, s0, sconv0, h0, lw):
    (_, gdn_conv_w, gdn_a_log, gdn_dt_bias, gdn_norm_w, ssm_conv_w, ssm_conv_b,
     ssm_a_log, ssm_dt_bias, ssm_d, ssm_norm_w, _) = lw
    b = seg["small"].shape[0]
    gpar = jnp.stack([-jnp.exp(gdn_a_log.astype(F32)), gdn_dt_bias.astype(F32)])
    spar = jnp.stack([-jnp.exp(ssm_a_log.astype(F32)), ssm_dt_bias.astype(F32), ssm_d.astype(F32)])
    row = lambda w: pl.BlockSpec((1, 1, w), lambda i: (i, 0, 0))
    full = lambda a: pl.BlockSpec(a.shape, lambda i: (0,) * a.ndim)
    per_seq = lambda *dims: pl.BlockSpec((1,) + dims, lambda i: (i,) + (0,) * len(dims))
    r3 = lambda a: a.reshape(b, 1, -1)
    consts = [gdn_conv_w, ssm_conv_w, ssm_conv_b.reshape(1, -1), gpar, spar,
              gdn_norm_w.reshape(1, HEAD_DIM), ssm_norm_w.reshape(1, GROUP_WIDTH)]
    state_specs = [per_seq(CONV_W - 1, GDN_CONV_DIM), per_seq(CONV_W - 1, SSM_CONV_DIM),
                   per_seq(N_HEADS, HEAD_DIM, HEAD_DIM), per_seq(N_HEADS, HEAD_DIM, SSM_STATE)]
    go, so, gc, sc, s, h = pl.pallas_call(
        _rec_step_kernel,
        grid=(b,),
        in_specs=[row(GDN_CONV_DIM), row(GROUP_WIDTH), row(SSM_CONV_DIM), row(GROUP_WIDTH), row(LANE)]
                 + state_specs + [full(c) for c in consts],
        out_specs=[row(GROUP_WIDTH), row(GROUP_WIDTH)] + state_specs,
        out_shape=[jax.ShapeDtypeStruct((b, 1, GROUP_WIDTH), F32), jax.ShapeDtypeStruct((b, 1, GROUP_WIDTH), F32),
                   jax.ShapeDtypeStruct((b, CONV_W - 1, GDN_CONV_DIM), F32),
                   jax.ShapeDtypeStruct((b, CONV_W - 1, SSM_CONV_DIM), F32),
                   jax.ShapeDtypeStruct((b, N_HEADS, HEAD_DIM, HEAD_DIM), F32),
                   jax.ShapeDtypeStruct((b, N_HEADS, HEAD_DIM, SSM_STATE), F32)],
        compiler_params=pltpu.CompilerParams(dimension_semantics=("arbitrary",), vmem_limit_bytes=VMEM_LIMIT),
        name="rec_step",
    )(r3(seg["gdn_qkv"]), r3(seg["gdn_z"]), r3(seg["ssm_xbc"]), r3(seg["ssm_z"]), r3(seg["small"]),
      gconv0, sconv0, s0, h0, *consts)
    return go.reshape(b, GROUP_WIDTH), so.reshape(b, GROUP_WIDTH), gc, sc, s, h


PAGES_PER_STEP = 8


def _decode_pass_kernel(pt_ref, sq_ref, mq_ref, *refs, n_steps):
    pp = PAGES_PER_STEP
    sk_refs, sv_refs, mk_refs = refs[0:pp], refs[pp:2 * pp], refs[2 * pp:3 * pp]
    sb_out_ref, gp_ref, accv_ref, carry_ref = refs[3 * pp:]
    p = pl.program_id(1)

    @pl.when(p == 0)
    def _():
        accv_ref[...] = jnp.zeros_like(accv_ref)
        carry_ref[...] = jnp.zeros_like(carry_ref)

    qs = sq_ref[0] * (HEAD_DIM ** -0.5)
    qm = mq_ref[0] * (1.0 / MOBA_BLOCK)
    row = lax.broadcasted_iota(I32, (PAGE_SIZE, PAGE_SIZE), 0)
    col = lax.broadcasted_iota(I32, (PAGE_SIZE, PAGE_SIZE), 1)
    upper = (row > col).astype(BF16)
    sub8 = lax.broadcasted_iota(I32, (8, LANE), 0)
    diag8 = sub8 == lax.broadcasted_iota(I32, (8, LANE), 1)
    heads = [slice(h * HEAD_DIM, (h + 1) * HEAD_DIM) for h in range(N_HEADS)]

    def head_rows(x):
        out = jnp.zeros((8, LANE), F32)
        for h, hs in enumerate(heads):
            out = jnp.where(sub8 == h, jnp.sum(x[hs, :], axis=0, keepdims=True), out)
        return out

    zs = [head_rows(sk_refs[i][0, 0] * qs) for i in range(pp)]
    ls = [-_softplus(z) for z in zs]
    las = _split_dot(jnp.concatenate(ls, axis=0), upper)
    accv = [accv_ref[hs, :] for hs in heads]
    carry = carry_ref[...]
    for i in range(pp):
        la = las[i * 8:(i + 1) * 8, :]
        w = jnp.exp(zs[i] + ls[i] + la + carry)
        vt = sv_refs[i][0, 0]
        accv = [a + vt[hs, :] * w[h:h + 1, :] for h, (hs, a) in enumerate(zip(heads, accv))]
        carry = carry + la[:, 0:1] + ls[i][:, 0:1]
        gp = jnp.sum(head_rows(mk_refs[i][0, 0] * qm), axis=-1, keepdims=True)
        gp_ref[0, pl.ds(p * pp + i, 1), :] = jnp.sum(jnp.where(diag8, gp, 0.0), axis=0, keepdims=True)
    for hs, a in zip(heads, accv):
        accv_ref[hs, :] = a
    carry_ref[...] = carry

    @pl.when(p == n_steps - 1)
    def _():
        sb_out_ref[0] = jnp.sum(accv_ref[...], axis=-1, keepdims=True)


def _decode_pass(layer, page_table, sq, mq, cache_sb_k, cache_sb_v, cache_moba_k):
    b, n_pages = page_table.shape
    pp = PAGES_PER_STEP
    assert n_pages % pp == 0
    n_steps = n_pages // pp

    def page_spec(i):
        return pl.BlockSpec((1, 1, GROUP_WIDTH, PAGE_SIZE),
                            lambda bi, p, pt: (layer, pt[bi, n_pages - 1 - (p * pp + i)], 0, 0))

    colspec = pl.BlockSpec((1, GROUP_WIDTH, 1), lambda bi, p, pt: (bi, 0, 0))
    grid_spec = pltpu.PrefetchScalarGridSpec(
        num_scalar_prefetch=1,
        grid=(b, n_steps),
        in_specs=[colspec, colspec] + [page_spec(i) for i in range(pp)] * 3,
        out_specs=[colspec, pl.BlockSpec((1, n_pages, LANE), lambda bi, p, pt: (bi, 0, 0))],
        scratch_shapes=[pltpu.VMEM((GROUP_WIDTH, PAGE_SIZE), F32), pltpu.VMEM((8, 1), F32)])
    sb_out, gp = pl.pallas_call(
        functools.partial(_decode_pass_kernel, n_steps=n_steps),
        grid_spec=grid_spec,
        out_shape=[jax.ShapeDtypeStruct((b, GROUP_WIDTH, 1), F32), jax.ShapeDtypeStruct((b, n_pages, LANE), F32)],
        compiler_params=pltpu.CompilerParams(dimension_semantics=("arbitrary", "arbitrary"), vmem_limit_bytes=VMEM_LIMIT),
        name="decode_pass",
    )(page_table, sq.reshape(b, GROUP_WIDTH, 1), mq.reshape(b, GROUP_WIDTH, 1),
      *([cache_sb_k] * pp), *([cache_sb_v] * pp), *([cache_moba_k] * pp))
    return sb_out.reshape(b, GROUP_WIDTH), gp


def _moba_decode_kernel(pg_ref, mq_ref, mk_ref, mv_ref, *refs, n_sel):
    kp_refs, vp_refs, o_ref = refs[:n_sel], refs[n_sel:2 * n_sel], refs[2 * n_sel]
    h = pl.program_id(1)
    lane = lax.broadcasted_iota(I32, (1, GROUP_WIDTH), 1)
    in_head = lane // HEAD_DIM == h
    q = jnp.where(in_head, mq_ref[0] * (HEAD_DIM ** -0.5), 0.0)

    @pl.when(h == 0)
    def _():
        o_ref[...] = jnp.zeros_like(o_ref)

    q8 = jnp.where(lax.broadcasted_iota(I32, (8, GROUP_WIDTH), 0) == 0, q, 0.0).astype(BF16)
    s_self = jnp.sum(q * mk_ref[0], axis=-1, keepdims=True)
    scores = [jnp.dot(q8, kp[0, 0].astype(BF16), preferred_element_type=F32)[0:1, :] for kp in kp_refs]
    m = s_self
    for s in scores:
        m = jnp.maximum(m, jnp.max(s, axis=-1, keepdims=True))
    p_self = jnp.exp(s_self - m)
    l = p_self
    acc = p_self * mv_ref[0]
    first_row = lax.broadcasted_iota(I32, (8, PAGE_SIZE), 0) == 0
    for s, vp in zip(scores, vp_refs):
        p = jnp.exp(s - m)
        l = l + jnp.sum(p, axis=-1, keepdims=True)
        p8 = jnp.where(first_row, p, 0.0).astype(BF16)
        acc = acc + lax.dot_general(p8, vp[0, 0].astype(BF16), (((1,), (1,)), ((), ())),
                                    preferred_element_type=F32)[0:1, :]
    o_ref[0] = jnp.where(in_head, acc / l, o_ref[0])


def _moba_decode(layer, sel_pages, mq, mk_new, mv_new, cache_moba_k, cache_moba_v):
    b, nh, n_sel = sel_pages.shape
    row = pl.BlockSpec((1, 1, GROUP_WIDTH), lambda bi, h, pg: (bi, 0, 0))

    def page(j):
        return pl.BlockSpec((1, 1, GROUP_WIDTH, PAGE_SIZE),
                            lambda bi, h, pg: (layer, pg[(bi * nh + h) * n_sel + j], 0, 0))

    grid_spec = pltpu.PrefetchScalarGridSpec(
        num_scalar_prefetch=1,
        grid=(b, nh),
        in_specs=[row, row, row] + [page(j) for j in range(n_sel)] * 2,
        out_specs=pl.BlockSpec((1, 1, GROUP_WIDTH), lambda bi, h, pg: (bi, 0, 0)))
    r3 = lambda a: a.reshape(b, 1, GROUP_WIDTH)
    out = pl.pallas_call(
        functools.partial(_moba_decode_kernel, n_sel=n_sel),
        grid_spec=grid_spec,
        out_shape=jax.ShapeDtypeStruct((b, 1, GROUP_WIDTH), F32),
        compiler_params=pltpu.CompilerParams(dimension_semantics=("arbitrary",) * 2, vmem_limit_bytes=VMEM_LIMIT),
        name="moba_decode",
    )(sel_pages.reshape(-1).astype(I32), r3(mq), r3(mk_new), r3(mv_new),
      *([cache_moba_k] * n_sel), *([cache_moba_v] * n_sel))
    return out.reshape(b, GROUP_WIDTH)


def _sample_attention(layer, page_table, seg, caches):
    cache_moba_k, cache_moba_v, cache_sb_k, cache_sb_v = caches
    b, n_pages = page_table.shape
    ppb = MOBA_BLOCK // PAGE_SIZE
    n_past_blocks = n_pages // ppb
    assert n_pages % ppb == 0 and n_past_blocks >= MOBA_TOPK
    sb_out, gp = _decode_pass(layer, page_table, seg["sb_q"], seg["moba_q"], cache_sb_k, cache_sb_v, cache_moba_k)
    gates = gp[:, ::-1, :N_HEADS].reshape(b, n_past_blocks, ppb, N_HEADS).sum(axis=2)
    _, sel = lax.top_k(jnp.swapaxes(gates, 1, 2), MOBA_TOPK)
    pages = sel[..., None] * ppb + jnp.arange(ppb)
    sel_pages = jnp.take_along_axis(page_table[:, None, :], pages.reshape(b, N_HEADS, -1), axis=2)
    moba_out = _moba_decode(layer, sel_pages, seg["moba_q"], seg["moba_k"], seg["moba_v"], cache_moba_k, cache_moba_v)
    return moba_out, sb_out


def _kv_rows(seg, bsz, length):
    return tuple(seg[name].reshape(bsz, length, N_HEADS, HEAD_DIM) for name in ("moba_k", "moba_v", "sb_k", "sb_v"))


def _prompt_mixers(x, lw):
    (w_in, gdn_conv_w, gdn_a_log, gdn_dt_bias, gdn_norm_w, ssm_conv_w, ssm_conv_b,
     ssm_a_log, ssm_dt_bias, ssm_d, ssm_norm_w, _) = lw
    bsz, length, _ = x.shape
    n = bsz * length
    flat = _in_proj(x.reshape(n, D_MODEL), w_in, jnp.arange(length), length)
    seg = {k: v.reshape(bsz, length, -1) for k, v in flat.items()}

    gdn_out, gdn_conv_new, gdn_s_new = _gdn_prompt(seg["gdn_qkv"], seg["gdn_z"], seg["small"],
                                                   gdn_conv_w, gdn_a_log, gdn_dt_bias, gdn_norm_w)
    ssm_out, ssm_conv_new, ssm_h_new = _ssd_prompt(seg["ssm_xbc"], seg["ssm_z"], seg["small"], ssm_conv_w, ssm_conv_b,
                                                   ssm_a_log, ssm_dt_bias, ssm_d, ssm_norm_w)
    moba_out = _moba_attention(seg["moba_q"], seg["moba_k"], seg["moba_v"])
    sb_out = _sb_attention(seg["sb_q"], seg["sb_k"], seg["sb_v"])

    outs = [o.reshape(n, GROUP_WIDTH) for o in (gdn_out, ssm_out, moba_out, sb_out)]
    return outs, _kv_rows(seg, bsz, length) + (gdn_conv_new, gdn_s_new, ssm_conv_new, ssm_h_new)


def _sample_mixers(x, layer, page_table, rec_state, caches, lw):
    gdn_conv0, gdn_s0, ssm_conv0, ssm_h0 = rec_state
    bsz, length, _ = x.shape
    assert length == 1
    pos0 = page_table.shape[1] * PAGE_SIZE
    seg = _in_proj(x.reshape(bsz, D_MODEL), lw[0], jnp.full((1,), pos0, I32), 1)
    gdn_out, ssm_out, gdn_conv_new, ssm_conv_new, gdn_s_new, ssm_h_new = _rec_step(
        seg, gdn_conv0, gdn_s0, ssm_conv0, ssm_h0, lw)
    moba_out, sb_out = _sample_attention(layer, page_table, seg, caches)
    outs = [gdn_out, ssm_out, moba_out, sb_out]
    return outs, _kv_rows(seg, bsz, length) + (gdn_conv_new, gdn_s_new, ssm_conv_new, ssm_h_new)


def kernel(x_prompt, x_sample, cache_moba_k, cache_moba_v, cache_sb_k, cache_sb_v,
           state_gdn_conv, state_gdn_rec, state_ssm_conv, state_ssm_rec, page_table,
           w_in, gdn_conv_w, gdn_a_log, gdn_dt_bias, gdn_norm_w,
           ssm_conv_w, ssm_conv_b, ssm_a_log, ssm_dt_bias, ssm_d, ssm_norm_w,
           w_out, ln1_g, ln1_b, router_w, router_b,
           expert_w_gu, expert_b_gu, expert_w_down, expert_b_down, ln2_g, ln2_b):
    caches = tuple(jnp.transpose(c, (0, 1, 3, 4, 2)).reshape(c.shape[0], c.shape[1], GROUP_WIDTH, PAGE_SIZE)
                   for c in (cache_moba_k, cache_moba_v, cache_sb_k, cache_sb_v))
    yp, ys = x_prompt, x_sample
    new_p, new_s = [], []
    for l in range(DEPTH):
        lw = (w_in[l], gdn_conv_w[l], gdn_a_log[l], gdn_dt_bias[l], gdn_norm_w[l], ssm_conv_w[l], ssm_conv_b[l],
              ssm_a_log[l], ssm_dt_bias[l], ssm_d[l], ssm_norm_w[l], w_out[l])
        fw = (ln1_g[l], ln1_b[l], router_w[l], router_b[l], expert_w_gu, expert_b_gu,
              expert_w_down, expert_b_down, ln2_g[l], ln2_b[l])
        rec_state = (state_gdn_conv[l], state_gdn_rec[l], state_ssm_conv[l], state_ssm_rec[l])
        outs_p, st_p = _prompt_mixers(yp, lw)
        yp = _post_mix_moe(l, outs_p, yp.reshape(-1, D_MODEL), w_out[l], fw).reshape(yp.shape)
        outs_s, st_s = _sample_mixers(ys, l, page_table, rec_state, caches, lw)
        ys = _post_mix_moe(l, outs_s, ys.reshape(-1, D_MODEL), w_out[l], fw).reshape(ys.shape)
        new_p.append(st_p)
        new_s.append(st_s)

    def stk(states, i):
        return jnp.stack([s[i] for s in states])

    return (yp, ys,
            stk(new_p, 0), stk(new_p, 1), stk(new_p, 2), stk(new_p, 3),
            stk(new_p, 4), stk(new_p, 5), stk(new_p, 6), stk(new_p, 7),
            stk(new_s, 0), stk(new_s, 1), stk(new_s, 2), stk(new_s, 3),
            stk(new_s, 4), stk(new_s, 5), stk(new_s, 6), stk(new_s, 7))
```

```python
import functools

import jax
import jax.numpy as jnp
from jax import lax
from jax.experimental import pallas as pl
from jax.experimental.pallas import tpu as pltpu

D_MODEL = 1024
DEPTH = 2
PAGE_SIZE = 128
HEAD_DIM = 64
N_MIXERS = 4
GROUP_WIDTH = D_MODEL // N_MIXERS
N_HEADS = GROUP_WIDTH // HEAD_DIM
CONV_W = 4
CHUNK = 64
GDN_CONV_DIM = 3 * GROUP_WIDTH
SSM_STATE = 128
SSM_NGROUPS = 2
SSM_CONV_DIM = GROUP_WIDTH + 2 * SSM_NGROUPS * SSM_STATE
MOBA_BLOCK = 256
MOBA_TOPK = 3
MOBA_QUERY_BLOCK = 64
ROPE_THETA = 500000.0
ROPE_DIM = HEAD_DIM // 4
QUERY_BLOCK = 128
N_EXPERTS = 32
TOP_K = 4
D_FF = D_MODEL
SWIGLU_LIMIT = 7.0
SWIGLU_ALPHA = 1.702
EXPERT_BLOCK = 256
DN_ALPHA = (2 * DEPTH) ** 0.25
LN_EPS = 1e-5
RMS_EPS = 1e-6

F32 = jnp.float32
BF16 = jnp.bfloat16
I32 = jnp.int32
HI = lax.Precision.HIGHEST

LANE = 128
VMEM_LIMIT = 56 * 1024 * 1024
NEG = -1e30
MOE_ROW_WINDOW = 128
GDN_PRECISE_SQUARINGS = 2

_SEG_SIZES = (GDN_CONV_DIM, GROUP_WIDTH, N_HEADS, N_HEADS, SSM_CONV_DIM, GROUP_WIDTH, N_HEADS, GDN_CONV_DIM, GDN_CONV_DIM)
_SEG_NAMES = ("gdn_qkv", "gdn_z", "gdn_b", "gdn_a", "ssm_xbc", "ssm_z", "ssm_dt", "moba_qkv", "sb_qkv")
_OUT_SEGS = (("gdn_qkv", GDN_CONV_DIM), ("gdn_z", GROUP_WIDTH), ("ssm_xbc", SSM_CONV_DIM), ("ssm_z", GROUP_WIDTH),
             ("moba_q", GROUP_WIDTH), ("moba_k", GROUP_WIDTH), ("moba_v", GROUP_WIDTH),
             ("sb_q", GROUP_WIDTH), ("sb_k", GROUP_WIDTH), ("sb_v", GROUP_WIDTH), ("small", LANE))
_ROTARY_SEGS = ("moba_q", "moba_k")


def _softplus(z):
    return jnp.maximum(z, 0.0) + jnp.log1p(jnp.exp(-jnp.abs(z)))


def _sigmoid(x):
    return 1.0 / (1.0 + jnp.exp(-x))


def _silu(x):
    return x * _sigmoid(x)


def _dot(a, b):
    return jnp.dot(a.astype(BF16), b.astype(BF16), preferred_element_type=F32)


def _dot_nt(a, b):
    return lax.dot_general(a.astype(BF16), b.astype(BF16), (((1,), (1,)), ((), ())), preferred_element_type=F32)


def _dot_tn(a, b):
    return lax.dot_general(a.astype(BF16), b.astype(BF16), (((0,), (0,)), ((), ())), preferred_element_type=F32)


def _dot_hi(a, b):
    return jnp.dot(a, b, preferred_element_type=F32, precision=HI)


def _split_dot(x, m_bf16):
    hi = x.astype(BF16)
    lo = (x - hi.astype(F32)).astype(BF16)
    return (jnp.dot(hi, m_bf16, preferred_element_type=F32)
            + jnp.dot(lo, m_bf16, preferred_element_type=F32))


def _split3(x):
    hi = x.astype(BF16)
    r = x - hi.astype(F32)
    mid = r.astype(BF16)
    return hi, mid, (r - mid.astype(F32)).astype(BF16)


def _dot3(a, b):
    a_hi = a.astype(BF16)
    a_lo = (a - a_hi.astype(F32)).astype(BF16)
    b_hi = b.astype(BF16)
    b_lo = (b - b_hi.astype(F32)).astype(BF16)
    return (jnp.dot(a_hi, b_hi, preferred_element_type=F32) + jnp.dot(a_hi, b_lo, preferred_element_type=F32)
            + jnp.dot(a_lo, b_hi, preferred_element_type=F32))


def _ln(h, g, b):
    mu = jnp.mean(h, axis=-1, keepdims=True)
    d = h - mu
    var = jnp.mean(d * d, axis=-1, keepdims=True)
    return d * lax.rsqrt(var + LN_EPS) * g + b


def _in_proj_kernel(x_ref, w_ref, cos_ref, sin_ref, *o_refs):
    y = jnp.dot(x_ref[...].astype(BF16), w_ref[...], preferred_element_type=F32)
    cos = jnp.concatenate([cos_ref[...]] * (GROUP_WIDTH // LANE), axis=1)
    sin = jnp.concatenate([sin_ref[...]] * (GROUP_WIDTH // LANE), axis=1)
    first_half = (lax.broadcasted_iota(I32, (1, GROUP_WIDTH), 1) % HEAD_DIM) < ROPE_DIM // 2
    off = 0
    for (name, width), o_ref in zip(_OUT_SEGS, o_refs):
        seg = y[:, off:off + width]
        if name in _ROTARY_SEGS:
            partner = jnp.where(first_half, pltpu.roll(seg, GROUP_WIDTH - ROPE_DIM // 2, 1), pltpu.roll(seg, ROPE_DIM // 2, 1))
            seg = seg * cos + partner * sin
        o_ref[...] = seg
        off += width


def _rotary_tables(pos):
    half = ROPE_DIM // 2
    inv = ROPE_THETA ** (-jnp.arange(half, dtype=F32) * 2.0 / ROPE_DIM)
    ang = pos.astype(F32)[:, None] * inv
    cc = jnp.arange(LANE) % HEAD_DIM
    cos = jnp.where(cc < ROPE_DIM, jnp.cos(ang)[:, cc % half], 1.0)
    sin = jnp.sin(ang)[:, cc % half]
    sin = jnp.where(cc < half, -sin, jnp.where(cc < ROPE_DIM, sin, 0.0))
    return cos, sin


def _in_proj(x, w_in, pos, seq_len):
    m = x.shape[0]
    tm = min(256, m)
    w_in = w_in.astype(BF16)
    cols, off = {}, 0
    for name, sz in zip(_SEG_NAMES, _SEG_SIZES):
        cols[name] = w_in[:, off:off + sz]
        off += sz
    for name in ("moba", "sb"):
        for j, part in enumerate("qkv"):
            cols[f"{name}_{part}"] = cols[f"{name}_qkv"][:, j * GROUP_WIDTH:(j + 1) * GROUP_WIDTH]
    cols["small"] = jnp.concatenate([cols["gdn_b"], cols["gdn_a"], cols["ssm_dt"],
                                     jnp.zeros((D_MODEL, LANE - 3 * N_HEADS), w_in.dtype)], axis=1)
    w = jnp.concatenate([cols[name] for name, _ in _OUT_SEGS], axis=1)
    n = w.shape[1]
    cos, sin = _rotary_tables(pos)
    if seq_len == 1:
        tab_spec = pl.BlockSpec((1, LANE), lambda i: (0, 0))
    else:
        assert seq_len % tm == 0
        tab_spec = pl.BlockSpec((tm, LANE), lambda i: (i % (seq_len // tm), 0))
    outs = pl.pallas_call(
        _in_proj_kernel,
        grid=(m // tm,),
        in_specs=[pl.BlockSpec((tm, D_MODEL), lambda i: (i, 0)),
                  pl.BlockSpec((D_MODEL, n), lambda i: (0, 0)), tab_spec, tab_spec],
        out_specs=[pl.BlockSpec((tm, width), lambda i: (i, 0)) for _, width in _OUT_SEGS],
        out_shape=[jax.ShapeDtypeStruct((m, width), F32) for _, width in _OUT_SEGS],
        compiler_params=pltpu.CompilerParams(dimension_semantics=("arbitrary",), vmem_limit_bytes=VMEM_LIMIT),
        name="in_proj",
    )(x, w, cos, sin)
    return {name: o for (name, _), o in zip(_OUT_SEGS, outs)}


def _sb_kernel(q_ref, k_ref, v_ref, o_ref, *, tq):
    i = pl.program_id(1)
    row = lax.broadcasted_iota(I32, (tq, tq), 0)
    col = lax.broadcasted_iota(I32, (tq, tq), 1)
    upper = (row > col).astype(BF16)
    past = col < row

    heads = [slice(h * HEAD_DIM, (h + 1) * HEAD_DIM) for h in range(N_HEADS)]
    qs = [(q_ref[0, :, hs] * (HEAD_DIM ** -0.5)).astype(BF16) for hs in heads]

    def block(j, state, diag):
        out = []
        for hs, q, (carry, acc) in zip(heads, qs, state):
            kj = k_ref[0, pl.ds(j * tq, tq), hs].astype(BF16)
            vj = v_ref[0, pl.ds(j * tq, tq), hs].astype(BF16)
            z = lax.dot_general(q, kj, (((1,), (1,)), ((), ())), preferred_element_type=F32)
            ls = -_softplus(z)
            lk = jnp.where(past, ls, 0.0) if diag else ls
            la = _split_dot(lk, upper)
            w = jnp.exp(z + ls + la + carry)
            if diag:
                w = jnp.where(past, w, 0.0)
            acc = acc + jnp.dot(w.astype(BF16), vj, preferred_element_type=F32)
            out.append((carry + la[:, 0:1] + lk[:, 0:1], acc))
        return tuple(out)

    state = tuple((jnp.zeros((tq, 1), F32), jnp.zeros((tq, HEAD_DIM), F32)) for _ in heads)
    state = block(i, state, True)
    state = lax.fori_loop(0, i, lambda t, c: block(i - 1 - t, c, False), state)
    for hs, (_, acc) in zip(heads, state):
        o_ref[0, :, hs] = acc


def _sb_attention(q, k, v, tq=256):
    b, l, d = q.shape
    assert l % tq == 0
    return pl.pallas_call(
        functools.partial(_sb_kernel, tq=tq),
        grid=(b, l // tq),
        in_specs=[pl.BlockSpec((1, tq, d), lambda bi, i: (bi, i, 0)),
                  pl.BlockSpec((1, l, d), lambda bi, i: (bi, 0, 0)),
                  pl.BlockSpec((1, l, d), lambda bi, i: (bi, 0, 0))],
        out_specs=pl.BlockSpec((1, tq, d), lambda bi, i: (bi, i, 0)),
        out_shape=jax.ShapeDtypeStruct((b, l, d), F32),
        compiler_params=pltpu.CompilerParams(dimension_semantics=("arbitrary",) * 2, vmem_limit_bytes=VMEM_LIMIT),
        name="sb_attention",
    )(q, k, v)


def _moba_kernel(q_ref, k_ref, v_ref, o_ref, kmean_ref, *, nb):
    t = MOBA_BLOCK
    i = pl.program_id(1)

    @pl.when(i == 0)
    def _():
        for n in range(nb):
            kmean_ref[n:n + 1, :] = jnp.mean(k_ref[0, n * t:(n + 1) * t, :], axis=0, keepdims=True)

    blk_id = lax.broadcasted_iota(I32, (nb, t), 0)
    row = lax.broadcasted_iota(I32, (t, t), 0)
    col = lax.broadcasted_iota(I32, (t, t), 1)

    heads = [slice(h * HEAD_DIM, (h + 1) * HEAD_DIM) for h in range(N_HEADS)]
    qs, sels, state = [], [], []
    for hs in heads:
        qf = q_ref[0, :, hs]
        q = (qf * (HEAD_DIM ** -0.5)).astype(BF16)
        gate = lax.dot_general(kmean_ref[:, hs], qf, (((1,), (1,)), ((), ())),
                               preferred_element_type=F32, precision=HI)
        cnt = jnp.zeros((nb, t), F32)
        for n2 in range(nb):
            cn = gate[n2:n2 + 1, :]
            beats = (cn > gate) | ((cn == gate) & (n2 < blk_id))
            cnt = cnt + jnp.where(beats & (n2 < i), 1.0, 0.0)
        sels.append(jnp.where((cnt < MOBA_TOPK) & (blk_id < i), 1.0, 0.0).astype(BF16))
        qs.append(q)

        ki = k_ref[0, pl.ds(i * t, t), hs].astype(BF16)
        vi = v_ref[0, pl.ds(i * t, t), hs].astype(BF16)
        s = lax.dot_general(q, ki, (((1,), (1,)), ((), ())), preferred_element_type=F32)
        s = jnp.where(col <= row, s, NEG)
        m = jnp.max(s, axis=-1, keepdims=True)
        p = jnp.exp(s - m)
        state.append((m, jnp.sum(p, axis=-1, keepdims=True), jnp.dot(p.astype(BF16), vi, preferred_element_type=F32)))

    def body(j, state):
        pick = ((lax.broadcasted_iota(I32, (nb, 8), 0) == j) & (lax.broadcasted_iota(I32, (nb, 8), 1) == 0)).astype(BF16)
        out = []
        for hs, q, sel, (m, l, acc) in zip(heads, qs, sels, state):
            kj = k_ref[0, pl.ds(j * t, t), hs].astype(BF16)
            vj = v_ref[0, pl.ds(j * t, t), hs].astype(BF16)
            s = lax.dot_general(q, kj, (((1,), (1,)), ((), ())), preferred_element_type=F32)
            sj = lax.dot_general(sel, pick, (((0,), (0,)), ((), ())), preferred_element_type=F32)[:, 0:1]
            s = jnp.where(sj > 0.5, s, NEG)
            m_new = jnp.maximum(m, jnp.max(s, axis=-1, keepdims=True))
            a = jnp.exp(m - m_new)
            p = jnp.exp(s - m_new)
            out.append((m_new, a * l + jnp.sum(p, axis=-1, keepdims=True),
                        a * acc + jnp.dot(p.astype(BF16), vj, preferred_element_type=F32)))
        return tuple(out)

    state = lax.fori_loop(0, i, body, tuple(state))
    for hs, (_, l, acc) in zip(heads, state):
        o_ref[0, :, hs] = acc / l


def _moba_attention(q, k, v):
    b, l, d = q.shape
    t = MOBA_BLOCK
    assert l % t == 0
    nb = l // t
    return pl.pallas_call(
        functools.partial(_moba_kernel, nb=nb),
        grid=(b, nb),
        in_specs=[pl.BlockSpec((1, t, d), lambda bi, i: (bi, i, 0)),
                  pl.BlockSpec((1, l, d), lambda bi, i: (bi, 0, 0)),
                  pl.BlockSpec((1, l, d), lambda bi, i: (bi, 0, 0))],
        out_specs=pl.BlockSpec((1, t, d), lambda bi, i: (bi, i, 0)),
        out_shape=jax.ShapeDtypeStruct((b, l, d), F32),
        scratch_shapes=[pltpu.VMEM((nb, d), F32)],
        compiler_params=pltpu.CompilerParams(dimension_semantics=("arbitrary",) * 2, vmem_limit_bytes=VMEM_LIMIT),
        name="moba_attention",
    )(q, k, v)


def _conv_tile(ext_ref, u, w_ref, tl):
    ext_ref[8:8 + tl, :] = u
    y = ext_ref[5:5 + tl, :] * w_ref[0:1, :]
    for i in range(1, CONV_W):
        y = y + ext_ref[5 + i:5 + i + tl, :] * w_ref[i:i + 1, :]
    tail = ext_ref[tl + 5:tl + 8, :]
    ext_ref[5:8, :] = tail
    return y, tail


def _gdn_kernel(qkv_ref, z_ref, small_ref, convw_ref, par_ref, normw_ref,
                o_ref, conv_out_ref, s_out_ref, ext_ref, s_ref, *, tl):
    c = CHUNK
    step = pl.program_id(1)

    @pl.when(step == 0)
    def _():
        ext_ref[...] = jnp.zeros_like(ext_ref)
        s_ref[...] = jnp.zeros_like(s_ref)

    y, tail = _conv_tile(ext_ref, qkv_ref[0], convw_ref, tl)
    conv_out_ref[0] = tail
    y = _silu(y)
    small = small_ref[0]
    beta_all = _sigmoid(small[:, 0:N_HEADS])
    g_all = par_ref[0:1, :] * _softplus(small[:, N_HEADS:2 * N_HEADS] + par_ref[1:2, :])

    row = lax.broadcasted_iota(I32, (tl, tl), 0)
    col = lax.broadcasted_iota(I32, (tl, tl), 1)
    same = (row // c) == (col // c)
    tril = same & (row >= col)
    strict = same & (row > col)
    tril_b = tril.astype(BF16)
    triu_b = (same & (row <= col)).astype(BF16)
    eye = (row == col).astype(F32)

    g8 = jnp.concatenate([g_all, jnp.zeros((tl, 8 - N_HEADS), F32)], axis=1)
    g_parts = _split3(g8)
    gc_cols = sum(jnp.dot(tril_b, part, preferred_element_type=F32) for part in g_parts)
    gc_rows = sum(lax.dot_general(part, triu_b, (((0,), (0,)), ((), ())), preferred_element_type=F32) for part in g_parts)

    for h in range(N_HEADS):
        hs = slice(h * HEAD_DIM, (h + 1) * HEAD_DIM)
        q = y[:, h * HEAD_DIM:(h + 1) * HEAD_DIM]
        k = y[:, GROUP_WIDTH + h * HEAD_DIM:GROUP_WIDTH + (h + 1) * HEAD_DIM]
        v = y[:, 2 * GROUP_WIDTH + h * HEAD_DIM:2 * GROUP_WIDTH + (h + 1) * HEAD_DIM]
        q = q * lax.rsqrt(jnp.sum(q * q, axis=-1, keepdims=True) + RMS_EPS) * (HEAD_DIM ** -0.5)
        k = k * lax.rsqrt(jnp.sum(k * k, axis=-1, keepdims=True) + RMS_EPS)
        beta = beta_all[:, h:h + 1]
        gcol = gc_cols[:, h:h + 1]
        grow = gc_rows[h:h + 1, :]
        decay = jnp.where(tril, jnp.exp(jnp.where(tril, gcol - grow, 0.0)), 0.0)
        kb = k * beta
        a = jnp.where(strict, _dot_nt(kb, k) * decay, 0.0)
        p = -a
        tinv = eye + p
        for m in range(5):
            mm = _dot3 if m < GDN_PRECISE_SQUARINGS else _dot
            p = mm(p, p)
            tinv = tinv + mm(tinv, p)
        u = _dot(tinv, v * beta)
        w = _dot(tinv, kb * jnp.exp(gcol))
        attn = jnp.where(tril, _dot_nt(q, k) * decay, 0.0)
        qe = q * jnp.exp(gcol)
        s = s_ref[h]
        for ci in range(tl // c):
            rs = slice(ci * c, (ci + 1) * c)
            g_last = gcol[(ci + 1) * c - 1:(ci + 1) * c, :]
            v_new = u[rs] - _dot(w[rs], s)
            o = _dot(qe[rs], s) + _dot(attn[rs, rs], v_new)
            s = s * jnp.exp(g_last) + _dot_tn(k[rs] * jnp.exp(g_last - gcol[rs]), v_new)
            o = o * lax.rsqrt(jnp.mean(o * o, axis=-1, keepdims=True) + RMS_EPS) * normw_ref[...] * _silu(z_ref[0, rs, hs])
            o_ref[0, rs, hs] = o
        s_ref[h] = s

    @pl.when(step == pl.num_programs(1) - 1)
    def _():
        s_out_ref[0] = s_ref[...]


def _gdn_prompt(qkv, z, small, conv_w, a_log, dt_bias, norm_w, tl=256):
    b, l, _ = qkv.shape
    assert l % tl == 0 and tl % CHUNK == 0
    par = jnp.stack([-jnp.exp(a_log.astype(F32)), dt_bias.astype(F32)])
    return pl.pallas_call(
        functools.partial(_gdn_kernel, tl=tl),
        grid=(b, l // tl),
        in_specs=[pl.BlockSpec((1, tl, GDN_CONV_DIM), lambda bi, i: (bi, i, 0)),
                  pl.BlockSpec((1, tl, GROUP_WIDTH), lambda bi, i: (bi, i, 0)),
                  pl.BlockSpec((1, tl, LANE), lambda bi, i: (bi, i, 0)),
                  pl.BlockSpec((CONV_W, GDN_CONV_DIM), lambda bi, i: (0, 0)),
                  pl.BlockSpec((2, N_HEADS), lambda bi, i: (0, 0)),
                  pl.BlockSpec((1, HEAD_DIM), lambda bi, i: (0, 0))],
        out_specs=[pl.BlockSpec((1, tl, GROUP_WIDTH), lambda bi, i: (bi, i, 0)),
                   pl.BlockSpec((1, CONV_W - 1, GDN_CONV_DIM), lambda bi, i: (bi, 0, 0)),
                   pl.BlockSpec((1, N_HEADS, HEAD_DIM, HEAD_DIM), lambda bi, i: (bi, 0, 0, 0))],
        out_shape=[jax.ShapeDtypeStruct((b, l, GROUP_WIDTH), F32),
                   jax.ShapeDtypeStruct((b, CONV_W - 1, GDN_CONV_DIM), F32),
                   jax.ShapeDtypeStruct((b, N_HEADS, HEAD_DIM, HEAD_DIM), F32)],
        scratch_shapes=[pltpu.VMEM((tl + 8, GDN_CONV_DIM), F32),
                        pltpu.VMEM((N_HEADS, HEAD_DIM, HEAD_DIM), F32)],
        compiler_params=pltpu.CompilerParams(dimension_semantics=("arbitrary", "arbitrary"),
                                             vmem_limit_bytes=VMEM_LIMIT),
        name="gdn_prompt",
    )(qkv, z, small, conv_w, par, norm_w.reshape(1, HEAD_DIM))


def _ssd_kernel(xbc_ref, z_ref, small_ref, convw_ref, convb_ref, par_ref, normw_ref,
                o_ref, conv_out_ref, h_out_ref, ext_ref, h_ref, *, tl):
    c = CHUNK
    step = pl.program_id(1)
    hpg = N_HEADS // SSM_NGROUPS

    @pl.when(step == 0)
    def _():
        ext_ref[...] = jnp.zeros_like(ext_ref)
        h_ref[...] = jnp.zeros_like(h_ref)

    y, tail = _conv_tile(ext_ref, xbc_ref[0], convw_ref, tl)
    conv_out_ref[0] = tail
    xbc = _silu(y + convb_ref[...])
    dt_all = _softplus(small_ref[0][:, 2 * N_HEADS:3 * N_HEADS] + par_ref[1:2, :])
    ad_all = dt_all * par_ref[0:1, :]

    row = lax.broadcasted_iota(I32, (c, c), 0)
    col = lax.broadcasted_iota(I32, (c, c), 1)
    tril = row >= col
    tril_f = tril.astype(F32)
    triu_f = (row <= col).astype(F32)
    ones = jnp.ones((c, c), F32)
    boff = GROUP_WIDTH
    coff = GROUP_WIDTH + SSM_NGROUPS * SSM_STATE

    for ci in range(tl // c):
        r0 = ci * c
        ad_c = ad_all[r0:r0 + c, :]
        acs_all = _dot_hi(tril_f, ad_c)
        ys = []
        for h in range(N_HEADS):
            grp = h // hpg
            x = xbc[r0:r0 + c, h * HEAD_DIM:(h + 1) * HEAD_DIM]
            bm = xbc[r0:r0 + c, boff + grp * SSM_STATE:boff + (grp + 1) * SSM_STATE]
            cm = xbc[r0:r0 + c, coff + grp * SSM_STATE:coff + (grp + 1) * SSM_STATE]
            xdt = x * dt_all[r0:r0 + c, h:h + 1]
            acol = acs_all[:, h:h + 1]
            arow = _dot_hi(ones, ad_c[:, h:h + 1] * triu_f)
            lmat = jnp.where(tril, jnp.exp(jnp.where(tril, acol - arow, 0.0)), 0.0)
            y_diag = _dot(_dot_nt(cm, bm) * lmat, xdt)
            a_last = acol[c - 1:c, :]
            states = _dot_tn(xdt * jnp.exp(a_last - acol), bm)
            hin = h_ref[h]
            y_off = _dot_nt(cm, hin) * jnp.exp(acol)
            h_ref[h] = hin * jnp.exp(a_last) + states
            yh = y_diag + y_off + par_ref[2:3, h:h + 1] * x
            ys.append(yh * _silu(z_ref[0, r0:r0 + c, h * HEAD_DIM:(h + 1) * HEAD_DIM]))
        for grp in range(SSM_NGROUPS):
            grp_ys = ys[grp * hpg:(grp + 1) * hpg]
            ms = sum(jnp.sum(a * a, axis=-1, keepdims=True) for a in grp_ys) / (hpg * HEAD_DIM)
            r = lax.rsqrt(ms + RMS_EPS)
            for j, a in enumerate(grp_ys):
                h = grp * hpg + j
                o_ref[0, r0:r0 + c, h * HEAD_DIM:(h + 1) * HEAD_DIM] = a * r * normw_ref[:, h * HEAD_DIM:(h + 1) * HEAD_DIM]

    @pl.when(step == pl.num_programs(1) - 1)
    def _():
        h_out_ref[0] = h_ref[...]


def _ssd_prompt(xbc, z, small, conv_w, conv_b, a_log, dt_bias, d_skip, norm_w, tl=256):
    b, l, _ = xbc.shape
    assert l % tl == 0 and tl % CHUNK == 0
    par = jnp.stack([-jnp.exp(a_log.astype(F32)), dt_bias.astype(F32), d_skip.astype(F32)])
    return pl.pallas_call(
        functools.partial(_ssd_kernel, tl=tl),
        grid=(b, l // tl),
        in_specs=[pl.BlockSpec((1, tl, SSM_CONV_DIM), lambda bi, i: (bi, i, 0)),
                  pl.BlockSpec((1, tl, GROUP_WIDTH), lambda bi, i: (bi, i, 0)),
                  pl.BlockSpec((1, tl, LANE), lambda bi, i: (bi, i, 0)),
                  pl.BlockSpec((CONV_W, SSM_CONV_DIM), lambda bi, i: (0, 0)),
                  pl.BlockSpec((1, SSM_CONV_DIM), lambda bi, i: (0, 0)),
                  pl.BlockSpec((3, N_HEADS), lambda bi, i: (0, 0)),
                  pl.BlockSpec((1, GROUP_WIDTH), lambda bi, i: (0, 0))],
        out_specs=[pl.BlockSpec((1, tl, GROUP_WIDTH), lambda bi, i: (bi, i, 0)),
                   pl.BlockSpec((1, CONV_W - 1, SSM_CONV_DIM), lambda bi, i: (bi, 0, 0)),
                   pl.BlockSpec((1, N_HEADS, HEAD_DIM, SSM_STATE), lambda bi, i: (bi, 0, 0, 0))],
        out_shape=[jax.ShapeDtypeStruct((b, l, GROUP_WIDTH), F32),
                   jax.ShapeDtypeStruct((b, CONV_W - 1, SSM_CONV_DIM), F32),
                   jax.ShapeDtypeStruct((b, N_HEADS, HEAD_DIM, SSM_STATE), F32)],
        scratch_shapes=[pltpu.VMEM((tl + 8, SSM_CONV_DIM), F32),
                        pltpu.VMEM((N_HEADS, HEAD_DIM, SSM_STATE), F32)],
        compiler_params=pltpu.CompilerParams(dimension_semantics=("arbitrary", "arbitrary"),
                                             vmem_limit_bytes=VMEM_LIMIT),
        name="ssd_prompt",
    )(xbc, z, small, conv_w, conv_b.reshape(1, SSM_CONV_DIM), par, norm_w.reshape(1, GROUP_WIDTH))


def _post_mix_kernel(o0_ref, o1_ref, o2_ref, o3_ref, x_ref, wout_ref, g_ref, b_ref, rw_ref, rb_ref,
                     xn_ref, xnb_ref, idx_ref, gate_ref, rank_ref, cum_ref, cnt_ref, *, tm):
    @pl.when(pl.program_id(0) == 0)
    def _():
        cnt_ref[...] = jnp.zeros_like(cnt_ref)

    mix = jnp.zeros((tm, D_MODEL), F32)
    for gi, o_ref in enumerate((o0_ref, o1_ref, o2_ref, o3_ref)):
        mix = mix + jnp.dot(o_ref[...].astype(BF16), wout_ref[gi * GROUP_WIDTH:(gi + 1) * GROUP_WIDTH, :],
                            preferred_element_type=F32)
    xn = _ln(DN_ALPHA * x_ref[...] + mix, g_ref[...], b_ref[...])
    xn_ref[...] = xn
    xnb_ref[...] = xn.astype(BF16)

    logits = _dot_hi(xn, rw_ref[...]) + rb_ref[...]
    lane_e = lax.broadcasted_iota(I32, (tm, N_EXPERTS), 1).astype(F32)
    lane = lax.broadcasted_iota(I32, (tm, LANE), 1)
    work = logits
    sel = jnp.zeros((tm, N_EXPERTS), F32)
    idx_out = jnp.zeros((tm, LANE), I32)
    val_out = jnp.zeros((tm, LANE), F32)
    hits = []
    for r in range(TOP_K):
        m = jnp.max(work, axis=-1, keepdims=True)
        pick = jnp.min(jnp.where(work == m, lane_e, float(N_EXPERTS)), axis=-1, keepdims=True)
        hit = lane_e == pick
        work = jnp.where(hit, -jnp.inf, work)
        sel = sel + hit.astype(F32)
        hits.append(hit)
        idx_out = jnp.where(lane == r, pick.astype(I32), idx_out)
        val_out = jnp.where(lane == r, m, val_out)
    ex = jnp.where(lane < TOP_K, jnp.exp(val_out - val_out[:, 0:1]), 0.0)
    gate_ref[...] = ex / jnp.sum(ex, axis=-1, keepdims=True)
    idx_ref[...] = idx_out

    row = lax.broadcasted_iota(I32, (tm, tm), 0)
    col = lax.broadcasted_iota(I32, (tm, tm), 1)
    before = jnp.dot((row > col).astype(BF16), sel.astype(BF16), preferred_element_type=F32) + cnt_ref[...]
    rank_out = jnp.zeros((tm, LANE), I32)
    for r in range(TOP_K):
        rk = jnp.sum(jnp.where(hits[r], before, 0.0), axis=-1, keepdims=True)
        rank_out = jnp.where(lane == r, rk.astype(I32), rank_out)
    rank_ref[...] = rank_out
    cum_ref[0] = jnp.broadcast_to(cnt_ref[...], (8, N_EXPERTS))
    cnt_ref[...] = cnt_ref[...] + jnp.sum(sel, axis=0, keepdims=True)


def _post_mix(outs, x, w_out_bf16, ln_g, ln_b, router_w, router_b, tm):
    n = x.shape[0]
    nt = n // tm
    tok = lambda w: pl.BlockSpec((tm, w), lambda i: (i, 0))
    full = lambda a: pl.BlockSpec(a.shape, lambda i: (0,) * a.ndim)
    g2, b2, rb2 = ln_g.reshape(1, -1), ln_b.reshape(1, -1), router_b.reshape(1, -1)
    return pl.pallas_call(
        functools.partial(_post_mix_kernel, tm=tm),
        grid=(nt,),
        in_specs=[tok(GROUP_WIDTH)] * 4 + [tok(D_MODEL), full(w_out_bf16), full(g2), full(b2), full(router_w), full(rb2)],
        out_specs=[tok(D_MODEL), tok(D_MODEL), tok(LANE), tok(LANE), tok(LANE),
                   pl.BlockSpec((1, 8, N_EXPERTS), lambda i: (i, 0, 0))],
        out_shape=[jax.ShapeDtypeStruct((n, D_MODEL), F32), jax.ShapeDtypeStruct((n, D_MODEL), BF16),
                   jax.ShapeDtypeStruct((n, LANE), I32), jax.ShapeDtypeStruct((n, LANE), F32),
                   jax.ShapeDtypeStruct((n, LANE), I32), jax.ShapeDtypeStruct((nt, 8, N_EXPERTS), F32)],
        scratch_shapes=[pltpu.VMEM((1, N_EXPERTS), F32)],
        compiler_params=pltpu.CompilerParams(dimension_semantics=("arbitrary",), vmem_limit_bytes=VMEM_LIMIT),
        name="post_mix",
    )(*outs, x, w_out_bf16, g2, b2, router_w, rb2)


def _moe_dispatch(idx, gates, rank, cum, n_tok, tm_cum, tm, blk):
    nt = n_tok // tm
    n_blocks = -(-n_tok * TOP_K // blk) + N_EXPERTS
    per_pair = -(-tm // blk) + 1
    e_idx = idx[:, :TOP_K]
    cum_i = cum[:, 0, :].astype(I32)
    onehot = e_idx[:, :, None] == jnp.arange(N_EXPERTS, dtype=I32)
    counts = cum_i[-1] + jnp.sum(onehot[n_tok - tm_cum:].astype(I32), axis=(0, 1))
    cum_t = cum_i[::tm // tm_cum]
    cum_all = jnp.concatenate([cum_t, counts[None]], 0)
    blocks_per_e = (counts + blk - 1) // blk
    blk_start = jnp.cumsum(blocks_per_e) - blocks_per_e
    dest = jnp.sum(jnp.where(onehot, blk_start, 0), axis=-1) * blk + rank[:, :TOP_K]
    dest_f = dest.astype(F32)
    dest_pad = jnp.pad(dest_f, ((0, 0), (0, LANE - TOP_K)), constant_values=-1.0)
    dest_t = jnp.pad(dest_f.T, ((0, 8 - TOP_K), (0, 0)), constant_values=-1.0)
    gate_t = jnp.pad(gates[:, :TOP_K].T, ((0, 8 - TOP_K), (0, 0)))
    eid_t = jnp.pad(e_idx.T, ((0, 8 - TOP_K), (0, 0)), constant_values=-1)

    lo, hi = cum_all[:-1].T, cum_all[1:].T
    has = hi > lo
    j0 = lo // blk
    j1 = jnp.maximum(hi - 1, 0) // blk
    jj = j0[:, :, None] + jnp.arange(per_pair, dtype=I32)
    valid = has[:, :, None] & (jj <= j1[:, :, None])
    blk_id = blk_start[:, None, None] + jj
    e_b = jnp.broadcast_to(jnp.arange(N_EXPERTS, dtype=I32)[:, None, None], jj.shape)
    t_b = jnp.broadcast_to(jnp.arange(nt, dtype=I32)[None, :, None], jj.shape)
    n_steps = min(N_EXPERTS * nt + n_blocks, per_pair * N_EXPERTS * nt)
    n_valid = jnp.sum(valid)
    pos = jnp.arange(n_steps)
    v = pos < n_valid

    win = min(MOE_ROW_WINDOW, blk)
    r_lo = jnp.maximum(lo[:, :, None], jj * blk) - jj * blk
    r_hi = jnp.minimum(hi[:, :, None], (jj + 1) * blk) - jj * blk
    w0 = (r_lo // 16) * 16
    n_win = jnp.maximum((r_hi - w0 + win - 1) // win, 1)
    win_bits = (n_win << 4) + (w0 << 8)

    def build(key):
        order = jnp.argsort(jnp.where(valid, key, jnp.int32(1 << 30)).reshape(-1))[:n_steps]
        src = order[jnp.minimum(pos, n_valid - 1)]
        return blk_id.reshape(-1)[src], t_b.reshape(-1)[src], e_b.reshape(-1)[src], win_bits.reshape(-1)[src]

    def edges(a):
        change = a[1:] != a[:-1]
        first = jnp.concatenate([jnp.ones((1,), bool), change]) & v
        last = (jnp.concatenate([change, jnp.ones((1,), bool)]) | (pos == n_valid - 1)) & v
        return first, last

    b1, t1, e1, wb1 = build(blk_id * nt + t_b)
    first_b, last_b = edges(b1)
    first_e, _ = edges(e1)
    flags1 = first_b.astype(I32) + 2 * last_b.astype(I32) + 4 * v.astype(I32) + 8 * first_e.astype(I32) + wb1
    b2, t2, e2, wb2 = build(t_b * (2 * n_blocks) + blk_id)
    first_t, last_t = edges(t2)
    flags2 = first_t.astype(I32) + 2 * last_t.astype(I32) + 4 * v.astype(I32) + wb2
    eid_pad = jnp.pad(e_idx.astype(F32), ((0, 0), (0, LANE - TOP_K)), constant_values=-1.0)
    return dict(tok=jnp.concatenate([eid_pad, dest_pad], axis=1),
                tok_t=jnp.concatenate([eid_t.astype(F32), dest_t, gate_t], axis=0),
                n_rows=n_blocks * blk, n_steps=n_steps,
                gather=(b1.astype(I32), t1.astype(I32), e1.astype(I32), flags1),
                combine=(b2.astype(I32), t2.astype(I32), e2.astype(I32), flags2))


def _expert_kernel(blk_s, tile_s, exp_s, flag_s, x_ref, tok_ref, wgu_ref, bgu_ref, wdn_ref, bdn_ref,
                   yb_ref, xacc_ref, gacc_ref, wgu_bf, wdn_bf, *, blk, tm):
    s = pl.program_id(0)
    flags = flag_s[s]

    @pl.when((flags & 8) != 0)
    def _():
        wgu_bf[...] = wgu_ref[0, 0].astype(BF16)
        wdn_bf[...] = wdn_ref[0, 0].astype(BF16)

    @pl.when((flags & 1) != 0)
    def _():
        xacc_ref[...] = jnp.zeros_like(xacc_ref)
        gacc_ref[...] = jnp.zeros_like(gacc_ref)

    @pl.when((flags & 4) != 0)
    def _():
        mine = tok_ref[0:8, :] == exp_s[s].astype(F32)
        d_sel = jnp.sum(jnp.where(mine, tok_ref[8:16, :] + 1.0, 0.0), axis=0, keepdims=True) - 1.0
        g_sel = jnp.sum(jnp.where(mine, tok_ref[16:24, :], 0.0), axis=0, keepdims=True)
        win = min(MOE_ROW_WINDOW, blk)
        w0 = flags >> 8
        base = blk_s[s] * blk

        def window(i, carry):
            start = w0 + i * win
            r0 = pl.multiple_of(jnp.minimum(start, blk - win), 16)
            local = r0 + lax.broadcasted_iota(I32, (win, tm), 0)
            hit = (d_sel == (base + local).astype(F32)) & (local >= start)
            p = jnp.where(hit, 1.0, 0.0).astype(BF16)
            xacc_ref[pl.ds(r0, win), :] += jnp.dot(p, x_ref[...], preferred_element_type=F32)
            gacc_ref[pl.ds(r0, win), :] += jnp.sum(jnp.where(hit, g_sel, 0.0), axis=-1, keepdims=True)
            return carry

        lax.fori_loop(0, (flags >> 4) & 15, window, 0)

    @pl.when((flags & 2) != 0)
    def _():
        h = jnp.dot(xacc_ref[...].astype(BF16), wgu_bf[...], preferred_element_type=F32) + bgu_ref[0, 0]
        gate = jnp.minimum(h[:, :D_FF], SWIGLU_LIMIT)
        up = jnp.clip(h[:, D_FF:], -SWIGLU_LIMIT, SWIGLU_LIMIT)
        act = (up + 1.0) * gate * _sigmoid(SWIGLU_ALPHA * gate)
        y = jnp.dot(act.astype(BF16), wdn_bf[...], preferred_element_type=F32) + bdn_ref[0, 0]
        yb_ref[...] = (y * gacc_ref[...]).astype(yb_ref.dtype)


def _expert_ffn(layer, xn_bf16, disp, w_gu, b_gu, w_down, b_down, tm, blk):
    blk_s, tile_s, exp_s, flag_s = disp["gather"]
    depth = w_gu.shape[0]
    tok_t = pl.BlockSpec((24, tm), lambda s, b, t, e, f: (0, t[s]))
    grid_spec = pltpu.PrefetchScalarGridSpec(
        num_scalar_prefetch=4,
        grid=(disp["n_steps"],),
        in_specs=[pl.BlockSpec((tm, D_MODEL), lambda s, b, t, e, f: (t[s], 0)), tok_t,
                  pl.BlockSpec((1, 1, D_MODEL, 2 * D_FF), lambda s, b, t, e, f: (layer, e[s], 0, 0)),
                  pl.BlockSpec((1, 1, 1, 2 * D_FF), lambda s, b, t, e, f: (layer, e[s], 0, 0)),
                  pl.BlockSpec((1, 1, D_FF, D_MODEL), lambda s, b, t, e, f: (layer, e[s], 0, 0)),
                  pl.BlockSpec((1, 1, 1, D_MODEL), lambda s, b, t, e, f: (layer, e[s], 0, 0))],
        out_specs=pl.BlockSpec((blk, D_MODEL), lambda s, b, t, e, f: (b[s], 0)),
        scratch_shapes=[pltpu.VMEM((blk, D_MODEL), F32), pltpu.VMEM((blk, 1), F32),
                        pltpu.VMEM((D_MODEL, 2 * D_FF), BF16), pltpu.VMEM((D_FF, D_MODEL), BF16)])
    return pl.pallas_call(
        functools.partial(_expert_kernel, blk=blk, tm=tm),
        grid_spec=grid_spec,
        out_shape=jax.ShapeDtypeStruct((disp["n_rows"], D_MODEL), BF16),
        compiler_params=pltpu.CompilerParams(dimension_semantics=("arbitrary",), vmem_limit_bytes=VMEM_LIMIT),
        name="expert_ffn",
    )(blk_s, tile_s, exp_s, flag_s, xn_bf16, disp["tok_t"],
      w_gu, b_gu.reshape(depth, N_EXPERTS, 1, 2 * D_FF), w_down, b_down.reshape(depth, N_EXPERTS, 1, D_MODEL))


def _combine_kernel(blk_s, tile_s, exp_s, flag_s, yb_ref, tok_ref, xn_ref, g_ref, b_ref, y_ref, acc_ref, *, blk, tm):
    s = pl.program_id(0)
    flags = flag_s[s]

    @pl.when((flags & 1) != 0)
    def _():
        acc_ref[...] = jnp.zeros_like(acc_ref)

    @pl.when((flags & 4) != 0)
    def _():
        mine = tok_ref[:, 0:LANE] == exp_s[s].astype(F32)
        d_sel = jnp.sum(jnp.where(mine, tok_ref[:, LANE:2 * LANE] + 1.0, 0.0), axis=-1, keepdims=True) - 1.0
        rowid = (blk_s[s] * blk + lax.broadcasted_iota(I32, (tm, blk), 1)).astype(F32)
        p = jnp.where(d_sel == rowid, 1.0, 0.0).astype(BF16)
        acc_ref[...] += jnp.dot(p, yb_ref[...], preferred_element_type=F32)

    @pl.when((flags & 2) != 0)
    def _():
        y_ref[...] = _ln(DN_ALPHA * xn_ref[...] + acc_ref[...], g_ref[...], b_ref[...])


def _moe_combine(yb, disp, xn, ln_g, ln_b, tm, blk):
    blk_s, tile_s, exp_s, flag_s = disp["combine"]
    n = xn.shape[0]
    tok = pl.BlockSpec((tm, 2 * LANE), lambda s, b, t, e, f: (t[s], 0))
    grid_spec = pltpu.PrefetchScalarGridSpec(
        num_scalar_prefetch=4,
        grid=(disp["n_steps"],),
        in_specs=[pl.BlockSpec((blk, D_MODEL), lambda s, b, t, e, f: (b[s], 0)), tok,
                  pl.BlockSpec((tm, D_MODEL), lambda s, b, t, e, f: (t[s], 0)),
                  pl.BlockSpec((1, D_MODEL), lambda s, b, t, e, f: (0, 0)),
                  pl.BlockSpec((1, D_MODEL), lambda s, b, t, e, f: (0, 0))],
        out_specs=pl.BlockSpec((tm, D_MODEL), lambda s, b, t, e, f: (t[s], 0)),
        scratch_shapes=[pltpu.VMEM((tm, D_MODEL), F32)])
    return pl.pallas_call(
        functools.partial(_combine_kernel, blk=blk, tm=tm),
        grid_spec=grid_spec,
        out_shape=jax.ShapeDtypeStruct((n, D_MODEL), F32),
        compiler_params=pltpu.CompilerParams(dimension_semantics=("arbitrary",), vmem_limit_bytes=VMEM_LIMIT),
        name="moe_combine",
    )(blk_s, tile_s, exp_s, flag_s, yb, disp["tok"], xn, ln_g.reshape(1, -1), ln_b.reshape(1, -1))


def _post_mix_moe(layer, outs, x, w_out, fw):
    ln1_g, ln1_b, router_w, router_b, w_gu, b_gu, w_down, b_down, ln2_g, ln2_b = fw
    n = x.shape[0]
    tm_cum = min(256, n)
    tm = min(512, n)
    blk = min(EXPERT_BLOCK, n * TOP_K)
    xn, xnb, idx, gates, rank, cum = _post_mix(outs, x, w_out.astype(BF16), ln1_g, ln1_b, router_w, router_b, tm_cum)
    disp = _moe_dispatch(idx, gates, rank, cum, n, tm_cum, tm, blk)
    yb = _expert_ffn(layer, xnb, disp, w_gu, b_gu, w_down, b_down, tm, blk)
    return _moe_combine(yb, disp, xn, ln2_g, ln2_b, tm, blk)


def _rec_step_kernel(gqkv_ref, gz_ref, xbc_ref, sz_ref, small_ref, gconv0_ref, sconv0_ref, s0_ref, h0_ref,
                     gcw_ref, scw_ref, scb_ref, gpar_ref, spar_ref, gnw_ref, snw_ref,
                     gout_ref, sout_ref, gconv_ref, sconv_ref, s_ref, h_ref):
    small = small_ref[0]

    def conv(u, buf_ref, w_ref, out_ref):
        buf = buf_ref[0]
        y = u * w_ref[CONV_W - 1:CONV_W, :]
        for i in range(CONV_W - 1):
            y = y + buf[i:i + 1, :] * w_ref[i:i + 1, :]
        out_ref[0, 0:CONV_W - 2, :] = buf[1:CONV_W - 1, :]
        out_ref[0, CONV_W - 2:CONV_W - 1, :] = u
        return y

    eye = (lax.broadcasted_iota(I32, (HEAD_DIM, HEAD_DIM), 0)
           == lax.broadcasted_iota(I32, (HEAD_DIM, HEAD_DIM), 1)).astype(F32)

    def to_col(r):
        return jnp.sum(eye * r, axis=-1, keepdims=True)

    y = _silu(conv(gqkv_ref[0], gconv0_ref, gcw_ref, gconv_ref))
    beta_all = _sigmoid(small[:, 0:N_HEADS])
    g_all = gpar_ref[0:1, :] * _softplus(small[:, N_HEADS:2 * N_HEADS] + gpar_ref[1:2, :])
    for h in range(N_HEADS):
        hs = slice(h * HEAD_DIM, (h + 1) * HEAD_DIM)
        q = y[:, h * HEAD_DIM:(h + 1) * HEAD_DIM]
        k = y[:, GROUP_WIDTH + h * HEAD_DIM:GROUP_WIDTH + (h + 1) * HEAD_DIM]
        v = y[:, 2 * GROUP_WIDTH + h * HEAD_DIM:2 * GROUP_WIDTH + (h + 1) * HEAD_DIM]
        q = q * lax.rsqrt(jnp.sum(q * q, axis=-1, keepdims=True) + RMS_EPS) * (HEAD_DIM ** -0.5)
        k = k * lax.rsqrt(jnp.sum(k * k, axis=-1, keepdims=True) + RMS_EPS)
        beta = beta_all[:, h:h + 1]
        eg = jnp.exp(g_all[:, h:h + 1])
        s0 = s0_ref[0, h]
        kc, qc = to_col(k), to_col(q)
        v_new = v * beta - jnp.sum(kc * (beta * eg) * s0, axis=0, keepdims=True)
        o = jnp.sum(qc * eg * s0, axis=0, keepdims=True) + jnp.sum(q * k, axis=-1, keepdims=True) * v_new
        s_ref[0, h] = s0 * eg + kc * v_new
        o = o * lax.rsqrt(jnp.mean(o * o, axis=-1, keepdims=True) + RMS_EPS) * gnw_ref[...] * _silu(gz_ref[0][:, hs])
        gout_ref[0, :, hs] = o

    xbc = _silu(conv(xbc_ref[0], sconv0_ref, scw_ref, sconv_ref) + scb_ref[...])
    dt_all = _softplus(small[:, 2 * N_HEADS:3 * N_HEADS] + spar_ref[1:2, :])
    hpg = N_HEADS // SSM_NGROUPS
    boff, coff = GROUP_WIDTH, GROUP_WIDTH + SSM_NGROUPS * SSM_STATE
    ys = []
    for h in range(N_HEADS):
        grp = h // hpg
        x = xbc[:, h * HEAD_DIM:(h + 1) * HEAD_DIM]
        bm = xbc[:, boff + grp * SSM_STATE:boff + (grp + 1) * SSM_STATE]
        cm = xbc[:, coff + grp * SSM_STATE:coff + (grp + 1) * SSM_STATE]
        dt = dt_all[:, h:h + 1]
        hn = h0_ref[0, h] * jnp.exp(dt * spar_ref[0:1, h:h + 1]) + to_col(x * dt) * bm
        h_ref[0, h] = hn
        yc = jnp.sum(hn * cm, axis=-1, keepdims=True)
        yh = jnp.sum(eye * yc, axis=0, keepdims=True) + spar_ref[2:3, h:h + 1] * x
        ys.append(yh * _silu(sz_ref[0][:, h * HEAD_DIM:(h + 1) * HEAD_DIM]))
    for grp in range(SSM_NGROUPS):
        grp_ys = ys[grp * hpg:(grp + 1) * hpg]
        ms = sum(jnp.sum(a * a, axis=-1, keepdims=True) for a in grp_ys) / (hpg * HEAD_DIM)
        r = lax.rsqrt(ms + RMS_EPS)
        for j, a in enumerate(grp_ys):
            h = grp * hpg + j
            sout_ref[0, :, h * HEAD_DIM:(h + 1) * HEAD_DIM] = a * r * snw_ref[:, h * HEAD_DIM:(h + 1) * HEAD_DIM]


def _rec_step(seg, gconv0, s0, sconv0, h0, lw):
    (_, gdn_conv_w, gdn_a_log, gdn_dt_bias, gdn_norm_w, ssm_conv_w, ssm_conv_b,
     ssm_a_log, ssm_dt_bias, ssm_d, ssm_norm_w, _) = lw
    b = seg["small"].shape[0]
    gpar = jnp.stack([-jnp.exp(gdn_a_log.astype(F32)), gdn_dt_bias.astype(F32)])
    spar = jnp.stack([-jnp.exp(ssm_a_log.astype(F32)), ssm_dt_bias.astype(F32), ssm_d.astype(F32)])
    row = lambda w: pl.BlockSpec((1, 1, w), lambda i: (i, 0, 0))
    full = lambda a: pl.BlockSpec(a.shape, lambda i: (0,) * a.ndim)
    per_seq = lambda *dims: pl.BlockSpec((1,) + dims, lambda i: (i,) + (0,) * len(dims))
    r3 = lambda a: a.reshape(b, 1, -1)
    consts = [gdn_conv_w, ssm_conv_w, ssm_conv_b.reshape(1, -1), gpar, spar,
              gdn_norm_w.reshape(1, HEAD_DIM), ssm_norm_w.reshape(1, GROUP_WIDTH)]
    state_specs = [per_seq(CONV_W - 1, GDN_CONV_DIM), per_seq(CONV_W - 1, SSM_CONV_DIM),
                   per_seq(N_HEADS, HEAD_DIM, HEAD_DIM), per_seq(N_HEADS, HEAD_DIM, SSM_STATE)]
    go, so, gc, sc, s, h = pl.pallas_call(
        _rec_step_kernel,
        grid=(b,),
        in_specs=[row(GDN_CONV_DIM), row(GROUP_WIDTH), row(SSM_CONV_DIM), row(GROUP_WIDTH), row(LANE)]
                 + state_specs + [full(c) for c in consts],
        out_specs=[row(GROUP_WIDTH), row(GROUP_WIDTH)] + state_specs,
        out_shape=[jax.ShapeDtypeStruct((b, 1, GROUP_WIDTH), F32), jax.ShapeDtypeStruct((b, 1, GROUP_WIDTH), F32),
                   jax.ShapeDtypeStruct((b, CONV_W - 1, GDN_CONV_DIM), F32),
                   jax.ShapeDtypeStruct((b, CONV_W - 1, SSM_CONV_DIM), F32),
                   jax.ShapeDtypeStruct((b, N_HEADS, HEAD_DIM, HEAD_DIM), F32),
                   jax.ShapeDtypeStruct((b, N_HEADS, HEAD_DIM, SSM_STATE), F32)],
        compiler_params=pltpu.CompilerParams(dimension_semantics=("arbitrary",), vmem_limit_bytes=VMEM_LIMIT),
        name="rec_step",
    )(r3(seg["gdn_qkv"]), r3(seg["gdn_z"]), r3(seg["ssm_xbc"]), r3(seg["ssm_z"]), r3(seg["small"]),
      gconv0, sconv0, s0, h0, *consts)
    return go.reshape(b, GROUP_WIDTH), so.reshape(b, GROUP_WIDTH), gc, sc, s, h


PAGES_PER_STEP = 8


def _decode_pass_kernel(pt_ref, sq_ref, mq_ref, *refs, n_steps):
    pp = PAGES_PER_STEP
    sk_refs, sv_refs, mk_refs = refs[0:pp], refs[pp:2 * pp], refs[2 * pp:3 * pp]
    sb_out_ref, gp_ref, accv_ref, carry_ref = refs[3 * pp:]
    p = pl.program_id(1)

    @pl.when(p == 0)
    def _():
        accv_ref[...] = jnp.zeros_like(accv_ref)
        carry_ref[...] = jnp.zeros_like(carry_ref)

    qs = sq_ref[0] * (HEAD_DIM ** -0.5)
    qm = mq_ref[0] * (1.0 / MOBA_BLOCK)
    row = lax.broadcasted_iota(I32, (PAGE_SIZE, PAGE_SIZE), 0)
    col = lax.broadcasted_iota(I32, (PAGE_SIZE, PAGE_SIZE), 1)
    upper = (row > col).astype(BF16)
    sub8 = lax.broadcasted_iota(I32, (8, LANE), 0)
    diag8 = sub8 == lax.broadcasted_iota(I32, (8, LANE), 1)
    heads = [slice(h * HEAD_DIM, (h + 1) * HEAD_DIM) for h in range(N_HEADS)]

    def head_rows(x):
        out = jnp.zeros((8, LANE), F32)
        for h, hs in enumerate(heads):
            out = jnp.where(sub8 == h, jnp.sum(x[hs, :], axis=0, keepdims=True), out)
        return out

    zs = [head_rows(sk_refs[i][0, 0] * qs) for i in range(pp)]
    ls = [-_softplus(z) for z in zs]
    las = _split_dot(jnp.concatenate(ls, axis=0), upper)
    accv = [accv_ref[hs, :] for hs in heads]
    carry = carry_ref[...]
    for i in range(pp):
        la = las[i * 8:(i + 1) * 8, :]
        w = jnp.exp(zs[i] + ls[i] + la + carry)
        vt = sv_refs[i][0, 0]
        accv = [a + vt[hs, :] * w[h:h + 1, :] for h, (hs, a) in enumerate(zip(heads, accv))]
        carry = carry + la[:, 0:1] + ls[i][:, 0:1]
        gp = jnp.sum(head_rows(mk_refs[i][0, 0] * qm), axis=-1, keepdims=True)
        gp_ref[0, pl.ds(p * pp + i, 1), :] = jnp.sum(jnp.where(diag8, gp, 0.0), axis=0, keepdims=True)
    for hs, a in zip(heads, accv):
        accv_ref[hs, :] = a
    carry_ref[...] = carry

    @pl.when(p == n_steps - 1)
    def _():
        sb_out_ref[0] = jnp.sum(accv_ref[...], axis=-1, keepdims=True)


def _decode_pass(layer, page_table, sq, mq, cache_sb_k, cache_sb_v, cache_moba_k):
    b, n_pages = page_table.shape
    pp = PAGES_PER_STEP
    assert n_pages % pp == 0
    n_steps = n_pages // pp

    def page_spec(i):
        return pl.BlockSpec((1, 1, GROUP_WIDTH, PAGE_SIZE),
                            lambda bi, p, pt: (layer, pt[bi, n_pages - 1 - (p * pp + i)], 0, 0))

    colspec = pl.BlockSpec((1, GROUP_WIDTH, 1), lambda bi, p, pt: (bi, 0, 0))
    grid_spec = pltpu.PrefetchScalarGridSpec(
        num_scalar_prefetch=1,
        grid=(b, n_steps),
        in_specs=[colspec, colspec] + [page_spec(i) for i in range(pp)] * 3,
        out_specs=[colspec, pl.BlockSpec((1, n_pages, LANE), lambda bi, p, pt: (bi, 0, 0))],
        scratch_shapes=[pltpu.VMEM((GROUP_WIDTH, PAGE_SIZE), F32), pltpu.VMEM((8, 1), F32)])
    sb_out, gp = pl.pallas_call(
        functools.partial(_decode_pass_kernel, n_steps=n_steps),
        grid_spec=grid_spec,
        out_shape=[jax.ShapeDtypeStruct((b, GROUP_WIDTH, 1), F32), jax.ShapeDtypeStruct((b, n_pages, LANE), F32)],
        compiler_params=pltpu.CompilerParams(dimension_semantics=("arbitrary", "arbitrary"), vmem_limit_bytes=VMEM_LIMIT),
        name="decode_pass",
    )(page_table, sq.reshape(b, GROUP_WIDTH, 1), mq.reshape(b, GROUP_WIDTH, 1),
      *([cache_sb_k] * pp), *([cache_sb_v] * pp), *([cache_moba_k] * pp))
    return sb_out.reshape(b, GROUP_WIDTH), gp


def _moba_decode_kernel(pg_ref, mq_ref, mk_ref, mv_ref, *refs, n_sel):
    kp_refs, vp_refs, o_ref = refs[:n_sel], refs[n_sel:2 * n_sel], refs[2 * n_sel]
    h = pl.program_id(1)
    lane = lax.broadcasted_iota(I32, (1, GROUP_WIDTH), 1)
    in_head = lane // HEAD_DIM == h
    q = jnp.where(in_head, mq_ref[0] * (HEAD_DIM ** -0.5), 0.0)

    @pl.when(h == 0)
    def _():
        o_ref[...] = jnp.zeros_like(o_ref)

    q8 = jnp.where(lax.broadcasted_iota(I32, (8, GROUP_WIDTH), 0) == 0, q, 0.0).astype(BF16)
    s_self = jnp.sum(q * mk_ref[0], axis=-1, keepdims=True)
    scores = [jnp.dot(q8, kp[0, 0].astype(BF16), preferred_element_type=F32)[0:1, :] for kp in kp_refs]
    m = s_self
    for s in scores:
        m = jnp.maximum(m, jnp.max(s, axis=-1, keepdims=True))
    p_self = jnp.exp(s_self - m)
    l = p_self
    acc = p_self * mv_ref[0]
    first_row = lax.broadcasted_iota(I32, (8, PAGE_SIZE), 0) == 0
    for s, vp in zip(scores, vp_refs):
        p = jnp.exp(s - m)
        l = l + jnp.sum(p, axis=-1, keepdims=True)
        p8 = jnp.where(first_row, p, 0.0).astype(BF16)
        acc = acc + lax.dot_general(p8, vp[0, 0].astype(BF16), (((1,), (1,)), ((), ())),
                                    preferred_element_type=F32)[0:1, :]
    o_ref[0] = jnp.where(in_head, acc / l, o_ref[0])


def _moba_decode(layer, sel_pages, mq, mk_new, mv_new, cache_moba_k, cache_moba_v):
    b, nh, n_sel = sel_pages.shape
    row = pl.BlockSpec((1, 1, GROUP_WIDTH), lambda bi, h, pg: (bi, 0, 0))

    def page(j):
        return pl.BlockSpec((1, 1, GROUP_WIDTH, PAGE_SIZE),
                            lambda bi, h, pg: (layer, pg[(bi * nh + h) * n_sel + j], 0, 0))

    grid_spec = pltpu.PrefetchScalarGridSpec(
        num_scalar_prefetch=1,
        grid=(b, nh),
        in_specs=[row, row, row] + [page(j) for j in range(n_sel)] * 2,
        out_specs=pl.BlockSpec((1, 1, GROUP_WIDTH), lambda bi, h, pg: (bi, 0, 0)))
    r3 = lambda a: a.reshape(b, 1, GROUP_WIDTH)
    out = pl.pallas_call(
        functools.partial(_moba_decode_kernel, n_sel=n_sel),
        grid_spec=grid_spec,
        out_shape=jax.ShapeDtypeStruct((b, 1, GROUP_WIDTH), F32),
        compiler_params=pltpu.CompilerParams(dimension_semantics=("arbitrary",) * 2, vmem_limit_bytes=VMEM_LIMIT),
        name="moba_decode",
    )(sel_pages.reshape(-1).astype(I32), r3(mq), r3(mk_new), r3(mv_new),
      *([cache_moba_k] * n_sel), *([cache_moba_v] * n_sel))
    return out.reshape(b, GROUP_WIDTH)


def _sample_attention(layer, page_table, seg, caches):
    cache_moba_k, cache_moba_v, cache_sb_k, cache_sb_v = caches
    b, n_pages = page_table.shape
    ppb = MOBA_BLOCK // PAGE_SIZE
    n_past_blocks = n_pages // ppb
    assert n_pages % ppb == 0 and n_past_blocks >= MOBA_TOPK
    sb_out, gp = _decode_pass(layer, page_table, seg["sb_q"], seg["moba_q"], cache_sb_k, cache_sb_v, cache_moba_k)
    gates = gp[:, ::-1, :N_HEADS].reshape(b, n_past_blocks, ppb, N_HEADS).sum(axis=2)
    _, sel = lax.top_k(jnp.swapaxes(gates, 1, 2), MOBA_TOPK)
    pages = sel[..., None] * ppb + jnp.arange(ppb)
    sel_pages = jnp.take_along_axis(page_table[:, None, :], pages.reshape(b, N_HEADS, -1), axis=2)
    moba_out = _moba_decode(layer, sel_pages, seg["moba_q"], seg["moba_k"], seg["moba_v"], cache_moba_k, cache_moba_v)
    return moba_out, sb_out


def _kv_rows(seg, bsz, length):
    return tuple(seg[name].reshape(bsz, length, N_HEADS, HEAD_DIM) for name in ("moba_k", "moba_v", "sb_k", "sb_v"))


def _prompt_mixers(x, lw):
    (w_in, gdn_conv_w, gdn_a_log, gdn_dt_bias, gdn_norm_w, ssm_conv_w, ssm_conv_b,
     ssm_a_log, ssm_dt_bias, ssm_d, ssm_norm_w, _) = lw
    bsz, length, _ = x.shape
    n = bsz * length
    flat = _in_proj(x.reshape(n, D_MODEL), w_in, jnp.arange(length), length)
    seg = {k: v.reshape(bsz, length, -1) for k, v in flat.items()}

    gdn_out, gdn_conv_new, gdn_s_new = _gdn_prompt(seg["gdn_qkv"], seg["gdn_z"], seg["small"],
                                                   gdn_conv_w, gdn_a_log, gdn_dt_bias, gdn_norm_w)
    ssm_out, ssm_conv_new, ssm_h_new = _ssd_prompt(seg["ssm_xbc"], seg["ssm_z"], seg["small"], ssm_conv_w, ssm_conv_b,
                                                   ssm_a_log, ssm_dt_bias, ssm_d, ssm_norm_w)
    moba_out = _moba_attention(seg["moba_q"], seg["moba_k"], seg["moba_v"])
    sb_out = _sb_attention(seg["sb_q"], seg["sb_k"], seg["sb_v"])

    outs = [o.reshape(n, GROUP_WIDTH) for o in (gdn_out, ssm_out, moba_out, sb_out)]
    return outs, _kv_rows(seg, bsz, length) + (gdn_conv_new, gdn_s_new, ssm_conv_new, ssm_h_new)


def _sample_mixers(x, layer, page_table, rec_state, caches, lw):
    gdn_conv0, gdn_s0, ssm_conv0, ssm_h0 = rec_state
    bsz, length, _ = x.shape
    assert length == 1
    pos0 = page_table.shape[1] * PAGE_SIZE
    seg = _in_proj(x.reshape(bsz, D_MODEL), lw[0], jnp.full((1,), pos0, I32), 1)
    gdn_out, ssm_out, gdn_conv_new, ssm_conv_new, gdn_s_new, ssm_h_new = _rec_step(
        seg, gdn_conv0, gdn_s0, ssm_conv0, ssm_h0, lw)
    moba_out, sb_out = _sample_attention(layer, page_table, seg, caches)
    outs = [gdn_out, ssm_out, moba_out, sb_out]
    return outs, _kv_rows(seg, bsz, length) + (gdn_conv_new, gdn_s_new, ssm_conv_new, ssm_h_new)


def kernel(x_prompt, x_sample, cache_moba_k, cache_moba_v, cache_sb_k, cache_sb_v,
           state_gdn_conv, state_gdn_rec, state_ssm_conv, state_ssm_rec, page_table,
           w_in, gdn_conv_w, gdn_a_log, gdn_dt_bias, gdn_norm_w,
           ssm_conv_w, ssm_conv_b, ssm_a_log, ssm_dt_bias, ssm_d, ssm_norm_w,
           w_out, ln1_g, ln1_b, router_w, router_b,
           expert_w_gu, expert_b_gu, expert_w_down, expert_b_down, ln2_g, ln2_b):
    caches = tuple(jnp.transpose(c, (0, 1, 3, 4, 2)).reshape(c.shape[0], c.shape[1], GROUP_WIDTH, PAGE_SIZE)
                   for c in (cache_moba_k, cache_moba_v, cache_sb_k, cache_sb_v))
    yp, ys = x_prompt, x_sample
    new_p, new_s = [], []
    for l in range(DEPTH):
        lw = (w_in[l], gdn_conv_w[l], gdn_a_log[l], gdn_dt_bias[l], gdn_norm_w[l], ssm_conv_w[l], ssm_conv_b[l],
              ssm_a_log[l], ssm_dt_bias[l], ssm_d[l], ssm_norm_w[l], w_out[l])
        fw = (ln1_g[l], ln1_b[l], router_w[l], router_b[l], expert_w_gu, expert_b_gu,
              expert_w_down, expert_b_down, ln2_g[l], ln2_b[l])
        rec_state = (state_gdn_conv[l], state_gdn_rec[l], state_ssm_conv[l], state_ssm_rec[l])
        outs_p, st_p = _prompt_mixers(yp, lw)
        yp = _post_mix_moe(l, outs_p, yp.reshape(-1, D_MODEL), w_out[l], fw).reshape(yp.shape)
        outs_s, st_s = _sample_mixers(ys, l, page_table, rec_state, caches, lw)
        ys = _post_mix_moe(l, outs_s, ys.reshape(-1, D_MODEL), w_out[l], fw).reshape(ys.shape)
        new_p.append(st_p)
        new_s.append(st_s)

    def stk(states, i):
        return jnp.stack([s[i] for s in states])

    return (yp, ys,
            stk(new_p, 0), stk(new_p, 1), stk(new_p, 2), stk(new_p, 3),
            stk(new_p, 4), stk(new_p, 5), stk(new_p, 6), stk(new_p, 7),
            stk(new_s, 0), stk(new_s, 1), stk(new_s, 2), stk(new_s, 3),
            stk(new_s, 4), stk(new_s, 5), stk(new_s, 6), stk(new_s, 7))
```

```python
import functools

import jax
import jax.numpy as jnp
from jax import lax
from jax.experimental import pallas as pl
from jax.experimental.pallas import tpu as pltpu

D_MODEL = 1024
DEPTH = 2
PAGE_SIZE = 128
HEAD_DIM = 64
N_MIXERS = 4
GROUP_WIDTH = D_MODEL // N_MIXERS
N_HEADS = GROUP_WIDTH // HEAD_DIM
CONV_W = 4
CHUNK = 64
GDN_CONV_DIM = 3 * GROUP_WIDTH
SSM_STATE = 128
SSM_NGROUPS = 2
SSM_CONV_DIM = GROUP_WIDTH + 2 * SSM_NGROUPS * SSM_STATE
MOBA_BLOCK = 256
MOBA_TOPK = 3
ROPE_THETA = 500000.0
ROPE_DIM = HEAD_DIM // 4
N_EXPERTS = 32
TOP_K = 4
D_FF = D_MODEL
SWIGLU_LIMIT = 7.0
SWIGLU_ALPHA = 1.702
EXPERT_BLOCK = 256
DN_ALPHA = (2 * DEPTH) ** 0.25
LN_EPS = 1e-5
RMS_EPS = 1e-6

F32 = jnp.float32
BF16 = jnp.bfloat16
I32 = jnp.int32
HI = lax.Precision.HIGHEST

LANE = 128
VMEM_LIMIT = 56 * 1024 * 1024
NEG = -1e30
MOE_ROW_WINDOW = 128
GDN_PRECISE_SQUARINGS = 2

_SEG_SIZES = (GDN_CONV_DIM, GROUP_WIDTH, N_HEADS, N_HEADS, SSM_CONV_DIM, GROUP_WIDTH, N_HEADS, GDN_CONV_DIM, GDN_CONV_DIM)
_SEG_NAMES = ("gdn_qkv", "gdn_z", "gdn_b", "gdn_a", "ssm_xbc", "ssm_z", "ssm_dt", "moba_qkv", "sb_qkv")
_OUT_SEGS = (("gdn_qkv", GDN_CONV_DIM), ("gdn_z", GROUP_WIDTH), ("ssm_xbc", SSM_CONV_DIM), ("ssm_z", GROUP_WIDTH),
             ("moba_q", GROUP_WIDTH), ("moba_k", GROUP_WIDTH), ("moba_v", GROUP_WIDTH),
             ("sb_q", GROUP_WIDTH), ("sb_k", GROUP_WIDTH), ("sb_v", GROUP_WIDTH), ("small", LANE))
_ROTARY_SEGS = ("moba_q", "moba_k")


def _softplus(z):
    return jnp.maximum(z, 0.0) + jnp.log1p(jnp.exp(-jnp.abs(z)))


def _sigmoid(x):
    return 1.0 / (1.0 + jnp.exp(-x))


def _silu(x):
    return x * _sigmoid(x)


def _dot(a, b):
    return jnp.dot(a.astype(BF16), b.astype(BF16), preferred_element_type=F32)


def _dot_nt(a, b):
    return lax.dot_general(a.astype(BF16), b.astype(BF16), (((1,), (1,)), ((), ())), preferred_element_type=F32)


def _dot_tn(a, b):
    return lax.dot_general(a.astype(BF16), b.astype(BF16), (((0,), (0,)), ((), ())), preferred_element_type=F32)


def _dot_hi(a, b):
    return jnp.dot(a, b, preferred_element_type=F32, precision=HI)


def _split_dot(x, m_bf16):
    hi = x.astype(BF16)
    lo = (x - hi.astype(F32)).astype(BF16)
    return (jnp.dot(hi, m_bf16, preferred_element_type=F32)
            + jnp.dot(lo, m_bf16, preferred_element_type=F32))


def _split3(x):
    hi = x.astype(BF16)
    r = x - hi.astype(F32)
    mid = r.astype(BF16)
    return hi, mid, (r - mid.astype(F32)).astype(BF16)


def _dot3(a, b):
    a_hi = a.astype(BF16)
    a_lo = (a - a_hi.astype(F32)).astype(BF16)
    b_hi = b.astype(BF16)
    b_lo = (b - b_hi.astype(F32)).astype(BF16)
    return (jnp.dot(a_hi, b_hi, preferred_element_type=F32) + jnp.dot(a_hi, b_lo, preferred_element_type=F32)
            + jnp.dot(a_lo, b_hi, preferred_element_type=F32))


def _ln(h, g, b):
    mu = jnp.mean(h, axis=-1, keepdims=True)
    d = h - mu
    var = jnp.mean(d * d, axis=-1, keepdims=True)
    return d * lax.rsqrt(var + LN_EPS) * g + b


def _in_proj_kernel(x_ref, w_ref, cos_ref, sin_ref, *o_refs):
    y = jnp.dot(x_ref[...].astype(BF16), w_ref[...], preferred_element_type=F32)
    cos = jnp.concatenate([cos_ref[...]] * (GROUP_WIDTH // LANE), axis=1)
    sin = jnp.concatenate([sin_ref[...]] * (GROUP_WIDTH // LANE), axis=1)
    first_half = (lax.broadcasted_iota(I32, (1, GROUP_WIDTH), 1) % HEAD_DIM) < ROPE_DIM // 2
    off = 0
    for (name, width), o_ref in zip(_OUT_SEGS, o_refs):
        seg = y[:, off:off + width]
        if name in _ROTARY_SEGS:
            partner = jnp.where(first_half, pltpu.roll(seg, GROUP_WIDTH - ROPE_DIM // 2, 1), pltpu.roll(seg, ROPE_DIM // 2, 1))
            seg = seg * cos + partner * sin
        o_ref[...] = seg
        off += width


def _rotary_tables(pos):
    half = ROPE_DIM // 2
    inv = ROPE_THETA ** (-jnp.arange(half, dtype=F32) * 2.0 / ROPE_DIM)
    ang = pos.astype(F32)[:, None] * inv
    cc = jnp.arange(LANE) % HEAD_DIM
    cos = jnp.where(cc < ROPE_DIM, jnp.cos(ang)[:, cc % half], 1.0)
    sin = jnp.sin(ang)[:, cc % half]
    sin = jnp.where(cc < half, -sin, jnp.where(cc < ROPE_DIM, sin, 0.0))
    return cos, sin


def _in_proj(x, w_in, pos, seq_len):
    m = x.shape[0]
    tm = min(256, m)
    w_in = w_in.astype(BF16)
    cols, off = {}, 0
    for name, sz in zip(_SEG_NAMES, _SEG_SIZES):
        cols[name] = w_in[:, off:off + sz]
        off += sz
    for name in ("moba", "sb"):
        for j, part in enumerate("qkv"):
            cols[f"{name}_{part}"] = cols[f"{name}_qkv"][:, j * GROUP_WIDTH:(j + 1) * GROUP_WIDTH]
    cols["small"] = jnp.concatenate([cols["gdn_b"], cols["gdn_a"], cols["ssm_dt"],
                                     jnp.zeros((D_MODEL, LANE - 3 * N_HEADS), w_in.dtype)], axis=1)
    w = jnp.concatenate([cols[name] for name, _ in _OUT_SEGS], axis=1)
    n = w.shape[1]
    cos, sin = _rotary_tables(pos)
    if seq_len == 1:
        tab_spec = pl.BlockSpec((1, LANE), lambda i: (0, 0))
    else:
        assert seq_len % tm == 0
        tab_spec = pl.BlockSpec((tm, LANE), lambda i: (i % (seq_len // tm), 0))
    outs = pl.pallas_call(
        _in_proj_kernel,
        grid=(m // tm,),
        in_specs=[pl.BlockSpec((tm, D_MODEL), lambda i: (i, 0)),
                  pl.BlockSpec((D_MODEL, n), lambda i: (0, 0)), tab_spec, tab_spec],
        out_specs=[pl.BlockSpec((tm, width), lambda i: (i, 0)) for _, width in _OUT_SEGS],
        out_shape=[jax.ShapeDtypeStruct((m, width), F32) for _, width in _OUT_SEGS],
        compiler_params=pltpu.CompilerParams(dimension_semantics=("arbitrary",), vmem_limit_bytes=VMEM_LIMIT),
        name="in_proj",
    )(x, w, cos, sin)
    return {name: o for (name, _), o in zip(_OUT_SEGS, outs)}


def _sb_kernel(q_ref, k_ref, v_ref, o_ref, *, tq):
    i = pl.program_id(1)
    row = lax.broadcasted_iota(I32, (tq, tq), 0)
    col = lax.broadcasted_iota(I32, (tq, tq), 1)
    upper = (row > col).astype(BF16)
    past = col < row

    heads = [slice(h * HEAD_DIM, (h + 1) * HEAD_DIM) for h in range(N_HEADS)]
    qs = [(q_ref[0, :, hs] * (HEAD_DIM ** -0.5)).astype(BF16) for hs in heads]

    def block(j, state, diag):
        out = []
        for hs, q, (carry, acc) in zip(heads, qs, state):
            kj = k_ref[0, pl.ds(j * tq, tq), hs].astype(BF16)
            vj = v_ref[0, pl.ds(j * tq, tq), hs].astype(BF16)
            z = lax.dot_general(q, kj, (((1,), (1,)), ((), ())), preferred_element_type=F32)
            ls = -_softplus(z)
            lk = jnp.where(past, ls, 0.0) if diag else ls
            la = _split_dot(lk, upper)
            w = jnp.exp(z + ls + la + carry)
            if diag:
                w = jnp.where(past, w, 0.0)
            acc = acc + jnp.dot(w.astype(BF16), vj, preferred_element_type=F32)
            out.append((carry + la[:, 0:1] + lk[:, 0:1], acc))
        return tuple(out)

    state = tuple((jnp.zeros((tq, 1), F32), jnp.zeros((tq, HEAD_DIM), F32)) for _ in heads)
    state = block(i, state, True)
    state = lax.fori_loop(0, i, lambda t, c: block(i - 1 - t, c, False), state)
    for hs, (_, acc) in zip(heads, state):
        o_ref[0, :, hs] = acc


def _sb_attention(q, k, v, tq=256):
    b, l, d = q.shape
    assert l % tq == 0
    return pl.pallas_call(
        functools.partial(_sb_kernel, tq=tq),
        grid=(b, l // tq),
        in_specs=[pl.BlockSpec((1, tq, d), lambda bi, i: (bi, i, 0)),
                  pl.BlockSpec((1, l, d), lambda bi, i: (bi, 0, 0)),
                  pl.BlockSpec((1, l, d), lambda bi, i: (bi, 0, 0))],
        out_specs=pl.BlockSpec((1, tq, d), lambda bi, i: (bi, i, 0)),
        out_shape=jax.ShapeDtypeStruct((b, l, d), F32),
        compiler_params=pltpu.CompilerParams(dimension_semantics=("arbitrary",) * 2, vmem_limit_bytes=VMEM_LIMIT),
        name="sb_attention",
    )(q, k, v)


def _moba_kernel(q_ref, k_ref, v_ref, o_ref, kmean_ref, *, nb):
    t = MOBA_BLOCK
    i = pl.program_id(1)

    @pl.when(i == 0)
    def _():
        for n in range(nb):
            kmean_ref[n:n + 1, :] = jnp.mean(k_ref[0, n * t:(n + 1) * t, :], axis=0, keepdims=True)

    blk_id = lax.broadcasted_iota(I32, (nb, t), 0)
    row = lax.broadcasted_iota(I32, (t, t), 0)
    col = lax.broadcasted_iota(I32, (t, t), 1)

    heads = [slice(h * HEAD_DIM, (h + 1) * HEAD_DIM) for h in range(N_HEADS)]
    qs, sels, state = [], [], []
    for hs in heads:
        qf = q_ref[0, :, hs]
        q = (qf * (HEAD_DIM ** -0.5)).astype(BF16)
        gate = lax.dot_general(kmean_ref[:, hs], qf, (((1,), (1,)), ((), ())),
                               preferred_element_type=F32, precision=HI)
        cnt = jnp.zeros((nb, t), F32)
        for n2 in range(nb):
            cn = gate[n2:n2 + 1, :]
            beats = (cn > gate) | ((cn == gate) & (n2 < blk_id))
            cnt = cnt + jnp.where(beats & (n2 < i), 1.0, 0.0)
        sels.append(jnp.where((cnt < MOBA_TOPK) & (blk_id < i), 1.0, 0.0).astype(BF16))
        qs.append(q)

        ki = k_ref[0, pl.ds(i * t, t), hs].astype(BF16)
        vi = v_ref[0, pl.ds(i * t, t), hs].astype(BF16)
        s = lax.dot_general(q, ki, (((1,), (1,)), ((), ())), preferred_element_type=F32)
        s = jnp.where(col <= row, s, NEG)
        m = jnp.max(s, axis=-1, keepdims=True)
        p = jnp.exp(s - m)
        state.append((m, jnp.sum(p, axis=-1, keepdims=True), jnp.dot(p.astype(BF16), vi, preferred_element_type=F32)))

    def body(j, state):
        pick = ((lax.broadcasted_iota(I32, (nb, 8), 0) == j) & (lax.broadcasted_iota(I32, (nb, 8), 1) == 0)).astype(BF16)
        out = []
        for hs, q, sel, (m, l, acc) in zip(heads, qs, sels, state):
            kj = k_ref[0, pl.ds(j * t, t), hs].astype(BF16)
            vj = v_ref[0, pl.ds(j * t, t), hs].astype(BF16)
            s = lax.dot_general(q, kj, (((1,), (1,)), ((), ())), preferred_element_type=F32)
            sj = lax.dot_general(sel, pick, (((0,), (0,)), ((), ())), preferred_element_type=F32)[:, 0:1]
            s = jnp.where(sj > 0.5, s, NEG)
            m_new = jnp.maximum(m, jnp.max(s, axis=-1, keepdims=True))
            a = jnp.exp(m - m_new)
            p = jnp.exp(s - m_new)
            out.append((m_new, a * l + jnp.sum(p, axis=-1, keepdims=True),
                        a * acc + jnp.dot(p.astype(BF16), vj, preferred_element_type=F32)))
        return tuple(out)

    state = lax.fori_loop(0, i, body, tuple(state))
    for hs, (_, l, acc) in zip(heads, state):
        o_ref[0, :, hs] = acc / l


def _moba_attention(q, k, v):
    b, l, d = q.shape
    t = MOBA_BLOCK
    assert l % t == 0
    nb = l // t
    return pl.pallas_call(
        functools.partial(_moba_kernel, nb=nb),
        grid=(b, nb),
        in_specs=[pl.BlockSpec((1, t, d), lambda bi, i: (bi, i, 0)),
                  pl.BlockSpec((1, l, d), lambda bi, i: (bi, 0, 0)),
                  pl.BlockSpec((1, l, d), lambda bi, i: (bi, 0, 0))],
        out_specs=pl.BlockSpec((1, t, d), lambda bi, i: (bi, i, 0)),
        out_shape=jax.ShapeDtypeStruct((b, l, d), F32),
        scratch_shapes=[pltpu.VMEM((nb, d), F32)],
        compiler_params=pltpu.CompilerParams(dimension_semantics=("arbitrary",) * 2, vmem_limit_bytes=VMEM_LIMIT),
        name="moba_attention",
    )(q, k, v)


def _conv_tile(ext_ref, u, w_ref, tl):
    ext_ref[8:8 + tl, :] = u
    y = ext_ref[5:5 + tl, :] * w_ref[0:1, :]
    for i in range(1, CONV_W):
        y = y + ext_ref[5 + i:5 + i + tl, :] * w_ref[i:i + 1, :]
    tail = ext_ref[tl + 5:tl + 8, :]
    ext_ref[5:8, :] = tail
    return y, tail


def _gdn_kernel(qkv_ref, z_ref, small_ref, convw_ref, par_ref, normw_ref,
                o_ref, conv_out_ref, s_out_ref, ext_ref, s_ref, *, tl):
    c = CHUNK
    step = pl.program_id(1)

    @pl.when(step == 0)
    def _():
        ext_ref[...] = jnp.zeros_like(ext_ref)
        s_ref[...] = jnp.zeros_like(s_ref)

    y, tail = _conv_tile(ext_ref, qkv_ref[0], convw_ref, tl)
    conv_out_ref[0] = tail
    y = _silu(y)
    small = small_ref[0]
    beta_all = _sigmoid(small[:, 0:N_HEADS])
    g_all = par_ref[0:1, :] * _softplus(small[:, N_HEADS:2 * N_HEADS] + par_ref[1:2, :])

    row = lax.broadcasted_iota(I32, (tl, tl), 0)
    col = lax.broadcasted_iota(I32, (tl, tl), 1)
    same = (row // c) == (col // c)
    tril = same & (row >= col)
    strict = same & (row > col)
    tril_b = tril.astype(BF16)
    triu_b = (same & (row <= col)).astype(BF16)
    eye = (row == col).astype(F32)

    g8 = jnp.concatenate([g_all, jnp.zeros((tl, 8 - N_HEADS), F32)], axis=1)
    g_parts = _split3(g8)
    gc_cols = sum(jnp.dot(tril_b, part, preferred_element_type=F32) for part in g_parts)
    gc_rows = sum(lax.dot_general(part, triu_b, (((0,), (0,)), ((), ())), preferred_element_type=F32) for part in g_parts)

    for h in range(N_HEADS):
        hs = slice(h * HEAD_DIM, (h + 1) * HEAD_DIM)
        q = y[:, h * HEAD_DIM:(h + 1) * HEAD_DIM]
        k = y[:, GROUP_WIDTH + h * HEAD_DIM:GROUP_WIDTH + (h + 1) * HEAD_DIM]
        v = y[:, 2 * GROUP_WIDTH + h * HEAD_DIM:2 * GROUP_WIDTH + (h + 1) * HEAD_DIM]
        q = q * lax.rsqrt(jnp.sum(q * q, axis=-1, keepdims=True) + RMS_EPS) * (HEAD_DIM ** -0.5)
        k = k * lax.rsqrt(jnp.sum(k * k, axis=-1, keepdims=True) + RMS_EPS)
        beta = beta_all[:, h:h + 1]
        gcol = gc_cols[:, h:h + 1]
        grow = gc_rows[h:h + 1, :]
        decay = jnp.where(tril, jnp.exp(jnp.where(tril, gcol - grow, 0.0)), 0.0)
        kb = k * beta
        a = jnp.where(strict, _dot_nt(kb, k) * decay, 0.0)
        p = -a
        tinv = eye + p
        for m in range(5):
            mm = _dot3 if m < GDN_PRECISE_SQUARINGS else _dot
            p = mm(p, p)
            tinv = tinv + mm(tinv, p)
        u = _dot(tinv, v * beta)
        w = _dot(tinv, kb * jnp.exp(gcol))
        attn = jnp.where(tril, _dot_nt(q, k) * decay, 0.0)
        qe = q * jnp.exp(gcol)
        s = s_ref[h]
        for ci in range(tl // c):
            rs = slice(ci * c, (ci + 1) * c)
            g_last = gcol[(ci + 1) * c - 1:(ci + 1) * c, :]
            v_new = u[rs] - _dot(w[rs], s)
            o = _dot(qe[rs], s) + _dot(attn[rs, rs], v_new)
            s = s * jnp.exp(g_last) + _dot_tn(k[rs] * jnp.exp(g_last - gcol[rs]), v_new)
            o = o * lax.rsqrt(jnp.mean(o * o, axis=-1, keepdims=True) + RMS_EPS) * normw_ref[...] * _silu(z_ref[0, rs, hs])
            o_ref[0, rs, hs] = o
        s_ref[h] = s

    @pl.when(step == pl.num_programs(1) - 1)
    def _():
        s_out_ref[0] = s_ref[...]


def _gdn_prompt(qkv, z, small, conv_w, a_log, dt_bias, norm_w, tl=256):
    b, l, _ = qkv.shape
    assert l % tl == 0 and tl % CHUNK == 0
    par = jnp.stack([-jnp.exp(a_log.astype(F32)), dt_bias.astype(F32)])
    return pl.pallas_call(
        functools.partial(_gdn_kernel, tl=tl),
        grid=(b, l // tl),
        in_specs=[pl.BlockSpec((1, tl, GDN_CONV_DIM), lambda bi, i: (bi, i, 0)),
                  pl.BlockSpec((1, tl, GROUP_WIDTH), lambda bi, i: (bi, i, 0)),
                  pl.BlockSpec((1, tl, LANE), lambda bi, i: (bi, i, 0)),
                  pl.BlockSpec((CONV_W, GDN_CONV_DIM), lambda bi, i: (0, 0)),
                  pl.BlockSpec((2, N_HEADS), lambda bi, i: (0, 0)),
                  pl.BlockSpec((1, HEAD_DIM), lambda bi, i: (0, 0))],
        out_specs=[pl.BlockSpec((1, tl, GROUP_WIDTH), lambda bi, i: (bi, i, 0)),
                   pl.BlockSpec((1, CONV_W - 1, GDN_CONV_DIM), lambda bi, i: (bi, 0, 0)),
                   pl.BlockSpec((1, N_HEADS, HEAD_DIM, HEAD_DIM), lambda bi, i: (bi, 0, 0, 0))],
        out_shape=[jax.ShapeDtypeStruct((b, l, GROUP_WIDTH), F32),
                   jax.ShapeDtypeStruct((b, CONV_W - 1, GDN_CONV_DIM), F32),
                   jax.ShapeDtypeStruct((b, N_HEADS, HEAD_DIM, HEAD_DIM), F32)],
        scratch_shapes=[pltpu.VMEM((tl + 8, GDN_CONV_DIM), F32),
                        pltpu.VMEM((N_HEADS, HEAD_DIM, HEAD_DIM), F32)],
        compiler_params=pltpu.CompilerParams(dimension_semantics=("arbitrary", "arbitrary"),
                                             vmem_limit_bytes=VMEM_LIMIT),
        name="gdn_prompt",
    )(qkv, z, small, conv_w, par, norm_w.reshape(1, HEAD_DIM))


def _ssd_kernel(xbc_ref, z_ref, small_ref, convw_ref, convb_ref, par_ref, normw_ref,
                o_ref, conv_out_ref, h_out_ref, ext_ref, h_ref, *, tl):
    c = CHUNK
    step = pl.program_id(1)
    hpg = N_HEADS // SSM_NGROUPS

    @pl.when(step == 0)
    def _():
        ext_ref[...] = jnp.zeros_like(ext_ref)
        h_ref[...] = jnp.zeros_like(h_ref)

    y, tail = _conv_tile(ext_ref, xbc_ref[0], convw_ref, tl)
    conv_out_ref[0] = tail
    xbc = _silu(y + convb_ref[...])
    dt_all = _softplus(small_ref[0][:, 2 * N_HEADS:3 * N_HEADS] + par_ref[1:2, :])
    ad_all = dt_all * par_ref[0:1, :]

    tril = lax.broadcasted_iota(I32, (c, c), 0) >= lax.broadcasted_iota(I32, (c, c), 1)
    boff = GROUP_WIDTH
    coff = GROUP_WIDTH + SSM_NGROUPS * SSM_STATE

    row = lax.broadcasted_iota(I32, (tl, tl), 0)
    col = lax.broadcasted_iota(I32, (tl, tl), 1)
    same = (row // c) == (col // c)
    tril_b = (same & (row >= col)).astype(BF16)
    triu_b = (same & (row <= col)).astype(BF16)
    a_parts = _split3(jnp.concatenate([ad_all, jnp.zeros((tl, 8 - N_HEADS), F32)], axis=1))
    acs_cols = sum(jnp.dot(tril_b, part, preferred_element_type=F32) for part in a_parts)
    acs_rows = sum(lax.dot_general(part, triu_b, (((0,), (0,)), ((), ())), preferred_element_type=F32) for part in a_parts)

    for ci in range(tl // c):
        r0 = ci * c
        ys = []
        for h in range(N_HEADS):
            grp = h // hpg
            x = xbc[r0:r0 + c, h * HEAD_DIM:(h + 1) * HEAD_DIM]
            bm = xbc[r0:r0 + c, boff + grp * SSM_STATE:boff + (grp + 1) * SSM_STATE]
            cm = xbc[r0:r0 + c, coff + grp * SSM_STATE:coff + (grp + 1) * SSM_STATE]
            xdt = x * dt_all[r0:r0 + c, h:h + 1]
            acol = acs_cols[r0:r0 + c, h:h + 1]
            arow = acs_rows[h:h + 1, r0:r0 + c]
            lmat = jnp.where(tril, jnp.exp(jnp.where(tril, acol - arow, 0.0)), 0.0)
            y_diag = _dot(_dot_nt(cm, bm) * lmat, xdt)
            a_last = acol[c - 1:c, :]
            states = _dot_tn(xdt * jnp.exp(a_last - acol), bm)
            hin = h_ref[h]
            y_off = _dot_nt(cm, hin) * jnp.exp(acol)
            h_ref[h] = hin * jnp.exp(a_last) + states
            yh = y_diag + y_off + par_ref[2:3, h:h + 1] * x
            ys.append(yh * _silu(z_ref[0, r0:r0 + c, h * HEAD_DIM:(h + 1) * HEAD_DIM]))
        for grp in range(SSM_NGROUPS):
            grp_ys = ys[grp * hpg:(grp + 1) * hpg]
            ms = sum(jnp.sum(a * a, axis=-1, keepdims=True) for a in grp_ys) / (hpg * HEAD_DIM)
            r = lax.rsqrt(ms + RMS_EPS)
            for j, a in enumerate(grp_ys):
                h = grp * hpg + j
                o_ref[0, r0:r0 + c, h * HEAD_DIM:(h + 1) * HEAD_DIM] = a * r * normw_ref[:, h * HEAD_DIM:(h + 1) * HEAD_DIM]

    @pl.when(step == pl.num_programs(1) - 1)
    def _():
        h_out_ref[0] = h_ref[...]


def _ssd_prompt(xbc, z, small, conv_w, conv_b, a_log, dt_bias, d_skip, norm_w, tl=256):
    b, l, _ = xbc.shape
    assert l % tl == 0 and tl % CHUNK == 0
    par = jnp.stack([-jnp.exp(a_log.astype(F32)), dt_bias.astype(F32), d_skip.astype(F32)])
    return pl.pallas_call(
        functools.partial(_ssd_kernel, tl=tl),
        grid=(b, l // tl),
        in_specs=[pl.BlockSpec((1, tl, SSM_CONV_DIM), lambda bi, i: (bi, i, 0)),
                  pl.BlockSpec((1, tl, GROUP_WIDTH), lambda bi, i: (bi, i, 0)),
                  pl.BlockSpec((1, tl, LANE), lambda bi, i: (bi, i, 0)),
                  pl.BlockSpec((CONV_W, SSM_CONV_DIM), lambda bi, i: (0, 0)),
                  pl.BlockSpec((1, SSM_CONV_DIM), lambda bi, i: (0, 0)),
                  pl.BlockSpec((3, N_HEADS), lambda bi, i: (0, 0)),
                  pl.BlockSpec((1, GROUP_WIDTH), lambda bi, i: (0, 0))],
        out_specs=[pl.BlockSpec((1, tl, GROUP_WIDTH), lambda bi, i: (bi, i, 0)),
                   pl.BlockSpec((1, CONV_W - 1, SSM_CONV_DIM), lambda bi, i: (bi, 0, 0)),
                   pl.BlockSpec((1, N_HEADS, HEAD_DIM, SSM_STATE), lambda bi, i: (bi, 0, 0, 0))],
        out_shape=[jax.ShapeDtypeStruct((b, l, GROUP_WIDTH), F32),
                   jax.ShapeDtypeStruct((b, CONV_W - 1, SSM_CONV_DIM), F32),
                   jax.ShapeDtypeStruct((b, N_HEADS, HEAD_DIM, SSM_STATE), F32)],
        scratch_shapes=[pltpu.VMEM((tl + 8, SSM_CONV_DIM), F32),
                        pltpu.VMEM((N_HEADS, HEAD_DIM, SSM_STATE), F32)],
        compiler_params=pltpu.CompilerParams(dimension_semantics=("arbitrary", "arbitrary"),
                                             vmem_limit_bytes=VMEM_LIMIT),
        name="ssd_prompt",
    )(xbc, z, small, conv_w, conv_b.reshape(1, SSM_CONV_DIM), par, norm_w.reshape(1, GROUP_WIDTH))


def _post_mix_kernel(o0_ref, o1_ref, o2_ref, o3_ref, x_ref, wout_ref, g_ref, b_ref, rw_ref, rb_ref,
                     xn_ref, xnb_ref, idx_ref, gate_ref, rank_ref, cum_ref, cnt_ref, *, tm):
    @pl.when(pl.program_id(0) == 0)
    def _():
        cnt_ref[...] = jnp.zeros_like(cnt_ref)

    mix = jnp.zeros((tm, D_MODEL), F32)
    for gi, o_ref in enumerate((o0_ref, o1_ref, o2_ref, o3_ref)):
        mix = mix + jnp.dot(o_ref[...].astype(BF16), wout_ref[gi * GROUP_WIDTH:(gi + 1) * GROUP_WIDTH, :],
                            preferred_element_type=F32)
    xn = _ln(DN_ALPHA * x_ref[...] + mix, g_ref[...], b_ref[...])
    xn_ref[...] = xn
    xnb_ref[...] = xn.astype(BF16)

    logits = _dot_hi(xn, rw_ref[...]) + rb_ref[...]
    lane_e = lax.broadcasted_iota(I32, (tm, N_EXPERTS), 1).astype(F32)
    lane = lax.broadcasted_iota(I32, (tm, LANE), 1)
    work = logits
    sel = jnp.zeros((tm, N_EXPERTS), F32)
    idx_out = jnp.zeros((tm, LANE), I32)
    val_out = jnp.zeros((tm, LANE), F32)
    hits = []
    for r in range(TOP_K):
        m = jnp.max(work, axis=-1, keepdims=True)
        pick = jnp.min(jnp.where(work == m, lane_e, float(N_EXPERTS)), axis=-1, keepdims=True)
        hit = lane_e == pick
        work = jnp.where(hit, -jnp.inf, work)
        sel = sel + hit.astype(F32)
        hits.append(hit)
        idx_out = jnp.where(lane == r, pick.astype(I32), idx_out)
        val_out = jnp.where(lane == r, m, val_out)
    ex = jnp.where(lane < TOP_K, jnp.exp(val_out - val_out[:, 0:1]), 0.0)
    gate_ref[...] = ex / jnp.sum(ex, axis=-1, keepdims=True)
    idx_ref[...] = idx_out

    row = lax.broadcasted_iota(I32, (tm, tm), 0)
    col = lax.broadcasted_iota(I32, (tm, tm), 1)
    before = jnp.dot((row > col).astype(BF16), sel.astype(BF16), preferred_element_type=F32) + cnt_ref[...]
    rank_out = jnp.zeros((tm, LANE), I32)
    for r in range(TOP_K):
        rk = jnp.sum(jnp.where(hits[r], before, 0.0), axis=-1, keepdims=True)
        rank_out = jnp.where(lane == r, rk.astype(I32), rank_out)
    rank_ref[...] = rank_out
    cum_ref[0] = jnp.broadcast_to(cnt_ref[...], (8, N_EXPERTS))
    cnt_ref[...] = cnt_ref[...] + jnp.sum(sel, axis=0, keepdims=True)


def _post_mix(outs, x, w_out_bf16, ln_g, ln_b, router_w, router_b, tm):
    n = x.shape[0]
    nt = n // tm
    tok = lambda w: pl.BlockSpec((tm, w), lambda i: (i, 0))
    full = lambda a: pl.BlockSpec(a.shape, lambda i: (0,) * a.ndim)
    g2, b2, rb2 = ln_g.reshape(1, -1), ln_b.reshape(1, -1), router_b.reshape(1, -1)
    return pl.pallas_call(
        functools.partial(_post_mix_kernel, tm=tm),
        grid=(nt,),
        in_specs=[tok(GROUP_WIDTH)] * 4 + [tok(D_MODEL), full(w_out_bf16), full(g2), full(b2), full(router_w), full(rb2)],
        out_specs=[tok(D_MODEL), tok(D_MODEL), tok(LANE), tok(LANE), tok(LANE),
                   pl.BlockSpec((1, 8, N_EXPERTS), lambda i: (i, 0, 0))],
        out_shape=[jax.ShapeDtypeStruct((n, D_MODEL), F32), jax.ShapeDtypeStruct((n, D_MODEL), BF16),
                   jax.ShapeDtypeStruct((n, LANE), I32), jax.ShapeDtypeStruct((n, LANE), F32),
                   jax.ShapeDtypeStruct((n, LANE), I32), jax.ShapeDtypeStruct((nt, 8, N_EXPERTS), F32)],
        scratch_shapes=[pltpu.VMEM((1, N_EXPERTS), F32)],
        compiler_params=pltpu.CompilerParams(dimension_semantics=("arbitrary",), vmem_limit_bytes=VMEM_LIMIT),
        name="post_mix",
    )(*outs, x, w_out_bf16, g2, b2, router_w, rb2)


def _moe_dispatch(idx, gates, rank, cum, n_tok, tm_cum, tm, blk):
    nt = n_tok // tm
    n_blocks = -(-n_tok * TOP_K // blk) + N_EXPERTS
    per_pair = -(-tm // blk) + 1
    e_idx = idx[:, :TOP_K]
    cum_i = cum[:, 0, :].astype(I32)
    onehot = e_idx[:, :, None] == jnp.arange(N_EXPERTS, dtype=I32)
    counts = cum_i[-1] + jnp.sum(onehot[n_tok - tm_cum:].astype(I32), axis=(0, 1))
    cum_t = cum_i[::tm // tm_cum]
    cum_all = jnp.concatenate([cum_t, counts[None]], 0)
    blocks_per_e = (counts + blk - 1) // blk
    blk_start = jnp.cumsum(blocks_per_e) - blocks_per_e
    dest = jnp.sum(jnp.where(onehot, blk_start, 0), axis=-1) * blk + rank[:, :TOP_K]
    dest_f = dest.astype(F32)
    dest_pad = jnp.pad(dest_f, ((0, 0), (0, LANE - TOP_K)), constant_values=-1.0)
    dest_t = jnp.pad(dest_f.T, ((0, 8 - TOP_K), (0, 0)), constant_values=-1.0)
    gate_t = jnp.pad(gates[:, :TOP_K].T, ((0, 8 - TOP_K), (0, 0)))
    eid_t = jnp.pad(e_idx.T, ((0, 8 - TOP_K), (0, 0)), constant_values=-1)

    lo, hi = cum_all[:-1].T, cum_all[1:].T
    has = hi > lo
    j0 = lo // blk
    j1 = jnp.maximum(hi - 1, 0) // blk
    jj = j0[:, :, None] + jnp.arange(per_pair, dtype=I32)
    valid = has[:, :, None] & (jj <= j1[:, :, None])
    blk_id = blk_start[:, None, None] + jj
    e_b = jnp.broadcast_to(jnp.arange(N_EXPERTS, dtype=I32)[:, None, None], jj.shape)
    t_b = jnp.broadcast_to(jnp.arange(nt, dtype=I32)[None, :, None], jj.shape)
    n_steps = min(N_EXPERTS * nt + n_blocks, per_pair * N_EXPERTS * nt)
    n_valid = jnp.sum(valid)
    pos = jnp.arange(n_steps)
    v = pos < n_valid

    win = min(MOE_ROW_WINDOW, blk)
    r_lo = jnp.maximum(lo[:, :, None], jj * blk) - jj * blk
    r_hi = jnp.minimum(hi[:, :, None], (jj + 1) * blk) - jj * blk
    w0 = (r_lo // 16) * 16
    n_win = jnp.maximum((r_hi - w0 + win - 1) // win, 1)
    win_bits = (n_win << 4) + (w0 << 8)

    def build(key):
        order = jnp.argsort(jnp.where(valid, key, jnp.int32(1 << 30)).reshape(-1))[:n_steps]
        src = order[jnp.minimum(pos, n_valid - 1)]
        return blk_id.reshape(-1)[src], t_b.reshape(-1)[src], e_b.reshape(-1)[src], win_bits.reshape(-1)[src]

    def edges(a):
        change = a[1:] != a[:-1]
        first = jnp.concatenate([jnp.ones((1,), bool), change]) & v
        last = (jnp.concatenate([change, jnp.ones((1,), bool)]) | (pos == n_valid - 1)) & v
        return first, last

    b1, t1, e1, wb1 = build(blk_id * nt + t_b)
    first_b, last_b = edges(b1)
    first_e, _ = edges(e1)
    flags1 = first_b.astype(I32) + 2 * last_b.astype(I32) + 4 * v.astype(I32) + 8 * first_e.astype(I32) + wb1
    b2, t2, e2, wb2 = build(t_b * (2 * n_blocks) + blk_id)
    first_t, last_t = edges(t2)
    flags2 = first_t.astype(I32) + 2 * last_t.astype(I32) + 4 * v.astype(I32) + wb2
    eid_pad = jnp.pad(e_idx.astype(F32), ((0, 0), (0, LANE - TOP_K)), constant_values=-1.0)
    return dict(tok=jnp.concatenate([eid_pad, dest_pad], axis=1),
                tok_t=jnp.concatenate([eid_t.astype(F32), dest_t, gate_t], axis=0),
                n_rows=n_blocks * blk, n_steps=n_steps,
                gather=(b1.astype(I32), t1.astype(I32), e1.astype(I32), flags1),
                combine=(b2.astype(I32), t2.astype(I32), e2.astype(I32), flags2))


def _expert_kernel(blk_s, tile_s, exp_s, flag_s, x_ref, tok_ref, wgu_ref, bgu_ref, wdn_ref, bdn_ref,
                   yb_ref, xacc_ref, gacc_ref, wgu_bf, wdn_bf, *, blk, tm):
    s = pl.program_id(0)
    flags = flag_s[s]

    @pl.when((flags & 8) != 0)
    def _():
        wgu_bf[...] = wgu_ref[0, 0].astype(BF16)
        wdn_bf[...] = wdn_ref[0, 0].astype(BF16)

    @pl.when((flags & 1) != 0)
    def _():
        xacc_ref[...] = jnp.zeros_like(xacc_ref)
        gacc_ref[...] = jnp.zeros_like(gacc_ref)

    @pl.when((flags & 4) != 0)
    def _():
        mine = tok_ref[0:8, :] == exp_s[s].astype(F32)
        d_sel = jnp.sum(jnp.where(mine, tok_ref[8:16, :] + 1.0, 0.0), axis=0, keepdims=True) - 1.0
        g_sel = jnp.sum(jnp.where(mine, tok_ref[16:24, :], 0.0), axis=0, keepdims=True)
        win = min(MOE_ROW_WINDOW, blk)
        w0 = flags >> 8
        base = blk_s[s] * blk

        def window(i, carry):
            start = w0 + i * win
            r0 = pl.multiple_of(jnp.minimum(start, blk - win), 16)
            local = r0 + lax.broadcasted_iota(I32, (win, tm), 0)
            hit = (d_sel == (base + local).astype(F32)) & (local >= start)
            p = jnp.where(hit, 1.0, 0.0).astype(BF16)
            xacc_ref[pl.ds(r0, win), :] += jnp.dot(p, x_ref[...], preferred_element_type=F32)
            gacc_ref[pl.ds(r0, win), :] += jnp.sum(jnp.where(hit, g_sel, 0.0), axis=-1, keepdims=True)
            return carry

        lax.fori_loop(0, (flags >> 4) & 15, window, 0)

    @pl.when((flags & 2) != 0)
    def _():
        h = jnp.dot(xacc_ref[...].astype(BF16), wgu_bf[...], preferred_element_type=F32) + bgu_ref[0, 0]
        gate = jnp.minimum(h[:, :D_FF], SWIGLU_LIMIT)
        up = jnp.clip(h[:, D_FF:], -SWIGLU_LIMIT, SWIGLU_LIMIT)
        act = (up + 1.0) * gate * _sigmoid(SWIGLU_ALPHA * gate)
        y = jnp.dot(act.astype(BF16), wdn_bf[...], preferred_element_type=F32) + bdn_ref[0, 0]
        yb_ref[...] = (y * gacc_ref[...]).astype(yb_ref.dtype)


def _expert_ffn(layer, xn_bf16, disp, w_gu, b_gu, w_down, b_down, tm, blk):
    blk_s, tile_s, exp_s, flag_s = disp["gather"]
    depth = w_gu.shape[0]
    tok_t = pl.BlockSpec((24, tm), lambda s, b, t, e, f: (0, t[s]))
    grid_spec = pltpu.PrefetchScalarGridSpec(
        num_scalar_prefetch=4,
        grid=(disp["n_steps"],),
        in_specs=[pl.BlockSpec((tm, D_MODEL), lambda s, b, t, e, f: (t[s], 0)), tok_t,
                  pl.BlockSpec((1, 1, D_MODEL, 2 * D_FF), lambda s, b, t, e, f: (layer, e[s], 0, 0)),
                  pl.BlockSpec((1, 1, 1, 2 * D_FF), lambda s, b, t, e, f: (layer, e[s], 0, 0)),
                  pl.BlockSpec((1, 1, D_FF, D_MODEL), lambda s, b, t, e, f: (layer, e[s], 0, 0)),
                  pl.BlockSpec((1, 1, 1, D_MODEL), lambda s, b, t, e, f: (layer, e[s], 0, 0))],
        out_specs=pl.BlockSpec((blk, D_MODEL), lambda s, b, t, e, f: (b[s], 0)),
        scratch_shapes=[pltpu.VMEM((blk, D_MODEL), F32), pltpu.VMEM((blk, 1), F32),
                        pltpu.VMEM((D_MODEL, 2 * D_FF), BF16), pltpu.VMEM((D_FF, D_MODEL), BF16)])
    return pl.pallas_call(
        functools.partial(_expert_kernel, blk=blk, tm=tm),
        grid_spec=grid_spec,
        out_shape=jax.ShapeDtypeStruct((disp["n_rows"], D_MODEL), BF16),
        compiler_params=pltpu.CompilerParams(dimension_semantics=("arbitrary",), vmem_limit_bytes=VMEM_LIMIT),
        name="expert_ffn",
    )(blk_s, tile_s, exp_s, flag_s, xn_bf16, disp["tok_t"],
      w_gu, b_gu.reshape(depth, N_EXPERTS, 1, 2 * D_FF), w_down, b_down.reshape(depth, N_EXPERTS, 1, D_MODEL))


def _combine_kernel(blk_s, tile_s, exp_s, flag_s, yb_ref, tok_ref, xn_ref, g_ref, b_ref, y_ref, acc_ref, *, blk, tm):
    s = pl.program_id(0)
    flags = flag_s[s]

    @pl.when((flags & 1) != 0)
    def _():
        acc_ref[...] = jnp.zeros_like(acc_ref)

    @pl.when((flags & 4) != 0)
    def _():
        mine = tok_ref[:, 0:LANE] == exp_s[s].astype(F32)
        d_sel = jnp.sum(jnp.where(mine, tok_ref[:, LANE:2 * LANE] + 1.0, 0.0), axis=-1, keepdims=True) - 1.0
        rowid = (blk_s[s] * blk + lax.broadcasted_iota(I32, (tm, blk), 1)).astype(F32)
        p = jnp.where(d_sel == rowid, 1.0, 0.0).astype(BF16)
        acc_ref[...] += jnp.dot(p, yb_ref[...], preferred_element_type=F32)

    @pl.when((flags & 2) != 0)
    def _():
        y_ref[...] = _ln(DN_ALPHA * xn_ref[...] + acc_ref[...], g_ref[...], b_ref[...])


def _moe_combine(yb, disp, xn, ln_g, ln_b, tm, blk):
    blk_s, tile_s, exp_s, flag_s = disp["combine"]
    n = xn.shape[0]
    tok = pl.BlockSpec((tm, 2 * LANE), lambda s, b, t, e, f: (t[s], 0))
    grid_spec = pltpu.PrefetchScalarGridSpec(
        num_scalar_prefetch=4,
        grid=(disp["n_steps"],),
        in_specs=[pl.BlockSpec((blk, D_MODEL), lambda s, b, t, e, f: (b[s], 0)), tok,
                  pl.BlockSpec((tm, D_MODEL), lambda s, b, t, e, f: (t[s], 0)),
                  pl.BlockSpec((1, D_MODEL), lambda s, b, t, e, f: (0, 0)),
                  pl.BlockSpec((1, D_MODEL), lambda s, b, t, e, f: (0, 0))],
        out_specs=pl.BlockSpec((tm, D_MODEL), lambda s, b, t, e, f: (t[s], 0)),
        scratch_shapes=[pltpu.VMEM((tm, D_MODEL), F32)])
    return pl.pallas_call(
        functools.partial(_combine_kernel, blk=blk, tm=tm),
        grid_spec=grid_spec,
        out_shape=jax.ShapeDtypeStruct((n, D_MODEL), F32),
        compiler_params=pltpu.CompilerParams(dimension_semantics=("arbitrary",), vmem_limit_bytes=VMEM_LIMIT),
        name="moe_combine",
    )(blk_s, tile_s, exp_s, flag_s, yb, disp["tok"], xn, ln_g.reshape(1, -1), ln_b.reshape(1, -1))


def _post_mix_moe(layer, outs, x, w_out, fw):
    ln1_g, ln1_b, router_w, router_b, w_gu, b_gu, w_down, b_down, ln2_g, ln2_b = fw
    n = x.shape[0]
    tm_cum = min(256, n)
    tm = min(512, n)
    blk = min(EXPERT_BLOCK, n * TOP_K)
    xn, xnb, idx, gates, rank, cum = _post_mix(outs, x, w_out.astype(BF16), ln1_g, ln1_b, router_w, router_b, tm_cum)
    disp = _moe_dispatch(idx, gates, rank, cum, n, tm_cum, tm, blk)
    yb = _expert_ffn(layer, xnb, disp, w_gu, b_gu, w_down, b_down, tm, blk)
    return _moe_combine(yb, disp, xn, ln2_g, ln2_b, tm, blk)


def _rec_step_kernel(gqkv_ref, gz_ref, xbc_ref, sz_ref, small_ref, gconv0_ref, sconv0_ref, s0_ref, h0_ref,
                     gcw_ref, scw_ref, scb_ref, gpar_ref, spar_ref, gnw_ref, snw_ref,
                     gout_ref, sout_ref, gconv_ref, sconv_ref, s_ref, h_ref):
    small = small_ref[0]

    def conv(u, buf_ref, w_ref, out_ref):
        buf = buf_ref[0]
        y = u * w_ref[CONV_W - 1:CONV_W, :]
        for i in range(CONV_W - 1):
            y = y + buf[i:i + 1, :] * w_ref[i:i + 1, :]
        out_ref[0, 0:CONV_W - 2, :] = buf[1:CONV_W - 1, :]
        out_ref[0, CONV_W - 2:CONV_W - 1, :] = u
        return y

    eye = (lax.broadcasted_iota(I32, (HEAD_DIM, HEAD_DIM), 0)
           == lax.broadcasted_iota(I32, (HEAD_DIM, HEAD_DIM), 1)).astype(F32)

    def to_col(r):
        return jnp.sum(eye * r, axis=-1, keepdims=True)

    y = _silu(conv(gqkv_ref[0], gconv0_ref, gcw_ref, gconv_ref))
    beta_all = _sigmoid(small[:, 0:N_HEADS])
    g_all = gpar_ref[0:1, :] * _softplus(small[:, N_HEADS:2 * N_HEADS] + gpar_ref[1:2, :])
    for h in range(N_HEADS):
        hs = slice(h * HEAD_DIM, (h + 1) * HEAD_DIM)
        q = y[:, h * HEAD_DIM:(h + 1) * HEAD_DIM]
        k = y[:, GROUP_WIDTH + h * HEAD_DIM:GROUP_WIDTH + (h + 1) * HEAD_DIM]
        v = y[:, 2 * GROUP_WIDTH + h * HEAD_DIM:2 * GROUP_WIDTH + (h + 1) * HEAD_DIM]
        q = q * lax.rsqrt(jnp.sum(q * q, axis=-1, keepdims=True) + RMS_EPS) * (HEAD_DIM ** -0.5)
        k = k * lax.rsqrt(jnp.sum(k * k, axis=-1, keepdims=True) + RMS_EPS)
        beta = beta_all[:, h:h + 1]
        eg = jnp.exp(g_all[:, h:h + 1])
        s0 = s0_ref[0, h]
        kc, qc = to_col(k), to_col(q)
        v_new = v * beta - jnp.sum(kc * (beta * eg) * s0, axis=0, keepdims=True)
        o = jnp.sum(qc * eg * s0, axis=0, keepdims=True) + jnp.sum(q * k, axis=-1, keepdims=True) * v_new
        s_ref[0, h] = s0 * eg + kc * v_new
        o = o * lax.rsqrt(jnp.mean(o * o, axis=-1, keepdims=True) + RMS_EPS) * gnw_ref[...] * _silu(gz_ref[0][:, hs])
        gout_ref[0, :, hs] = o

    xbc = _silu(conv(xbc_ref[0], sconv0_ref, scw_ref, sconv_ref) + scb_ref[...])
    dt_all = _softplus(small[:, 2 * N_HEADS:3 * N_HEADS] + spar_ref[1:2, :])
    hpg = N_HEADS // SSM_NGROUPS
    boff, coff = GROUP_WIDTH, GROUP_WIDTH + SSM_NGROUPS * SSM_STATE
    ys = []
    for h in range(N_HEADS):
        grp = h // hpg
        x = xbc[:, h * HEAD_DIM:(h + 1) * HEAD_DIM]
        bm = xbc[:, boff + grp * SSM_STATE:boff + (grp + 1) * SSM_STATE]
        cm = xbc[:, coff + grp * SSM_STATE:coff + (grp + 1) * SSM_STATE]
        dt = dt_all[:, h:h + 1]
        hn = h0_ref[0, h] * jnp.exp(dt * spar_ref[0:1, h:h + 1]) + to_col(x * dt) * bm
        h_ref[0, h] = hn
        yc = jnp.sum(hn * cm, axis=-1, keepdims=True)
        yh = jnp.sum(eye * yc, axis=0, keepdims=True) + spar_ref[2:3, h:h + 1] * x
        ys.append(yh * _silu(sz_ref[0][:, h * HEAD_DIM:(h + 1) * HEAD_DIM]))
    for grp in range(SSM_NGROUPS):
        grp_ys = ys[grp * hpg:(grp + 1) * hpg]
        ms = sum(jnp.sum(a * a, axis=-1, keepdims=True) for a in grp_ys) / (hpg * HEAD_DIM)
        r = lax.rsqrt(ms + RMS_EPS)
        for j, a in enumerate(grp_ys):
            h = grp * hpg + j
            sout_ref[0, :, h * HEAD_DIM:(h + 1) * HEAD_DIM] = a * r * snw_ref[:, h * HEAD_DIM:(h + 1) * HEAD_DIM]


def _rec_step(seg, gconv0, s0, sconv0, h0, lw):
    (_, gdn_conv_w, gdn_a_log, gdn_dt_bias, gdn_norm_w, ssm_conv_w, ssm_conv_b,
     ssm_a_log, ssm_dt_bias, ssm_d, ssm_norm_w, _) = lw
    b = seg["small"].shape[0]
    gpar = jnp.stack([-jnp.exp(gdn_a_log.astype(F32)), gdn_dt_bias.astype(F32)])
    spar = jnp.stack([-jnp.exp(ssm_a_log.astype(F32)), ssm_dt_bias.astype(F32), ssm_d.astype(F32)])
    row = lambda w: pl.BlockSpec((1, 1, w), lambda i: (i, 0, 0))
    full = lambda a: pl.BlockSpec(a.shape, lambda i: (0,) * a.ndim)
    per_seq = lambda *dims: pl.BlockSpec((1,) + dims, lambda i: (i,) + (0,) * len(dims))
    r3 = lambda a: a.reshape(b, 1, -1)
    consts = [gdn_conv_w, ssm_conv_w, ssm_conv_b.reshape(1, -1), gpar, spar,
              gdn_norm_w.reshape(1, HEAD_DIM), ssm_norm_w.reshape(1, GROUP_WIDTH)]
    state_specs = [per_seq(CONV_W - 1, GDN_CONV_DIM), per_seq(CONV_W - 1, SSM_CONV_DIM),
                   per_seq(N_HEADS, HEAD_DIM, HEAD_DIM), per_seq(N_HEADS, HEAD_DIM, SSM_STATE)]
    go, so, gc, sc, s, h = pl.pallas_call(
        _rec_step_kernel,
        grid=(b,),
        in_specs=[row(GDN_CONV_DIM), row(GROUP_WIDTH), row(SSM_CONV_DIM), row(GROUP_WIDTH), row(LANE)]
                 + state_specs + [full(c) for c in consts],
        out_specs=[row(GROUP_WIDTH), row(GROUP_WIDTH)] + state_specs,
        out_shape=[jax.ShapeDtypeStruct((b, 1, GROUP_WIDTH), F32), jax.ShapeDtypeStruct((b, 1, GROUP_WIDTH), F32),
                   jax.ShapeDtypeStruct((b, CONV_W - 1, GDN_CONV_DIM), F32),
                   jax.ShapeDtypeStruct((b, CONV_W - 1, SSM_CONV_DIM), F32),
                   jax.ShapeDtypeStruct((b, N_HEADS, HEAD_DIM, HEAD_DIM), F32),
                   jax.ShapeDtypeStruct((b, N_HEADS, HEAD_DIM, SSM_STATE), F32)],
        compiler_params=pltpu.CompilerParams(dimension_semantics=("arbitrary",), vmem_limit_bytes=VMEM_LIMIT),
        name="rec_step",
    )(r3(seg["gdn_qkv"]), r3(seg["gdn_z"]), r3(seg["ssm_xbc"]), r3(seg["ssm_z"]), r3(seg["small"]),
      gconv0, sconv0, s0, h0, *consts)
    return go.reshape(b, GROUP_WIDTH), so.reshape(b, GROUP_WIDTH), gc, sc, s, h


PAGES_PER_STEP = 8


def _decode_pass_kernel(pt_ref, sq_ref, mq_ref, *refs, n_steps):
    pp = PAGES_PER_STEP
    sk_refs, sv_refs, mk_refs = refs[0:pp], refs[pp:2 * pp], refs[2 * pp:3 * pp]
    sb_out_ref, gp_ref, accv_ref, carry_ref = refs[3 * pp:]
    p = pl.program_id(1)

    @pl.when(p == 0)
    def _():
        accv_ref[...] = jnp.zeros_like(accv_ref)
        carry_ref[...] = jnp.zeros_like(carry_ref)

    qs = sq_ref[0] * (HEAD_DIM ** -0.5)
    qm = mq_ref[0] * (1.0 / MOBA_BLOCK)
    row = lax.broadcasted_iota(I32, (PAGE_SIZE, PAGE_SIZE), 0)
    col = lax.broadcasted_iota(I32, (PAGE_SIZE, PAGE_SIZE), 1)
    upper = (row > col).astype(BF16)
    sub8 = lax.broadcasted_iota(I32, (8, LANE), 0)
    diag8 = sub8 == lax.broadcasted_iota(I32, (8, LANE), 1)
    heads = [slice(h * HEAD_DIM, (h + 1) * HEAD_DIM) for h in range(N_HEADS)]

    def head_rows(x):
        out = jnp.zeros((8, LANE), F32)
        for h, hs in enumerate(heads):
            out = jnp.where(sub8 == h, jnp.sum(x[hs, :], axis=0, keepdims=True), out)
        return out

    zs = [head_rows(sk_refs[i][0, 0] * qs) for i in range(pp)]
    ls = [-_softplus(z) for z in zs]
    las = _split_dot(jnp.concatenate(ls, axis=0), upper)
    accv = [accv_ref[hs, :] for hs in heads]
    carry = carry_ref[...]
    for i in range(pp):
        la = las[i * 8:(i + 1) * 8, :]
        w = jnp.exp(zs[i] + ls[i] + la + carry)
        vt = sv_refs[i][0, 0]
        accv = [a + vt[hs, :] * w[h:h + 1, :] for h, (hs, a) in enumerate(zip(heads, accv))]
        carry = carry + la[:, 0:1] + ls[i][:, 0:1]
        gp = jnp.sum(head_rows(mk_refs[i][0, 0] * qm), axis=-1, keepdims=True)
        gp_ref[0, pl.ds(p * pp + i, 1), :] = jnp.sum(jnp.where(diag8, gp, 0.0), axis=0, keepdims=True)
    for hs, a in zip(heads, accv):
        accv_ref[hs, :] = a
    carry_ref[...] = carry

    @pl.when(p == n_steps - 1)
    def _():
        sb_out_ref[0] = jnp.sum(accv_ref[...], axis=-1, keepdims=True)


def _decode_pass(layer, page_table, sq, mq, cache_sb_k, cache_sb_v, cache_moba_k):
    b, n_pages = page_table.shape
    pp = PAGES_PER_STEP
    assert n_pages % pp == 0
    n_steps = n_pages // pp

    def page_spec(i):
        return pl.BlockSpec((1, 1, GROUP_WIDTH, PAGE_SIZE),
                            lambda bi, p, pt: (layer, pt[bi, n_pages - 1 - (p * pp + i)], 0, 0))

    colspec = pl.BlockSpec((1, GROUP_WIDTH, 1), lambda bi, p, pt: (bi, 0, 0))
    grid_spec = pltpu.PrefetchScalarGridSpec(
        num_scalar_prefetch=1,
        grid=(b, n_steps),
        in_specs=[colspec, colspec] + [page_spec(i) for i in range(pp)] * 3,
        out_specs=[colspec, pl.BlockSpec((1, n_pages, LANE), lambda bi, p, pt: (bi, 0, 0))],
        scratch_shapes=[pltpu.VMEM((GROUP_WIDTH, PAGE_SIZE), F32), pltpu.VMEM((8, 1), F32)])
    sb_out, gp = pl.pallas_call(
        functools.partial(_decode_pass_kernel, n_steps=n_steps),
        grid_spec=grid_spec,
        out_shape=[jax.ShapeDtypeStruct((b, GROUP_WIDTH, 1), F32), jax.ShapeDtypeStruct((b, n_pages, LANE), F32)],
        compiler_params=pltpu.CompilerParams(dimension_semantics=("arbitrary", "arbitrary"), vmem_limit_bytes=VMEM_LIMIT),
        name="decode_pass",
    )(page_table, sq.reshape(b, GROUP_WIDTH, 1), mq.reshape(b, GROUP_WIDTH, 1),
      *([cache_sb_k] * pp), *([cache_sb_v] * pp), *([cache_moba_k] * pp))
    return sb_out.reshape(b, GROUP_WIDTH), gp


def _moba_decode_kernel(pg_ref, mq_ref, mk_ref, mv_ref, *refs, n_sel):
    kp_refs, vp_refs, o_ref = refs[:n_sel], refs[n_sel:2 * n_sel], refs[2 * n_sel]
    h = pl.program_id(1)
    lane = lax.broadcasted_iota(I32, (1, GROUP_WIDTH), 1)
    in_head = lane // HEAD_DIM == h
    q = jnp.where(in_head, mq_ref[0] * (HEAD_DIM ** -0.5), 0.0)

    @pl.when(h == 0)
    def _():
        o_ref[...] = jnp.zeros_like(o_ref)

    q8 = jnp.where(lax.broadcasted_iota(I32, (8, GROUP_WIDTH), 0) == 0, q, 0.0).astype(BF16)
    s_self = jnp.sum(q * mk_ref[0], axis=-1, keepdims=True)
    scores = [jnp.dot(q8, kp[0, 0].astype(BF16), preferred_element_type=F32)[0:1, :] for kp in kp_refs]
    m = s_self
    for s in scores:
        m = jnp.maximum(m, jnp.max(s, axis=-1, keepdims=True))
    p_self = jnp.exp(s_self - m)
    l = p_self
    acc = p_self * mv_ref[0]
    first_row = lax.broadcasted_iota(I32, (8, PAGE_SIZE), 0) == 0
    for s, vp in zip(scores, vp_refs):
        p = jnp.exp(s - m)
        l = l + jnp.sum(p, axis=-1, keepdims=True)
        p8 = jnp.where(first_row, p, 0.0).astype(BF16)
        acc = acc + lax.dot_general(p8, vp[0, 0].astype(BF16), (((1,), (1,)), ((), ())),
                                    preferred_element_type=F32)[0:1, :]
    o_ref[0] = jnp.where(in_head, acc / l, o_ref[0])


def _moba_decode(layer, sel_pages, mq, mk_new, mv_new, cache_moba_k, cache_moba_v):
    b, nh, n_sel = sel_pages.shape
    row = pl.BlockSpec((1, 1, GROUP_WIDTH), lambda bi, h, pg: (bi, 0, 0))

    def page(j):
        return pl.BlockSpec((1, 1, GROUP_WIDTH, PAGE_SIZE),
                            lambda bi, h, pg: (layer, pg[(bi * nh + h) * n_sel + j], 0, 0))

    grid_spec = pltpu.PrefetchScalarGridSpec(
        num_scalar_prefetch=1,
        grid=(b, nh),
        in_specs=[row, row, row] + [page(j) for j in range(n_sel)] * 2,
        out_specs=pl.BlockSpec((1, 1, GROUP_WIDTH), lambda bi, h, pg: (bi, 0, 0)))
    r3 = lambda a: a.reshape(b, 1, GROUP_WIDTH)
    out = pl.pallas_call(
        functools.partial(_moba_decode_kernel, n_sel=n_sel),
        grid_spec=grid_spec,
        out_shape=jax.ShapeDtypeStruct((b, 1, GROUP_WIDTH), F32),
        compiler_params=pltpu.CompilerParams(dimension_semantics=("arbitrary",) * 2, vmem_limit_bytes=VMEM_LIMIT),
        name="moba_decode",
    )(sel_pages.reshape(-1).astype(I32), r3(mq), r3(mk_new), r3(mv_new),
      *([cache_moba_k] * n_sel), *([cache_moba_v] * n_sel))
    return out.reshape(b, GROUP_WIDTH)


def _sample_attention(layer, page_table, seg, caches):
    cache_moba_k, cache_moba_v, cache_sb_k, cache_sb_v = caches
    b, n_pages = page_table.shape
    ppb = MOBA_BLOCK // PAGE_SIZE
    n_past_blocks = n_pages // ppb
    assert n_pages % ppb == 0 and n_past_blocks >= MOBA_TOPK
    sb_out, gp = _decode_pass(layer, page_table, seg["sb_q"], seg["moba_q"], cache_sb_k, cache_sb_v, cache_moba_k)
    gates = gp[:, ::-1, :N_HEADS].reshape(b, n_past_blocks, ppb, N_HEADS).sum(axis=2)
    _, sel = lax.top_k(jnp.swapaxes(gates, 1, 2), MOBA_TOPK)
    pages = sel[..., None] * ppb + jnp.arange(ppb)
    sel_pages = jnp.take_along_axis(page_table[:, None, :], pages.reshape(b, N_HEADS, -1), axis=2)
    moba_out = _moba_decode(layer, sel_pages, seg["moba_q"], seg["moba_k"], seg["moba_v"], cache_moba_k, cache_moba_v)
    return moba_out, sb_out


def _kv_rows(seg, bsz, length):
    return tuple(seg[name].reshape(bsz, length, N_HEADS, HEAD_DIM) for name in ("moba_k", "moba_v", "sb_k", "sb_v"))


def _prompt_mixers(x, lw):
    (w_in, gdn_conv_w, gdn_a_log, gdn_dt_bias, gdn_norm_w, ssm_conv_w, ssm_conv_b,
     ssm_a_log, ssm_dt_bias, ssm_d, ssm_norm_w, _) = lw
    bsz, length, _ = x.shape
    n = bsz * length
    flat = _in_proj(x.reshape(n, D_MODEL), w_in, jnp.arange(length), length)
    seg = {k: v.reshape(bsz, length, -1) for k, v in flat.items()}

    gdn_out, gdn_conv_new, gdn_s_new = _gdn_prompt(seg["gdn_qkv"], seg["gdn_z"], seg["small"],
                                                   gdn_conv_w, gdn_a_log, gdn_dt_bias, gdn_norm_w)
    ssm_out, ssm_conv_new, ssm_h_new = _ssd_prompt(seg["ssm_xbc"], seg["ssm_z"], seg["small"], ssm_conv_w, ssm_conv_b,
                                                   ssm_a_log, ssm_dt_bias, ssm_d, ssm_norm_w)
    moba_out = _moba_attention(seg["moba_q"], seg["moba_k"], seg["moba_v"])
    sb_out = _sb_attention(seg["sb_q"], seg["sb_k"], seg["sb_v"])

    outs = [o.reshape(n, GROUP_WIDTH) for o in (gdn_out, ssm_out, moba_out, sb_out)]
    return outs, _kv_rows(seg, bsz, length) + (gdn_conv_new, gdn_s_new, ssm_conv_new, ssm_h_new)


def _sample_mixers(x, layer, page_table, rec_state, caches, lw):
    gdn_conv0, gdn_s0, ssm_conv0, ssm_h0 = rec_state
    bsz, length, _ = x.shape
    assert length == 1
    pos0 = page_table.shape[1] * PAGE_SIZE
    seg = _in_proj(x.reshape(bsz, D_MODEL), lw[0], jnp.full((1,), pos0, I32), 1)
    gdn_out, ssm_out, gdn_conv_new, ssm_conv_new, gdn_s_new, ssm_h_new = _rec_step(
        seg, gdn_conv0, gdn_s0, ssm_conv0, ssm_h0, lw)
    moba_out, sb_out = _sample_attention(layer, page_table, seg, caches)
    outs = [gdn_out, ssm_out, moba_out, sb_out]
    return outs, _kv_rows(seg, bsz, length) + (gdn_conv_new, gdn_s_new, ssm_conv_new, ssm_h_new)


def kernel(x_prompt, x_sample, cache_moba_k, cache_moba_v, cache_sb_k, cache_sb_v,
           state_gdn_conv, state_gdn_rec, state_ssm_conv, state_ssm_rec, page_table,
           w_in, gdn_conv_w, gdn_a_log, gdn_dt_bias, gdn_norm_w,
           ssm_conv_w, ssm_conv_b, ssm_a_log, ssm_dt_bias, ssm_d, ssm_norm_w,
           w_out, ln1_g, ln1_b, router_w, router_b,
           expert_w_gu, expert_b_gu, expert_w_down, expert_b_down, ln2_g, ln2_b):
    caches = tuple(jnp.transpose(c, (0, 1, 3, 4, 2)).reshape(c.shape[0], c.shape[1], GROUP_WIDTH, PAGE_SIZE)
                   for c in (cache_moba_k, cache_moba_v, cache_sb_k, cache_sb_v))
    yp, ys = x_prompt, x_sample
    new_p, new_s = [], []
    for l in range(DEPTH):
        lw = (w_in[l], gdn_conv_w[l], gdn_a_log[l], gdn_dt_bias[l], gdn_norm_w[l], ssm_conv_w[l], ssm_conv_b[l],
              ssm_a_log[l], ssm_dt_bias[l], ssm_d[l], ssm_norm_w[l], w_out[l])
        fw = (ln1_g[l], ln1_b[l], router_w[l], router_b[l], expert_w_gu, expert_b_gu,
              expert_w_down, expert_b_down, ln2_g[l], ln2_b[l])
        rec_state = (state_gdn_conv[l], state_gdn_rec[l], state_ssm_conv[l], state_ssm_rec[l])
        outs_p, st_p = _prompt_mixers(yp, lw)
        yp = _post_mix_moe(l, outs_p, yp.reshape(-1, D_MODEL), w_out[l], fw).reshape(yp.shape)
        outs_s, st_s = _sample_mixers(ys, l, page_table, rec_state, caches, lw)
        ys = _post_mix_moe(l, outs_s, ys.reshape(-1, D_MODEL), w_out[l], fw).reshape(ys.shape)
        new_p.append(st_p)
        new_s.append(st_s)

    def stk(states, i):
        return jnp.stack([s[i] for s in states])

    return (yp, ys,
            stk(new_p, 0), stk(new_p, 1), stk(new_p, 2), stk(new_p, 3),
            stk(new_p, 4), stk(new_p, 5), stk(new_p, 6), stk(new_p, 7),
            stk(new_s, 0), stk(new_s, 1), stk(new_s, 2), stk(new_s, 3),
            stk(new_s, 4), stk(new_s, 5), stk(new_s, 6), stk(new_s, 7))
```

```python
import functools

import jax
import jax.numpy as jnp
from jax import lax
from jax.experimental import pallas as pl
from jax.experimental.pallas import tpu as pltpu

D_MODEL = 1024
DEPTH = 2
PAGE_SIZE = 128
HEAD_DIM = 64
N_MIXERS = 4
GROUP_WIDTH = D_MODEL // N_MIXERS
N_HEADS = GROUP_WIDTH // HEAD_DIM
CONV_W = 4
CHUNK = 64
GDN_CONV_DIM = 3 * GROUP_WIDTH
SSM_STATE = 128
SSM_NGROUPS = 2
SSM_CONV_DIM = GROUP_WIDTH + 2 * SSM_NGROUPS * SSM_STATE
MOBA_BLOCK = 256
MOBA_TOPK = 3
ROPE_THETA = 500000.0
ROPE_DIM = HEAD_DIM // 4
N_EXPERTS = 32
TOP_K = 4
D_FF = D_MODEL
SWIGLU_LIMIT = 7.0
SWIGLU_ALPHA = 1.702
EXPERT_BLOCK = 256
DN_ALPHA = (2 * DEPTH) ** 0.25
LN_EPS = 1e-5
RMS_EPS = 1e-6

F32 = jnp.float32
BF16 = jnp.bfloat16
I32 = jnp.int32
HI = lax.Precision.HIGHEST

LANE = 128
VMEM_LIMIT = 56 * 1024 * 1024
NEG = -1e30
MOE_ROW_WINDOW = 128
GDN_PRECISE_SQUARINGS = 2

_SEG_SIZES = (GDN_CONV_DIM, GROUP_WIDTH, N_HEADS, N_HEADS, SSM_CONV_DIM, GROUP_WIDTH, N_HEADS, GDN_CONV_DIM, GDN_CONV_DIM)
_SEG_NAMES = ("gdn_qkv", "gdn_z", "gdn_b", "gdn_a", "ssm_xbc", "ssm_z", "ssm_dt", "moba_qkv", "sb_qkv")
_OUT_SEGS = (("gdn_qkv", GDN_CONV_DIM), ("gdn_z", GROUP_WIDTH), ("ssm_xbc", SSM_CONV_DIM), ("ssm_z", GROUP_WIDTH),
             ("moba_q", GROUP_WIDTH), ("moba_k", GROUP_WIDTH), ("moba_v", GROUP_WIDTH),
             ("sb_q", GROUP_WIDTH), ("sb_k", GROUP_WIDTH), ("sb_v", GROUP_WIDTH), ("small", LANE))
_ROTARY_SEGS = ("moba_q", "moba_k")


def _softplus(z):
    return jnp.maximum(z, 0.0) + jnp.log1p(jnp.exp(-jnp.abs(z)))


def _sigmoid(x):
    return 1.0 / (1.0 + jnp.exp(-x))


def _silu(x):
    return x * _sigmoid(x)


def _dot(a, b):
    return jnp.dot(a.astype(BF16), b.astype(BF16), preferred_element_type=F32)


def _dot_nt(a, b):
    return lax.dot_general(a.astype(BF16), b.astype(BF16), (((1,), (1,)), ((), ())), preferred_element_type=F32)


def _dot_tn(a, b):
    return lax.dot_general(a.astype(BF16), b.astype(BF16), (((0,), (0,)), ((), ())), preferred_element_type=F32)


def _dot_hi(a, b):
    return jnp.dot(a, b, preferred_element_type=F32, precision=HI)


def _split_dot(x, m_bf16):
    hi = x.astype(BF16)
    lo = (x - hi.astype(F32)).astype(BF16)
    return (jnp.dot(hi, m_bf16, preferred_element_type=F32)
            + jnp.dot(lo, m_bf16, preferred_element_type=F32))


def _split3(x):
    hi = x.astype(BF16)
    r = x - hi.astype(F32)
    mid = r.astype(BF16)
    return hi, mid, (r - mid.astype(F32)).astype(BF16)


def _dot3(a, b):
    a_hi = a.astype(BF16)
    a_lo = (a - a_hi.astype(F32)).astype(BF16)
    b_hi = b.astype(BF16)
    b_lo = (b - b_hi.astype(F32)).astype(BF16)
    return (jnp.dot(a_hi, b_hi, preferred_element_type=F32) + jnp.dot(a_hi, b_lo, preferred_element_type=F32)
            + jnp.dot(a_lo, b_hi, preferred_element_type=F32))


def _ln(h, g, b):
    mu = jnp.mean(h, axis=-1, keepdims=True)
    d = h - mu
    var = jnp.mean(d * d, axis=-1, keepdims=True)
    return d * lax.rsqrt(var + LN_EPS) * g + b


def _in_proj_kernel(x_ref, w_ref, cos_ref, sin_ref, *o_refs):
    y = jnp.dot(x_ref[...].astype(BF16), w_ref[...], preferred_element_type=F32)
    cos = jnp.concatenate([cos_ref[...]] * (GROUP_WIDTH // LANE), axis=1)
    sin = jnp.concatenate([sin_ref[...]] * (GROUP_WIDTH // LANE), axis=1)
    first_half = (lax.broadcasted_iota(I32, (1, GROUP_WIDTH), 1) % HEAD_DIM) < ROPE_DIM // 2
    off = 0
    for (name, width), o_ref in zip(_OUT_SEGS, o_refs):
        seg = y[:, off:off + width]
        if name in _ROTARY_SEGS:
            partner = jnp.where(first_half, pltpu.roll(seg, GROUP_WIDTH - ROPE_DIM // 2, 1), pltpu.roll(seg, ROPE_DIM // 2, 1))
            seg = seg * cos + partner * sin
        o_ref[...] = seg
        off += width


def _rotary_tables(pos):
    half = ROPE_DIM // 2
    inv = ROPE_THETA ** (-jnp.arange(half, dtype=F32) * 2.0 / ROPE_DIM)
    ang = pos.astype(F32)[:, None] * inv
    cc = jnp.arange(LANE) % HEAD_DIM
    cos = jnp.where(cc < ROPE_DIM, jnp.cos(ang)[:, cc % half], 1.0)
    sin = jnp.sin(ang)[:, cc % half]
    sin = jnp.where(cc < half, -sin, jnp.where(cc < ROPE_DIM, sin, 0.0))
    return cos, sin


def _in_proj(x, w_in, pos, seq_len):
    m = x.shape[0]
    tm = min(256, m)
    w_in = w_in.astype(BF16)
    cols, off = {}, 0
    for name, sz in zip(_SEG_NAMES, _SEG_SIZES):
        cols[name] = w_in[:, off:off + sz]
        off += sz
    for name in ("moba", "sb"):
        for j, part in enumerate("qkv"):
            cols[f"{name}_{part}"] = cols[f"{name}_qkv"][:, j * GROUP_WIDTH:(j + 1) * GROUP_WIDTH]
    cols["small"] = jnp.concatenate([cols["gdn_b"], cols["gdn_a"], cols["ssm_dt"],
                                     jnp.zeros((D_MODEL, LANE - 3 * N_HEADS), w_in.dtype)], axis=1)
    w = jnp.concatenate([cols[name] for name, _ in _OUT_SEGS], axis=1)
    n = w.shape[1]
    cos, sin = _rotary_tables(pos)
    if seq_len == 1:
        tab_spec = pl.BlockSpec((1, LANE), lambda i: (0, 0))
    else:
        assert seq_len % tm == 0
        tab_spec = pl.BlockSpec((tm, LANE), lambda i: (i % (seq_len // tm), 0))
    outs = pl.pallas_call(
        _in_proj_kernel,
        grid=(m // tm,),
        in_specs=[pl.BlockSpec((tm, D_MODEL), lambda i: (i, 0)),
                  pl.BlockSpec((D_MODEL, n), lambda i: (0, 0)), tab_spec, tab_spec],
        out_specs=[pl.BlockSpec((tm, width), lambda i: (i, 0)) for _, width in _OUT_SEGS],
        out_shape=[jax.ShapeDtypeStruct((m, width), F32) for _, width in _OUT_SEGS],
        compiler_params=pltpu.CompilerParams(dimension_semantics=("arbitrary",), vmem_limit_bytes=VMEM_LIMIT),
        name="in_proj",
    )(x, w, cos, sin)
    return {name: o for (name, _), o in zip(_OUT_SEGS, outs)}


def _sb_kernel(q_ref, k_ref, v_ref, o_ref, *, tq):
    i = pl.program_id(1)
    row = lax.broadcasted_iota(I32, (tq, tq), 0)
    col = lax.broadcasted_iota(I32, (tq, tq), 1)
    upper = (row > col).astype(BF16)
    past = col < row

    heads = [slice(h * HEAD_DIM, (h + 1) * HEAD_DIM) for h in range(N_HEADS)]
    qs = [(q_ref[0, :, hs] * (HEAD_DIM ** -0.5)).astype(BF16) for hs in heads]

    def block(j, state, diag):
        out = []
        for hs, q, (carry, acc) in zip(heads, qs, state):
            kj = k_ref[0, pl.ds(j * tq, tq), hs].astype(BF16)
            vj = v_ref[0, pl.ds(j * tq, tq), hs].astype(BF16)
            z = lax.dot_general(q, kj, (((1,), (1,)), ((), ())), preferred_element_type=F32)
            ls = -_softplus(z)
            lk = jnp.where(past, ls, 0.0) if diag else ls
            la = _split_dot(lk, upper)
            w = jnp.exp(z + ls + la + carry)
            if diag:
                w = jnp.where(past, w, 0.0)
            acc = acc + jnp.dot(w.astype(BF16), vj, preferred_element_type=F32)
            out.append((carry + la[:, 0:1] + lk[:, 0:1], acc))
        return tuple(out)

    state = tuple((jnp.zeros((tq, 1), F32), jnp.zeros((tq, HEAD_DIM), F32)) for _ in heads)
    state = block(i, state, True)
    state = lax.fori_loop(0, i, lambda t, c: block(i - 1 - t, c, False), state)
    for hs, (_, acc) in zip(heads, state):
        o_ref[0, :, hs] = acc


def _sb_attention(q, k, v, tq=256):
    b, l, d = q.shape
    assert l % tq == 0
    return pl.pallas_call(
        functools.partial(_sb_kernel, tq=tq),
        grid=(b, l // tq),
        in_specs=[pl.BlockSpec((1, tq, d), lambda bi, i: (bi, i, 0)),
                  pl.BlockSpec((1, l, d), lambda bi, i: (bi, 0, 0)),
                  pl.BlockSpec((1, l, d), lambda bi, i: (bi, 0, 0))],
        out_specs=pl.BlockSpec((1, tq, d), lambda bi, i: (bi, i, 0)),
        out_shape=jax.ShapeDtypeStruct((b, l, d), F32),
        compiler_params=pltpu.CompilerParams(dimension_semantics=("arbitrary",) * 2, vmem_limit_bytes=VMEM_LIMIT),
        name="sb_attention",
    )(q, k, v)


def _moba_kernel(q_ref, k_ref, v_ref, o_ref, kmean_ref, *, nb):
    t = MOBA_BLOCK
    i = pl.program_id(1)

    @pl.when(i == 0)
    def _():
        for n in range(nb):
            kmean_ref[n:n + 1, :] = jnp.mean(k_ref[0, n * t:(n + 1) * t, :], axis=0, keepdims=True)

    blk_id = lax.broadcasted_iota(I32, (nb, t), 0)
    row = lax.broadcasted_iota(I32, (t, t), 0)
    col = lax.broadcasted_iota(I32, (t, t), 1)

    heads = [slice(h * HEAD_DIM, (h + 1) * HEAD_DIM) for h in range(N_HEADS)]
    qs, sels, state = [], [], []
    for hs in heads:
        qf = q_ref[0, :, hs]
        q = (qf * (HEAD_DIM ** -0.5)).astype(BF16)
        gate = lax.dot_general(kmean_ref[:, hs], qf, (((1,), (1,)), ((), ())),
                               preferred_element_type=F32, precision=HI)
        cnt = jnp.zeros((nb, t), F32)
        for n2 in range(nb):
            cn = gate[n2:n2 + 1, :]
            beats = (cn > gate) | ((cn == gate) & (n2 < blk_id))
            cnt = cnt + jnp.where(beats & (n2 < i), 1.0, 0.0)
        sels.append(jnp.where((cnt < MOBA_TOPK) & (blk_id < i), 1.0, 0.0).astype(BF16))
        qs.append(q)

        ki = k_ref[0, pl.ds(i * t, t), hs].astype(BF16)
        vi = v_ref[0, pl.ds(i * t, t), hs].astype(BF16)
        s = lax.dot_general(q, ki, (((1,), (1,)), ((), ())), preferred_element_type=F32)
        s = jnp.where(col <= row, s, NEG)
        m = jnp.max(s, axis=-1, keepdims=True)
        p = jnp.exp(s - m)
        state.append((m, jnp.sum(p, axis=-1, keepdims=True), jnp.dot(p.astype(BF16), vi, preferred_element_type=F32)))

    def body(j, state):
        pick = ((lax.broadcasted_iota(I32, (nb, 8), 0) == j) & (lax.broadcasted_iota(I32, (nb, 8), 1) == 0)).astype(BF16)
        out = []
        for hs, q, sel, (m, l, acc) in zip(heads, qs, sels, state):
            kj = k_ref[0, pl.ds(j * t, t), hs].astype(BF16)
            vj = v_ref[0, pl.ds(j * t, t), hs].astype(BF16)
            s = lax.dot_general(q, kj, (((1,), (1,)), ((), ())), preferred_element_type=F32)
            sj = lax.dot_general(sel, pick, (((0,), (0,)), ((), ())), preferred_element_type=F32)[:, 0:1]
            s = jnp.where(sj > 0.5, s, NEG)
            m_new = jnp.maximum(m, jnp.max(s, axis=-1, keepdims=True))
            a = jnp.exp(m - m_new)
            p = jnp.exp(s - m_new)
            out.append((m_new, a * l + jnp.sum(p, axis=-1, keepdims=True),
                        a * acc + jnp.dot(p.astype(BF16), vj, preferred_element_type=F32)))
        return tuple(out)

    state = lax.fori_loop(0, i, body, tuple(state))
    for hs, (_, l, acc) in zip(heads, state):
        o_ref[0, :, hs] = acc / l


def _moba_attention(q, k, v):
    b, l, d = q.shape
    t = MOBA_BLOCK
    assert l % t == 0
    nb = l // t
    return pl.pallas_call(
        functools.partial(_moba_kernel, nb=nb),
        grid=(b, nb),
        in_specs=[pl.BlockSpec((1, t, d), lambda bi, i: (bi, i, 0)),
                  pl.BlockSpec((1, l, d), lambda bi, i: (bi, 0, 0)),
                  pl.BlockSpec((1, l, d), lambda bi, i: (bi, 0, 0))],
        out_specs=pl.BlockSpec((1, t, d), lambda bi, i: (bi, i, 0)),
        out_shape=jax.ShapeDtypeStruct((b, l, d), F32),
        scratch_shapes=[pltpu.VMEM((nb, d), F32)],
        compiler_params=pltpu.CompilerParams(dimension_semantics=("arbitrary",) * 2, vmem_limit_bytes=VMEM_LIMIT),
        name="moba_attention",
    )(q, k, v)


def _conv_tile(ext_ref, u, w_ref, tl):
    ext_ref[8:8 + tl, :] = u
    y = ext_ref[5:5 + tl, :] * w_ref[0:1, :]
    for i in range(1, CONV_W):
        y = y + ext_ref[5 + i:5 + i + tl, :] * w_ref[i:i + 1, :]
    tail = ext_ref[tl + 5:tl + 8, :]
    ext_ref[5:8, :] = tail
    return y, tail


def _gdn_kernel(qkv_ref, z_ref, small_ref, convw_ref, par_ref, normw_ref,
                o_ref, conv_out_ref, s_out_ref, ext_ref, s_ref, *, tl):
    c = CHUNK
    step = pl.program_id(1)

    @pl.when(step == 0)
    def _():
        ext_ref[...] = jnp.zeros_like(ext_ref)
        s_ref[...] = jnp.zeros_like(s_ref)

    y, tail = _conv_tile(ext_ref, qkv_ref[0], convw_ref, tl)
    conv_out_ref[0] = tail
    y = _silu(y)
    small = small_ref[0]
    beta_all = _sigmoid(small[:, 0:N_HEADS])
    g_all = par_ref[0:1, :] * _softplus(small[:, N_HEADS:2 * N_HEADS] + par_ref[1:2, :])

    row = lax.broadcasted_iota(I32, (tl, tl), 0)
    col = lax.broadcasted_iota(I32, (tl, tl), 1)
    same = (row // c) == (col // c)
    tril = same & (row >= col)
    strict = same & (row > col)
    tril_b = tril.astype(BF16)
    triu_b = (same & (row <= col)).astype(BF16)
    eye = (row == col).astype(F32)

    g8 = jnp.concatenate([g_all, jnp.zeros((tl, 8 - N_HEADS), F32)], axis=1)
    g_parts = _split3(g8)
    gc_cols = sum(jnp.dot(tril_b, part, preferred_element_type=F32) for part in g_parts)
    gc_rows = sum(lax.dot_general(part, triu_b, (((0,), (0,)), ((), ())), preferred_element_type=F32) for part in g_parts)

    for h in range(N_HEADS):
        hs = slice(h * HEAD_DIM, (h + 1) * HEAD_DIM)
        q = y[:, h * HEAD_DIM:(h + 1) * HEAD_DIM]
        k = y[:, GROUP_WIDTH + h * HEAD_DIM:GROUP_WIDTH + (h + 1) * HEAD_DIM]
        v = y[:, 2 * GROUP_WIDTH + h * HEAD_DIM:2 * GROUP_WIDTH + (h + 1) * HEAD_DIM]
        q = q * lax.rsqrt(jnp.sum(q * q, axis=-1, keepdims=True) + RMS_EPS) * (HEAD_DIM ** -0.5)
        k = k * lax.rsqrt(jnp.sum(k * k, axis=-1, keepdims=True) + RMS_EPS)
        beta = beta_all[:, h:h + 1]
        gcol = gc_cols[:, h:h + 1]
        grow = gc_rows[h:h + 1, :]
        decay = jnp.where(tril, jnp.exp(jnp.where(tril, gcol - grow, 0.0)), 0.0)
        kb = k * beta
        a = jnp.where(strict, _dot_nt(kb, k) * decay, 0.0)
        p = -a
        tinv = eye + p
        for m in range(5):
            mm = _dot3 if m < GDN_PRECISE_SQUARINGS else _dot
            p = mm(p, p)
            tinv = tinv + mm(tinv, p)
        u = _dot(tinv, v * beta)
        w = _dot(tinv, kb * jnp.exp(gcol))
        attn = jnp.where(tril, _dot_nt(q, k) * decay, 0.0)
        qe = q * jnp.exp(gcol)
        s = s_ref[h]
        for ci in range(tl // c):
            rs = slice(ci * c, (ci + 1) * c)
            g_last = gcol[(ci + 1) * c - 1:(ci + 1) * c, :]
            v_new = u[rs] - _dot(w[rs], s)
            o = _dot(qe[rs], s) + _dot(attn[rs, rs], v_new)
            s = s * jnp.exp(g_last) + _dot_tn(k[rs] * jnp.exp(g_last - gcol[rs]), v_new)
            o = o * lax.rsqrt(jnp.mean(o * o, axis=-1, keepdims=True) + RMS_EPS) * normw_ref[...] * _silu(z_ref[0, rs, hs])
            o_ref[0, rs, hs] = o
        s_ref[h] = s

    @pl.when(step == pl.num_programs(1) - 1)
    def _():
        s_out_ref[0] = s_ref[...]


def _gdn_prompt(qkv, z, small, conv_w, a_log, dt_bias, norm_w, tl=256):
    b, l, _ = qkv.shape
    assert l % tl == 0 and tl % CHUNK == 0
    par = jnp.stack([-jnp.exp(a_log.astype(F32)), dt_bias.astype(F32)])
    return pl.pallas_call(
        functools.partial(_gdn_kernel, tl=tl),
        grid=(b, l // tl),
        in_specs=[pl.BlockSpec((1, tl, GDN_CONV_DIM), lambda bi, i: (bi, i, 0)),
                  pl.BlockSpec((1, tl, GROUP_WIDTH), lambda bi, i: (bi, i, 0)),
                  pl.BlockSpec((1, tl, LANE), lambda bi, i: (bi, i, 0)),
                  pl.BlockSpec((CONV_W, GDN_CONV_DIM), lambda bi, i: (0, 0)),
                  pl.BlockSpec((2, N_HEADS), lambda bi, i: (0, 0)),
                  pl.BlockSpec((1, HEAD_DIM), lambda bi, i: (0, 0))],
        out_specs=[pl.BlockSpec((1, tl, GROUP_WIDTH), lambda bi, i: (bi, i, 0)),
                   pl.BlockSpec((1, CONV_W - 1, GDN_CONV_DIM), lambda bi, i: (bi, 0, 0)),
                   pl.BlockSpec((1, N_HEADS, HEAD_DIM, HEAD_DIM), lambda bi, i: (bi, 0, 0, 0))],
        out_shape=[jax.ShapeDtypeStruct((b, l, GROUP_WIDTH), F32),
                   jax.ShapeDtypeStruct((b, CONV_W - 1, GDN_CONV_DIM), F32),
                   jax.ShapeDtypeStruct((b, N_HEADS, HEAD_DIM, HEAD_DIM), F32)],
        scratch_shapes=[pltpu.VMEM((tl + 8, GDN_CONV_DIM), F32),
                        pltpu.VMEM((N_HEADS, HEAD_DIM, HEAD_DIM), F32)],
        compiler_params=pltpu.CompilerParams(dimension_semantics=("arbitrary", "arbitrary"),
                                             vmem_limit_bytes=VMEM_LIMIT),
        name="gdn_prompt",
    )(qkv, z, small, conv_w, par, norm_w.reshape(1, HEAD_DIM))


def _ssd_kernel(xbc_ref, z_ref, small_ref, convw_ref, convb_ref, par_ref, normw_ref,
                o_ref, conv_out_ref, h_out_ref, ext_ref, h_ref, *, tl):
    c = CHUNK
    step = pl.program_id(1)
    hpg = N_HEADS // SSM_NGROUPS

    @pl.when(step == 0)
    def _():
        ext_ref[...] = jnp.zeros_like(ext_ref)
        h_ref[...] = jnp.zeros_like(h_ref)

    y, tail = _conv_tile(ext_ref, xbc_ref[0], convw_ref, tl)
    conv_out_ref[0] = tail
    xbc = _silu(y + convb_ref[...])
    dt_all = _softplus(small_ref[0][:, 2 * N_HEADS:3 * N_HEADS] + par_ref[1:2, :])
    ad_all = dt_all * par_ref[0:1, :]

    tril = lax.broadcasted_iota(I32, (c, c), 0) >= lax.broadcasted_iota(I32, (c, c), 1)
    boff = GROUP_WIDTH
    coff = GROUP_WIDTH + SSM_NGROUPS * SSM_STATE

    row = lax.broadcasted_iota(I32, (tl, tl), 0)
    col = lax.broadcasted_iota(I32, (tl, tl), 1)
    same = (row // c) == (col // c)
    tril_b = (same & (row >= col)).astype(BF16)
    triu_b = (same & (row <= col)).astype(BF16)
    a_parts = _split3(jnp.concatenate([ad_all, jnp.zeros((tl, 8 - N_HEADS), F32)], axis=1))
    acs_cols = sum(jnp.dot(tril_b, part, preferred_element_type=F32) for part in a_parts)
    acs_rows = sum(lax.dot_general(part, triu_b, (((0,), (0,)), ((), ())), preferred_element_type=F32) for part in a_parts)

    for ci in range(tl // c):
        r0 = ci * c
        ys = []
        for h in range(N_HEADS):
            grp = h // hpg
            x = xbc[r0:r0 + c, h * HEAD_DIM:(h + 1) * HEAD_DIM]
            bm = xbc[r0:r0 + c, boff + grp * SSM_STATE:boff + (grp + 1) * SSM_STATE]
            cm = xbc[r0:r0 + c, coff + grp * SSM_STATE:coff + (grp + 1) * SSM_STATE]
            xdt = x * dt_all[r0:r0 + c, h:h + 1]
            acol = acs_cols[r0:r0 + c, h:h + 1]
            arow = acs_rows[h:h + 1, r0:r0 + c]
            lmat = jnp.where(tril, jnp.exp(jnp.where(tril, acol - arow, 0.0)), 0.0)
            y_diag = _dot(_dot_nt(cm, bm) * lmat, xdt)
            a_last = acol[c - 1:c, :]
            states = _dot_tn(xdt * jnp.exp(a_last - acol), bm)
            hin = h_ref[h]
            y_off = _dot_nt(cm, hin) * jnp.exp(acol)
            h_ref[h] = hin * jnp.exp(a_last) + states
            yh = y_diag + y_off + par_ref[2:3, h:h + 1] * x
            ys.append(yh * _silu(z_ref[0, r0:r0 + c, h * HEAD_DIM:(h + 1) * HEAD_DIM]))
        for grp in range(SSM_NGROUPS):
            grp_ys = ys[grp * hpg:(grp + 1) * hpg]
            ms = sum(jnp.sum(a * a, axis=-1, keepdims=True) for a in grp_ys) / (hpg * HEAD_DIM)
            r = lax.rsqrt(ms + RMS_EPS)
            for j, a in enumerate(grp_ys):
                h = grp * hpg + j
                o_ref[0, r0:r0 + c, h * HEAD_DIM:(h + 1) * HEAD_DIM] = a * r * normw_ref[:, h * HEAD_DIM:(h + 1) * HEAD_DIM]

    @pl.when(step == pl.num_programs(1) - 1)
    def _():
        h_out_ref[0] = h_ref[...]


def _ssd_prompt(xbc, z, small, conv_w, conv_b, a_log, dt_bias, d_skip, norm_w, tl=256):
    b, l, _ = xbc.shape
    assert l % tl == 0 and tl % CHUNK == 0
    par = jnp.stack([-jnp.exp(a_log.astype(F32)), dt_bias.astype(F32), d_skip.astype(F32)])
    return pl.pallas_call(
        functools.partial(_ssd_kernel, tl=tl),
        grid=(b, l // tl),
        in_specs=[pl.BlockSpec((1, tl, SSM_CONV_DIM), lambda bi, i: (bi, i, 0)),
                  pl.BlockSpec((1, tl, GROUP_WIDTH), lambda bi, i: (bi, i, 0)),
                  pl.BlockSpec((1, tl, LANE), lambda bi, i: (bi, i, 0)),
                  pl.BlockSpec((CONV_W, SSM_CONV_DIM), lambda bi, i: (0, 0)),
                  pl.BlockSpec((1, SSM_CONV_DIM), lambda bi, i: (0, 0)),
                  pl.BlockSpec((3, N_HEADS), lambda bi, i: (0, 0)),
                  pl.BlockSpec((1, GROUP_WIDTH), lambda bi, i: (0, 0))],
        out_specs=[pl.BlockSpec((1, tl, GROUP_WIDTH), lambda bi, i: (bi, i, 0)),
                   pl.BlockSpec((1, CONV_W - 1, SSM_CONV_DIM), lambda bi, i: (bi, 0, 0)),
                   pl.BlockSpec((1, N_HEADS, HEAD_DIM, SSM_STATE), lambda bi, i: (bi, 0, 0, 0))],
        out_shape=[jax.ShapeDtypeStruct((b, l, GROUP_WIDTH), F32),
                   jax.ShapeDtypeStruct((b, CONV_W - 1, SSM_CONV_DIM), F32),
                   jax.ShapeDtypeStruct((b, N_HEADS, HEAD_DIM, SSM_STATE), F32)],
        scratch_shapes=[pltpu.VMEM((tl + 8, SSM_CONV_DIM), F32),
                        pltpu.VMEM((N_HEADS, HEAD_DIM, SSM_STATE), F32)],
        compiler_params=pltpu.CompilerParams(dimension_semantics=("arbitrary", "arbitrary"),
                                             vmem_limit_bytes=VMEM_LIMIT),
        name="ssd_prompt",
    )(xbc, z, small, conv_w, conv_b.reshape(1, SSM_CONV_DIM), par, norm_w.reshape(1, GROUP_WIDTH))


def _post_mix_kernel(o0_ref, o1_ref, o2_ref, o3_ref, x_ref, wout_ref, g_ref, b_ref, rw_ref, rb_ref,
                     xn_ref, xnb_ref, idx_ref, gate_ref, rank_ref, cum_ref, cnt_ref, *, tm):
    @pl.when(pl.program_id(0) == 0)
    def _():
        cnt_ref[...] = jnp.zeros_like(cnt_ref)

    mix = jnp.zeros((tm, D_MODEL), F32)
    for gi, o_ref in enumerate((o0_ref, o1_ref, o2_ref, o3_ref)):
        mix = mix + jnp.dot(o_ref[...].astype(BF16), wout_ref[gi * GROUP_WIDTH:(gi + 1) * GROUP_WIDTH, :],
                            preferred_element_type=F32)
    xn = _ln(DN_ALPHA * x_ref[...] + mix, g_ref[...], b_ref[...])
    xn_ref[...] = xn
    xnb_ref[...] = xn.astype(BF16)

    logits = _dot_hi(xn, rw_ref[...]) + rb_ref[...]
    lane_e = lax.broadcasted_iota(I32, (tm, N_EXPERTS), 1).astype(F32)
    lane = lax.broadcasted_iota(I32, (tm, LANE), 1)
    work = logits
    sel = jnp.zeros((tm, N_EXPERTS), F32)
    idx_out = jnp.zeros((tm, LANE), I32)
    val_out = jnp.zeros((tm, LANE), F32)
    hits = []
    for r in range(TOP_K):
        m = jnp.max(work, axis=-1, keepdims=True)
        pick = jnp.min(jnp.where(work == m, lane_e, float(N_EXPERTS)), axis=-1, keepdims=True)
        hit = lane_e == pick
        work = jnp.where(hit, -jnp.inf, work)
        sel = sel + hit.astype(F32)
        hits.append(hit)
        idx_out = jnp.where(lane == r, pick.astype(I32), idx_out)
        val_out = jnp.where(lane == r, m, val_out)
    ex = jnp.where(lane < TOP_K, jnp.exp(val_out - val_out[:, 0:1]), 0.0)
    gate_ref[...] = ex / jnp.sum(ex, axis=-1, keepdims=True)
    idx_ref[...] = idx_out

    row = lax.broadcasted_iota(I32, (tm, tm), 0)
    col = lax.broadcasted_iota(I32, (tm, tm), 1)
    before = jnp.dot((row > col).astype(BF16), sel.astype(BF16), preferred_element_type=F32) + cnt_ref[...]
    rank_out = jnp.zeros((tm, LANE), I32)
    for r in range(TOP_K):
        rk = jnp.sum(jnp.where(hits[r], before, 0.0), axis=-1, keepdims=True)
        rank_out = jnp.where(lane == r, rk.astype(I32), rank_out)
    rank_ref[...] = rank_out
    cum_ref[0] = jnp.broadcast_to(cnt_ref[...], (8, N_EXPERTS))
    cnt_ref[...] = cnt_ref[...] + jnp.sum(sel, axis=0, keepdims=True)


def _post_mix(outs, x, w_out_bf16, ln_g, ln_b, router_w, router_b, tm):
    n = x.shape[0]
    nt = n // tm
    tok = lambda w: pl.BlockSpec((tm, w), lambda i: (i, 0))
    full = lambda a: pl.BlockSpec(a.shape, lambda i: (0,) * a.ndim)
    g2, b2, rb2 = ln_g.reshape(1, -1), ln_b.reshape(1, -1), router_b.reshape(1, -1)
    return pl.pallas_call(
        functools.partial(_post_mix_kernel, tm=tm),
        grid=(nt,),
        in_specs=[tok(GROUP_WIDTH)] * 4 + [tok(D_MODEL), full(w_out_bf16), full(g2), full(b2), full(router_w), full(rb2)],
        out_specs=[tok(D_MODEL), tok(D_MODEL), tok(LANE), tok(LANE), tok(LANE),
                   pl.BlockSpec((1, 8, N_EXPERTS), lambda i: (i, 0, 0))],
        out_shape=[jax.ShapeDtypeStruct((n, D_MODEL), F32), jax.ShapeDtypeStruct((n, D_MODEL), BF16),
                   jax.ShapeDtypeStruct((n, LANE), I32), jax.ShapeDtypeStruct((n, LANE), F32),
                   jax.ShapeDtypeStruct((n, LANE), I32), jax.ShapeDtypeStruct((nt, 8, N_EXPERTS), F32)],
        scratch_shapes=[pltpu.VMEM((1, N_EXPERTS), F32)],
        compiler_params=pltpu.CompilerParams(dimension_semantics=("arbitrary",), vmem_limit_bytes=VMEM_LIMIT),
        name="post_mix",
    )(*outs, x, w_out_bf16, g2, b2, router_w, rb2)


def _moe_dispatch(idx, gates, rank, cum, n_tok, tm_cum, tm, blk):
    nt = n_tok // tm
    n_blocks = -(-n_tok * TOP_K // blk) + N_EXPERTS
    per_pair = -(-tm // blk) + 1
    e_idx = idx[:, :TOP_K]
    cum_i = cum[:, 0, :].astype(I32)
    onehot = e_idx[:, :, None] == jnp.arange(N_EXPERTS, dtype=I32)
    counts = cum_i[-1] + jnp.sum(onehot[n_tok - tm_cum:].astype(I32), axis=(0, 1))
    cum_t = cum_i[::tm // tm_cum]
    cum_all = jnp.concatenate([cum_t, counts[None]], 0)
    blocks_per_e = (counts + blk - 1) // blk
    blk_start = jnp.cumsum(blocks_per_e) - blocks_per_e
    dest = jnp.sum(jnp.where(onehot, blk_start, 0), axis=-1) * blk + rank[:, :TOP_K]
    dest_f = dest.astype(F32)
    dest_pad = jnp.pad(dest_f, ((0, 0), (0, LANE - TOP_K)), constant_values=-1.0)
    dest_t = jnp.pad(dest_f.T, ((0, 8 - TOP_K), (0, 0)), constant_values=-1.0)
    gate_t = jnp.pad(gates[:, :TOP_K].T, ((0, 8 - TOP_K), (0, 0)))
    eid_t = jnp.pad(e_idx.T, ((0, 8 - TOP_K), (0, 0)), constant_values=-1)

    lo, hi = cum_all[:-1].T, cum_all[1:].T
    has = hi > lo
    j0 = lo // blk
    j1 = jnp.maximum(hi - 1, 0) // blk
    jj = j0[:, :, None] + jnp.arange(per_pair, dtype=I32)
    valid = has[:, :, None] & (jj <= j1[:, :, None])
    blk_id = blk_start[:, None, None] + jj
    e_b = jnp.broadcast_to(jnp.arange(N_EXPERTS, dtype=I32)[:, None, None], jj.shape)
    t_b = jnp.broadcast_to(jnp.arange(nt, dtype=I32)[None, :, None], jj.shape)
    n_steps = min(N_EXPERTS * nt + n_blocks, per_pair * N_EXPERTS * nt)
    n_valid = jnp.sum(valid)
    pos = jnp.arange(n_steps)
    v = pos < n_valid

    win = min(MOE_ROW_WINDOW, blk)
    r_lo = jnp.maximum(lo[:, :, None], jj * blk) - jj * blk
    r_hi = jnp.minimum(hi[:, :, None], (jj + 1) * blk) - jj * blk
    w0 = (r_lo // 16) * 16
    n_win = jnp.maximum((r_hi - w0 + win - 1) // win, 1)
    win_bits = (n_win << 4) + (w0 << 8)

    def build(key):
        order = jnp.argsort(jnp.where(valid, key, jnp.int32(1 << 30)).reshape(-1))[:n_steps]
        src = order[jnp.minimum(pos, n_valid - 1)]
        return blk_id.reshape(-1)[src], t_b.reshape(-1)[src], e_b.reshape(-1)[src], win_bits.reshape(-1)[src]

    def edges(a):
        change = a[1:] != a[:-1]
        first = jnp.concatenate([jnp.ones((1,), bool), change]) & v
        last = (jnp.concatenate([change, jnp.ones((1,), bool)]) | (pos == n_valid - 1)) & v
        return first, last

    b1, t1, e1, wb1 = build(blk_id * nt + t_b)
    first_b, last_b = edges(b1)
    first_e, _ = edges(e1)
    flags1 = first_b.astype(I32) + 2 * last_b.astype(I32) + 4 * v.astype(I32) + 8 * first_e.astype(I32) + wb1
    b2, t2, e2, wb2 = build(t_b * (2 * n_blocks) + blk_id)
    first_t, last_t = edges(t2)
    flags2 = first_t.astype(I32) + 2 * last_t.astype(I32) + 4 * v.astype(I32) + wb2
    eid_pad = jnp.pad(e_idx.astype(F32), ((0, 0), (0, LANE - TOP_K)), constant_values=-1.0)
    return dict(tok=jnp.concatenate([eid_pad, dest_pad], axis=1),
                tok_t=jnp.concatenate([eid_t.astype(F32), dest_t, gate_t], axis=0),
                n_rows=n_blocks * blk, n_steps=n_steps,
                gather=(b1.astype(I32), t1.astype(I32), e1.astype(I32), flags1),
                combine=(b2.astype(I32), t2.astype(I32), e2.astype(I32), flags2))


def _expert_kernel(blk_s, tile_s, exp_s, flag_s, x_ref, tok_ref, wgu_ref, bgu_ref, wdn_ref, bdn_ref,
                   yb_ref, xacc_ref, gacc_ref, wgu_bf, wdn_bf, *, blk, tm):
    s = pl.program_id(0)
    flags = flag_s[s]

    @pl.when((flags & 8) != 0)
    def _():
        wgu_bf[...] = wgu_ref[0, 0].astype(BF16)
        wdn_bf[...] = wdn_ref[0, 0].astype(BF16)

    @pl.when((flags & 1) != 0)
    def _():
        xacc_ref[...] = jnp.zeros_like(xacc_ref)
        gacc_ref[...] = jnp.zeros_like(gacc_ref)

    @pl.when((flags & 4) != 0)
    def _():
        mine = tok_ref[0:8, :] == exp_s[s].astype(F32)
        d_sel = jnp.sum(jnp.where(mine, tok_ref[8:16, :] + 1.0, 0.0), axis=0, keepdims=True) - 1.0
        g_sel = jnp.sum(jnp.where(mine, tok_ref[16:24, :], 0.0), axis=0, keepdims=True)
        win = min(MOE_ROW_WINDOW, blk)
        w0 = flags >> 8
        base = blk_s[s] * blk

        def window(i, carry):
            start = w0 + i * win
            r0 = pl.multiple_of(jnp.minimum(start, blk - win), 16)
            local = r0 + lax.broadcasted_iota(I32, (win, tm), 0)
            hit = (d_sel == (base + local).astype(F32)) & (local >= start)
            p = jnp.where(hit, 1.0, 0.0).astype(BF16)
            xacc_ref[pl.ds(r0, win), :] += jnp.dot(p, x_ref[...], preferred_element_type=F32)
            gacc_ref[pl.ds(r0, win), :] += jnp.sum(jnp.where(hit, g_sel, 0.0), axis=-1, keepdims=True)
            return carry

        lax.fori_loop(0, (flags >> 4) & 15, window, 0)

    @pl.when((flags & 2) != 0)
    def _():
        h = jnp.dot(xacc_ref[...].astype(BF16), wgu_bf[...], preferred_element_type=F32) + bgu_ref[0, 0]
        gate = jnp.minimum(h[:, :D_FF], SWIGLU_LIMIT)
        up = jnp.clip(h[:, D_FF:], -SWIGLU_LIMIT, SWIGLU_LIMIT)
        act = (up + 1.0) * gate * _sigmoid(SWIGLU_ALPHA * gate)
        y = jnp.dot(act.astype(BF16), wdn_bf[...], preferred_element_type=F32) + bdn_ref[0, 0]
        yb_ref[...] = (y * gacc_ref[...]).astype(yb_ref.dtype)


def _expert_ffn(layer, xn_bf16, disp, w_gu, b_gu, w_down, b_down, tm, blk):
    blk_s, tile_s, exp_s, flag_s = disp["gather"]
    depth = w_gu.shape[0]
    tok_t = pl.BlockSpec((24, tm), lambda s, b, t, e, f: (0, t[s]))
    grid_spec = pltpu.PrefetchScalarGridSpec(
        num_scalar_prefetch=4,
        grid=(disp["n_steps"],),
        in_specs=[pl.BlockSpec((tm, D_MODEL), lambda s, b, t, e, f: (t[s], 0)), tok_t,
                  pl.BlockSpec((1, 1, D_MODEL, 2 * D_FF), lambda s, b, t, e, f: (layer, e[s], 0, 0)),
                  pl.BlockSpec((1, 1, 1, 2 * D_FF), lambda s, b, t, e, f: (layer, e[s], 0, 0)),
                  pl.BlockSpec((1, 1, D_FF, D_MODEL), lambda s, b, t, e, f: (layer, e[s], 0, 0)),
                  pl.BlockSpec((1, 1, 1, D_MODEL), lambda s, b, t, e, f: (layer, e[s], 0, 0))],
        out_specs=pl.BlockSpec((blk, D_MODEL), lambda s, b, t, e, f: (b[s], 0)),
        scratch_shapes=[pltpu.VMEM((blk, D_MODEL), F32), pltpu.VMEM((blk, 1), F32),
                        pltpu.VMEM((D_MODEL, 2 * D_FF), BF16), pltpu.VMEM((D_FF, D_MODEL), BF16)])
    return pl.pallas_call(
        functools.partial(_expert_kernel, blk=blk, tm=tm),
        grid_spec=grid_spec,
        out_shape=jax.ShapeDtypeStruct((disp["n_rows"], D_MODEL), BF16),
        compiler_params=pltpu.CompilerParams(dimension_semantics=("arbitrary",), vmem_limit_bytes=VMEM_LIMIT),
        name="expert_ffn",
    )(blk_s, tile_s, exp_s, flag_s, xn_bf16, disp["tok_t"],
      w_gu, b_gu.reshape(depth, N_EXPERTS, 1, 2 * D_FF), w_down, b_down.reshape(depth, N_EXPERTS, 1, D_MODEL))


def _combine_kernel(blk_s, tile_s, exp_s, flag_s, yb_ref, tok_ref, xn_ref, g_ref, b_ref, y_ref, acc_ref, *, blk, tm):
    s = pl.program_id(0)
    flags = flag_s[s]

    @pl.when((flags & 1) != 0)
    def _():
        acc_ref[...] = jnp.zeros_like(acc_ref)

    @pl.when((flags & 4) != 0)
    def _():
        mine = tok_ref[:, 0:LANE] == exp_s[s].astype(F32)
        d_sel = jnp.sum(jnp.where(mine, tok_ref[:, LANE:2 * LANE] + 1.0, 0.0), axis=-1, keepdims=True) - 1.0
        rowid = (blk_s[s] * blk + lax.broadcasted_iota(I32, (tm, blk), 1)).astype(F32)
        p = jnp.where(d_sel == rowid, 1.0, 0.0).astype(BF16)
        acc_ref[...] += jnp.dot(p, yb_ref[...], preferred_element_type=F32)

    @pl.when((flags & 2) != 0)
    def _():
        y_ref[...] = _ln(DN_ALPHA * xn_ref[...] + acc_ref[...], g_ref[...], b_ref[...])


def _moe_combine(yb, disp, xn, ln_g, ln_b, tm, blk):
    blk_s, tile_s, exp_s, flag_s = disp["combine"]
    n = xn.shape[0]
    tok = pl.BlockSpec((tm, 2 * LANE), lambda s, b, t, e, f: (t[s], 0))
    grid_spec = pltpu.PrefetchScalarGridSpec(
        num_scalar_prefetch=4,
        grid=(disp["n_steps"],),
        in_specs=[pl.BlockSpec((blk, D_MODEL), lambda s, b, t, e, f: (b[s], 0)), tok,
                  pl.BlockSpec((tm, D_MODEL), lambda s, b, t, e, f: (t[s], 0)),
                  pl.BlockSpec((1, D_MODEL), lambda s, b, t, e, f: (0, 0)),
                  pl.BlockSpec((1, D_MODEL), lambda s, b, t, e, f: (0, 0))],
        out_specs=pl.BlockSpec((tm, D_MODEL), lambda s, b, t, e, f: (t[s], 0)),
        scratch_shapes=[pltpu.VMEM((tm, D_MODEL), F32)])
    return pl.pallas_call(
        functools.partial(_combine_kernel, blk=blk, tm=tm),
        grid_spec=grid_spec,
        out_shape=jax.ShapeDtypeStruct((n, D_MODEL), F32),
        compiler_params=pltpu.CompilerParams(dimension_semantics=("arbitrary",), vmem_limit_bytes=VMEM_LIMIT),
        name="moe_combine",
    )(blk_s, tile_s, exp_s, flag_s, yb, disp["tok"], xn, ln_g.reshape(1, -1), ln_b.reshape(1, -1))


def _post_mix_moe(layer, outs, x, w_out, fw):
    ln1_g, ln1_b, router_w, router_b, w_gu, b_gu, w_down, b_down, ln2_g, ln2_b = fw
    n = x.shape[0]
    tm_cum = min(256, n)
    tm = min(1024, n)
    blk = min(EXPERT_BLOCK, n * TOP_K)
    xn, xnb, idx, gates, rank, cum = _post_mix(outs, x, w_out.astype(BF16), ln1_g, ln1_b, router_w, router_b, tm_cum)
    disp = _moe_dispatch(idx, gates, rank, cum, n, tm_cum, tm, blk)
    yb = _expert_ffn(layer, xnb, disp, w_gu, b_gu, w_down, b_down, tm, blk)
    return _moe_combine(yb, disp, xn, ln2_g, ln2_b, tm, blk)


def _rec_step_kernel(gqkv_ref, gz_ref, xbc_ref, sz_ref, small_ref, gconv0_ref, sconv0_ref, s0_ref, h0_ref,
                     gcw_ref, scw_ref, scb_ref, gpar_ref, spar_ref, gnw_ref, snw_ref,
                     gout_ref, sout_ref, gconv_ref, sconv_ref, s_ref, h_ref):
    small = small_ref[0]

    def conv(u, buf_ref, w_ref, out_ref):
        buf = buf_ref[0]
        y = u * w_ref[CONV_W - 1:CONV_W, :]
        for i in range(CONV_W - 1):
            y = y + buf[i:i + 1, :] * w_ref[i:i + 1, :]
        out_ref[0, 0:CONV_W - 2, :] = buf[1:CONV_W - 1, :]
        out_ref[0, CONV_W - 2:CONV_W - 1, :] = u
        return y

    eye = (lax.broadcasted_iota(I32, (HEAD_DIM, HEAD_DIM), 0)
           == lax.broadcasted_iota(I32, (HEAD_DIM, HEAD_DIM), 1)).astype(F32)

    def to_col(r):
        return jnp.sum(eye * r, axis=-1, keepdims=True)

    y = _silu(conv(gqkv_ref[0], gconv0_ref, gcw_ref, gconv_ref))
    beta_all = _sigmoid(small[:, 0:N_HEADS])
    g_all = gpar_ref[0:1, :] * _softplus(small[:, N_HEADS:2 * N_HEADS] + gpar_ref[1:2, :])
    for h in range(N_HEADS):
        hs = slice(h * HEAD_DIM, (h + 1) * HEAD_DIM)
        q = y[:, h * HEAD_DIM:(h + 1) * HEAD_DIM]
        k = y[:, GROUP_WIDTH + h * HEAD_DIM:GROUP_WIDTH + (h + 1) * HEAD_DIM]
        v = y[:, 2 * GROUP_WIDTH + h * HEAD_DIM:2 * GROUP_WIDTH + (h + 1) * HEAD_DIM]
        q = q * lax.rsqrt(jnp.sum(q * q, axis=-1, keepdims=True) + RMS_EPS) * (HEAD_DIM ** -0.5)
        k = k * lax.rsqrt(jnp.sum(k * k, axis=-1, keepdims=True) + RMS_EPS)
        beta = beta_all[:, h:h + 1]
        eg = jnp.exp(g_all[:, h:h + 1])
        s0 = s0_ref[0, h]
        kc, qc = to_col(k), to_col(q)
        v_new = v * beta - jnp.sum(kc * (beta * eg) * s0, axis=0, keepdims=True)
        o = jnp.sum(qc * eg * s0, axis=0, keepdims=True) + jnp.sum(q * k, axis=-1, keepdims=True) * v_new
        s_ref[0, h] = s0 * eg + kc * v_new
        o = o * lax.rsqrt(jnp.mean(o * o, axis=-1, keepdims=True) + RMS_EPS) * gnw_ref[...] * _silu(gz_ref[0][:, hs])
        gout_ref[0, :, hs] = o

    xbc = _silu(conv(xbc_ref[0], sconv0_ref, scw_ref, sconv_ref) + scb_ref[...])
    dt_all = _softplus(small[:, 2 * N_HEADS:3 * N_HEADS] + spar_ref[1:2, :])
    hpg = N_HEADS // SSM_NGROUPS
    boff, coff = GROUP_WIDTH, GROUP_WIDTH + SSM_NGROUPS * SSM_STATE
    ys = []
    for h in range(N_HEADS):
        grp = h // hpg
        x = xbc[:, h * HEAD_DIM:(h + 1) * HEAD_DIM]
        bm = xbc[:, boff + grp * SSM_STATE:boff + (grp + 1) * SSM_STATE]
        cm = xbc[:, coff + grp * SSM_STATE:coff + (grp + 1) * SSM_STATE]
        dt = dt_all[:, h:h + 1]
        hn = h0_ref[0, h] * jnp.exp(dt * spar_ref[0:1, h:h + 1]) + to_col(x * dt) * bm
        h_ref[0, h] = hn
        yc = jnp.sum(hn * cm, axis=-1, keepdims=True)
        yh = jnp.sum(eye * yc, axis=0, keepdims=True) + spar_ref[2:3, h:h + 1] * x
        ys.append(yh * _silu(sz_ref[0][:, h * HEAD_DIM:(h + 1) * HEAD_DIM]))
    for grp in range(SSM_NGROUPS):
        grp_ys = ys[grp * hpg:(grp + 1) * hpg]
        ms = sum(jnp.sum(a * a, axis=-1, keepdims=True) for a in grp_ys) / (hpg * HEAD_DIM)
        r = lax.rsqrt(ms + RMS_EPS)
        for j, a in enumerate(grp_ys):
            h = grp * hpg + j
            sout_ref[0, :, h * HEAD_DIM:(h + 1) * HEAD_DIM] = a * r * snw_ref[:, h * HEAD_DIM:(h + 1) * HEAD_DIM]


def _rec_step(seg, gconv0, s0, sconv0, h0, lw):
    (_, gdn_conv_w, gdn_a_log, gdn_dt_bias, gdn_norm_w, ssm_conv_w, ssm_conv_b,
     ssm_a_log, ssm_dt_bias, ssm_d, ssm_norm_w, _) = lw
    b = seg["small"].shape[0]
    gpar = jnp.stack([-jnp.exp(gdn_a_log.astype(F32)), gdn_dt_bias.astype(F32)])
    spar = jnp.stack([-jnp.exp(ssm_a_log.astype(F32)), ssm_dt_bias.astype(F32), ssm_d.astype(F32)])
    row = lambda w: pl.BlockSpec((1, 1, w), lambda i: (i, 0, 0))
    full = lambda a: pl.BlockSpec(a.shape, lambda i: (0,) * a.ndim)
    per_seq = lambda *dims: pl.BlockSpec((1,) + dims, lambda i: (i,) + (0,) * len(dims))
    r3 = lambda a: a.reshape(b, 1, -1)
    consts = [gdn_conv_w, ssm_conv_w, ssm_conv_b.reshape(1, -1), gpar, spar,
              gdn_norm_w.reshape(1, HEAD_DIM), ssm_norm_w.reshape(1, GROUP_WIDTH)]
    state_specs = [per_seq(CONV_W - 1, GDN_CONV_DIM), per_seq(CONV_W - 1, SSM_CONV_DIM),
                   per_seq(N_HEADS, HEAD_DIM, HEAD_DIM), per_seq(N_HEADS, HEAD_DIM, SSM_STATE)]
    go, so, gc, sc, s, h = pl.pallas_call(
        _rec_step_kernel,
        grid=(b,),
        in_specs=[row(GDN_CONV_DIM), row(GROUP_WIDTH), row(SSM_CONV_DIM), row(GROUP_WIDTH), row(LANE)]
                 + state_specs + [full(c) for c in consts],
        out_specs=[row(GROUP_WIDTH), row(GROUP_WIDTH)] + state_specs,
        out_shape=[jax.ShapeDtypeStruct((b, 1, GROUP_WIDTH), F32), jax.ShapeDtypeStruct((b, 1, GROUP_WIDTH), F32),
                   jax.ShapeDtypeStruct((b, CONV_W - 1, GDN_CONV_DIM), F32),
                   jax.ShapeDtypeStruct((b, CONV_W - 1, SSM_CONV_DIM), F32),
                   jax.ShapeDtypeStruct((b, N_HEADS, HEAD_DIM, HEAD_DIM), F32),
                   jax.ShapeDtypeStruct((b, N_HEADS, HEAD_DIM, SSM_STATE), F32)],
        compiler_params=pltpu.CompilerParams(dimension_semantics=("arbitrary",), vmem_limit_bytes=VMEM_LIMIT),
        name="rec_step",
    )(r3(seg["gdn_qkv"]), r3(seg["gdn_z"]), r3(seg["ssm_xbc"]), r3(seg["ssm_z"]), r3(seg["small"]),
      gconv0, sconv0, s0, h0, *consts)
    return go.reshape(b, GROUP_WIDTH), so.reshape(b, GROUP_WIDTH), gc, sc, s, h


PAGES_PER_STEP = 8


def _decode_pass_kernel(pt_ref, sq_ref, mq_ref, *refs, n_steps):
    pp = PAGES_PER_STEP
    sk_refs, sv_refs, mk_refs = refs[0:pp], refs[pp:2 * pp], refs[2 * pp:3 * pp]
    sb_out_ref, gp_ref, accv_ref, carry_ref = refs[3 * pp:]
    p = pl.program_id(1)

    @pl.when(p == 0)
    def _():
        accv_ref[...] = jnp.zeros_like(accv_ref)
        carry_ref[...] = jnp.zeros_like(carry_ref)

    qs = sq_ref[0] * (HEAD_DIM ** -0.5)
    qm = mq_ref[0] * (1.0 / MOBA_BLOCK)
    row = lax.broadcasted_iota(I32, (PAGE_SIZE, PAGE_SIZE), 0)
    col = lax.broadcasted_iota(I32, (PAGE_SIZE, PAGE_SIZE), 1)
    upper = (row > col).astype(BF16)
    sub8 = lax.broadcasted_iota(I32, (8, LANE), 0)
    diag8 = sub8 == lax.broadcasted_iota(I32, (8, LANE), 1)
    heads = [slice(h * HEAD_DIM, (h + 1) * HEAD_DIM) for h in range(N_HEADS)]

    def head_rows(x):
        out = jnp.zeros((8, LANE), F32)
        for h, hs in enumerate(heads):
            out = jnp.where(sub8 == h, jnp.sum(x[hs, :], axis=0, keepdims=True), out)
        return out

    zs = [head_rows(sk_refs[i][0, 0] * qs) for i in range(pp)]
    ls = [-_softplus(z) for z in zs]
    las = _split_dot(jnp.concatenate(ls, axis=0), upper)
    accv = [accv_ref[hs, :] for hs in heads]
    carry = carry_ref[...]
    for i in range(pp):
        la = las[i * 8:(i + 1) * 8, :]
        w = jnp.exp(zs[i] + ls[i] + la + carry)
        vt = sv_refs[i][0, 0]
        accv = [a + vt[hs, :] * w[h:h + 1, :] for h, (hs, a) in enumerate(zip(heads, accv))]
        carry = carry + la[:, 0:1] + ls[i][:, 0:1]
        gp = jnp.sum(head_rows(mk_refs[i][0, 0] * qm), axis=-1, keepdims=True)
        gp_ref[0, pl.ds(p * pp + i, 1), :] = jnp.sum(jnp.where(diag8, gp, 0.0), axis=0, keepdims=True)
    for hs, a in zip(heads, accv):
        accv_ref[hs, :] = a
    carry_ref[...] = carry

    @pl.when(p == n_steps - 1)
    def _():
        sb_out_ref[0] = jnp.sum(accv_ref[...], axis=-1, keepdims=True)


def _decode_pass(layer, page_table, sq, mq, cache_sb_k, cache_sb_v, cache_moba_k):
    b, n_pages = page_table.shape
    pp = PAGES_PER_STEP
    assert n_pages % pp == 0
    n_steps = n_pages // pp

    def page_spec(i):
        return pl.BlockSpec((1, 1, GROUP_WIDTH, PAGE_SIZE),
                            lambda bi, p, pt: (layer, pt[bi, n_pages - 1 - (p * pp + i)], 0, 0))

    colspec = pl.BlockSpec((1, GROUP_WIDTH, 1), lambda bi, p, pt: (bi, 0, 0))
    grid_spec = pltpu.PrefetchScalarGridSpec(
        num_scalar_prefetch=1,
        grid=(b, n_steps),
        in_specs=[colspec, colspec] + [page_spec(i) for i in range(pp)] * 3,
        out_specs=[colspec, pl.BlockSpec((1, n_pages, LANE), lambda bi, p, pt: (bi, 0, 0))],
        scratch_shapes=[pltpu.VMEM((GROUP_WIDTH, PAGE_SIZE), F32), pltpu.VMEM((8, 1), F32)])
    sb_out, gp = pl.pallas_call(
        functools.partial(_decode_pass_kernel, n_steps=n_steps),
        grid_spec=grid_spec,
        out_shape=[jax.ShapeDtypeStruct((b, GROUP_WIDTH, 1), F32), jax.ShapeDtypeStruct((b, n_pages, LANE), F32)],
        compiler_params=pltpu.CompilerParams(dimension_semantics=("arbitrary", "arbitrary"), vmem_limit_bytes=VMEM_LIMIT),
        name="decode_pass",
    )(page_table, sq.reshape(b, GROUP_WIDTH, 1), mq.reshape(b, GROUP_WIDTH, 1),
      *([cache_sb_k] * pp), *([cache_sb_v] * pp), *([cache_moba_k] * pp))
    return sb_out.reshape(b, GROUP_WIDTH), gp


def _moba_decode_kernel(pg_ref, mq_ref, mk_ref, mv_ref, *refs, n_sel):
    kp_refs, vp_refs, o_ref = refs[:n_sel], refs[n_sel:2 * n_sel], refs[2 * n_sel]
    h = pl.program_id(1)
    lane = lax.broadcasted_iota(I32, (1, GROUP_WIDTH), 1)
    in_head = lane // HEAD_DIM == h
    q = jnp.where(in_head, mq_ref[0] * (HEAD_DIM ** -0.5), 0.0)

    @pl.when(h == 0)
    def _():
        o_ref[...] = jnp.zeros_like(o_ref)

    q8 = jnp.where(lax.broadcasted_iota(I32, (8, GROUP_WIDTH), 0) == 0, q, 0.0).astype(BF16)
    s_self = jnp.sum(q * mk_ref[0], axis=-1, keepdims=True)
    scores = [jnp.dot(q8, kp[0, 0].astype(BF16), preferred_element_type=F32)[0:1, :] for kp in kp_refs]
    m = s_self
    for s in scores:
        m = jnp.maximum(m, jnp.max(s, axis=-1, keepdims=True))
    p_self = jnp.exp(s_self - m)
    l = p_self
    acc = p_self * mv_ref[0]
    first_row = lax.broadcasted_iota(I32, (8, PAGE_SIZE), 0) == 0
    for s, vp in zip(scores, vp_refs):
        p = jnp.exp(s - m)
        l = l + jnp.sum(p, axis=-1, keepdims=True)
        p8 = jnp.where(first_row, p, 0.0).astype(BF16)
        acc = acc + lax.dot_general(p8, vp[0, 0].astype(BF16), (((1,), (1,)), ((), ())),
                                    preferred_element_type=F32)[0:1, :]
    o_ref[0] = jnp.where(in_head, acc / l, o_ref[0])


def _moba_decode(layer, sel_pages, mq, mk_new, mv_new, cache_moba_k, cache_moba_v):
    b, nh, n_sel = sel_pages.shape
    row = pl.BlockSpec((1, 1, GROUP_WIDTH), lambda bi, h, pg: (bi, 0, 0))

    def page(j):
        return pl.BlockSpec((1, 1, GROUP_WIDTH, PAGE_SIZE),
                            lambda bi, h, pg: (layer, pg[(bi * nh + h) * n_sel + j], 0, 0))

    grid_spec = pltpu.PrefetchScalarGridSpec(
        num_scalar_prefetch=1,
        grid=(b, nh),
        in_specs=[row, row, row] + [page(j) for j in range(n_sel)] * 2,
        out_specs=pl.BlockSpec((1, 1, GROUP_WIDTH), lambda bi, h, pg: (bi, 0, 0)))
    r3 = lambda a: a.reshape(b, 1, GROUP_WIDTH)
    out = pl.pallas_call(
        functools.partial(_moba_decode_kernel, n_sel=n_sel),
        grid_spec=grid_spec,
        out_shape=jax.ShapeDtypeStruct((b, 1, GROUP_WIDTH), F32),
        compiler_params=pltpu.CompilerParams(dimension_semantics=("arbitrary",) * 2, vmem_limit_bytes=VMEM_LIMIT),
        name="moba_decode",
    )(sel_pages.reshape(-1).astype(I32), r3(mq), r3(mk_new), r3(mv_new),
      *([cache_moba_k] * n_sel), *([cache_moba_v] * n_sel))
    return out.reshape(b, GROUP_WIDTH)


def _sample_attention(layer, page_table, seg, caches):
    cache_moba_k, cache_moba_v, cache_sb_k, cache_sb_v = caches
    b, n_pages = page_table.shape
    ppb = MOBA_BLOCK // PAGE_SIZE
    n_past_blocks = n_pages // ppb
    assert n_pages % ppb == 0 and n_past_blocks >= MOBA_TOPK
    sb_out, gp = _decode_pass(layer, page_table, seg["sb_q"], seg["moba_q"], cache_sb_k, cache_sb_v, cache_moba_k)
    gates = gp[:, ::-1, :N_HEADS].reshape(b, n_past_blocks, ppb, N_HEADS).sum(axis=2)
    _, sel = lax.top_k(jnp.swapaxes(gates, 1, 2), MOBA_TOPK)
    pages = sel[..., None] * ppb + jnp.arange(ppb)
    sel_pages = jnp.take_along_axis(page_table[:, None, :], pages.reshape(b, N_HEADS, -1), axis=2)
    moba_out = _moba_decode(layer, sel_pages, seg["moba_q"], seg["moba_k"], seg["moba_v"], cache_moba_k, cache_moba_v)
    return moba_out, sb_out


def _kv_rows(seg, bsz, length):
    return tuple(seg[name].reshape(bsz, length, N_HEADS, HEAD_DIM) for name in ("moba_k", "moba_v", "sb_k", "sb_v"))


def _prompt_mixers(x, lw):
    (w_in, gdn_conv_w, gdn_a_log, gdn_dt_bias, gdn_norm_w, ssm_conv_w, ssm_conv_b,
     ssm_a_log, ssm_dt_bias, ssm_d, ssm_norm_w, _) = lw
    bsz, length, _ = x.shape
    n = bsz * length
    flat = _in_proj(x.reshape(n, D_MODEL), w_in, jnp.arange(length), length)
    seg = {k: v.reshape(bsz, length, -1) for k, v in flat.items()}

    gdn_out, gdn_conv_new, gdn_s_new = _gdn_prompt(seg["gdn_qkv"], seg["gdn_z"], seg["small"],
                                                   gdn_conv_w, gdn_a_log, gdn_dt_bias, gdn_norm_w)
    ssm_out, ssm_conv_new, ssm_h_new = _ssd_prompt(seg["ssm_xbc"], seg["ssm_z"], seg["small"], ssm_conv_w, ssm_conv_b,
                                                   ssm_a_log, ssm_dt_bias, ssm_d, ssm_norm_w)
    moba_out = _moba_attention(seg["moba_q"], seg["moba_k"], seg["moba_v"])
    sb_out = _sb_attention(seg["sb_q"], seg["sb_k"], seg["sb_v"])

    outs = [o.reshape(n, GROUP_WIDTH) for o in (gdn_out, ssm_out, moba_out, sb_out)]
    return outs, _kv_rows(seg, bsz, length) + (gdn_conv_new, gdn_s_new, ssm_conv_new, ssm_h_new)


def _sample_mixers(x, layer, page_table, rec_state, caches, lw):
    gdn_conv0, gdn_s0, ssm_conv0, ssm_h0 = rec_state
    bsz, length, _ = x.shape
    assert length == 1
    pos0 = page_table.shape[1] * PAGE_SIZE
    seg = _in_proj(x.reshape(bsz, D_MODEL), lw[0], jnp.full((1,), pos0, I32), 1)
    gdn_out, ssm_out, gdn_conv_new, ssm_conv_new, gdn_s_new, ssm_h_new = _rec_step(
        seg, gdn_conv0, gdn_s0, ssm_conv0, ssm_h0, lw)
    moba_out, sb_out = _sample_attention(layer, page_table, seg, caches)
    outs = [gdn_out, ssm_out, moba_out, sb_out]
    return outs, _kv_rows(seg, bsz, length) + (gdn_conv_new, gdn_s_new, ssm_conv_new, ssm_h_new)


def kernel(x_prompt, x_sample, cache_moba_k, cache_moba_v, cache_sb_k, cache_sb_v,
           state_gdn_conv, state_gdn_rec, state_ssm_conv, state_ssm_rec, page_table,
           w_in, gdn_conv_w, gdn_a_log, gdn_dt_bias, gdn_norm_w,
           ssm_conv_w, ssm_conv_b, ssm_a_log, ssm_dt_bias, ssm_d, ssm_norm_w,
           w_out, ln1_g, ln1_b, router_w, router_b,
           expert_w_gu, expert_b_gu, expert_w_down, expert_b_down, ln2_g, ln2_b):
    caches = tuple(jnp.transpose(c, (0, 1, 3, 4, 2)).reshape(c.shape[0], c.shape[1], GROUP_WIDTH, PAGE_SIZE)
                   for c in (cache_moba_k, cache_moba_v, cache_sb_k, cache_sb_v))
    yp, ys = x_prompt, x_sample
    new_p, new_s = [], []
    for l in range(DEPTH):
        lw = (w_in[l], gdn_conv_w[l], gdn_a_log[l], gdn_dt_bias[l], gdn_norm_w[l], ssm_conv_w[l], ssm_conv_b[l],
              ssm_a_log[l], ssm_dt_bias[l], ssm_d[l], ssm_norm_w[l], w_out[l])
        fw = (ln1_g[l], ln1_b[l], router_w[l], router_b[l], expert_w_gu, expert_b_gu,
              expert_w_down, expert_b_down, ln2_g[l], ln2_b[l])
        rec_state = (state_gdn_conv[l], state_gdn_rec[l], state_ssm_conv[l], state_ssm_rec[l])
        outs_p, st_p = _prompt_mixers(yp, lw)
        yp = _post_mix_moe(l, outs_p, yp.reshape(-1, D_MODEL), w_out[l], fw).reshape(yp.shape)
        outs_s, st_s = _sample_mixers(ys, l, page_table, rec_state, caches, lw)
        ys = _post_mix_moe(l, outs_s, ys.reshape(-1, D_MODEL), w_out[l], fw).reshape(ys.shape)
        new_p.append(st_p)
        new_s.append(st_s)

    def stk(states, i):
        return jnp.stack([s[i] for s in states])

    return (yp, ys,
            stk(new_p, 0), stk(new_p, 1), stk(new_p, 2), stk(new_p, 3),
            stk(new_p, 4), stk(new_p, 5), stk(new_p, 6), stk(new_p, 7),
            stk(new_s, 0), stk(new_s, 1), stk(new_s, 2), stk(new_s, 3),
            stk(new_s, 4), stk(new_s, 5), stk(new_s, 6), stk(new_s, 7))
```
